```python
import math
import jax, jax.numpy as jnp
from jax import lax
import numpy as np

D_MODEL = 1024
BATCH = 2
SEQ = 8192
DEPTH = 2

HEAD_DIM = 64
N_HEADS = 8
N_KV_HEADS = 2
GQA = N_HEADS // N_KV_HEADS
ATTN_WIDTH = N_HEADS * HEAD_DIM
KV_WIDTH = N_KV_HEADS * HEAD_DIM
WINDOW = 128
ATTN_BLOCK = 128
SSM_WIDTH = D_MODEL - ATTN_WIDTH
SSM_GROUP_DIM = 16
SSM_GROUPS = SSM_WIDTH // SSM_GROUP_DIM
SSM_STATE = 64
DT_MIN = 1e-3
DT_MAX = 1e-1
IN_WIDTH = ATTN_WIDTH + 2 * KV_WIDTH + SSM_WIDTH
D_FF = 3584
N_EXPERTS = 8
TOP_K = 2
MOE_BLOCK = 256
N_DENSE = (DEPTH + 1) // 2
N_MOE = DEPTH // 2
NORM_EPS = 1e-6
NEG_INF = -1e30

kernel_name = "hymba_swa_sink_s5_moe"


def _rms(x, g):
    xf = x.astype(jnp.float32)
    y = xf * lax.rsqrt(jnp.mean(xf * xf, axis=-1, keepdims=True) + NORM_EPS)
    return y * g.astype(jnp.float32)


def _swa_sink_attention(q, k, v, sinks):
    bsz, seq = q.shape[0], q.shape[1]
    nb = seq // ATTN_BLOCK
    qb = q.reshape(bsz, nb, ATTN_BLOCK, N_KV_HEADS, GQA, HEAD_DIM)
    kb = k.reshape(bsz, nb, ATTN_BLOCK, N_KV_HEADS, HEAD_DIM)
    vb = v.reshape(bsz, nb, ATTN_BLOCK, N_KV_HEADS, HEAD_DIM)

    def with_prev(t):
        prev = jnp.concatenate([jnp.zeros_like(t[:, :1]), t[:, :-1]], axis=1)
        return jnp.concatenate([prev, t], axis=2)

    kk, vv = with_prev(kb), with_prev(vb)
    s = jnp.einsum('bnqkgd,bnskd->bnkgqs', qb, kk) * (HEAD_DIM ** -0.5)
    qi = jnp.arange(ATTN_BLOCK)[:, None]
    sj = jnp.arange(2 * ATTN_BLOCK)[None, :]
    delta = qi + ATTN_BLOCK - sj
    band = (delta >= 0) & (delta < WINDOW)
    not_first = (jnp.arange(nb) > 0)[:, None, None]
    valid = band[None] & (not_first | (sj >= ATTN_BLOCK)[None])
    s = jnp.where(valid[None, :, None, None], s, NEG_INF)
    sink = jnp.broadcast_to(
        sinks.astype(jnp.float32).reshape(1, 1, N_KV_HEADS, GQA, 1, 1), s.shape[:-1] + (1,))
    p = jax.nn.softmax(jnp.concatenate([s, sink], axis=-1), axis=-1)[..., :-1]
    o = jnp.einsum('bnkgqs,bnskd->bnqkgd', p, vv)
    return o.reshape(bsz, seq, ATTN_WIDTH)


def _s5_mixer(u, lam_re, lam_im, log_dt, b_re, b_im, c_re, c_im, d_skip, w_glu):
    f32 = jnp.float32
    bsz, seq = u.shape[0], u.shape[1]
    ug = u.astype(f32).reshape(bsz, seq, SSM_GROUPS, SSM_GROUP_DIM)
    lr, li = lam_re.astype(f32), lam_im.astype(f32)
    dt = jnp.exp(log_dt.astype(f32))[:, None]
    mag = jnp.exp(lr * dt)
    ang = li * dt
    ab_re, ab_im = mag * jnp.cos(ang), mag * jnp.sin(ang)
    nr = ab_re - 1.0
    den = lr * lr + li * li
    f_re = (nr * lr + ab_im * li) / den
    f_im = (ab_im * lr - nr * li) / den
    br, bi = b_re.astype(f32), b_im.astype(f32)
    bb_re = f_re[..., None] * br - f_im[..., None] * bi
    bb_im = f_re[..., None] * bi + f_im[..., None] * br
    bu_re = jnp.einsum('blgc,gnc->blgn', ug, bb_re)
    bu_im = jnp.einsum('blgc,gnc->blgn', ug, bb_im)
    a_re = jnp.broadcast_to(ab_re, bu_re.shape)
    a_im = jnp.broadcast_to(ab_im, bu_im.shape)

    def combine(e1, e2):
        a1r, a1i, b1r, b1i = e1
        a2r, a2i, b2r, b2i = e2
        return (a2r * a1r - a2i * a1i,
                a2r * a1i + a2i * a1r,
                a2r * b1r - a2i * b1i + b2r,
                a2r * b1i + a2i * b1r + b2i)

    _, _, s_re, s_im = lax.associative_scan(combine, (a_re, a_im, bu_re, bu_im), axis=1)
    y = (jnp.einsum('blgn,gcn->blgc', s_re, c_re.astype(f32))
         - jnp.einsum('blgn,gcn->blgc', s_im, c_im.astype(f32))
         + d_skip.astype(f32).reshape(SSM_GROUPS, SSM_GROUP_DIM) * ug)
    y = jax.nn.gelu(y.reshape(bsz, seq, SSM_WIDTH))
    return y * jax.nn.sigmoid(y @ w_glu.astype(f32))


def _swiglu(h, wg, wu, wd):
    return (jax.nn.silu(h @ wg) * (h @ wu)) @ wd


def _moe_swiglu(x2d, w_router, wg, wu, wd):
    n_tok = x2d.shape[0]
    logits = (x2d @ w_router).astype(jnp.float32)
    top_v, top_i = lax.top_k(logits, TOP_K)
    gates = jax.nn.softmax(top_v, axis=-1).astype(x2d.dtype)
    n_assign = n_tok * TOP_K
    flat_e = top_i.reshape(-1)
    flat_tok = jnp.repeat(jnp.arange(n_tok, dtype=jnp.int32), TOP_K)
    flat_g = gates.reshape(-1)
    order = jnp.argsort(flat_e)
    se = flat_e[order]
    counts = jnp.bincount(flat_e, length=N_EXPERTS)
    start = jnp.cumsum(counts) - counts
    padded = ((counts + MOE_BLOCK - 1) // MOE_BLOCK) * MOE_BLOCK
    pend = jnp.cumsum(padded)
    pstart = pend - padded
    dest = pstart[se] + jnp.arange(n_assign) - start[se]
    nblk = -(-n_assign // MOE_BLOCK) + N_EXPERTS
    cap = nblk * MOE_BLOCK
    buf_tok = jnp.zeros((cap,), jnp.int32).at[dest].set(flat_tok[order])
    buf_gate = jnp.zeros((cap,), x2d.dtype).at[dest].set(flat_g[order])
    block_e = jnp.clip(jnp.searchsorted(pend, jnp.arange(nblk) * MOE_BLOCK, side='right'),
                       0, N_EXPERTS - 1)

    def run_block(args):
        tok, g, e = args
        xb = x2d[tok]
        return _swiglu(xb, wg[e], wu[e], wd[e]) * g[:, None]

    y = lax.map(run_block, (buf_tok.reshape(nblk, MOE_BLOCK),
                            buf_gate.reshape(nblk, MOE_BLOCK), block_e))
    return jnp.zeros_like(x2d).at[buf_tok].add(y.reshape(cap, D_MODEL))


def setup_inputs(seed: int = 0) -> dict:
    key = jax.random.key(seed)
    ks = iter(jax.random.split(key, 32))
    nrm = lambda shape, scale: jax.random.normal(next(ks), shape, jnp.float32) * scale
    gain = lambda shape: 1.0 + nrm(shape, 0.02)
    n_idx = jnp.arange(SSM_STATE, dtype=jnp.float32)[None, None, :]
    return {
        "x": nrm((BATCH, SEQ, D_MODEL), 1.0),
        "attn_norm_g": gain((DEPTH, D_MODEL)),
        "w_in": nrm((DEPTH, D_MODEL, IN_WIDTH), D_MODEL ** -0.5),
        "q_norm_g": gain((DEPTH, HEAD_DIM)),
        "k_norm_g": gain((DEPTH, HEAD_DIM)),
        "sinks": nrm((DEPTH, N_HEADS), 0.5),
        "lam_re": -0.5 + nrm((DEPTH, SSM_GROUPS, SSM_STATE), 0.01),
        "lam_im": math.pi * n_idx + nrm((DEPTH, SSM_GROUPS, SSM_STATE), 0.01),
        "log_dt": jax.random.uniform(next(ks), (DEPTH, SSM_GROUPS), jnp.float32,
                                     math.log(DT_MIN), math.log(DT_MAX)),
        "b_re": nrm((DEPTH, SSM_GROUPS, SSM_STATE, SSM_GROUP_DIM), (2 * SSM_GROUP_DIM) ** -0.5),
        "b_im": nrm((DEPTH, SSM_GROUPS, SSM_STATE, SSM_GROUP_DIM), (2 * SSM_GROUP_DIM) ** -0.5),
        "c_re": nrm((DEPTH, SSM_GROUPS, SSM_GROUP_DIM, SSM_STATE), (2 * SSM_STATE) ** -0.5),
        "c_im": nrm((DEPTH, SSM_GROUPS, SSM_GROUP_DIM, SSM_STATE), (2 * SSM_STATE) ** -0.5),
        "d_skip": nrm((DEPTH, SSM_WIDTH), 1.0),
        "w_glu": nrm((DEPTH, SSM_WIDTH, SSM_WIDTH), SSM_WIDTH ** -0.5),
        "attn_out_g": gain((DEPTH, ATTN_WIDTH)),
        "ssm_out_g": gain((DEPTH, SSM_WIDTH)),
        "w_o": nrm((DEPTH, D_MODEL, D_MODEL), D_MODEL ** -0.5),
        "ffn_norm_g": gain((DEPTH, D_MODEL)),
        "dense_wg": nrm((N_DENSE, D_MODEL, D_FF), D_MODEL ** -0.5),
        "dense_wu": nrm((N_DENSE, D_MODEL, D_FF), D_MODEL ** -0.5),
        "dense_wd": nrm((N_DENSE, D_FF, D_MODEL), D_FF ** -0.5),
        "router_w": nrm((N_MOE, D_MODEL, N_EXPERTS), D_MODEL ** -0.5),
        "moe_wg": nrm((N_MOE, N_EXPERTS, D_MODEL, D_FF), D_MODEL ** -0.5),
        "moe_wu": nrm((N_MOE, N_EXPERTS, D_MODEL, D_FF), D_MODEL ** -0.5),
        "moe_wd": nrm((N_MOE, N_EXPERTS, D_FF, D_MODEL), D_FF ** -0.5),
    }


def reference(x, attn_norm_g, w_in, q_norm_g, k_norm_g, sinks, lam_re, lam_im, log_dt,
              b_re, b_im, c_re, c_im, d_skip, w_glu, attn_out_g, ssm_out_g, w_o,
              ffn_norm_g, dense_wg, dense_wu, dense_wd, router_w, moe_wg, moe_wu, moe_wd):
    bsz, seq = x.shape[0], x.shape[1]
    for l in range(DEPTH):
        hn = _rms(x, attn_norm_g[l]).astype(x.dtype)
        proj = hn @ w_in[l]
        q = proj[..., :ATTN_WIDTH].reshape(bsz, seq, N_HEADS, HEAD_DIM)
        k = proj[..., ATTN_WIDTH:ATTN_WIDTH + KV_WIDTH].reshape(bsz, seq, N_KV_HEADS, HEAD_DIM)
        v = proj[..., ATTN_WIDTH + KV_WIDTH:ATTN_WIDTH + 2 * KV_WIDTH].reshape(
            bsz, seq, N_KV_HEADS, HEAD_DIM)
        u = proj[..., ATTN_WIDTH + 2 * KV_WIDTH:]
        q = _rms(q, q_norm_g[l])
        k = _rms(k, k_norm_g[l])
        attn = _swa_sink_attention(q, k, v.astype(jnp.float32), sinks[l])
        ssm = _s5_mixer(u, lam_re[l], lam_im[l], log_dt[l], b_re[l], b_im[l],
                        c_re[l], c_im[l], d_skip[l], w_glu[l])
        mix = jnp.concatenate([_rms(attn, attn_out_g[l]), _rms(ssm, ssm_out_g[l])],
                              axis=-1).astype(x.dtype)
        x = x + mix @ w_o[l]
        hn = _rms(x, ffn_norm_g[l]).astype(x.dtype)
        if l % 2 == 0:
            i = l // 2
            x = x + _swiglu(hn, dense_wg[i], dense_wu[i], dense_wd[i])
        else:
            i = l // 2
            y = _moe_swiglu(hn.reshape(bsz * seq, D_MODEL), router_w[i],
                            moe_wg[i], moe_wu[i], moe_wd[i])
            x = x + y.reshape(bsz, seq, D_MODEL)
    return x
```

```python
import functools
import math

import jax
import jax.numpy as jnp
from jax import lax
from jax.experimental import pallas as pl
from jax.experimental.pallas import tpu as pltpu

F32 = jnp.float32
BF16 = jnp.bfloat16

D_MODEL = 1024
HEAD_DIM = 64
N_HEADS = 8
N_KV_HEADS = 2
GQA = N_HEADS // N_KV_HEADS
ATTN_WIDTH = N_HEADS * HEAD_DIM
KV_WIDTH = N_KV_HEADS * HEAD_DIM
ATTN_BLOCK = 128
SSM_WIDTH = D_MODEL - ATTN_WIDTH
SSM_GROUP_DIM = 16
SSM_GROUPS = SSM_WIDTH // SSM_GROUP_DIM
SSM_STATE = 64
IN_WIDTH = ATTN_WIDTH + 2 * KV_WIDTH + SSM_WIDTH
D_FF = 3584
N_EXPERTS = 8
TOP_K = 2
NORM_EPS = 1e-6
NEG_INF = -1e30

ROW_TILE = 512
SSM_CHUNK = 16
SSM_SEQS = 8
FF_CHUNK = 512
MOE_ROWS = 512
TOK_WIN = 256
VMEM_LIMIT = 56 * 1024 * 1024


def _cparams(sem, vmem=VMEM_LIMIT):
    return pltpu.CompilerParams(dimension_semantics=sem, vmem_limit_bytes=vmem)


def _const_spec(shape):
    n = len(shape)
    return pl.BlockSpec(shape, lambda *_: (0,) * n)


def _resident_spec(shape):
    n = len(shape)
    return pl.BlockSpec(shape, lambda *_: (0,) * n, pipeline_mode=pl.Buffered(1))


def _inproj_body(x_ref, g_ref, w_ref, qg_ref, kg_ref, avg_ref, q_out, k_out, v_out, u_out):
    x = x_ref[...]
    ms = jnp.mean(x * x, axis=-1, keepdims=True)
    hn = (x * lax.rsqrt(ms + NORM_EPS) * g_ref[...]).astype(BF16)
    proj = jnp.dot(hn, w_ref[...], preferred_element_type=F32)
    q = proj[:, :ATTN_WIDTH]
    qms = jnp.dot((q * q).astype(BF16), avg_ref[...], preferred_element_type=F32)
    q_out[...] = (q * lax.rsqrt(qms + NORM_EPS) * qg_ref[...]).astype(BF16)
    k = proj[:, ATTN_WIDTH:ATTN_WIDTH + KV_WIDTH]
    kms = jnp.dot((k * k).astype(BF16), avg_ref[:KV_WIDTH, :KV_WIDTH], preferred_element_type=F32)
    k_out[...] = (k * lax.rsqrt(kms + NORM_EPS) * kg_ref[...]).astype(BF16)
    v_out[...] = proj[:, ATTN_WIDTH + KV_WIDTH:ATTN_WIDTH + 2 * KV_WIDTH].astype(BF16)
    u_out[...] = proj[:, ATTN_WIDTH + 2 * KV_WIDTH:].astype(BF16)


def _inproj(x2d, g, w_bf, qg, kg, avg):
    t = x2d.shape[0]
    row = lambda w: pl.BlockSpec((ROW_TILE, w), lambda i: (i, 0))
    return pl.pallas_call(
        _inproj_body,
        grid=(t // ROW_TILE,),
        in_specs=[row(D_MODEL), _const_spec((1, D_MODEL)), _const_spec((D_MODEL, IN_WIDTH)),
                  _const_spec((1, ATTN_WIDTH)), _const_spec((1, KV_WIDTH)),
                  _const_spec((ATTN_WIDTH, ATTN_WIDTH))],
        out_specs=[row(ATTN_WIDTH), row(KV_WIDTH), row(KV_WIDTH), row(SSM_WIDTH)],
        out_shape=[jax.ShapeDtypeStruct((t, ATTN_WIDTH), BF16), jax.ShapeDtypeStruct((t, KV_WIDTH), BF16),
                   jax.ShapeDtypeStruct((t, KV_WIDTH), BF16), jax.ShapeDtypeStruct((t, SSM_WIDTH), BF16)],
        compiler_params=_cparams(("parallel",)),
        name="inproj",
    )(x2d, g, w_bf, qg, kg, avg)


def _attn_body(sink_ref, q_ref, k_ref, v_ref, g_ref, o_ref, *, blocks_per_seq):
    i = pl.program_id(0)
    rows = lax.broadcasted_iota(jnp.int32, (ATTN_BLOCK, ATTN_BLOCK), 0)
    cols = lax.broadcasted_iota(jnp.int32, (ATTN_BLOCK, ATTN_BLOCK), 1)
    cur_ok = cols <= rows
    prev_ok = cols > rows
    contract_last = (((1,), (1,)), ((), ()))
    for blk in range(ROW_TILE // ATTN_BLOCK):
        gblk = i * (ROW_TILE // ATTN_BLOCK) + blk
        row0 = pl.multiple_of(gblk * ATTN_BLOCK, ATTN_BLOCK)
        prev0 = pl.multiple_of(jnp.maximum(gblk - 1, 0) * ATTN_BLOCK, ATTN_BLOCK)
        has_prev = (gblk % blocks_per_seq) != 0
        kc = k_ref[pl.ds(row0, ATTN_BLOCK), :]
        kp = k_ref[pl.ds(prev0, ATTN_BLOCK), :]
        vc = v_ref[pl.ds(row0, ATTN_BLOCK), :]
        vp = v_ref[pl.ds(prev0, ATTN_BLOCK), :]
        qb = q_ref[blk * ATTN_BLOCK:(blk + 1) * ATTN_BLOCK, :]
        pmask = jnp.logical_and(prev_ok, has_prev)
        outs = []
        for h in range(N_HEADS):
            kv = h // GQA
            sl = slice(kv * HEAD_DIM, (kv + 1) * HEAD_DIM)
            qh = qb[:, h * HEAD_DIM:(h + 1) * HEAD_DIM]
            sc = lax.dot_general(qh, kc[:, sl], contract_last, preferred_element_type=F32)
            sp = lax.dot_general(qh, kp[:, sl], contract_last, preferred_element_type=F32)
            sc = jnp.where(cur_ok, sc, NEG_INF)
            sp = jnp.where(pmask, sp, NEG_INF)
            sink = sink_ref[h]
            m = jnp.maximum(jnp.maximum(jnp.max(sc, axis=-1, keepdims=True),
                                        jnp.max(sp, axis=-1, keepdims=True)), sink)
            pc = jnp.exp(sc - m)
            pp = jnp.exp(sp - m)
            den = (jnp.sum(pc, axis=-1, keepdims=True) + jnp.sum(pp, axis=-1, keepdims=True)
                   + jnp.exp(sink - m))
            o = (jnp.dot(pc.astype(BF16), vc[:, sl], preferred_element_type=F32)
                 + jnp.dot(pp.astype(BF16), vp[:, sl], preferred_element_type=F32))
            outs.append(o / den)
        a = jnp.concatenate(outs, axis=-1)
        ms = jnp.mean(a * a, axis=-1, keepdims=True)
        o_ref[blk * ATTN_BLOCK:(blk + 1) * ATTN_BLOCK, :] = (
            a * lax.rsqrt(ms + NORM_EPS) * g_ref[...]).astype(BF16)


def _attention(q, k, v, sinks, g, seq):
    t = q.shape[0]
    return pl.pallas_call(
        functools.partial(_attn_body, blocks_per_seq=seq // ATTN_BLOCK),
        grid=(t // ROW_TILE,),
        in_specs=[pl.BlockSpec(memory_space=pltpu.SMEM),
                  pl.BlockSpec((ROW_TILE, ATTN_WIDTH), lambda i: (i, 0)),
                  _const_spec((t, KV_WIDTH)), _const_spec((t, KV_WIDTH)),
                  _const_spec((1, ATTN_WIDTH))],
        out_specs=pl.BlockSpec((ROW_TILE, ATTN_WIDTH), lambda i: (i, 0)),
        out_shape=jax.ShapeDtypeStruct((t, ATTN_WIDTH), BF16),
        compiler_params=_cparams(("parallel",)),
        name="swa_attention",
    )(sinks, q, k, v, g)


def _ssm_tables(lam_re, lam_im, log_dt, b_re, b_im, c_re, c_im, d_skip, n_steps):
    hi = lax.Precision.HIGHEST
    g_, n_, c_ = SSM_GROUPS, SSM_STATE, SSM_GROUP_DIM
    lr, li = lam_re.astype(F32), lam_im.astype(F32)
    dt = jnp.exp(log_dt.astype(F32))[:, None]
    mag = jnp.exp(lr * dt)
    ang = li * dt
    ab_re, ab_im = mag * jnp.cos(ang), mag * jnp.sin(ang)
    nr = ab_re - 1.0
    den = lr * lr + li * li
    f_re = (nr * lr + ab_im * li) / den
    f_im = (ab_im * lr - nr * li) / den
    br, bi = b_re.astype(F32), b_im.astype(F32)
    bb_re = f_re[..., None] * br - f_im[..., None] * bi
    bb_im = f_re[..., None] * bi + f_im[..., None] * br
    cr, ci = c_re.astype(F32), c_im.astype(F32)

    def apow(p):
        p = p.astype(F32)[None, :, None]
        m = jnp.exp(lr[:, None, :] * dt[:, :, None] * p)
        an = li[:, None, :] * dt[:, :, None] * p
        return m * jnp.cos(an), m * jnp.sin(an)

    ch = SSM_CHUNK
    pr, pi = apow(jnp.arange(ch + 1))
    ca_re = cr[:, None] * pr[:, :ch, None, :] - ci[:, None] * pi[:, :ch, None, :]
    ca_im = cr[:, None] * pi[:, :ch, None, :] + ci[:, None] * pr[:, :ch, None, :]
    kk = (jnp.einsum('gtcn,gnd->gtcd', ca_re, bb_re, precision=hi)
          - jnp.einsum('gtcn,gnd->gtcd', ca_im, bb_im, precision=hi))
    tt = jnp.arange(ch)[:, None]
    jj = jnp.arange(ch)[None, :]
    tau = tt - jj
    m5 = jnp.where((tau >= 0)[None, :, :, None, None], kk[:, jnp.clip(tau, 0, ch - 1)], 0.0)
    mt = m5.transpose(0, 2, 4, 1, 3).reshape(g_, ch * c_, ch * c_)
    rr, ri = pr[:, ch - 1::-1][:, :ch], pi[:, ch - 1::-1][:, :ch]
    w_re = rr[:, :, None, :] * bb_re.transpose(0, 2, 1)[:, None] - ri[:, :, None, :] * bb_im.transpose(0, 2, 1)[:, None]
    w_im = rr[:, :, None, :] * bb_im.transpose(0, 2, 1)[:, None] + ri[:, :, None, :] * bb_re.transpose(0, 2, 1)[:, None]
    w_re = w_re.reshape(g_, ch * c_, n_)
    w_im = w_im.reshape(g_, ch * c_, n_)
    v_re = (cr[:, None] * pr[:, 1:, None, :] - ci[:, None] * pi[:, 1:, None, :])
    v_im = -(cr[:, None] * pi[:, 1:, None, :] + ci[:, None] * pr[:, 1:, None, :])
    v_re = v_re.transpose(0, 3, 1, 2).reshape(g_, n_, ch * c_)
    v_im = v_im.transpose(0, 3, 1, 2).reshape(g_, n_, ch * c_)
    np_ = g_ // 2
    z = jnp.zeros_like(w_re[0::2])
    wbd_re = jnp.concatenate([jnp.concatenate([w_re[0::2], z], axis=2),
                              jnp.concatenate([z, w_re[1::2]], axis=2)], axis=1)
    wbd_im = jnp.concatenate([jnp.concatenate([w_im[0::2], z], axis=2),
                              jnp.concatenate([z, w_im[1::2]], axis=2)], axis=1)
    zv = jnp.zeros_like(v_re)
    even = (jnp.arange(g_) % 2 == 0)[:, None, None]
    vpad_re = jnp.where(even, jnp.concatenate([v_re, zv], axis=1), jnp.concatenate([zv, v_re], axis=1))
    vpad_im = jnp.where(even, jnp.concatenate([v_im, zv], axis=1), jnp.concatenate([zv, v_im], axis=1))
    pair = lambda x: x.reshape(np_, 2, *x.shape[1:]).swapaxes(1, 2).reshape(np_, x.shape[1], 2 * n_)
    qr, qi = apow(jnp.arange(n_steps) * ch)
    sr, si = apow(jnp.array([ch, ch * n_steps]))
    consts = jnp.stack([pair(sr), pair(si)], axis=1)
    apw = jnp.stack([pair(qr), pair(qi)], axis=1)
    dtile = jnp.tile(d_skip.astype(F32).reshape(g_, 1, c_), (1, ch, 1)).reshape(g_, 1, ch * c_)
    return (mt.astype(BF16), wbd_re.astype(BF16), wbd_im.astype(BF16),
            vpad_re.astype(BF16), vpad_im.astype(BF16), consts, apw, dtile)


def _ssm_body(u_ref, mt_ref, wre_ref, wim_ref, vre_ref, vim_ref, c_ref, apw_ref, d_ref, y_ref,
              pre, pim, sre, sim, *, n_steps, halves):
    lanes = 2 * SSM_STATE
    u0 = u_ref[0]
    u1 = u_ref[1]
    ucat = jnp.concatenate([u0, u1], axis=1)
    pre[...] = jnp.dot(ucat, wre_ref[0], preferred_element_type=F32)
    pim[...] = jnp.dot(ucat, wim_ref[0], preferred_element_type=F32)
    ar = jnp.broadcast_to(c_ref[0, 0, 0:1, :], (SSM_SEQS, lanes))
    ai = jnp.broadcast_to(c_ref[0, 1, 0:1, :], (SSM_SEQS, lanes))

    def step(i, carry):
        s_re, s_im = carry
        r = pl.multiple_of(i * SSM_SEQS, SSM_SEQS)
        sre[pl.ds(r, SSM_SEQS), :] = s_re
        sim[pl.ds(r, SSM_SEQS), :] = s_im
        p_r = pre[pl.ds(r, SSM_SEQS), :]
        p_i = pim[pl.ds(r, SSM_SEQS), :]
        return ar * s_re - ai * s_im + p_r, ar * s_im + ai * s_re + p_i

    zero = jnp.zeros((SSM_SEQS, lanes), F32)
    e_re, e_im = lax.fori_loop(0, n_steps, step, (zero, zero))

    row = lax.broadcasted_iota(jnp.int32, (SSM_SEQS, lanes), 0)
    seq_start = (row % halves) == 0
    br = jnp.broadcast_to(c_ref[0, 0, 1:2, :], (SSM_SEQS, lanes))
    bi = jnp.broadcast_to(c_ref[0, 1, 1:2, :], (SSM_SEQS, lanes))
    shift = lambda x: jnp.where(seq_start, 0.0, pltpu.roll(x, 1, axis=0))
    c_re = jnp.zeros_like(e_re)
    c_im = jnp.zeros_like(e_im)
    for _ in range(halves - 1):
        n_re = e_re + br * c_re - bi * c_im
        n_im = e_im + br * c_im + bi * c_re
        c_re, c_im = shift(n_re), shift(n_im)

    apr = apw_ref[0, 0][:, None, :]
    api = apw_ref[0, 1][:, None, :]
    s3r = sre[...].reshape(n_steps, SSM_SEQS, lanes)
    s3i = sim[...].reshape(n_steps, SSM_SEQS, lanes)
    t_re = (s3r + apr * c_re[None] - api * c_im[None]).reshape(n_steps * SSM_SEQS, lanes)
    t_im = (s3i + apr * c_im[None] + api * c_re[None]).reshape(n_steps * SSM_SEQS, lanes)
    tr_hi = t_re.astype(BF16)
    tr_lo = (t_re - tr_hi.astype(F32)).astype(BF16)
    ti_hi = t_im.astype(BF16)
    ti_lo = (t_im - ti_hi.astype(F32)).astype(BF16)
    for gi, ug in enumerate((u0, u1)):
        y = jnp.dot(ug, mt_ref[gi], preferred_element_type=F32)
        y += jnp.dot(tr_hi, vre_ref[gi], preferred_element_type=F32)
        y += jnp.dot(tr_lo, vre_ref[gi], preferred_element_type=F32)
        y += jnp.dot(ti_hi, vim_ref[gi], preferred_element_type=F32)
        y += jnp.dot(ti_lo, vim_ref[gi], preferred_element_type=F32)
        y_ref[gi] = y + d_ref[gi] * ug.astype(F32)


def _ssm(u2d, tables, bsz, seq):
    mt, wre, wim, vre, vim, consts, apw, dtile = tables
    t = u2d.shape[0]
    halves = SSM_SEQS // bsz
    n_steps = seq // (SSM_CHUNK * halves)
    n_rows = n_steps * SSM_SEQS
    vec = SSM_CHUNK * SSM_GROUP_DIM
    ug = (u2d.reshape(bsz, halves, n_steps, SSM_CHUNK, SSM_GROUPS, SSM_GROUP_DIM)
          .transpose(4, 2, 0, 1, 3, 5).reshape(SSM_GROUPS, n_rows, vec))
    npairs = SSM_GROUPS // 2
    lanes = 2 * SSM_STATE
    three = lambda a, b: pl.BlockSpec((2, a, b), lambda p: (p, 0, 0))
    yg = pl.pallas_call(
        functools.partial(_ssm_body, n_steps=n_steps, halves=halves),
        grid=(npairs,),
        in_specs=[three(n_rows, vec), three(vec, vec),
                  pl.BlockSpec((1, 2 * vec, lanes), lambda p: (p, 0, 0)),
                  pl.BlockSpec((1, 2 * vec, lanes), lambda p: (p, 0, 0)),
                  three(lanes, vec), three(lanes, vec),
                  pl.BlockSpec((1, 2, 2, lanes), lambda p: (p, 0, 0, 0)),
                  pl.BlockSpec((1, 2, n_steps, lanes), lambda p: (p, 0, 0, 0)),
                  three(1, vec)],
        out_specs=three(n_rows, vec),
        out_shape=jax.ShapeDtypeStruct((SSM_GROUPS, n_rows, vec), F32),
        scratch_shapes=[pltpu.VMEM((n_rows, lanes), F32)] * 4,
        compiler_params=_cparams(("parallel",)),
        name="s5_ssm",
    )(ug, mt, wre, wim, vre, vim, consts, apw, dtile)
    return (yg.reshape(SSM_GROUPS, n_steps, bsz, halves, SSM_CHUNK, SSM_GROUP_DIM)
            .transpose(2, 3, 1, 4, 0, 5).reshape(t, SSM_WIDTH))


def _mix_body(x_ref, a_ref, y_ref, wglu_ref, sg_ref, wo_ref, fg_ref, rw_hi_ref, rw_lo_ref,
              x1_out, hn_out, lg_out):
    y = jax.nn.gelu(y_ref[...])
    z = y * jax.nn.sigmoid(jnp.dot(y.astype(BF16), wglu_ref[...], preferred_element_type=F32))
    ms = jnp.mean(z * z, axis=-1, keepdims=True)
    sn = (z * lax.rsqrt(ms + NORM_EPS) * sg_ref[...]).astype(BF16)
    x1 = (x_ref[...]
          + jnp.dot(a_ref[...], wo_ref[:ATTN_WIDTH, :], preferred_element_type=F32)
          + jnp.dot(sn, wo_ref[ATTN_WIDTH:, :], preferred_element_type=F32))
    x1_out[...] = x1
    ms1 = jnp.mean(x1 * x1, axis=-1, keepdims=True)
    hn = x1 * lax.rsqrt(ms1 + NORM_EPS) * fg_ref[...]
    hn_hi = hn.astype(BF16)
    hn_out[...] = hn_hi
    hn_lo = (hn - hn_hi.astype(F32)).astype(BF16)
    lg_out[...] = (jnp.dot(hn_hi, rw_hi_ref[...], preferred_element_type=F32)
                   + jnp.dot(hn_hi, rw_lo_ref[...], preferred_element_type=F32)
                   + jnp.dot(hn_lo, rw_hi_ref[...], preferred_element_type=F32))


def _mix(x2d, attn, y, wglu, sg, wo, fg, rw_hi, rw_lo):
    t = x2d.shape[0]
    ne = rw_hi.shape[1]
    row = lambda w: pl.BlockSpec((ROW_TILE, w), lambda i: (i, 0))
    return pl.pallas_call(
        _mix_body,
        grid=(t // ROW_TILE,),
        in_specs=[row(D_MODEL), row(ATTN_WIDTH), row(SSM_WIDTH),
                  _const_spec((SSM_WIDTH, SSM_WIDTH)), _const_spec((1, SSM_WIDTH)),
                  _const_spec((D_MODEL, D_MODEL)), _const_spec((1, D_MODEL)),
                  _const_spec((D_MODEL, ne)), _const_spec((D_MODEL, ne))],
        out_specs=[row(D_MODEL), row(D_MODEL), row(ne)],
        out_shape=[jax.ShapeDtypeStruct((t, D_MODEL), F32), jax.ShapeDtypeStruct((t, D_MODEL), BF16),
                   jax.ShapeDtypeStruct((t, ne), F32)],
        compiler_params=_cparams(("parallel",)),
        name="mix_outproj",
    )(x2d, attn, y, wglu, sg, wo, fg, rw_hi, rw_lo)


def _ffn_body(x_ref, h_ref, wg_ref, wu_ref, wd_ref, o_ref, acc):
    h = h_ref[...]
    acc[...] = x_ref[...]
    for c in range(D_FF // FF_CHUNK):
        sl = slice(c * FF_CHUNK, (c + 1) * FF_CHUNK)
        g = jnp.dot(h, wg_ref[:, sl], preferred_element_type=F32)
        u = jnp.dot(h, wu_ref[:, sl], preferred_element_type=F32)
        a = (jax.nn.silu(g) * u).astype(BF16)
        acc[...] += jnp.dot(a, wd_ref[sl, :], preferred_element_type=F32)
    o_ref[...] = acc[...]


def _dense_ffn(x1, hn, wg, wu, wd):
    t = x1.shape[0]
    row = lambda w: pl.BlockSpec((ROW_TILE, w), lambda i: (i, 0))
    return pl.pallas_call(
        _ffn_body,
        grid=(t // ROW_TILE,),
        in_specs=[row(D_MODEL), row(D_MODEL), _resident_spec((D_MODEL, D_FF)),
                  _resident_spec((D_MODEL, D_FF)), _resident_spec((D_FF, D_MODEL))],
        out_specs=row(D_MODEL),
        out_shape=jax.ShapeDtypeStruct((t, D_MODEL), F32),
        scratch_shapes=[pltpu.VMEM((ROW_TILE, D_MODEL), F32)],
        compiler_params=_cparams(("parallel",)),
        name="dense_ffn",
    )(x1, hn, wg, wu, wd)


def _dispatch_body(jlo_ref, jhi_ref, tok_ref, h_ref, o_ref, acc):
    b = pl.program_id(0)
    acc[...] = jnp.zeros_like(acc)
    tok = tok_ref[0]
    lane = lax.broadcasted_iota(jnp.int32, (1, TOK_WIN), 1)

    def window(j, carry):
        base = pl.multiple_of(j * TOK_WIN, TOK_WIN)
        onehot = jnp.where(tok - base == lane, 1.0, 0.0).astype(BF16)
        acc[...] += jnp.dot(onehot, h_ref[pl.ds(base, TOK_WIN), :], preferred_element_type=F32)
        return carry

    lax.fori_loop(jlo_ref[b], jhi_ref[b] + 1, window, 0)
    o_ref[...] = acc[...].astype(BF16)


def _dispatch(hn, buf_tok, jlo, jhi, nblk):
    t = hn.shape[0]
    return pl.pallas_call(
        _dispatch_body,
        grid_spec=pltpu.PrefetchScalarGridSpec(
            num_scalar_prefetch=2, grid=(nblk,),
            in_specs=[pl.BlockSpec((1, MOE_ROWS, 1), lambda b, *_: (b, 0, 0)),
                      pl.BlockSpec((t, D_MODEL), lambda b, *_: (0, 0), pipeline_mode=pl.Buffered(1))],
            out_specs=pl.BlockSpec((MOE_ROWS, D_MODEL), lambda b, *_: (b, 0)),
            scratch_shapes=[pltpu.VMEM((MOE_ROWS, D_MODEL), F32)]),
        out_shape=jax.ShapeDtypeStruct((nblk * MOE_ROWS, D_MODEL), BF16),
        compiler_params=_cparams(("arbitrary",)),
        name="moe_dispatch",
    )(jlo, jhi, buf_tok.reshape(nblk, MOE_ROWS, 1), hn)


def _expert_ffn_body(e_ref, x_ref, gate_ref, wg_ref, wu_ref, wd_ref, o_ref, acc):
    c = pl.program_id(1)

    @pl.when(c == 0)
    def _():
        acc[...] = jnp.zeros_like(acc)

    x = x_ref[...]
    g = jnp.dot(x, wg_ref[0], preferred_element_type=F32)
    u = jnp.dot(x, wu_ref[0], preferred_element_type=F32)
    a = (jax.nn.silu(g) * u).astype(BF16)
    acc[...] += jnp.dot(a, wd_ref[0], preferred_element_type=F32)

    @pl.when(c == pl.num_programs(1) - 1)
    def _():
        o_ref[...] = (acc[...] * gate_ref[0]).astype(BF16)


def _expert_ffn(xs, buf_gate, block_e, wg, wu, wd, nblk):
    return pl.pallas_call(
        _expert_ffn_body,
        grid_spec=pltpu.PrefetchScalarGridSpec(
            num_scalar_prefetch=1, grid=(nblk, D_FF // FF_CHUNK),
            in_specs=[pl.BlockSpec((MOE_ROWS, D_MODEL), lambda b, c, e: (b, 0)),
                      pl.BlockSpec((1, MOE_ROWS, 1), lambda b, c, e: (b, 0, 0)),
                      pl.BlockSpec((1, D_MODEL, FF_CHUNK), lambda b, c, e: (e[b], 0, c)),
                      pl.BlockSpec((1, D_MODEL, FF_CHUNK), lambda b, c, e: (e[b], 0, c)),
                      pl.BlockSpec((1, FF_CHUNK, D_MODEL), lambda b, c, e: (e[b], c, 0))],
            out_specs=pl.BlockSpec((MOE_ROWS, D_MODEL), lambda b, c, e: (b, 0)),
            scratch_shapes=[pltpu.VMEM((MOE_ROWS, D_MODEL), F32)]),
        out_shape=jax.ShapeDtypeStruct((nblk * MOE_ROWS, D_MODEL), BF16),
        compiler_params=_cparams(("arbitrary", "arbitrary")),
        name="moe_expert_ffn",
    )(block_e, xs, buf_gate.reshape(nblk, MOE_ROWS, 1), wg, wu, wd)


def _combine_body(blk_ref, ok_ref, x_ref, pos_ref, y_ref, o_ref):
    j = pl.program_id(0)
    s = pl.program_id(1)

    @pl.when(s == 0)
    def _():
        o_ref[...] = x_ref[...]

    @pl.when(ok_ref[j, s] != 0)
    def _():
        base = blk_ref[j, s] * MOE_ROWS
        lane = lax.broadcasted_iota(jnp.int32, (1, MOE_ROWS), 1)
        p0 = pos_ref[:, 0:1] - base
        p1 = pos_ref[:, 1:2] - base
        onehot = jnp.where(jnp.logical_or(p0 == lane, p1 == lane), 1.0, 0.0).astype(BF16)
        o_ref[...] += jnp.dot(onehot, y_ref[...], preferred_element_type=F32)


def _combine(x1, pos2, ys, blk, ok):
    t = x1.shape[0]
    nslots = blk.shape[1]
    return pl.pallas_call(
        _combine_body,
        grid_spec=pltpu.PrefetchScalarGridSpec(
            num_scalar_prefetch=2, grid=(t // ROW_TILE, nslots),
            in_specs=[pl.BlockSpec((ROW_TILE, D_MODEL), lambda j, s, blk, ok: (j, 0)),
                      pl.BlockSpec((ROW_TILE, TOP_K), lambda j, s, blk, ok: (j, 0)),
                      pl.BlockSpec((MOE_ROWS, D_MODEL), lambda j, s, blk, ok: (blk[j, s], 0))],
            out_specs=pl.BlockSpec((ROW_TILE, D_MODEL), lambda j, s, blk, ok: (j, 0))),
        out_shape=jax.ShapeDtypeStruct((t, D_MODEL), F32),
        compiler_params=_cparams(("arbitrary", "arbitrary")),
        name="moe_combine",
    )(blk, ok, x1, pos2, ys)


def _moe(x1, hn, logits, wg, wu, wd):
    t = x1.shape[0]
    n_assign = t * TOP_K
    nblk = n_assign // MOE_ROWS + N_EXPERTS
    top_v, top_i = lax.top_k(logits, TOP_K)
    gates = jax.nn.softmax(top_v, axis=-1)
    flat_e = top_i.reshape(-1)
    flat_tok = jnp.repeat(jnp.arange(t, dtype=jnp.int32), TOP_K)
    onehot = (flat_e[:, None] == jnp.arange(N_EXPERTS)[None, :]).astype(jnp.int32)
    csum = jnp.cumsum(onehot, axis=0)
    rank = jnp.take_along_axis(csum, flat_e[:, None], axis=1)[:, 0] - 1
    counts = csum[-1]
    padded = ((counts + MOE_ROWS - 1) // MOE_ROWS) * MOE_ROWS
    pend = jnp.cumsum(padded)
    pstart = pend - padded
    pos = (pstart[flat_e] + rank).astype(jnp.int32)
    cap = nblk * MOE_ROWS
    buf_tok = jnp.full((cap,), -1, jnp.int32).at[pos].set(flat_tok)
    buf_gate = jnp.zeros((cap,), F32).at[pos].set(gates.reshape(-1))
    block_e = jnp.clip(jnp.searchsorted(pend, jnp.arange(nblk) * MOE_ROWS, side='right'),
                       0, N_EXPERTS - 1).astype(jnp.int32)
    bt = buf_tok.reshape(nblk, MOE_ROWS)
    tmax = jnp.max(bt, axis=1)
    tmin = jnp.min(jnp.where(bt < 0, t, bt), axis=1)
    jlo = jnp.where(tmax >= 0, tmin // TOK_WIN, 0).astype(jnp.int32)
    jhi = jnp.where(tmax >= 0, tmax // TOK_WIN, -1).astype(jnp.int32)

    xs = _dispatch(hn, buf_tok, jlo, jhi, nblk)
    ys = _expert_ffn(xs, buf_gate, block_e, wg, wu, wd, nblk)

    ntile = t // ROW_TILE
    before = jnp.concatenate([jnp.zeros((1, N_EXPERTS), jnp.int32),
                              csum[ROW_TILE * TOP_K - 1::ROW_TILE * TOP_K]], axis=0)
    lo = pstart[None, :] + before[:-1]
    hi = pstart[None, :] + before[1:] - 1
    has = hi >= lo
    b_lo = lo // MOE_ROWS
    b_hi = hi // MOE_ROWS
    blk = jnp.stack([b_lo, b_hi], axis=2).reshape(ntile, 2 * N_EXPERTS)
    ok = jnp.stack([has, has & (b_hi != b_lo)], axis=2).reshape(ntile, 2 * N_EXPERTS)
    idx = jnp.where(ok, jnp.arange(2 * N_EXPERTS)[None, :], -1)
    last = lax.cummax(idx, axis=1)
    blk = jnp.where(last >= 0, jnp.take_along_axis(blk, jnp.maximum(last, 0), axis=1), 0)
    return _combine(x1, pos.reshape(t, TOP_K), ys, blk.astype(jnp.int32), ok.astype(jnp.int32))


def kernel(x, attn_norm_g, w_in, q_norm_g, k_norm_g, sinks, lam_re, lam_im, log_dt, b_re, b_im,
           c_re, c_im, d_skip, w_glu, attn_out_g, ssm_out_g, w_o, ffn_norm_g, dense_wg, dense_wu,
           dense_wd, router_w, moe_wg, moe_wu, moe_wd):
    bsz, seq, _ = x.shape
    depth = w_in.shape[0]
    assert seq % (SSM_CHUNK * (SSM_SEQS // bsz)) == 0 and SSM_SEQS % bsz == 0
    assert (bsz * seq) % ROW_TILE == 0 and seq % ROW_TILE == 0
    x2d = x.reshape(bsz * seq, D_MODEL).astype(F32)
    head = jnp.arange(ATTN_WIDTH) // HEAD_DIM
    avg = jnp.where(head[:, None] == head[None, :], 1.0 / HEAD_DIM, 0.0).astype(BF16)
    n_steps = seq // (SSM_CHUNK * (SSM_SEQS // bsz))
    row = lambda v: v.astype(F32).reshape(1, -1)
    for l in range(depth):
        qg = jnp.tile(q_norm_g[l].astype(F32), N_HEADS) * (HEAD_DIM ** -0.5)
        kg = jnp.tile(k_norm_g[l].astype(F32), N_KV_HEADS)
        q, k, v, u = _inproj(x2d, row(attn_norm_g[l]), w_in[l].astype(BF16), row(qg), row(kg), avg)
        attn = _attention(q, k, v, sinks[l].astype(F32), row(attn_out_g[l]), seq)
        tables = _ssm_tables(lam_re[l], lam_im[l], log_dt[l], b_re[l], b_im[l], c_re[l], c_im[l],
                             d_skip[l], n_steps)
        y = _ssm(u, tables, bsz, seq)
        i = l // 2
        if l % 2 == 0:
            rw = jnp.zeros((D_MODEL, N_EXPERTS), F32)
        else:
            rw = router_w[i].astype(F32)
        rw_hi = rw.astype(BF16)
        rw_lo = (rw - rw_hi.astype(F32)).astype(BF16)
        x1, hn, logits = _mix(x2d, attn, y, w_glu[l].astype(BF16), row(ssm_out_g[l]),
                              w_o[l].astype(BF16), row(ffn_norm_g[l]), rw_hi, rw_lo)
        if l % 2 == 0:
            x2d = _dense_ffn(x1, hn, dense_wg[i].astype(BF16), dense_wu[i].astype(BF16),
                             dense_wd[i].astype(BF16))
        else:
            x2d = _moe(x1, hn, logits, moe_wg[i].astype(BF16), moe_wu[i].astype(BF16),
                       moe_wd[i].astype(BF16))
    return x2d.reshape(bsz, seq, D_MODEL)
```

```python
import functools
import math

import jax
import jax.numpy as jnp
from jax import lax
from jax.experimental import pallas as pl
from jax.experimental.pallas import tpu as pltpu

F32 = jnp.float32
BF16 = jnp.bfloat16

D_MODEL = 1024
HEAD_DIM = 64
N_HEADS = 8
N_KV_HEADS = 2
GQA = N_HEADS // N_KV_HEADS
ATTN_WIDTH = N_HEADS * HEAD_DIM
KV_WIDTH = N_KV_HEADS * HEAD_DIM
ATTN_BLOCK = 128
SSM_WIDTH = D_MODEL - ATTN_WIDTH
SSM_GROUP_DIM = 16
SSM_GROUPS = SSM_WIDTH // SSM_GROUP_DIM
SSM_STATE = 64
IN_WIDTH = ATTN_WIDTH + 2 * KV_WIDTH + SSM_WIDTH
D_FF = 3584
N_EXPERTS = 8
TOP_K = 2
NORM_EPS = 1e-6
NEG_INF = -1e30

LANES = 128
ROW_TILE = 512
SSM_CHUNK = 16
SSM_SEQS = 8
SSM_VEC = SSM_CHUNK * SSM_GROUP_DIM
SSM_LANE_GROUPS = LANES // SSM_GROUP_DIM
FF_CHUNK = 512
MOE_ROWS = 512
TOK_WIN = 256
VMEM_LIMIT = 56 * 1024 * 1024

NT_DIMS = (((1,), (1,)), ((), ()))


def _cparams(sem, vmem=VMEM_LIMIT):
    return pltpu.CompilerParams(dimension_semantics=sem, vmem_limit_bytes=vmem)


def _const_spec(shape):
    n = len(shape)
    return pl.BlockSpec(shape, lambda *_: (0,) * n)


def _resident_spec(shape):
    n = len(shape)
    return pl.BlockSpec(shape, lambda *_: (0,) * n, pipeline_mode=pl.Buffered(1))


def _inproj_body(x_ref, g_ref, wqt_ref, wk_ref, wvt_ref, wu_ref, qg_ref, kg_ref, avg_ref,
                 qt_out, k_out, v3_out, u3_out, uscr):
    x = x_ref[...]
    ms = jnp.mean(x * x, axis=-1, keepdims=True)
    hn = (x * lax.rsqrt(ms + NORM_EPS) * g_ref[...]).astype(BF16)
    qt = lax.dot_general(wqt_ref[...], hn, NT_DIMS, preferred_element_type=F32)
    qms = jnp.dot(avg_ref[...], (qt * qt).astype(BF16), preferred_element_type=F32)
    qt_out[...] = (qt * lax.rsqrt(qms + NORM_EPS) * qg_ref[...]).astype(BF16)
    k = jnp.dot(hn, wk_ref[...], preferred_element_type=F32)
    kms = jnp.dot((k * k).astype(BF16), avg_ref[:KV_WIDTH, :KV_WIDTH], preferred_element_type=F32)
    k_out[...] = (k * lax.rsqrt(kms + NORM_EPS) * kg_ref[...]).astype(BF16)
    vt = lax.dot_general(wvt_ref[...], hn, NT_DIMS, preferred_element_type=F32)
    for b in range(ROW_TILE // ATTN_BLOCK):
        v3_out[b] = vt[:, b * ATTN_BLOCK:(b + 1) * ATTN_BLOCK].astype(BF16)
    u = jnp.dot(hn, wu_ref[...], preferred_element_type=F32)
    for lb in range(SSM_WIDTH // LANES):
        uscr[lb] = u[:, lb * LANES:(lb + 1) * LANES]
    for t in range(SSM_CHUNK):
        for lb in range(SSM_WIDTH // LANES):
            u3_out[t, :, lb * LANES:(lb + 1) * LANES] = (
                uscr[lb, pl.ds(t, ROW_TILE // SSM_CHUNK, stride=SSM_CHUNK), :].astype(BF16))


def _inproj(x2d, g, wqt, wk, wvt, wu, qg, kg, avg):
    t = x2d.shape[0]
    cpt = ROW_TILE // SSM_CHUNK
    return pl.pallas_call(
        _inproj_body,
        grid=(t // ROW_TILE,),
        in_specs=[pl.BlockSpec((ROW_TILE, D_MODEL), lambda i: (i, 0)),
                  _const_spec((1, D_MODEL)), _const_spec((ATTN_WIDTH, D_MODEL)),
                  _const_spec((D_MODEL, KV_WIDTH)), _const_spec((KV_WIDTH, D_MODEL)),
                  _const_spec((D_MODEL, SSM_WIDTH)), _const_spec((ATTN_WIDTH, 1)),
                  _const_spec((1, KV_WIDTH)), _const_spec((ATTN_WIDTH, ATTN_WIDTH))],
        out_specs=[pl.BlockSpec((ATTN_WIDTH, ROW_TILE), lambda i: (0, i)),
                   pl.BlockSpec((ROW_TILE, KV_WIDTH), lambda i: (i, 0)),
                   pl.BlockSpec((ROW_TILE // ATTN_BLOCK, KV_WIDTH, ATTN_BLOCK), lambda i: (i, 0, 0)),
                   pl.BlockSpec((SSM_CHUNK, cpt, SSM_WIDTH), lambda i: (0, i, 0))],
        out_shape=[jax.ShapeDtypeStruct((ATTN_WIDTH, t), BF16),
                   jax.ShapeDtypeStruct((t, KV_WIDTH), BF16),
                   jax.ShapeDtypeStruct((t // ATTN_BLOCK, KV_WIDTH, ATTN_BLOCK), BF16),
                   jax.ShapeDtypeStruct((SSM_CHUNK, t // SSM_CHUNK, SSM_WIDTH), BF16)],
        scratch_shapes=[pltpu.VMEM((SSM_WIDTH // LANES, ROW_TILE, LANES), F32)],
        compiler_params=_cparams(("parallel",)),
        name="inproj",
    )(x2d, g, wqt, wk, wvt, wu, qg, kg, avg)


def _attn_body(sink_ref, qt_ref, k_ref, v3_ref, g_ref, eye_ref, o_ref, at_scr, *, blocks_per_seq):
    i = pl.program_id(0)
    key = lax.broadcasted_iota(jnp.int32, (ATTN_BLOCK, ATTN_BLOCK), 0)
    qry = lax.broadcasted_iota(jnp.int32, (ATTN_BLOCK, ATTN_BLOCK), 1)
    cur_ok = key <= qry
    prev_ok = key > qry
    zpad = jnp.zeros((HEAD_DIM, ATTN_BLOCK), BF16)
    nblk = ROW_TILE // ATTN_BLOCK
    for blk in range(nblk):
        gblk = i * nblk + blk
        pblk = jnp.maximum(gblk - 1, 0)
        row0 = pl.multiple_of(gblk * ATTN_BLOCK, ATTN_BLOCK)
        prev0 = pl.multiple_of(pblk * ATTN_BLOCK, ATTN_BLOCK)
        has_prev = (gblk % blocks_per_seq) != 0
        kc = k_ref[pl.ds(row0, ATTN_BLOCK), :]
        kp = k_ref[pl.ds(prev0, ATTN_BLOCK), :]
        vc = v3_ref[gblk]
        vp = v3_ref[pblk]
        qb = qt_ref[:, blk * ATTN_BLOCK:(blk + 1) * ATTN_BLOCK]
        pmask = jnp.logical_and(prev_ok, has_prev)
        outs = []
        for h in range(N_HEADS):
            kv = h // GQA
            qh = qb[h * HEAD_DIM:(h + 1) * HEAD_DIM, :]
            qpad = jnp.concatenate([qh, zpad] if kv == 0 else [zpad, qh], axis=0)
            sc = jnp.dot(kc, qpad, preferred_element_type=F32)
            sp = jnp.dot(kp, qpad, preferred_element_type=F32)
            sc = jnp.where(cur_ok, sc, NEG_INF)
            sp = jnp.where(pmask, sp, NEG_INF)
            sink = sink_ref[h]
            m = jnp.maximum(jnp.maximum(jnp.max(sc, axis=0, keepdims=True),
                                        jnp.max(sp, axis=0, keepdims=True)), sink)
            pc = jnp.exp(sc - m)
            pp = jnp.exp(sp - m)
            den = (jnp.sum(pc, axis=0, keepdims=True) + jnp.sum(pp, axis=0, keepdims=True)
                   + jnp.exp(sink - m))
            vs = slice(kv * HEAD_DIM, (kv + 1) * HEAD_DIM)
            o = (jnp.dot(vc[vs, :], pc.astype(BF16), preferred_element_type=F32)
                 + jnp.dot(vp[vs, :], pp.astype(BF16), preferred_element_type=F32))
            outs.append(o / den)
        a = jnp.concatenate(outs, axis=0)
        ms = jnp.mean(a * a, axis=0, keepdims=True)
        at_scr[:, blk * ATTN_BLOCK:(blk + 1) * ATTN_BLOCK] = (
            a * lax.rsqrt(ms + NORM_EPS) * g_ref[...]).astype(BF16)
    o_ref[...] = lax.dot_general(eye_ref[...], at_scr[...], NT_DIMS,
                                 preferred_element_type=F32).astype(BF16)


def _attention(qt, k, v3, sinks, gcol, eye, seq):
    t = k.shape[0]
    return pl.pallas_call(
        functools.partial(_attn_body, blocks_per_seq=seq // ATTN_BLOCK),
        grid=(t // ROW_TILE,),
        in_specs=[pl.BlockSpec(memory_space=pltpu.SMEM),
                  pl.BlockSpec((ATTN_WIDTH, ROW_TILE), lambda i: (0, i)),
                  _const_spec((t, KV_WIDTH)), _const_spec((t // ATTN_BLOCK, KV_WIDTH, ATTN_BLOCK)),
                  _const_spec((ATTN_WIDTH, 1)), _const_spec((ROW_TILE, ROW_TILE))],
        out_specs=pl.BlockSpec((ROW_TILE, ATTN_WIDTH), lambda i: (i, 0)),
        out_shape=jax.ShapeDtypeStruct((t, ATTN_WIDTH), BF16),
        scratch_shapes=[pltpu.VMEM((ATTN_WIDTH, ROW_TILE), BF16)],
        compiler_params=_cparams(("parallel",)),
        name="swa_attention",
    )(sinks, qt, k, v3, gcol, eye)


def _ssm_tables(lam_re, lam_im, log_dt, b_re, b_im, c_re, c_im, d_skip, n_steps):
    hi = lax.Precision.HIGHEST
    g_, n_, c_ = SSM_GROUPS, SSM_STATE, SSM_GROUP_DIM
    lr, li = lam_re.astype(F32), lam_im.astype(F32)
    dt = jnp.exp(log_dt.astype(F32))[:, None]
    mag = jnp.exp(lr * dt)
    ang = li * dt
    ab_re, ab_im = mag * jnp.cos(ang), mag * jnp.sin(ang)
    nr = ab_re - 1.0
    den = lr * lr + li * li
    f_re = (nr * lr + ab_im * li) / den
    f_im = (ab_im * lr - nr * li) / den
    br, bi = b_re.astype(F32), b_im.astype(F32)
    bb_re = f_re[..., None] * br - f_im[..., None] * bi
    bb_im = f_re[..., None] * bi + f_im[..., None] * br
    cr, ci = c_re.astype(F32), c_im.astype(F32)

    def apow(p):
        p = p.astype(F32)[None, :, None]
        m = jnp.exp(lr[:, None, :] * dt[:, :, None] * p)
        an = li[:, None, :] * dt[:, :, None] * p
        return m * jnp.cos(an), m * jnp.sin(an)

    ch = SSM_CHUNK
    pr, pi = apow(jnp.arange(ch + 1))
    ca_re = cr[:, None] * pr[:, :ch, None, :] - ci[:, None] * pi[:, :ch, None, :]
    ca_im = cr[:, None] * pi[:, :ch, None, :] + ci[:, None] * pr[:, :ch, None, :]
    kk = (jnp.einsum('gtcn,gnd->gtcd', ca_re, bb_re, precision=hi)
          - jnp.einsum('gtcn,gnd->gtcd', ca_im, bb_im, precision=hi))
    tt = jnp.arange(ch)[:, None]
    jj = jnp.arange(ch)[None, :]
    tau = tt - jj
    m5 = jnp.where((tau >= 0)[None, :, :, None, None], kk[:, jnp.clip(tau, 0, ch - 1)], 0.0)
    m = m5.transpose(0, 1, 3, 2, 4).reshape(g_, ch * c_, ch * c_)
    rr, ri = pr[:, ch - 1::-1][:, :ch], pi[:, ch - 1::-1][:, :ch]
    bt_re, bt_im = bb_re.transpose(0, 2, 1)[:, None], bb_im.transpose(0, 2, 1)[:, None]
    w_re = (rr[:, :, None, :] * bt_re - ri[:, :, None, :] * bt_im).reshape(g_, ch * c_, n_)
    w_im = (rr[:, :, None, :] * bt_im + ri[:, :, None, :] * bt_re).reshape(g_, ch * c_, n_)
    w_re, w_im = w_re.transpose(0, 2, 1), w_im.transpose(0, 2, 1)
    v_re = (cr[:, None] * pr[:, 1:, None, :] - ci[:, None] * pi[:, 1:, None, :])
    v_im = -(cr[:, None] * pi[:, 1:, None, :] + ci[:, None] * pr[:, 1:, None, :])
    v_re = v_re.reshape(g_, ch * c_, n_)
    v_im = v_im.reshape(g_, ch * c_, n_)
    np_ = g_ // 2
    z = jnp.zeros_like(w_re[0::2])
    wt_re = jnp.concatenate([jnp.concatenate([w_re[0::2], z], axis=2),
                             jnp.concatenate([z, w_re[1::2]], axis=2)], axis=1)
    wt_im = jnp.concatenate([jnp.concatenate([w_im[0::2], z], axis=2),
                             jnp.concatenate([z, w_im[1::2]], axis=2)], axis=1)
    zv = jnp.zeros_like(v_re)
    even = (jnp.arange(g_) % 2 == 0)[:, None, None]
    vt_re = jnp.where(even, jnp.concatenate([v_re, zv], axis=2), jnp.concatenate([zv, v_re], axis=2))
    vt_im = jnp.where(even, jnp.concatenate([v_im, zv], axis=2), jnp.concatenate([zv, v_im], axis=2))
    pair = lambda x: x.reshape(np_, 2, *x.shape[1:]).swapaxes(1, 2).reshape(np_, x.shape[1], 2 * n_)
    qr, qi = apow(jnp.arange(n_steps) * ch)
    sr, si = apow(jnp.array([ch, ch * n_steps]))
    consts = jnp.stack([pair(sr), pair(si)], axis=1)
    apw = jnp.stack([pair(qr), pair(qi)], axis=1)
    dcol = jnp.tile(d_skip.astype(F32).reshape(g_, 1, c_), (1, ch, 1)).reshape(g_, ch * c_, 1)
    return (m.astype(BF16), wt_re.astype(BF16), wt_im.astype(BF16),
            vt_re.astype(BF16), vt_im.astype(BF16), consts, apw, dcol)


def _ssm_body(u_ref, eye_ref, m_ref, wre_ref, wim_ref, vre_ref, vim_ref, c_ref, apw_ref, d_ref,
              y_ref, ut, yt, pre, pim, sre, sim, *, n_steps, halves):
    lanes = 2 * SSM_STATE
    nk = u_ref.shape[1]
    gd = SSM_GROUP_DIM
    for t in range(SSM_CHUNK):
        xt = lax.dot_general(eye_ref[...], u_ref[t], NT_DIMS,
                             preferred_element_type=F32).astype(BF16)
        for g in range(SSM_LANE_GROUPS):
            ut[g, t * gd:(t + 1) * gd, :] = xt[g * gd:(g + 1) * gd, :]

    row = lax.broadcasted_iota(jnp.int32, (SSM_SEQS, lanes), 0)
    seq_start = (row % halves) == 0
    shift = lambda x: jnp.where(seq_start, 0.0, pltpu.roll(x, 1, axis=0))

    for pp in range(SSM_LANE_GROUPS // 2):
        ucat = jnp.concatenate([ut[2 * pp], ut[2 * pp + 1]], axis=0)
        pre[...] = jnp.dot(wre_ref[pp], ucat, preferred_element_type=F32).T
        pim[...] = jnp.dot(wim_ref[pp], ucat, preferred_element_type=F32).T
        ar = jnp.broadcast_to(c_ref[pp, 0, 0:1, :], (SSM_SEQS, lanes))
        ai = jnp.broadcast_to(c_ref[pp, 1, 0:1, :], (SSM_SEQS, lanes))

        def step(i, carry):
            s_re, s_im = carry
            idx = pl.ds(i, SSM_SEQS, stride=n_steps)
            sre[idx, :] = s_re
            sim[idx, :] = s_im
            p_r = pre[idx, :]
            p_i = pim[idx, :]
            return ar * s_re - ai * s_im + p_r, ar * s_im + ai * s_re + p_i

        zero = jnp.zeros((SSM_SEQS, lanes), F32)
        e_re, e_im = lax.fori_loop(0, n_steps, step, (zero, zero))

        br = jnp.broadcast_to(c_ref[pp, 0, 1:2, :], (SSM_SEQS, lanes))
        bi = jnp.broadcast_to(c_ref[pp, 1, 1:2, :], (SSM_SEQS, lanes))
        c_re = jnp.zeros_like(e_re)
        c_im = jnp.zeros_like(e_im)
        for _ in range(halves - 1):
            n_re = e_re + br * c_re - bi * c_im
            n_im = e_im + br * c_im + bi * c_re
            c_re, c_im = shift(n_re), shift(n_im)

        apr = apw_ref[pp, 0][None, :, :]
        api = apw_ref[pp, 1][None, :, :]
        s3r = sre[...].reshape(SSM_SEQS, n_steps, lanes)
        s3i = sim[...].reshape(SSM_SEQS, n_steps, lanes)
        t_re = (s3r + apr * c_re[:, None, :] - api * c_im[:, None, :]).reshape(nk, lanes)
        t_im = (s3i + apr * c_im[:, None, :] + api * c_re[:, None, :]).reshape(nk, lanes)
        tr_hi = t_re.astype(BF16)
        tr_lo = (t_re - tr_hi.astype(F32)).astype(BF16)
        ti_hi = t_im.astype(BF16)
        ti_lo = (t_im - ti_hi.astype(F32)).astype(BF16)
        for g in (2 * pp, 2 * pp + 1):
            ug = ut[g]
            y = jnp.dot(m_ref[g], ug, preferred_element_type=F32)
            y += lax.dot_general(vre_ref[g], tr_hi, NT_DIMS, preferred_element_type=F32)
            y += lax.dot_general(vre_ref[g], tr_lo, NT_DIMS, preferred_element_type=F32)
            y += lax.dot_general(vim_ref[g], ti_hi, NT_DIMS, preferred_element_type=F32)
            y += lax.dot_general(vim_ref[g], ti_lo, NT_DIMS, preferred_element_type=F32)
            yt[g] = y + d_ref[g] * ug.astype(F32)

    for t in range(SSM_CHUNK):
        rows = jnp.concatenate([yt[g, t * gd:(t + 1) * gd, :] for g in range(SSM_LANE_GROUPS)], axis=0)
        y_ref[t] = rows.T


def _ssm(u3, tables, eye, bsz, seq):
    m, wre, wim, vre, vim, consts, apw, dcol = tables
    nk = u3.shape[1]
    halves = SSM_SEQS // bsz
    n_steps = seq // (SSM_CHUNK * halves)
    lanes = 2 * SSM_STATE
    lg = SSM_LANE_GROUPS
    lead = lambda shape: pl.BlockSpec(shape, lambda b: (b,) + (0,) * (len(shape) - 1))
    return pl.pallas_call(
        functools.partial(_ssm_body, n_steps=n_steps, halves=halves),
        grid=(SSM_GROUPS // lg,),
        in_specs=[pl.BlockSpec((SSM_CHUNK, nk, LANES), lambda b: (0, 0, b)),
                  _const_spec((LANES, LANES)),
                  lead((lg, SSM_VEC, SSM_VEC)),
                  lead((lg // 2, lanes, 2 * SSM_VEC)), lead((lg // 2, lanes, 2 * SSM_VEC)),
                  lead((lg, SSM_VEC, lanes)), lead((lg, SSM_VEC, lanes)),
                  lead((lg // 2, 2, 2, lanes)), lead((lg // 2, 2, n_steps, lanes)),
                  lead((lg, SSM_VEC, 1))],
        out_specs=pl.BlockSpec((SSM_CHUNK, nk, LANES), lambda b: (0, 0, b)),
        out_shape=jax.ShapeDtypeStruct((SSM_CHUNK, nk, SSM_WIDTH), F32),
        scratch_shapes=[pltpu.VMEM((lg, SSM_VEC, nk), BF16), pltpu.VMEM((lg, SSM_VEC, nk), F32)]
                       + [pltpu.VMEM((nk, lanes), F32)] * 4,
        compiler_params=_cparams(("parallel",)),
        name="s5_ssm",
    )(u3, eye, m, wre, wim, vre, vim, consts, apw, dcol)


def _mix_body(*refs, with_router):
    if with_router:
        (x_ref, a_ref, y3_ref, wglu_ref, sg_ref, wo_ref, fg_ref, rw_hi_ref, rw_lo_ref,
         x1_out, hn_out, lg_out, yscr) = refs
    else:
        x_ref, a_ref, y3_ref, wglu_ref, sg_ref, wo_ref, fg_ref, x1_out, hn_out, yscr = refs
    nlb = SSM_WIDTH // LANES
    for t in range(SSM_CHUNK):
        for lb in range(nlb):
            yscr[lb, pl.ds(t, ROW_TILE // SSM_CHUNK, stride=SSM_CHUNK), :] = (
                y3_ref[t, :, lb * LANES:(lb + 1) * LANES])
    y = jax.nn.gelu(jnp.concatenate([yscr[lb] for lb in range(nlb)], axis=1))
    z = y * jax.nn.sigmoid(jnp.dot(y.astype(BF16), wglu_ref[...], preferred_element_type=F32))
    ms = jnp.mean(z * z, axis=-1, keepdims=True)
    sn = (z * lax.rsqrt(ms + NORM_EPS) * sg_ref[...]).astype(BF16)
    x1 = (x_ref[...]
          + jnp.dot(a_ref[...], wo_ref[:ATTN_WIDTH, :], preferred_element_type=F32)
          + jnp.dot(sn, wo_ref[ATTN_WIDTH:, :], preferred_element_type=F32))
    x1_out[...] = x1
    ms1 = jnp.mean(x1 * x1, axis=-1, keepdims=True)
    hn = x1 * lax.rsqrt(ms1 + NORM_EPS) * fg_ref[...]
    hn_hi = hn.astype(BF16)
    hn_out[...] = hn_hi
    if with_router:
        hn_lo = (hn - hn_hi.astype(F32)).astype(BF16)
        lg_out[...] = (jnp.dot(hn_hi, rw_hi_ref[...], preferred_element_type=F32)
                       + jnp.dot(hn_hi, rw_lo_ref[...], preferred_element_type=F32)
                       + jnp.dot(hn_lo, rw_hi_ref[...], preferred_element_type=F32))


def _mix(x2d, attn, y3, wglu, sg, wo, fg, router=None):
    t = x2d.shape[0]
    cpt = ROW_TILE // SSM_CHUNK
    row = lambda w: pl.BlockSpec((ROW_TILE, w), lambda i: (i, 0))
    in_specs = [row(D_MODEL), row(ATTN_WIDTH),
                pl.BlockSpec((SSM_CHUNK, cpt, SSM_WIDTH), lambda i: (0, i, 0)),
                _const_spec((SSM_WIDTH, SSM_WIDTH)), _const_spec((1, SSM_WIDTH)),
                _const_spec((D_MODEL, D_MODEL)), _const_spec((1, D_MODEL))]
    out_specs = [row(D_MODEL), row(D_MODEL)]
    out_shape = [jax.ShapeDtypeStruct((t, D_MODEL), F32), jax.ShapeDtypeStruct((t, D_MODEL), BF16)]
    args = [x2d, attn, y3, wglu, sg, wo, fg]
    if router is not None:
        ne = router[0].shape[1]
        in_specs += [_const_spec((D_MODEL, ne)), _const_spec((D_MODEL, ne))]
        out_specs.append(row(ne))
        out_shape.append(jax.ShapeDtypeStruct((t, ne), F32))
        args += list(router)
    return pl.pallas_call(
        functools.partial(_mix_body, with_router=router is not None),
        grid=(t // ROW_TILE,),
        in_specs=in_specs, out_specs=out_specs, out_shape=out_shape,
        scratch_shapes=[pltpu.VMEM((SSM_WIDTH // LANES, ROW_TILE, LANES), F32)],
        compiler_params=_cparams(("parallel",)),
        name="mix_outproj",
    )(*args)


def _ffn_body(x_ref, h_ref, wg_ref, wu_ref, wd_ref, o_ref, acc):
    h = h_ref[...]
    acc[...] = x_ref[...]
    for c in range(D_FF // FF_CHUNK):
        sl = slice(c * FF_CHUNK, (c + 1) * FF_CHUNK)
        g = jnp.dot(h, wg_ref[:, sl], preferred_element_type=F32)
        u = jnp.dot(h, wu_ref[:, sl], preferred_element_type=F32)
        a = (jax.nn.silu(g) * u).astype(BF16)
        acc[...] += jnp.dot(a, wd_ref[sl, :], preferred_element_type=F32)
    o_ref[...] = acc[...]


def _dense_ffn(x1, hn, wg, wu, wd):
    t = x1.shape[0]
    row = lambda w: pl.BlockSpec((ROW_TILE, w), lambda i: (i, 0))
    return pl.pallas_call(
        _ffn_body,
        grid=(t // ROW_TILE,),
        in_specs=[row(D_MODEL), row(D_MODEL), _resident_spec((D_MODEL, D_FF)),
                  _resident_spec((D_MODEL, D_FF)), _resident_spec((D_FF, D_MODEL))],
        out_specs=row(D_MODEL),
        out_shape=jax.ShapeDtypeStruct((t, D_MODEL), F32),
        scratch_shapes=[pltpu.VMEM((ROW_TILE, D_MODEL), F32)],
        compiler_params=_cparams(("parallel",)),
        name="dense_ffn",
    )(x1, hn, wg, wu, wd)


def _dispatch_body(jlo_ref, jhi_ref, tok_ref, h_ref, o_ref, acc):
    b = pl.program_id(0)
    acc[...] = jnp.zeros_like(acc)
    tok = tok_ref[0]
    lane = lax.broadcasted_iota(jnp.int32, (1, TOK_WIN), 1)

    def window(j, carry):
        base = pl.multiple_of(j * TOK_WIN, TOK_WIN)
        onehot = jnp.where(tok - base == lane, 1.0, 0.0).astype(BF16)
        acc[...] += jnp.dot(onehot, h_ref[pl.ds(base, TOK_WIN), :], preferred_element_type=F32)
        return carry

    lax.fori_loop(jlo_ref[b], jhi_ref[b] + 1, window, 0)
    o_ref[...] = acc[...].astype(BF16)


def _dispatch(hn, buf_tok, jlo, jhi, nblk):
    t = hn.shape[0]
    return pl.pallas_call(
        _dispatch_body,
        grid_spec=pltpu.PrefetchScalarGridSpec(
            num_scalar_prefetch=2, grid=(nblk,),
            in_specs=[pl.BlockSpec((1, MOE_ROWS, 1), lambda b, *_: (b, 0, 0)),
                      pl.BlockSpec((t, D_MODEL), lambda b, *_: (0, 0), pipeline_mode=pl.Buffered(1))],
            out_specs=pl.BlockSpec((MOE_ROWS, D_MODEL), lambda b, *_: (b, 0)),
            scratch_shapes=[pltpu.VMEM((MOE_ROWS, D_MODEL), F32)]),
        out_shape=jax.ShapeDtypeStruct((nblk * MOE_ROWS, D_MODEL), BF16),
        compiler_params=_cparams(("arbitrary",)),
        name="moe_dispatch",
    )(jlo, jhi, buf_tok.reshape(nblk, MOE_ROWS, 1), hn)


def _expert_ffn_body(e_ref, x_ref, gate_ref, wg_ref, wu_ref, wd_ref, o_ref, acc):
    c = pl.program_id(1)

    @pl.when(c == 0)
    def _():
        acc[...] = jnp.zeros_like(acc)

    x = x_ref[...]
    g = jnp.dot(x, wg_ref[0], preferred_element_type=F32)
    u = jnp.dot(x, wu_ref[0], preferred_element_type=F32)
    a = (jax.nn.silu(g) * u).astype(BF16)
    acc[...] += jnp.dot(a, wd_ref[0], preferred_element_type=F32)

    @pl.when(c == pl.num_programs(1) - 1)
    def _():
        o_ref[...] = (acc[...] * gate_ref[0]).astype(BF16)


def _expert_ffn(xs, buf_gate, block_e, wg, wu, wd, nblk):
    return pl.pallas_call(
        _expert_ffn_body,
        grid_spec=pltpu.PrefetchScalarGridSpec(
            num_scalar_prefetch=1, grid=(nblk, D_FF // FF_CHUNK),
            in_specs=[pl.BlockSpec((MOE_ROWS, D_MODEL), lambda b, c, e: (b, 0)),
                      pl.BlockSpec((1, MOE_ROWS, 1), lambda b, c, e: (b, 0, 0)),
                      pl.BlockSpec((1, D_MODEL, FF_CHUNK), lambda b, c, e: (e[b], 0, c)),
                      pl.BlockSpec((1, D_MODEL, FF_CHUNK), lambda b, c, e: (e[b], 0, c)),
                      pl.BlockSpec((1, FF_CHUNK, D_MODEL), lambda b, c, e: (e[b], c, 0))],
            out_specs=pl.BlockSpec((MOE_ROWS, D_MODEL), lambda b, c, e: (b, 0)),
            scratch_shapes=[pltpu.VMEM((MOE_ROWS, D_MODEL), F32)]),
        out_shape=jax.ShapeDtypeStruct((nblk * MOE_ROWS, D_MODEL), BF16),
        compiler_params=_cparams(("arbitrary", "arbitrary")),
        name="moe_expert_ffn",
    )(block_e, xs, buf_gate.reshape(nblk, MOE_ROWS, 1), wg, wu, wd)


def _combine_body(blk_ref, ok_ref, x_ref, pos_ref, y_ref, o_ref):
    j = pl.program_id(0)
    s = pl.program_id(1)

    @pl.when(s == 0)
    def _():
        o_ref[...] = x_ref[...]

    @pl.when(ok_ref[j, s] != 0)
    def _():
        base = blk_ref[j, s] * MOE_ROWS
        lane = lax.broadcasted_iota(jnp.int32, (1, MOE_ROWS), 1)
        p0 = pos_ref[:, 0:1] - base
        p1 = pos_ref[:, 1:2] - base
        onehot = jnp.where(jnp.logical_or(p0 == lane, p1 == lane), 1.0, 0.0).astype(BF16)
        o_ref[...] += jnp.dot(onehot, y_ref[...], preferred_element_type=F32)


def _combine(x1, pos2, ys, blk, ok):
    t = x1.shape[0]
    nslots = blk.shape[1]
    return pl.pallas_call(
        _combine_body,
        grid_spec=pltpu.PrefetchScalarGridSpec(
            num_scalar_prefetch=2, grid=(t // ROW_TILE, nslots),
            in_specs=[pl.BlockSpec((ROW_TILE, D_MODEL), lambda j, s, blk, ok: (j, 0)),
                      pl.BlockSpec((ROW_TILE, TOP_K), lambda j, s, blk, ok: (j, 0)),
                      pl.BlockSpec((MOE_ROWS, D_MODEL), lambda j, s, blk, ok: (blk[j, s], 0))],
            out_specs=pl.BlockSpec((ROW_TILE, D_MODEL), lambda j, s, blk, ok: (j, 0))),
        out_shape=jax.ShapeDtypeStruct((t, D_MODEL), F32),
        compiler_params=_cparams(("arbitrary", "arbitrary")),
        name="moe_combine",
    )(blk, ok, x1, pos2, ys)


def _moe(x1, hn, logits, wg, wu, wd):
    t = x1.shape[0]
    n_assign = t * TOP_K
    nblk = n_assign // MOE_ROWS + N_EXPERTS
    top_v, top_i = lax.top_k(logits, TOP_K)
    gates = jax.nn.softmax(top_v, axis=-1)
    flat_e = top_i.reshape(-1)
    flat_tok = jnp.repeat(jnp.arange(t, dtype=jnp.int32), TOP_K)
    onehot = (flat_e[:, None] == jnp.arange(N_EXPERTS)[None, :]).astype(jnp.int32)
    csum = jnp.cumsum(onehot, axis=0)
    rank = jnp.take_along_axis(csum, flat_e[:, None], axis=1)[:, 0] - 1
    counts = csum[-1]
    padded = ((counts + MOE_ROWS - 1) // MOE_ROWS) * MOE_ROWS
    pend = jnp.cumsum(padded)
    pstart = pend - padded
    pos = (pstart[flat_e] + rank).astype(jnp.int32)
    cap = nblk * MOE_ROWS
    buf_tok = jnp.full((cap,), -1, jnp.int32).at[pos].set(flat_tok)
    buf_gate = jnp.zeros((cap,), F32).at[pos].set(gates.reshape(-1))
    block_e = jnp.clip(jnp.searchsorted(pend, jnp.arange(nblk) * MOE_ROWS, side='right'),
                       0, N_EXPERTS - 1).astype(jnp.int32)
    bt = buf_tok.reshape(nblk, MOE_ROWS)
    tmax = jnp.max(bt, axis=1)
    tmin = jnp.min(jnp.where(bt < 0, t, bt), axis=1)
    jlo = jnp.where(tmax >= 0, tmin // TOK_WIN, 0).astype(jnp.int32)
    jhi = jnp.where(tmax >= 0, tmax // TOK_WIN, -1).astype(jnp.int32)

    xs = _dispatch(hn, buf_tok, jlo, jhi, nblk)
    ys = _expert_ffn(xs, buf_gate, block_e, wg, wu, wd, nblk)

    ntile = t // ROW_TILE
    before = jnp.concatenate([jnp.zeros((1, N_EXPERTS), jnp.int32),
                              csum[ROW_TILE * TOP_K - 1::ROW_TILE * TOP_K]], axis=0)
    lo = pstart[None, :] + before[:-1]
    hi = pstart[None, :] + before[1:] - 1
    has = hi >= lo
    b_lo = lo // MOE_ROWS
    b_hi = hi // MOE_ROWS
    blk = jnp.stack([b_lo, b_hi], axis=2).reshape(ntile, 2 * N_EXPERTS)
    ok = jnp.stack([has, has & (b_hi != b_lo)], axis=2).reshape(ntile, 2 * N_EXPERTS)
    idx = jnp.where(ok, jnp.arange(2 * N_EXPERTS)[None, :], -1)
    last = lax.cummax(idx, axis=1)
    blk = jnp.where(last >= 0, jnp.take_along_axis(blk, jnp.maximum(last, 0), axis=1), 0)
    return _combine(x1, pos.reshape(t, TOP_K), ys, blk.astype(jnp.int32), ok.astype(jnp.int32))


def kernel(x, attn_norm_g, w_in, q_norm_g, k_norm_g, sinks, lam_re, lam_im, log_dt, b_re, b_im,
           c_re, c_im, d_skip, w_glu, attn_out_g, ssm_out_g, w_o, ffn_norm_g, dense_wg, dense_wu,
           dense_wd, router_w, moe_wg, moe_wu, moe_wd):
    bsz, seq, _ = x.shape
    depth = w_in.shape[0]
    assert SSM_SEQS % bsz == 0 and seq % (SSM_CHUNK * (SSM_SEQS // bsz)) == 0
    assert seq % ROW_TILE == 0 and ROW_TILE % ATTN_BLOCK == 0
    x2d = x.reshape(bsz * seq, D_MODEL).astype(F32)
    head = jnp.arange(ATTN_WIDTH) // HEAD_DIM
    avg = jnp.where(head[:, None] == head[None, :], 1.0 / HEAD_DIM, 0.0).astype(BF16)
    eye = jnp.eye(ROW_TILE, dtype=BF16)
    n_steps = seq // (SSM_CHUNK * (SSM_SEQS // bsz))
    row = lambda v: v.astype(F32).reshape(1, -1)
    col = lambda v: v.astype(F32).reshape(-1, 1)
    ko, vo, uo = ATTN_WIDTH, ATTN_WIDTH + KV_WIDTH, ATTN_WIDTH + 2 * KV_WIDTH
    for l in range(depth):
        qg = jnp.tile(q_norm_g[l].astype(F32), N_HEADS) * (HEAD_DIM ** -0.5)
        kg = jnp.tile(k_norm_g[l].astype(F32), N_KV_HEADS)
        w = w_in[l].astype(BF16)
        qt, k, v3, u3 = _inproj(x2d, row(attn_norm_g[l]), w[:, :ko].T, w[:, ko:vo], w[:, vo:uo].T,
                                w[:, uo:], col(qg), row(kg), avg)
        attn = _attention(qt, k, v3, sinks[l].astype(F32), col(attn_out_g[l]), eye, seq)
        tables = _ssm_tables(lam_re[l], lam_im[l], log_dt[l], b_re[l], b_im[l], c_re[l], c_im[l],
                             d_skip[l], n_steps)
        y3 = _ssm(u3, tables, eye[:LANES, :LANES], bsz, seq)
        i = l // 2
        router = None
        if l % 2 == 1:
            rw = router_w[i].astype(F32)
            rw_hi = rw.astype(BF16)
            router = (rw_hi, (rw - rw_hi.astype(F32)).astype(BF16))
        outs = _mix(x2d, attn, y3, w_glu[l].astype(BF16), row(ssm_out_g[l]), w_o[l].astype(BF16),
                    row(ffn_norm_g[l]), router)
        if l % 2 == 0:
            x1, hn = outs
            x2d = _dense_ffn(x1, hn, dense_wg[i].astype(BF16), dense_wu[i].astype(BF16),
                             dense_wd[i].astype(BF16))
        else:
            x1, hn, logits = outs
            x2d = _moe(x1, hn, logits, moe_wg[i].astype(BF16), moe_wu[i].astype(BF16),
                       moe_wd[i].astype(BF16))
    return x2d.reshape(bsz, seq, D_MODEL)
```

```python
import functools
import math

import jax
import jax.numpy as jnp
from jax import lax
from jax.experimental import pallas as pl
from jax.experimental.pallas import tpu as pltpu

F32 = jnp.float32
BF16 = jnp.bfloat16

D_MODEL = 1024
HEAD_DIM = 64
N_HEADS = 8
N_KV_HEADS = 2
GQA = N_HEADS // N_KV_HEADS
ATTN_WIDTH = N_HEADS * HEAD_DIM
KV_WIDTH = N_KV_HEADS * HEAD_DIM
ATTN_BLOCK = 128
SSM_WIDTH = D_MODEL - ATTN_WIDTH
SSM_GROUP_DIM = 16
SSM_GROUPS = SSM_WIDTH // SSM_GROUP_DIM
SSM_STATE = 64
IN_WIDTH = ATTN_WIDTH + 2 * KV_WIDTH + SSM_WIDTH
D_FF = 3584
N_EXPERTS = 8
TOP_K = 2
NORM_EPS = 1e-6
NEG_INF = -1e30

LANES = 128
ROW_TILE = 512
SSM_CHUNK = 16
SSM_SEQS = 8
SSM_VEC = SSM_CHUNK * SSM_GROUP_DIM
SSM_LANE_GROUPS = LANES // SSM_GROUP_DIM
FF_CHUNK = 512
MOE_ROWS = 256
VMEM_LIMIT = 56 * 1024 * 1024
MOE_VMEM_LIMIT = 60 * 1024 * 1024

NT_DIMS = (((1,), (1,)), ((), ()))


def _cparams(sem, vmem=VMEM_LIMIT):
    return pltpu.CompilerParams(dimension_semantics=sem, vmem_limit_bytes=vmem)


def _const_spec(shape):
    n = len(shape)
    return pl.BlockSpec(shape, lambda *_: (0,) * n)


def _resident_spec(shape):
    n = len(shape)
    return pl.BlockSpec(shape, lambda *_: (0,) * n, pipeline_mode=pl.Buffered(1))


def _inproj_body(x_ref, g_ref, wqt_ref, wk_ref, wvt_ref, wu_ref, qg_ref, kg_ref, avg_ref,
                 qt_out, k_out, v3_out, u3_out, uscr):
    x = x_ref[...]
    ms = jnp.mean(x * x, axis=-1, keepdims=True)
    hn = (x * lax.rsqrt(ms + NORM_EPS) * g_ref[...]).astype(BF16)
    qt = lax.dot_general(wqt_ref[...], hn, NT_DIMS, preferred_element_type=F32)
    qms = jnp.dot(avg_ref[...], (qt * qt).astype(BF16), preferred_element_type=F32)
    qt_out[...] = (qt * lax.rsqrt(qms + NORM_EPS) * qg_ref[...]).astype(BF16)
    k = jnp.dot(hn, wk_ref[...], preferred_element_type=F32)
    kms = jnp.dot((k * k).astype(BF16), avg_ref[:KV_WIDTH, :KV_WIDTH], preferred_element_type=F32)
    k_out[...] = (k * lax.rsqrt(kms + NORM_EPS) * kg_ref[...]).astype(BF16)
    vt = lax.dot_general(wvt_ref[...], hn, NT_DIMS, preferred_element_type=F32)
    for b in range(ROW_TILE // ATTN_BLOCK):
        v3_out[b] = vt[:, b * ATTN_BLOCK:(b + 1) * ATTN_BLOCK].astype(BF16)
    u = jnp.dot(hn, wu_ref[...], preferred_element_type=F32)
    for lb in range(SSM_WIDTH // LANES):
        uscr[lb] = u[:, lb * LANES:(lb + 1) * LANES]
    for t in range(SSM_CHUNK):
        for lb in range(SSM_WIDTH // LANES):
            u3_out[t, :, lb * LANES:(lb + 1) * LANES] = (
                uscr[lb, pl.ds(t, ROW_TILE // SSM_CHUNK, stride=SSM_CHUNK), :].astype(BF16))


def _inproj(x2d, g, wqt, wk, wvt, wu, qg, kg, avg):
    t = x2d.shape[0]
    cpt = ROW_TILE // SSM_CHUNK
    return pl.pallas_call(
        _inproj_body,
        grid=(t // ROW_TILE,),
        in_specs=[pl.BlockSpec((ROW_TILE, D_MODEL), lambda i: (i, 0)),
                  _const_spec((1, D_MODEL)), _const_spec((ATTN_WIDTH, D_MODEL)),
                  _const_spec((D_MODEL, KV_WIDTH)), _const_spec((KV_WIDTH, D_MODEL)),
                  _const_spec((D_MODEL, SSM_WIDTH)), _const_spec((ATTN_WIDTH, 1)),
                  _const_spec((1, KV_WIDTH)), _const_spec((ATTN_WIDTH, ATTN_WIDTH))],
        out_specs=[pl.BlockSpec((ATTN_WIDTH, ROW_TILE), lambda i: (0, i)),
                   pl.BlockSpec((ROW_TILE, KV_WIDTH), lambda i: (i, 0)),
                   pl.BlockSpec((ROW_TILE // ATTN_BLOCK, KV_WIDTH, ATTN_BLOCK), lambda i: (i, 0, 0)),
                   pl.BlockSpec((SSM_CHUNK, cpt, SSM_WIDTH), lambda i: (0, i, 0))],
        out_shape=[jax.ShapeDtypeStruct((ATTN_WIDTH, t), BF16),
                   jax.ShapeDtypeStruct((t, KV_WIDTH), BF16),
                   jax.ShapeDtypeStruct((t // ATTN_BLOCK, KV_WIDTH, ATTN_BLOCK), BF16),
                   jax.ShapeDtypeStruct((SSM_CHUNK, t // SSM_CHUNK, SSM_WIDTH), BF16)],
        scratch_shapes=[pltpu.VMEM((SSM_WIDTH // LANES, ROW_TILE, LANES), F32)],
        compiler_params=_cparams(("parallel",)),
        name="inproj",
    )(x2d, g, wqt, wk, wvt, wu, qg, kg, avg)


def _attn_body(sink_ref, qt_ref, k_ref, v3_ref, g_ref, eye_ref, o_ref, at_scr, *, blocks_per_seq):
    i = pl.program_id(0)
    key = lax.broadcasted_iota(jnp.int32, (ATTN_BLOCK, ATTN_BLOCK), 0)
    qry = lax.broadcasted_iota(jnp.int32, (ATTN_BLOCK, ATTN_BLOCK), 1)
    cur_ok = key <= qry
    prev_ok = key > qry
    zpad = jnp.zeros((HEAD_DIM, ATTN_BLOCK), BF16)
    nblk = ROW_TILE // ATTN_BLOCK
    for blk in range(nblk):
        gblk = i * nblk + blk
        pblk = jnp.maximum(gblk - 1, 0)
        row0 = pl.multiple_of(gblk * ATTN_BLOCK, ATTN_BLOCK)
        prev0 = pl.multiple_of(pblk * ATTN_BLOCK, ATTN_BLOCK)
        has_prev = (gblk % blocks_per_seq) != 0
        kc = k_ref[pl.ds(row0, ATTN_BLOCK), :]
        kp = k_ref[pl.ds(prev0, ATTN_BLOCK), :]
        vc = v3_ref[gblk]
        vp = v3_ref[pblk]
        qb = qt_ref[:, blk * ATTN_BLOCK:(blk + 1) * ATTN_BLOCK]
        pmask = jnp.logical_and(prev_ok, has_prev)
        outs = []
        for h in range(N_HEADS):
            kv = h // GQA
            qh = qb[h * HEAD_DIM:(h + 1) * HEAD_DIM, :]
            qpad = jnp.concatenate([qh, zpad] if kv == 0 else [zpad, qh], axis=0)
            sc = jnp.dot(kc, qpad, preferred_element_type=F32)
            sp = jnp.dot(kp, qpad, preferred_element_type=F32)
            sc = jnp.where(cur_ok, sc, NEG_INF)
            sp = jnp.where(pmask, sp, NEG_INF)
            sink = sink_ref[h]
            m = jnp.maximum(jnp.maximum(jnp.max(sc, axis=0, keepdims=True),
                                        jnp.max(sp, axis=0, keepdims=True)), sink)
            pc = jnp.exp(sc - m)
            pp = jnp.exp(sp - m)
            den = (jnp.sum(pc, axis=0, keepdims=True) + jnp.sum(pp, axis=0, keepdims=True)
                   + jnp.exp(sink - m))
            vs = slice(kv * HEAD_DIM, (kv + 1) * HEAD_DIM)
            o = (jnp.dot(vc[vs, :], pc.astype(BF16), preferred_element_type=F32)
                 + jnp.dot(vp[vs, :], pp.astype(BF16), preferred_element_type=F32))
            outs.append(o / den)
        a = jnp.concatenate(outs, axis=0)
        ms = jnp.mean(a * a, axis=0, keepdims=True)
        at_scr[:, blk * ATTN_BLOCK:(blk + 1) * ATTN_BLOCK] = (
            a * lax.rsqrt(ms + NORM_EPS) * g_ref[...]).astype(BF16)
    o_ref[...] = lax.dot_general(eye_ref[...], at_scr[...], NT_DIMS,
                                 preferred_element_type=F32).astype(BF16)


def _attention(qt, k, v3, sinks, gcol, eye, seq):
    t = k.shape[0]
    return pl.pallas_call(
        functools.partial(_attn_body, blocks_per_seq=seq // ATTN_BLOCK),
        grid=(t // ROW_TILE,),
        in_specs=[pl.BlockSpec(memory_space=pltpu.SMEM),
                  pl.BlockSpec((ATTN_WIDTH, ROW_TILE), lambda i: (0, i)),
                  _const_spec((t, KV_WIDTH)), _const_spec((t // ATTN_BLOCK, KV_WIDTH, ATTN_BLOCK)),
                  _const_spec((ATTN_WIDTH, 1)), _const_spec((ROW_TILE, ROW_TILE))],
        out_specs=pl.BlockSpec((ROW_TILE, ATTN_WIDTH), lambda i: (i, 0)),
        out_shape=jax.ShapeDtypeStruct((t, ATTN_WIDTH), BF16),
        scratch_shapes=[pltpu.VMEM((ATTN_WIDTH, ROW_TILE), BF16)],
        compiler_params=_cparams(("parallel",)),
        name="swa_attention",
    )(sinks, qt, k, v3, gcol, eye)


def _ssm_tables(lam_re, lam_im, log_dt, b_re, b_im, c_re, c_im, d_skip, n_steps):
    hi = lax.Precision.HIGHEST
    g_, n_, c_ = SSM_GROUPS, SSM_STATE, SSM_GROUP_DIM
    lr, li = lam_re.astype(F32), lam_im.astype(F32)
    dt = jnp.exp(log_dt.astype(F32))[:, None]
    mag = jnp.exp(lr * dt)
    ang = li * dt
    ab_re, ab_im = mag * jnp.cos(ang), mag * jnp.sin(ang)
    nr = ab_re - 1.0
    den = lr * lr + li * li
    f_re = (nr * lr + ab_im * li) / den
    f_im = (ab_im * lr - nr * li) / den
    br, bi = b_re.astype(F32), b_im.astype(F32)
    bb_re = f_re[..., None] * br - f_im[..., None] * bi
    bb_im = f_re[..., None] * bi + f_im[..., None] * br
    cr, ci = c_re.astype(F32), c_im.astype(F32)

    def apow(p):
        p = p.astype(F32)[None, :, None]
        m = jnp.exp(lr[:, None, :] * dt[:, :, None] * p)
        an = li[:, None, :] * dt[:, :, None] * p
        return m * jnp.cos(an), m * jnp.sin(an)

    ch = SSM_CHUNK
    pr, pi = apow(jnp.arange(ch + 1))
    ca_re = cr[:, None] * pr[:, :ch, None, :] - ci[:, None] * pi[:, :ch, None, :]
    ca_im = cr[:, None] * pi[:, :ch, None, :] + ci[:, None] * pr[:, :ch, None, :]
    kk = (jnp.einsum('gtcn,gnd->gtcd', ca_re, bb_re, precision=hi)
          - jnp.einsum('gtcn,gnd->gtcd', ca_im, bb_im, precision=hi))
    tt = jnp.arange(ch)[:, None]
    jj = jnp.arange(ch)[None, :]
    tau = tt - jj
    m5 = jnp.where((tau >= 0)[None, :, :, None, None], kk[:, jnp.clip(tau, 0, ch - 1)], 0.0)
    m = m5.transpose(0, 1, 3, 2, 4).reshape(g_, ch * c_, ch * c_)
    rr, ri = pr[:, ch - 1::-1][:, :ch], pi[:, ch - 1::-1][:, :ch]
    bt_re, bt_im = bb_re.transpose(0, 2, 1)[:, None], bb_im.transpose(0, 2, 1)[:, None]
    w_re = (rr[:, :, None, :] * bt_re - ri[:, :, None, :] * bt_im).reshape(g_, ch * c_, n_)
    w_im = (rr[:, :, None, :] * bt_im + ri[:, :, None, :] * bt_re).reshape(g_, ch * c_, n_)
    w_re, w_im = w_re.transpose(0, 2, 1), w_im.transpose(0, 2, 1)
    v_re = (cr[:, None] * pr[:, 1:, None, :] - ci[:, None] * pi[:, 1:, None, :])
    v_im = -(cr[:, None] * pi[:, 1:, None, :] + ci[:, None] * pr[:, 1:, None, :])
    v_re = v_re.reshape(g_, ch * c_, n_)
    v_im = v_im.reshape(g_, ch * c_, n_)
    np_ = g_ // 2
    z = jnp.zeros_like(w_re[0::2])
    wt_re = jnp.concatenate([jnp.concatenate([w_re[0::2], z], axis=2),
                             jnp.concatenate([z, w_re[1::2]], axis=2)], axis=1)
    wt_im = jnp.concatenate([jnp.concatenate([w_im[0::2], z], axis=2),
                             jnp.concatenate([z, w_im[1::2]], axis=2)], axis=1)
    zv = jnp.zeros_like(v_re)
    even = (jnp.arange(g_) % 2 == 0)[:, None, None]
    vt_re = jnp.where(even, jnp.concatenate([v_re, zv], axis=2), jnp.concatenate([zv, v_re], axis=2))
    vt_im = jnp.where(even, jnp.concatenate([v_im, zv], axis=2), jnp.concatenate([zv, v_im], axis=2))
    pair = lambda x: x.reshape(np_, 2, *x.shape[1:]).swapaxes(1, 2).reshape(np_, x.shape[1], 2 * n_)
    qr, qi = apow(jnp.arange(n_steps) * ch)
    sr, si = apow(jnp.array([ch, ch * n_steps]))
    consts = jnp.stack([pair(sr), pair(si)], axis=1)
    apw = jnp.stack([pair(qr), pair(qi)], axis=1)
    dcol = jnp.tile(d_skip.astype(F32).reshape(g_, 1, c_), (1, ch, 1)).reshape(g_, ch * c_, 1)
    return (m.astype(BF16), wt_re.astype(BF16), wt_im.astype(BF16),
            vt_re.astype(BF16), vt_im.astype(BF16), consts, apw, dcol)


def _ssm_body(u_ref, eye_ref, m_ref, wre_ref, wim_ref, vre_ref, vim_ref, c_ref, apw_ref, d_ref,
              y_ref, ut, yt, pre, pim, sre, sim, *, n_steps, halves):
    lanes = 2 * SSM_STATE
    nk = u_ref.shape[1]
    gd = SSM_GROUP_DIM
    for t in range(SSM_CHUNK):
        xt = lax.dot_general(eye_ref[...], u_ref[t], NT_DIMS,
                             preferred_element_type=F32).astype(BF16)
        for g in range(SSM_LANE_GROUPS):
            ut[g, t * gd:(t + 1) * gd, :] = xt[g * gd:(g + 1) * gd, :]

    row = lax.broadcasted_iota(jnp.int32, (SSM_SEQS, lanes), 0)
    seq_start = (row % halves) == 0
    shift = lambda x: jnp.where(seq_start, 0.0, pltpu.roll(x, 1, axis=0))

    for pp in range(SSM_LANE_GROUPS // 2):
        ucat = jnp.concatenate([ut[2 * pp], ut[2 * pp + 1]], axis=0)
        pre[...] = jnp.dot(wre_ref[pp], ucat, preferred_element_type=F32).T
        pim[...] = jnp.dot(wim_ref[pp], ucat, preferred_element_type=F32).T
        ar = jnp.broadcast_to(c_ref[pp, 0, 0:1, :], (SSM_SEQS, lanes))
        ai = jnp.broadcast_to(c_ref[pp, 1, 0:1, :], (SSM_SEQS, lanes))

        def step(i, carry):
            s_re, s_im = carry
            idx = pl.ds(i, SSM_SEQS, stride=n_steps)
            sre[idx, :] = s_re
            sim[idx, :] = s_im
            p_r = pre[idx, :]
            p_i = pim[idx, :]
            return ar * s_re - ai * s_im + p_r, ar * s_im + ai * s_re + p_i

        zero = jnp.zeros((SSM_SEQS, lanes), F32)
        e_re, e_im = lax.fori_loop(0, n_steps, step, (zero, zero))

        br = jnp.broadcast_to(c_ref[pp, 0, 1:2, :], (SSM_SEQS, lanes))
        bi = jnp.broadcast_to(c_ref[pp, 1, 1:2, :], (SSM_SEQS, lanes))
        c_re = jnp.zeros_like(e_re)
        c_im = jnp.zeros_like(e_im)
        for _ in range(halves - 1):
            n_re = e_re + br * c_re - bi * c_im
            n_im = e_im + br * c_im + bi * c_re
            c_re, c_im = shift(n_re), shift(n_im)

        apr = apw_ref[pp, 0][None, :, :]
        api = apw_ref[pp, 1][None, :, :]
        s3r = sre[...].reshape(SSM_SEQS, n_steps, lanes)
        s3i = sim[...].reshape(SSM_SEQS, n_steps, lanes)
        t_re = (s3r + apr * c_re[:, None, :] - api * c_im[:, None, :]).reshape(nk, lanes)
        t_im = (s3i + apr * c_im[:, None, :] + api * c_re[:, None, :]).reshape(nk, lanes)
        tr_hi = t_re.astype(BF16)
        tr_lo = (t_re - tr_hi.astype(F32)).astype(BF16)
        ti_hi = t_im.astype(BF16)
        ti_lo = (t_im - ti_hi.astype(F32)).astype(BF16)
        for g in (2 * pp, 2 * pp + 1):
            ug = ut[g]
            y = jnp.dot(m_ref[g], ug, preferred_element_type=F32)
            y += lax.dot_general(vre_ref[g], tr_hi, NT_DIMS, preferred_element_type=F32)
            y += lax.dot_general(vre_ref[g], tr_lo, NT_DIMS, preferred_element_type=F32)
            y += lax.dot_general(vim_ref[g], ti_hi, NT_DIMS, preferred_element_type=F32)
            y += lax.dot_general(vim_ref[g], ti_lo, NT_DIMS, preferred_element_type=F32)
            yt[g] = y + d_ref[g] * ug.astype(F32)

    for t in range(SSM_CHUNK):
        rows = jnp.concatenate([yt[g, t * gd:(t + 1) * gd, :] for g in range(SSM_LANE_GROUPS)], axis=0)
        y_ref[t] = rows.T


def _ssm(u3, tables, eye, bsz, seq):
    m, wre, wim, vre, vim, consts, apw, dcol = tables
    nk = u3.shape[1]
    halves = SSM_SEQS // bsz
    n_steps = seq // (SSM_CHUNK * halves)
    lanes = 2 * SSM_STATE
    lg = SSM_LANE_GROUPS
    lead = lambda shape: pl.BlockSpec(shape, lambda b: (b,) + (0,) * (len(shape) - 1))
    return pl.pallas_call(
        functools.partial(_ssm_body, n_steps=n_steps, halves=halves),
        grid=(SSM_GROUPS // lg,),
        in_specs=[pl.BlockSpec((SSM_CHUNK, nk, LANES), lambda b: (0, 0, b)),
                  _const_spec((LANES, LANES)),
                  lead((lg, SSM_VEC, SSM_VEC)),
                  lead((lg // 2, lanes, 2 * SSM_VEC)), lead((lg // 2, lanes, 2 * SSM_VEC)),
                  lead((lg, SSM_VEC, lanes)), lead((lg, SSM_VEC, lanes)),
                  lead((lg // 2, 2, 2, lanes)), lead((lg // 2, 2, n_steps, lanes)),
                  lead((lg, SSM_VEC, 1))],
        out_specs=pl.BlockSpec((SSM_CHUNK, nk, LANES), lambda b: (0, 0, b)),
        out_shape=jax.ShapeDtypeStruct((SSM_CHUNK, nk, SSM_WIDTH), F32),
        scratch_shapes=[pltpu.VMEM((lg, SSM_VEC, nk), BF16), pltpu.VMEM((lg, SSM_VEC, nk), F32)]
                       + [pltpu.VMEM((nk, lanes), F32)] * 4,
        compiler_params=_cparams(("parallel",)),
        name="s5_ssm",
    )(u3, eye, m, wre, wim, vre, vim, consts, apw, dcol)


def _mix_body(*refs, with_router):
    if with_router:
        (x_ref, a_ref, y3_ref, wglu_ref, sg_ref, wo_ref, fg_ref, rw_hi_ref, rw_lo_ref,
         x1_out, lg_out, yscr) = refs
    else:
        x_ref, a_ref, y3_ref, wglu_ref, sg_ref, wo_ref, fg_ref, x1_out, hn_out, yscr = refs
    nlb = SSM_WIDTH // LANES
    for t in range(SSM_CHUNK):
        for lb in range(nlb):
            yscr[lb, pl.ds(t, ROW_TILE // SSM_CHUNK, stride=SSM_CHUNK), :] = (
                y3_ref[t, :, lb * LANES:(lb + 1) * LANES])
    y = jax.nn.gelu(jnp.concatenate([yscr[lb] for lb in range(nlb)], axis=1))
    z = y * jax.nn.sigmoid(jnp.dot(y.astype(BF16), wglu_ref[...], preferred_element_type=F32))
    ms = jnp.mean(z * z, axis=-1, keepdims=True)
    sn = (z * lax.rsqrt(ms + NORM_EPS) * sg_ref[...]).astype(BF16)
    x1 = (x_ref[...]
          + jnp.dot(a_ref[...], wo_ref[:ATTN_WIDTH, :], preferred_element_type=F32)
          + jnp.dot(sn, wo_ref[ATTN_WIDTH:, :], preferred_element_type=F32))
    x1_out[...] = x1
    ms1 = jnp.mean(x1 * x1, axis=-1, keepdims=True)
    hn = x1 * lax.rsqrt(ms1 + NORM_EPS) * fg_ref[...]
    hn_hi = hn.astype(BF16)
    if with_router:
        hn_lo = (hn - hn_hi.astype(F32)).astype(BF16)
        lg_out[...] = (jnp.dot(hn_hi, rw_hi_ref[...], preferred_element_type=F32)
                       + jnp.dot(hn_hi, rw_lo_ref[...], preferred_element_type=F32)
                       + jnp.dot(hn_lo, rw_hi_ref[...], preferred_element_type=F32))
    else:
        hn_out[...] = hn_hi


def _mix(x2d, attn, y3, wglu, sg, wo, fg, router=None):
    t = x2d.shape[0]
    cpt = ROW_TILE // SSM_CHUNK
    row = lambda w: pl.BlockSpec((ROW_TILE, w), lambda i: (i, 0))
    in_specs = [row(D_MODEL), row(ATTN_WIDTH),
                pl.BlockSpec((SSM_CHUNK, cpt, SSM_WIDTH), lambda i: (0, i, 0)),
                _const_spec((SSM_WIDTH, SSM_WIDTH)), _const_spec((1, SSM_WIDTH)),
                _const_spec((D_MODEL, D_MODEL)), _const_spec((1, D_MODEL))]
    args = [x2d, attn, y3, wglu, sg, wo, fg]
    if router is not None:
        ne = router[0].shape[1]
        in_specs += [_const_spec((D_MODEL, ne)), _const_spec((D_MODEL, ne))]
        args += list(router)
        second = (row(ne), jax.ShapeDtypeStruct((t, ne), F32))
    else:
        second = (row(D_MODEL), jax.ShapeDtypeStruct((t, D_MODEL), BF16))
    return pl.pallas_call(
        functools.partial(_mix_body, with_router=router is not None),
        grid=(t // ROW_TILE,),
        in_specs=in_specs, out_specs=[row(D_MODEL), second[0]],
        out_shape=[jax.ShapeDtypeStruct((t, D_MODEL), F32), second[1]],
        scratch_shapes=[pltpu.VMEM((SSM_WIDTH // LANES, ROW_TILE, LANES), F32)],
        compiler_params=_cparams(("parallel",)),
        name="mix_outproj",
    )(*args)


def _ffn_body(x_ref, h_ref, wg_ref, wu_ref, wd_ref, o_ref, acc):
    h = h_ref[...]
    acc[...] = x_ref[...]
    for c in range(D_FF // FF_CHUNK):
        sl = slice(c * FF_CHUNK, (c + 1) * FF_CHUNK)
        g = jnp.dot(h, wg_ref[:, sl], preferred_element_type=F32)
        u = jnp.dot(h, wu_ref[:, sl], preferred_element_type=F32)
        a = (jax.nn.silu(g) * u).astype(BF16)
        acc[...] += jnp.dot(a, wd_ref[sl, :], preferred_element_type=F32)
    o_ref[...] = acc[...]


def _dense_ffn(x1, hn, wg, wu, wd):
    t = x1.shape[0]
    row = lambda w: pl.BlockSpec((ROW_TILE, w), lambda i: (i, 0))
    return pl.pallas_call(
        _ffn_body,
        grid=(t // ROW_TILE,),
        in_specs=[row(D_MODEL), row(D_MODEL), _resident_spec((D_MODEL, D_FF)),
                  _resident_spec((D_MODEL, D_FF)), _resident_spec((D_FF, D_MODEL))],
        out_specs=row(D_MODEL),
        out_shape=jax.ShapeDtypeStruct((t, D_MODEL), F32),
        scratch_shapes=[pltpu.VMEM((ROW_TILE, D_MODEL), F32)],
        compiler_params=_cparams(("parallel",)),
        name="dense_ffn",
    )(x1, hn, wg, wu, wd)


def _row_copies(src, dst, idx_ref, base, sem, n, *, indexed_src):
    def body(r, carry):
        i = idx_ref[base + r]
        if indexed_src:
            pltpu.make_async_copy(src.at[pl.ds(i, 1)], dst.at[pl.ds(r, 1)], sem).start()
        else:
            pltpu.make_async_copy(src.at[pl.ds(r, 1)], dst.at[pl.ds(i, 1)], sem).start()
        return carry
    if isinstance(n, int):
        lax.fori_loop(0, n, body, 0, unroll=8)
    else:
        lax.fori_loop(0, n, body, 0)


def _expert_ffn_body(e_ref, nused_ref, nvalid_ref, tok_ref, dst_ref, x1_hbm, fg_ref, gate_ref,
                     wg_ref, wu_ref, wd_ref, out_hbm, xbuf, ybuf, gsem, ssem):
    b = pl.program_id(0)
    n_used = nused_ref[0]
    slot = b % 2

    def wait_gather(s):
        pltpu.make_async_copy(x1_hbm.at[pl.ds(0, MOE_ROWS)], xbuf.at[s], gsem.at[s]).wait()

    def wait_scatter(s, n):
        n8 = pl.multiple_of(n & -8, 8)

        @pl.when(n8 > 0)
        def _():
            pltpu.make_async_copy(ybuf.at[s, pl.ds(0, n8)], out_hbm.at[pl.ds(0, n8)], ssem.at[s]).wait()

        def one(_, carry):
            pltpu.make_async_copy(ybuf.at[s, pl.ds(0, 1)], out_hbm.at[pl.ds(0, 1)], ssem.at[s]).wait()
            return carry
        lax.fori_loop(0, n - n8, one, 0)

    @pl.when(b < n_used)
    def _():
        @pl.when(b == 0)
        def _():
            _row_copies(x1_hbm, xbuf.at[0], tok_ref, 0, gsem.at[0], MOE_ROWS, indexed_src=True)

        @pl.when(b + 1 < n_used)
        def _():
            _row_copies(x1_hbm, xbuf.at[1 - slot], tok_ref, (b + 1) * MOE_ROWS, gsem.at[1 - slot],
                        MOE_ROWS, indexed_src=True)

        wait_gather(slot)
        x = xbuf[slot]
        ms = jnp.mean(x * x, axis=-1, keepdims=True)
        h = (x * lax.rsqrt(ms + NORM_EPS) * fg_ref[...]).astype(BF16)

        @pl.when(b >= 2)
        def _():
            wait_scatter(slot, nvalid_ref[b - 2])

        yb = ybuf.at[slot]
        for c in range(D_FF // FF_CHUNK):
            sl = slice(c * FF_CHUNK, (c + 1) * FF_CHUNK)
            g = jnp.dot(h, wg_ref[0, :, sl], preferred_element_type=F32)
            u = jnp.dot(h, wu_ref[0, :, sl], preferred_element_type=F32)
            a = (jax.nn.silu(g) * u).astype(BF16)
            part = jnp.dot(a, wd_ref[0, sl, :], preferred_element_type=F32)
            if c == 0:
                yb[...] = part
            else:
                yb[...] += part
        yb[...] = yb[...] * gate_ref[0]
        _row_copies(ybuf.at[slot], out_hbm, dst_ref, b * MOE_ROWS, ssem.at[slot], nvalid_ref[b],
                    indexed_src=False)

        @pl.when(b == n_used - 1)
        def _():
            wait_scatter(slot, nvalid_ref[b])

            @pl.when(b >= 1)
            def _():
                wait_scatter(1 - slot, nvalid_ref[b - 1])


def _expert_ffn(x1, fg, tok, dst, gate, block_e, n_used, n_valid, wg, wu, wd, nblk, out_rows):
    wspec = lambda shape: pl.BlockSpec(shape, lambda b, e, *_: (e[b], 0, 0))
    return pl.pallas_call(
        _expert_ffn_body,
        grid_spec=pltpu.PrefetchScalarGridSpec(
            num_scalar_prefetch=5, grid=(nblk,),
            in_specs=[pl.BlockSpec(memory_space=pl.ANY),
                      pl.BlockSpec((1, D_MODEL), lambda b, *_: (0, 0)),
                      pl.BlockSpec((1, MOE_ROWS, 1), lambda b, *_: (b, 0, 0)),
                      wspec((1, D_MODEL, D_FF)), wspec((1, D_MODEL, D_FF)), wspec((1, D_FF, D_MODEL))],
            out_specs=pl.BlockSpec(memory_space=pl.ANY),
            scratch_shapes=[pltpu.VMEM((2, MOE_ROWS, D_MODEL), F32),
                            pltpu.VMEM((2, MOE_ROWS, D_MODEL), F32),
                            pltpu.SemaphoreType.DMA((2,)), pltpu.SemaphoreType.DMA((2,))]),
        out_shape=jax.ShapeDtypeStruct((out_rows, D_MODEL), F32),
        compiler_params=pltpu.CompilerParams(dimension_semantics=("arbitrary",),
                                             vmem_limit_bytes=MOE_VMEM_LIMIT),
        name="moe_expert_ffn",
    )(block_e, n_used, n_valid, tok, dst, x1, fg, gate.reshape(nblk, MOE_ROWS, 1), wg, wu, wd)


def _combine_body(x_ref, y_ref, o_ref):
    o_ref[...] = x_ref[...] + y_ref[:, :D_MODEL] + y_ref[:, D_MODEL:]


def _combine(x1, y2):
    t = x1.shape[0]
    return pl.pallas_call(
        _combine_body,
        grid=(t // ROW_TILE,),
        in_specs=[pl.BlockSpec((ROW_TILE, D_MODEL), lambda i: (i, 0)),
                  pl.BlockSpec((ROW_TILE, TOP_K * D_MODEL), lambda i: (i, 0))],
        out_specs=pl.BlockSpec((ROW_TILE, D_MODEL), lambda i: (i, 0)),
        out_shape=jax.ShapeDtypeStruct((t, D_MODEL), F32),
        compiler_params=_cparams(("parallel",)),
        name="moe_combine",
    )(x1, y2)


def _moe(x1, fg, logits, wg, wu, wd):
    t = x1.shape[0]
    n_assign = t * TOP_K
    nblk = n_assign // MOE_ROWS + N_EXPERTS
    cap = nblk * MOE_ROWS
    top_v, top_i = lax.top_k(logits, TOP_K)
    gates = jax.nn.softmax(top_v, axis=-1).reshape(-1)
    flat_e = top_i.reshape(-1).astype(jnp.int32)
    order = jnp.argsort(flat_e, stable=True).astype(jnp.int32)
    experts = jnp.arange(N_EXPERTS, dtype=jnp.int32)
    counts = jnp.sum((flat_e[:, None] == experts[None, :]).astype(jnp.int32), axis=0)
    start = jnp.cumsum(counts) - counts
    padded = ((counts + MOE_ROWS - 1) // MOE_ROWS) * MOE_ROWS
    pend = jnp.cumsum(padded)
    pstart = pend - padded
    p = jnp.arange(cap, dtype=jnp.int32)
    e_p = jnp.minimum(jnp.sum((p[:, None] >= pend[None, :]).astype(jnp.int32), axis=1), N_EXPERTS - 1)
    rank = p - pstart[e_p]
    valid = jnp.logical_and(rank < counts[e_p], p < pend[-1])
    a_p = order[jnp.clip(start[e_p] + rank, 0, n_assign - 1)]
    tok = jnp.where(valid, a_p // TOP_K, 0).astype(jnp.int32)
    dst = jnp.where(valid, a_p, 0).astype(jnp.int32)
    gate = jnp.where(valid, gates[a_p], 0.0).astype(F32)
    block_e = e_p[::MOE_ROWS]
    n_used = (pend[-1:] // MOE_ROWS).astype(jnp.int32)
    n_valid = jnp.sum(valid.reshape(nblk, MOE_ROWS).astype(jnp.int32), axis=1)
    y2 = _expert_ffn(x1, fg, tok, dst, gate, block_e, n_used, n_valid, wg, wu, wd, nblk, n_assign)
    return _combine(x1, y2.reshape(t, TOP_K * D_MODEL))


def kernel(x, attn_norm_g, w_in, q_norm_g, k_norm_g, sinks, lam_re, lam_im, log_dt, b_re, b_im,
           c_re, c_im, d_skip, w_glu, attn_out_g, ssm_out_g, w_o, ffn_norm_g, dense_wg, dense_wu,
           dense_wd, router_w, moe_wg, moe_wu, moe_wd):
    bsz, seq, _ = x.shape
    depth = w_in.shape[0]
    assert SSM_SEQS % bsz == 0 and seq % (SSM_CHUNK * (SSM_SEQS // bsz)) == 0
    assert seq % ROW_TILE == 0 and ROW_TILE % ATTN_BLOCK == 0
    x2d = x.reshape(bsz * seq, D_MODEL).astype(F32)
    head = jnp.arange(ATTN_WIDTH) // HEAD_DIM
    avg = jnp.where(head[:, None] == head[None, :], 1.0 / HEAD_DIM, 0.0).astype(BF16)
    eye = jnp.eye(ROW_TILE, dtype=BF16)
    n_steps = seq // (SSM_CHUNK * (SSM_SEQS // bsz))
    row = lambda v: v.astype(F32).reshape(1, -1)
    col = lambda v: v.astype(F32).reshape(-1, 1)
    ko, vo, uo = ATTN_WIDTH, ATTN_WIDTH + KV_WIDTH, ATTN_WIDTH + 2 * KV_WIDTH
    for l in range(depth):
        qg = jnp.tile(q_norm_g[l].astype(F32), N_HEADS) * (HEAD_DIM ** -0.5)
        kg = jnp.tile(k_norm_g[l].astype(F32), N_KV_HEADS)
        w = w_in[l].astype(BF16)
        qt, k, v3, u3 = _inproj(x2d, row(attn_norm_g[l]), w[:, :ko].T, w[:, ko:vo], w[:, vo:uo].T,
                                w[:, uo:], col(qg), row(kg), avg)
        attn = _attention(qt, k, v3, sinks[l].astype(F32), col(attn_out_g[l]), eye, seq)
        tables = _ssm_tables(lam_re[l], lam_im[l], log_dt[l], b_re[l], b_im[l], c_re[l], c_im[l],
                             d_skip[l], n_steps)
        y3 = _ssm(u3, tables, eye[:LANES, :LANES], bsz, seq)
        i = l // 2
        router = None
        if l % 2 == 1:
            rw = router_w[i].astype(F32)
            rw_hi = rw.astype(BF16)
            router = (rw_hi, (rw - rw_hi.astype(F32)).astype(BF16))
        x1, second = _mix(x2d, attn, y3, w_glu[l].astype(BF16), row(ssm_out_g[l]),
                          w_o[l].astype(BF16), row(ffn_norm_g[l]), router)
        if l % 2 == 0:
            x2d = _dense_ffn(x1, second, dense_wg[i].astype(BF16), dense_wu[i].astype(BF16),
                             dense_wd[i].astype(BF16))
        else:
            x2d = _moe(x1, row(ffn_norm_g[l]), second, moe_wg[i].astype(BF16), moe_wu[i].astype(BF16),
                       moe_wd[i].astype(BF16))
    return x2d.reshape(bsz, seq, D_MODEL)
```

```python
import functools
import math

import jax
import jax.numpy as jnp
from jax import lax
from jax.experimental import pallas as pl
from jax.experimental.pallas import tpu as pltpu

F32 = jnp.float32
BF16 = jnp.bfloat16

D_MODEL = 1024
HEAD_DIM = 64
N_HEADS = 8
N_KV_HEADS = 2
GQA = N_HEADS // N_KV_HEADS
ATTN_WIDTH = N_HEADS * HEAD_DIM
KV_WIDTH = N_KV_HEADS * HEAD_DIM
ATTN_BLOCK = 128
SSM_WIDTH = D_MODEL - ATTN_WIDTH
SSM_GROUP_DIM = 16
SSM_GROUPS = SSM_WIDTH // SSM_GROUP_DIM
SSM_STATE = 64
IN_WIDTH = ATTN_WIDTH + 2 * KV_WIDTH + SSM_WIDTH
D_FF = 3584
N_EXPERTS = 8
TOP_K = 2
NORM_EPS = 1e-6
NEG_INF = -1e30

LANES = 128
ROW_TILE = 512
SSM_CHUNK = 16
SSM_SEQS = 8
SSM_VEC = SSM_CHUNK * SSM_GROUP_DIM
SSM_LANE_GROUPS = LANES // SSM_GROUP_DIM
FF_CHUNK = 512
MOE_ROWS = 256
VMEM_LIMIT = 56 * 1024 * 1024
MOE_VMEM_LIMIT = 60 * 1024 * 1024

NT_DIMS = (((1,), (1,)), ((), ()))


def _cparams(sem, vmem=VMEM_LIMIT):
    return pltpu.CompilerParams(dimension_semantics=sem, vmem_limit_bytes=vmem)


def _const_spec(shape):
    n = len(shape)
    return pl.BlockSpec(shape, lambda *_: (0,) * n)


def _resident_spec(shape):
    n = len(shape)
    return pl.BlockSpec(shape, lambda *_: (0,) * n, pipeline_mode=pl.Buffered(1))


def _inproj_body(x_ref, g_ref, wqt_ref, wk_ref, wvt_ref, wu_ref, qg_ref, kg_ref, avg_ref,
                 qt_out, k_out, v3_out, u3_out, uscr):
    x = x_ref[...]
    ms = jnp.mean(x * x, axis=-1, keepdims=True)
    hn = (x * lax.rsqrt(ms + NORM_EPS) * g_ref[...]).astype(BF16)
    qt = lax.dot_general(wqt_ref[...], hn, NT_DIMS, preferred_element_type=F32)
    qms = jnp.dot(avg_ref[...], (qt * qt).astype(BF16), preferred_element_type=F32)
    qt_out[...] = (qt * lax.rsqrt(qms + NORM_EPS) * qg_ref[...]).astype(BF16)
    k = jnp.dot(hn, wk_ref[...], preferred_element_type=F32)
    kms = jnp.dot((k * k).astype(BF16), avg_ref[:KV_WIDTH, :KV_WIDTH], preferred_element_type=F32)
    k_out[...] = (k * lax.rsqrt(kms + NORM_EPS) * kg_ref[...]).astype(BF16)
    vt = lax.dot_general(wvt_ref[...], hn, NT_DIMS, preferred_element_type=F32)
    for b in range(ROW_TILE // ATTN_BLOCK):
        v3_out[b] = vt[:, b * ATTN_BLOCK:(b + 1) * ATTN_BLOCK].astype(BF16)
    u = jnp.dot(hn, wu_ref[...], preferred_element_type=F32)
    for lb in range(SSM_WIDTH // LANES):
        uscr[lb] = u[:, lb * LANES:(lb + 1) * LANES]
    for t in range(SSM_CHUNK):
        for lb in range(SSM_WIDTH // LANES):
            u3_out[t, :, lb * LANES:(lb + 1) * LANES] = (
                uscr[lb, pl.ds(t, ROW_TILE // SSM_CHUNK, stride=SSM_CHUNK), :].astype(BF16))


def _inproj(x2d, g, wqt, wk, wvt, wu, qg, kg, avg):
    t = x2d.shape[0]
    cpt = ROW_TILE // SSM_CHUNK
    return pl.pallas_call(
        _inproj_body,
        grid=(t // ROW_TILE,),
        in_specs=[pl.BlockSpec((ROW_TILE, D_MODEL), lambda i: (i, 0)),
                  _const_spec((1, D_MODEL)), _const_spec((ATTN_WIDTH, D_MODEL)),
                  _const_spec((D_MODEL, KV_WIDTH)), _const_spec((KV_WIDTH, D_MODEL)),
                  _const_spec((D_MODEL, SSM_WIDTH)), _const_spec((ATTN_WIDTH, 1)),
                  _const_spec((1, KV_WIDTH)), _const_spec((ATTN_WIDTH, ATTN_WIDTH))],
        out_specs=[pl.BlockSpec((ATTN_WIDTH, ROW_TILE), lambda i: (0, i)),
                   pl.BlockSpec((ROW_TILE, KV_WIDTH), lambda i: (i, 0)),
                   pl.BlockSpec((ROW_TILE // ATTN_BLOCK, KV_WIDTH, ATTN_BLOCK), lambda i: (i, 0, 0)),
                   pl.BlockSpec((SSM_CHUNK, cpt, SSM_WIDTH), lambda i: (0, i, 0))],
        out_shape=[jax.ShapeDtypeStruct((ATTN_WIDTH, t), BF16),
                   jax.ShapeDtypeStruct((t, KV_WIDTH), BF16),
                   jax.ShapeDtypeStruct((t // ATTN_BLOCK, KV_WIDTH, ATTN_BLOCK), BF16),
                   jax.ShapeDtypeStruct((SSM_CHUNK, t // SSM_CHUNK, SSM_WIDTH), BF16)],
        scratch_shapes=[pltpu.VMEM((SSM_WIDTH // LANES, ROW_TILE, LANES), F32)],
        compiler_params=_cparams(("parallel",)),
        name="inproj",
    )(x2d, g, wqt, wk, wvt, wu, qg, kg, avg)


def _attn_body(sink_ref, qt_ref, k_ref, v3_ref, g_ref, eye_ref, o_ref, at_scr, *, blocks_per_seq):
    i = pl.program_id(0)
    key = lax.broadcasted_iota(jnp.int32, (ATTN_BLOCK, ATTN_BLOCK), 0)
    qry = lax.broadcasted_iota(jnp.int32, (ATTN_BLOCK, ATTN_BLOCK), 1)
    cur_ok = key <= qry
    prev_ok = key > qry
    zpad = jnp.zeros((HEAD_DIM, ATTN_BLOCK), BF16)
    nblk = ROW_TILE // ATTN_BLOCK
    for blk in range(nblk):
        gblk = i * nblk + blk
        pblk = jnp.maximum(gblk - 1, 0)
        row0 = pl.multiple_of(gblk * ATTN_BLOCK, ATTN_BLOCK)
        prev0 = pl.multiple_of(pblk * ATTN_BLOCK, ATTN_BLOCK)
        has_prev = (gblk % blocks_per_seq) != 0
        kc = k_ref[pl.ds(row0, ATTN_BLOCK), :]
        kp = k_ref[pl.ds(prev0, ATTN_BLOCK), :]
        vc = v3_ref[gblk]
        vp = v3_ref[pblk]
        qb = qt_ref[:, blk * ATTN_BLOCK:(blk + 1) * ATTN_BLOCK]
        pmask = jnp.logical_and(prev_ok, has_prev)
        outs = []
        for h in range(N_HEADS):
            kv = h // GQA
            qh = qb[h * HEAD_DIM:(h + 1) * HEAD_DIM, :]
            qpad = jnp.concatenate([qh, zpad] if kv == 0 else [zpad, qh], axis=0)
            sc = jnp.dot(kc, qpad, preferred_element_type=F32)
            sp = jnp.dot(kp, qpad, preferred_element_type=F32)
            sc = jnp.where(cur_ok, sc, NEG_INF)
            sp = jnp.where(pmask, sp, NEG_INF)
            sink = sink_ref[h]
            m = jnp.maximum(jnp.maximum(jnp.max(sc, axis=0, keepdims=True),
                                        jnp.max(sp, axis=0, keepdims=True)), sink)
            pc = jnp.exp(sc - m)
            pp = jnp.exp(sp - m)
            den = (jnp.sum(pc, axis=0, keepdims=True) + jnp.sum(pp, axis=0, keepdims=True)
                   + jnp.exp(sink - m))
            vs = slice(kv * HEAD_DIM, (kv + 1) * HEAD_DIM)
            o = (jnp.dot(vc[vs, :], pc.astype(BF16), preferred_element_type=F32)
                 + jnp.dot(vp[vs, :], pp.astype(BF16), preferred_element_type=F32))
            outs.append(o / den)
        a = jnp.concatenate(outs, axis=0)
        ms = jnp.mean(a * a, axis=0, keepdims=True)
        at_scr[:, blk * ATTN_BLOCK:(blk + 1) * ATTN_BLOCK] = (
            a * lax.rsqrt(ms + NORM_EPS) * g_ref[...]).astype(BF16)
    o_ref[...] = lax.dot_general(eye_ref[...], at_scr[...], NT_DIMS,
                                 preferred_element_type=F32).astype(BF16)


def _attention(qt, k, v3, sinks, gcol, eye, seq):
    t = k.shape[0]
    return pl.pallas_call(
        functools.partial(_attn_body, blocks_per_seq=seq // ATTN_BLOCK),
        grid=(t // ROW_TILE,),
        in_specs=[pl.BlockSpec(memory_space=pltpu.SMEM),
                  pl.BlockSpec((ATTN_WIDTH, ROW_TILE), lambda i: (0, i)),
                  _const_spec((t, KV_WIDTH)), _const_spec((t // ATTN_BLOCK, KV_WIDTH, ATTN_BLOCK)),
                  _const_spec((ATTN_WIDTH, 1)), _const_spec((ROW_TILE, ROW_TILE))],
        out_specs=pl.BlockSpec((ROW_TILE, ATTN_WIDTH), lambda i: (i, 0)),
        out_shape=jax.ShapeDtypeStruct((t, ATTN_WIDTH), BF16),
        scratch_shapes=[pltpu.VMEM((ATTN_WIDTH, ROW_TILE), BF16)],
        compiler_params=_cparams(("parallel",)),
        name="swa_attention",
    )(sinks, qt, k, v3, gcol, eye)


def _ssm_tables(lam_re, lam_im, log_dt, b_re, b_im, c_re, c_im, d_skip, n_steps):
    hi = lax.Precision.HIGHEST
    g_, n_, c_ = SSM_GROUPS, SSM_STATE, SSM_GROUP_DIM
    lr, li = lam_re.astype(F32), lam_im.astype(F32)
    dt = jnp.exp(log_dt.astype(F32))[:, None]
    mag = jnp.exp(lr * dt)
    ang = li * dt
    ab_re, ab_im = mag * jnp.cos(ang), mag * jnp.sin(ang)
    nr = ab_re - 1.0
    den = lr * lr + li * li
    f_re = (nr * lr + ab_im * li) / den
    f_im = (ab_im * lr - nr * li) / den
    br, bi = b_re.astype(F32), b_im.astype(F32)
    bb_re = f_re[..., None] * br - f_im[..., None] * bi
    bb_im = f_re[..., None] * bi + f_im[..., None] * br
    cr, ci = c_re.astype(F32), c_im.astype(F32)

    def apow(p):
        p = p.astype(F32)[None, :, None]
        m = jnp.exp(lr[:, None, :] * dt[:, :, None] * p)
        an = li[:, None, :] * dt[:, :, None] * p
        return m * jnp.cos(an), m * jnp.sin(an)

    ch = SSM_CHUNK
    pr, pi = apow(jnp.arange(ch + 1))
    ca_re = cr[:, None] * pr[:, :ch, None, :] - ci[:, None] * pi[:, :ch, None, :]
    ca_im = cr[:, None] * pi[:, :ch, None, :] + ci[:, None] * pr[:, :ch, None, :]
    kk = (jnp.einsum('gtcn,gnd->gtcd', ca_re, bb_re, precision=hi)
          - jnp.einsum('gtcn,gnd->gtcd', ca_im, bb_im, precision=hi))
    tt = jnp.arange(ch)[:, None]
    jj = jnp.arange(ch)[None, :]
    tau = tt - jj
    m5 = jnp.where((tau >= 0)[None, :, :, None, None], kk[:, jnp.clip(tau, 0, ch - 1)], 0.0)
    m = m5.transpose(0, 1, 3, 2, 4).reshape(g_, ch * c_, ch * c_)
    rr, ri = pr[:, ch - 1::-1][:, :ch], pi[:, ch - 1::-1][:, :ch]
    bt_re, bt_im = bb_re.transpose(0, 2, 1)[:, None], bb_im.transpose(0, 2, 1)[:, None]
    w_re = (rr[:, :, None, :] * bt_re - ri[:, :, None, :] * bt_im).reshape(g_, ch * c_, n_)
    w_im = (rr[:, :, None, :] * bt_im + ri[:, :, None, :] * bt_re).reshape(g_, ch * c_, n_)
    w_re, w_im = w_re.transpose(0, 2, 1), w_im.transpose(0, 2, 1)
    v_re = (cr[:, None] * pr[:, 1:, None, :] - ci[:, None] * pi[:, 1:, None, :])
    v_im = -(cr[:, None] * pi[:, 1:, None, :] + ci[:, None] * pr[:, 1:, None, :])
    v_re = v_re.reshape(g_, ch * c_, n_)
    v_im = v_im.reshape(g_, ch * c_, n_)
    np_ = g_ // 2
    z = jnp.zeros_like(w_re[0::2])
    wt_re = jnp.concatenate([jnp.concatenate([w_re[0::2], z], axis=2),
                             jnp.concatenate([z, w_re[1::2]], axis=2)], axis=1)
    wt_im = jnp.concatenate([jnp.concatenate([w_im[0::2], z], axis=2),
                             jnp.concatenate([z, w_im[1::2]], axis=2)], axis=1)
    zv = jnp.zeros_like(v_re)
    even = (jnp.arange(g_) % 2 == 0)[:, None, None]
    vt_re = jnp.where(even, jnp.concatenate([v_re, zv], axis=2), jnp.concatenate([zv, v_re], axis=2))
    vt_im = jnp.where(even, jnp.concatenate([v_im, zv], axis=2), jnp.concatenate([zv, v_im], axis=2))
    pair = lambda x: x.reshape(np_, 2, *x.shape[1:]).swapaxes(1, 2).reshape(np_, x.shape[1], 2 * n_)
    qr, qi = apow(jnp.arange(n_steps) * ch)
    sr, si = apow(jnp.array([ch, ch * n_steps]))
    consts = jnp.stack([pair(sr), pair(si)], axis=1)
    apw = jnp.stack([pair(qr), pair(qi)], axis=1)
    dcol = jnp.tile(d_skip.astype(F32).reshape(g_, 1, c_), (1, ch, 1)).reshape(g_, ch * c_, 1)
    return (m.astype(BF16), wt_re.astype(BF16), wt_im.astype(BF16),
            vt_re.astype(BF16), vt_im.astype(BF16), consts, apw, dcol)


def _ssm_body(u_ref, eye_ref, m_ref, wre_ref, wim_ref, vre_ref, vim_ref, c_ref, apw_ref, d_ref,
              y_ref, ut, yt, pre, pim, sre, sim, *, n_steps, halves):
    lanes = 2 * SSM_STATE
    nk = u_ref.shape[1]
    gd = SSM_GROUP_DIM
    for t in range(SSM_CHUNK):
        xt = lax.dot_general(eye_ref[...], u_ref[t], NT_DIMS,
                             preferred_element_type=F32).astype(BF16)
        for g in range(SSM_LANE_GROUPS):
            ut[g, t * gd:(t + 1) * gd, :] = xt[g * gd:(g + 1) * gd, :]

    row = lax.broadcasted_iota(jnp.int32, (SSM_SEQS, lanes), 0)
    seq_start = (row % halves) == 0
    shift = lambda x: jnp.where(seq_start, 0.0, pltpu.roll(x, 1, axis=0))

    for pp in range(SSM_LANE_GROUPS // 2):
        ucat = jnp.concatenate([ut[2 * pp], ut[2 * pp + 1]], axis=0)
        pre[...] = jnp.dot(wre_ref[pp], ucat, preferred_element_type=F32).T
        pim[...] = jnp.dot(wim_ref[pp], ucat, preferred_element_type=F32).T
        ar = jnp.broadcast_to(c_ref[pp, 0, 0:1, :], (SSM_SEQS, lanes))
        ai = jnp.broadcast_to(c_ref[pp, 1, 0:1, :], (SSM_SEQS, lanes))

        def step(i, carry):
            s_re, s_im = carry
            idx = pl.ds(i, SSM_SEQS, stride=n_steps)
            sre[idx, :] = s_re
            sim[idx, :] = s_im
            p_r = pre[idx, :]
            p_i = pim[idx, :]
            return ar * s_re - ai * s_im + p_r, ar * s_im + ai * s_re + p_i

        zero = jnp.zeros((SSM_SEQS, lanes), F32)
        e_re, e_im = lax.fori_loop(0, n_steps, step, (zero, zero))

        br = jnp.broadcast_to(c_ref[pp, 0, 1:2, :], (SSM_SEQS, lanes))
        bi = jnp.broadcast_to(c_ref[pp, 1, 1:2, :], (SSM_SEQS, lanes))
        c_re = jnp.zeros_like(e_re)
        c_im = jnp.zeros_like(e_im)
        for _ in range(halves - 1):
            n_re = e_re + br * c_re - bi * c_im
            n_im = e_im + br * c_im + bi * c_re
            c_re, c_im = shift(n_re), shift(n_im)

        apr = apw_ref[pp, 0][None, :, :]
        api = apw_ref[pp, 1][None, :, :]
        s3r = sre[...].reshape(SSM_SEQS, n_steps, lanes)
        s3i = sim[...].reshape(SSM_SEQS, n_steps, lanes)
        t_re = (s3r + apr * c_re[:, None, :] - api * c_im[:, None, :]).reshape(nk, lanes)
        t_im = (s3i + apr * c_im[:, None, :] + api * c_re[:, None, :]).reshape(nk, lanes)
        tr_hi = t_re.astype(BF16)
        tr_lo = (t_re - tr_hi.astype(F32)).astype(BF16)
        ti_hi = t_im.astype(BF16)
        ti_lo = (t_im - ti_hi.astype(F32)).astype(BF16)
        for g in (2 * pp, 2 * pp + 1):
            ug = ut[g]
            y = jnp.dot(m_ref[g], ug, preferred_element_type=F32)
            y += lax.dot_general(vre_ref[g], tr_hi, NT_DIMS, preferred_element_type=F32)
            y += lax.dot_general(vre_ref[g], tr_lo, NT_DIMS, preferred_element_type=F32)
            y += lax.dot_general(vim_ref[g], ti_hi, NT_DIMS, preferred_element_type=F32)
            y += lax.dot_general(vim_ref[g], ti_lo, NT_DIMS, preferred_element_type=F32)
            yt[g] = y + d_ref[g] * ug.astype(F32)

    for t in range(SSM_CHUNK):
        rows = jnp.concatenate([yt[g, t * gd:(t + 1) * gd, :] for g in range(SSM_LANE_GROUPS)], axis=0)
        y_ref[t] = rows.T


def _ssm(u3, tables, eye, bsz, seq):
    m, wre, wim, vre, vim, consts, apw, dcol = tables
    nk = u3.shape[1]
    halves = SSM_SEQS // bsz
    n_steps = seq // (SSM_CHUNK * halves)
    lanes = 2 * SSM_STATE
    lg = SSM_LANE_GROUPS
    lead = lambda shape: pl.BlockSpec(shape, lambda b: (b,) + (0,) * (len(shape) - 1))
    return pl.pallas_call(
        functools.partial(_ssm_body, n_steps=n_steps, halves=halves),
        grid=(SSM_GROUPS // lg,),
        in_specs=[pl.BlockSpec((SSM_CHUNK, nk, LANES), lambda b: (0, 0, b)),
                  _const_spec((LANES, LANES)),
                  lead((lg, SSM_VEC, SSM_VEC)),
                  lead((lg // 2, lanes, 2 * SSM_VEC)), lead((lg // 2, lanes, 2 * SSM_VEC)),
                  lead((lg, SSM_VEC, lanes)), lead((lg, SSM_VEC, lanes)),
                  lead((lg // 2, 2, 2, lanes)), lead((lg // 2, 2, n_steps, lanes)),
                  lead((lg, SSM_VEC, 1))],
        out_specs=pl.BlockSpec((SSM_CHUNK, nk, LANES), lambda b: (0, 0, b)),
        out_shape=jax.ShapeDtypeStruct((SSM_CHUNK, nk, SSM_WIDTH), F32),
        scratch_shapes=[pltpu.VMEM((lg, SSM_VEC, nk), BF16), pltpu.VMEM((lg, SSM_VEC, nk), F32)]
                       + [pltpu.VMEM((nk, lanes), F32)] * 4,
        compiler_params=_cparams(("parallel",)),
        name="s5_ssm",
    )(u3, eye, m, wre, wim, vre, vim, consts, apw, dcol)


def _mix_body(*refs, with_router):
    if with_router:
        (x_ref, a_ref, y3_ref, wglu_ref, sg_ref, wo_ref, fg_ref, rw_hi_ref, rw_lo_ref,
         x1_out, lg_out, yscr) = refs
    else:
        x_ref, a_ref, y3_ref, wglu_ref, sg_ref, wo_ref, fg_ref, x1_out, hn_out, yscr = refs
    nlb = SSM_WIDTH // LANES
    for t in range(SSM_CHUNK):
        for lb in range(nlb):
            yscr[lb, pl.ds(t, ROW_TILE // SSM_CHUNK, stride=SSM_CHUNK), :] = (
                y3_ref[t, :, lb * LANES:(lb + 1) * LANES])
    y = jax.nn.gelu(jnp.concatenate([yscr[lb] for lb in range(nlb)], axis=1))
    z = y * jax.nn.sigmoid(jnp.dot(y.astype(BF16), wglu_ref[...], preferred_element_type=F32))
    ms = jnp.mean(z * z, axis=-1, keepdims=True)
    sn = (z * lax.rsqrt(ms + NORM_EPS) * sg_ref[...]).astype(BF16)
    x1 = (x_ref[...]
          + jnp.dot(a_ref[...], wo_ref[:ATTN_WIDTH, :], preferred_element_type=F32)
          + jnp.dot(sn, wo_ref[ATTN_WIDTH:, :], preferred_element_type=F32))
    x1_out[...] = x1
    ms1 = jnp.mean(x1 * x1, axis=-1, keepdims=True)
    hn = x1 * lax.rsqrt(ms1 + NORM_EPS) * fg_ref[...]
    hn_hi = hn.astype(BF16)
    if with_router:
        hn_lo = (hn - hn_hi.astype(F32)).astype(BF16)
        lg_out[...] = (jnp.dot(hn_hi, rw_hi_ref[...], preferred_element_type=F32)
                       + jnp.dot(hn_hi, rw_lo_ref[...], preferred_element_type=F32)
                       + jnp.dot(hn_lo, rw_hi_ref[...], preferred_element_type=F32))
    else:
        hn_out[...] = hn_hi


def _mix(x2d, attn, y3, wglu, sg, wo, fg, router=None):
    t = x2d.shape[0]
    cpt = ROW_TILE // SSM_CHUNK
    row = lambda w: pl.BlockSpec((ROW_TILE, w), lambda i: (i, 0))
    in_specs = [row(D_MODEL), row(ATTN_WIDTH),
                pl.BlockSpec((SSM_CHUNK, cpt, SSM_WIDTH), lambda i: (0, i, 0)),
                _const_spec((SSM_WIDTH, SSM_WIDTH)), _const_spec((1, SSM_WIDTH)),
                _const_spec((D_MODEL, D_MODEL)), _const_spec((1, D_MODEL))]
    args = [x2d, attn, y3, wglu, sg, wo, fg]
    if router is not None:
        ne = router[0].shape[1]
        in_specs += [_const_spec((D_MODEL, ne)), _const_spec((D_MODEL, ne))]
        args += list(router)
        second = (row(ne), jax.ShapeDtypeStruct((t, ne), F32))
    else:
        second = (row(D_MODEL), jax.ShapeDtypeStruct((t, D_MODEL), BF16))
    return pl.pallas_call(
        functools.partial(_mix_body, with_router=router is not None),
        grid=(t // ROW_TILE,),
        in_specs=in_specs, out_specs=[row(D_MODEL), second[0]],
        out_shape=[jax.ShapeDtypeStruct((t, D_MODEL), F32), second[1]],
        scratch_shapes=[pltpu.VMEM((SSM_WIDTH // LANES, ROW_TILE, LANES), F32)],
        compiler_params=_cparams(("parallel",)),
        name="mix_outproj",
    )(*args)


def _ffn_body(x_ref, h_ref, wg_ref, wu_ref, wd_ref, o_ref, acc):
    h = h_ref[...]
    acc[...] = x_ref[...]
    for c in range(D_FF // FF_CHUNK):
        sl = slice(c * FF_CHUNK, (c + 1) * FF_CHUNK)
        g = jnp.dot(h, wg_ref[:, sl], preferred_element_type=F32)
        u = jnp.dot(h, wu_ref[:, sl], preferred_element_type=F32)
        a = (jax.nn.silu(g) * u).astype(BF16)
        acc[...] += jnp.dot(a, wd_ref[sl, :], preferred_element_type=F32)
    o_ref[...] = acc[...]


def _dense_ffn(x1, hn, wg, wu, wd):
    t = x1.shape[0]
    row = lambda w: pl.BlockSpec((ROW_TILE, w), lambda i: (i, 0))
    return pl.pallas_call(
        _ffn_body,
        grid=(t // ROW_TILE,),
        in_specs=[row(D_MODEL), row(D_MODEL), _resident_spec((D_MODEL, D_FF)),
                  _resident_spec((D_MODEL, D_FF)), _resident_spec((D_FF, D_MODEL))],
        out_specs=row(D_MODEL),
        out_shape=jax.ShapeDtypeStruct((t, D_MODEL), F32),
        scratch_shapes=[pltpu.VMEM((ROW_TILE, D_MODEL), F32)],
        compiler_params=_cparams(("parallel",)),
        name="dense_ffn",
    )(x1, hn, wg, wu, wd)


def _expert_ffn_body(e_ref, nused_ref, nvalid_ref, tok_ref, dst_ref, x1_hbm, fg_ref, gate_ref,
                     wg_ref, wu_ref, wd_ref, out_hbm, xbuf, ybuf, gsem, ssem):
    b = pl.program_id(0)
    n_used = nused_ref[0]
    slot = b % 2
    other = 1 - slot
    n_ff = D_FF // FF_CHUNK
    rows_per_chunk = -(-MOE_ROWS // n_ff)

    def gather_row(blk, s, r):
        i = tok_ref[blk * MOE_ROWS + r]
        pltpu.make_async_copy(x1_hbm.at[pl.ds(i, 1)], xbuf.at[s, pl.ds(r, 1)], gsem.at[s]).start()

    def scatter_row(blk, s, r):
        i = dst_ref[blk * MOE_ROWS + r]
        pltpu.make_async_copy(ybuf.at[s, pl.ds(r, 1)], out_hbm.at[pl.ds(i, 1)], ssem.at[s]).start()

    def scatter_rows(blk, s, n):
        def body(r, carry):
            scatter_row(blk, s, r)
            return carry
        lax.fori_loop(0, n, body, 0)

    def wait_gather(s):
        pltpu.make_async_copy(x1_hbm.at[pl.ds(0, MOE_ROWS)], xbuf.at[s], gsem.at[s]).wait()

    def wait_scatter(s, n):
        n8 = pl.multiple_of(n & -8, 8)

        @pl.when(n8 > 0)
        def _():
            pltpu.make_async_copy(ybuf.at[s, pl.ds(0, n8)], out_hbm.at[pl.ds(0, n8)], ssem.at[s]).wait()

        def one(_, carry):
            pltpu.make_async_copy(ybuf.at[s, pl.ds(0, 1)], out_hbm.at[pl.ds(0, 1)], ssem.at[s]).wait()
            return carry
        lax.fori_loop(0, n - n8, one, 0)

    def ffn(scatter_prev):
        wait_gather(slot)
        x = xbuf[slot]
        ms = jnp.mean(x * x, axis=-1, keepdims=True)
        h = (x * lax.rsqrt(ms + NORM_EPS) * fg_ref[...]).astype(BF16)
        yb = ybuf.at[slot]
        for c in range(n_ff):
            for r in range(c * rows_per_chunk, min((c + 1) * rows_per_chunk, MOE_ROWS)):
                gather_row(b + 1, other, r)
                if scatter_prev:
                    scatter_row(b - 1, other, r)
            sl = slice(c * FF_CHUNK, (c + 1) * FF_CHUNK)
            g = jnp.dot(h, wg_ref[0, :, sl], preferred_element_type=F32)
            u = jnp.dot(h, wu_ref[0, :, sl], preferred_element_type=F32)
            a = (jax.nn.silu(g) * u).astype(BF16)
            part = jnp.dot(a, wd_ref[0, sl, :], preferred_element_type=F32)
            if c == 0:
                yb[...] = part
            else:
                yb[...] += part
        yb[...] = yb[...] * gate_ref[0]

    @pl.when(b < n_used)
    def _():
        @pl.when(b == 0)
        def _():
            def body(r, carry):
                gather_row(0, 0, r)
                return carry
            lax.fori_loop(0, MOE_ROWS, body, 0, unroll=8)

        @pl.when(b >= 2)
        def _():
            wait_scatter(slot, nvalid_ref[b - 2])

        n_prev = jnp.where(b >= 1, nvalid_ref[jnp.maximum(b - 1, 0)], 0)

        @pl.when(n_prev == MOE_ROWS)
        def _():
            ffn(scatter_prev=True)

        @pl.when(n_prev != MOE_ROWS)
        def _():
            scatter_rows(b - 1, other, n_prev)
            ffn(scatter_prev=False)

        @pl.when(b == n_used - 1)
        def _():
            scatter_rows(b, slot, nvalid_ref[b])
            wait_scatter(slot, nvalid_ref[b])
            wait_scatter(other, n_prev)
            wait_gather(other)


def _expert_ffn(x1, fg, tok, dst, gate, block_e, n_used, n_valid, wg, wu, wd, nblk, out_rows):
    wspec = lambda shape: pl.BlockSpec(shape, lambda b, e, *_: (e[b], 0, 0))
    return pl.pallas_call(
        _expert_ffn_body,
        grid_spec=pltpu.PrefetchScalarGridSpec(
            num_scalar_prefetch=5, grid=(nblk,),
            in_specs=[pl.BlockSpec(memory_space=pl.ANY),
                      pl.BlockSpec((1, D_MODEL), lambda b, *_: (0, 0)),
                      pl.BlockSpec((1, MOE_ROWS, 1), lambda b, *_: (b, 0, 0)),
                      wspec((1, D_MODEL, D_FF)), wspec((1, D_MODEL, D_FF)), wspec((1, D_FF, D_MODEL))],
            out_specs=pl.BlockSpec(memory_space=pl.ANY),
            scratch_shapes=[pltpu.VMEM((2, MOE_ROWS, D_MODEL), F32),
                            pltpu.VMEM((2, MOE_ROWS, D_MODEL), F32),
                            pltpu.SemaphoreType.DMA((2,)), pltpu.SemaphoreType.DMA((2,))]),
        out_shape=jax.ShapeDtypeStruct((out_rows, D_MODEL), F32),
        compiler_params=pltpu.CompilerParams(dimension_semantics=("arbitrary",),
                                             vmem_limit_bytes=MOE_VMEM_LIMIT),
        name="moe_expert_ffn",
    )(block_e, n_used, n_valid, tok, dst, x1, fg, gate.reshape(nblk, MOE_ROWS, 1), wg, wu, wd)


def _combine_body(x_ref, y0_ref, y1_ref, o_ref):
    o_ref[...] = x_ref[...] + y0_ref[...] + y1_ref[...]


def _combine(x1, y2):
    t = x1.shape[0]
    nt = t // ROW_TILE
    return pl.pallas_call(
        _combine_body,
        grid=(nt,),
        in_specs=[pl.BlockSpec((ROW_TILE, D_MODEL), lambda i: (i, 0)),
                  pl.BlockSpec((ROW_TILE, D_MODEL), lambda i: (i, 0)),
                  pl.BlockSpec((ROW_TILE, D_MODEL), lambda i: (i + nt, 0))],
        out_specs=pl.BlockSpec((ROW_TILE, D_MODEL), lambda i: (i, 0)),
        out_shape=jax.ShapeDtypeStruct((t, D_MODEL), F32),
        compiler_params=_cparams(("parallel",)),
        name="moe_combine",
    )(x1, y2, y2)


def _moe(x1, fg, logits, wg, wu, wd):
    t = x1.shape[0]
    n_assign = t * TOP_K
    nblk = n_assign // MOE_ROWS + N_EXPERTS
    cap = nblk * MOE_ROWS
    top_v, top_i = lax.top_k(logits, TOP_K)
    gates = jax.nn.softmax(top_v, axis=-1).reshape(-1)
    flat_e = top_i.reshape(-1).astype(jnp.int32)
    order = jnp.argsort(flat_e, stable=True).astype(jnp.int32)
    experts = jnp.arange(N_EXPERTS, dtype=jnp.int32)
    counts = jnp.sum((flat_e[:, None] == experts[None, :]).astype(jnp.int32), axis=0)
    start = jnp.cumsum(counts) - counts
    padded = ((counts + MOE_ROWS - 1) // MOE_ROWS) * MOE_ROWS
    pend = jnp.cumsum(padded)
    pstart = pend - padded
    p = jnp.arange(cap, dtype=jnp.int32)
    e_p = jnp.minimum(jnp.sum((p[:, None] >= pend[None, :]).astype(jnp.int32), axis=1), N_EXPERTS - 1)
    rank = p - pstart[e_p]
    valid = jnp.logical_and(rank < counts[e_p], p < pend[-1])
    a_p = order[jnp.clip(start[e_p] + rank, 0, n_assign - 1)]
    tok = jnp.where(valid, a_p // TOP_K, 0).astype(jnp.int32)
    dst = jnp.where(valid, (a_p % TOP_K) * t + a_p // TOP_K, 0).astype(jnp.int32)
    gate = jnp.where(valid, gates[a_p], 0.0).astype(F32)
    block_e = e_p[::MOE_ROWS]
    n_used = (pend[-1:] // MOE_ROWS).astype(jnp.int32)
    n_valid = jnp.sum(valid.reshape(nblk, MOE_ROWS).astype(jnp.int32), axis=1)
    y2 = _expert_ffn(x1, fg, tok, dst, gate, block_e, n_used, n_valid, wg, wu, wd, nblk, n_assign)
    return _combine(x1, y2)


def kernel(x, attn_norm_g, w_in, q_norm_g, k_norm_g, sinks, lam_re, lam_im, log_dt, b_re, b_im,
           c_re, c_im, d_skip, w_glu, attn_out_g, ssm_out_g, w_o, ffn_norm_g, dense_wg, dense_wu,
           dense_wd, router_w, moe_wg, moe_wu, moe_wd):
    bsz, seq, _ = x.shape
    depth = w_in.shape[0]
    assert SSM_SEQS % bsz == 0 and seq % (SSM_CHUNK * (SSM_SEQS // bsz)) == 0
    assert seq % ROW_TILE == 0 and ROW_TILE % ATTN_BLOCK == 0
    x2d = x.reshape(bsz * seq, D_MODEL).astype(F32)
    head = jnp.arange(ATTN_WIDTH) // HEAD_DIM
    avg = jnp.where(head[:, None] == head[None, :], 1.0 / HEAD_DIM, 0.0).astype(BF16)
    eye = jnp.eye(ROW_TILE, dtype=BF16)
    n_steps = seq // (SSM_CHUNK * (SSM_SEQS // bsz))
    row = lambda v: v.astype(F32).reshape(1, -1)
    col = lambda v: v.astype(F32).reshape(-1, 1)
    ko, vo, uo = ATTN_WIDTH, ATTN_WIDTH + KV_WIDTH, ATTN_WIDTH + 2 * KV_WIDTH
    for l in range(depth):
        qg = jnp.tile(q_norm_g[l].astype(F32), N_HEADS) * (HEAD_DIM ** -0.5)
        kg = jnp.tile(k_norm_g[l].astype(F32), N_KV_HEADS)
        w = w_in[l].astype(BF16)
        qt, k, v3, u3 = _inproj(x2d, row(attn_norm_g[l]), w[:, :ko].T, w[:, ko:vo], w[:, vo:uo].T,
                                w[:, uo:], col(qg), row(kg), avg)
        attn = _attention(qt, k, v3, sinks[l].astype(F32), col(attn_out_g[l]), eye, seq)
        tables = _ssm_tables(lam_re[l], lam_im[l], log_dt[l], b_re[l], b_im[l], c_re[l], c_im[l],
                             d_skip[l], n_steps)
        y3 = _ssm(u3, tables, eye[:LANES, :LANES], bsz, seq)
        i = l // 2
        router = None
        if l % 2 == 1:
            rw = router_w[i].astype(F32)
            rw_hi = rw.astype(BF16)
            router = (rw_hi, (rw - rw_hi.astype(F32)).astype(BF16))
        x1, second = _mix(x2d, attn, y3, w_glu[l].astype(BF16), row(ssm_out_g[l]),
                          w_o[l].astype(BF16), row(ffn_norm_g[l]), router)
        if l % 2 == 0:
            x2d = _dense_ffn(x1, second, dense_wg[i].astype(BF16), dense_wu[i].astype(BF16),
                             dense_wd[i].astype(BF16))
        else:
            x2d = _moe(x1, row(ffn_norm_g[l]), second, moe_wg[i].astype(BF16), moe_wu[i].astype(BF16),
                       moe_wd[i].astype(BF16))
    return x2d.reshape(bsz, seq, D_MODEL)
```

```python
import functools
import math

import jax
import jax.numpy as jnp
from jax import lax
from jax.experimental import pallas as pl
from jax.experimental.pallas import tpu as pltpu

F32 = jnp.float32
BF16 = jnp.bfloat16

D_MODEL = 1024
HEAD_DIM = 64
N_HEADS = 8
N_KV_HEADS = 2
GQA = N_HEADS // N_KV_HEADS
ATTN_WIDTH = N_HEADS * HEAD_DIM
KV_WIDTH = N_KV_HEADS * HEAD_DIM
ATTN_BLOCK = 128
SSM_WIDTH = D_MODEL - ATTN_WIDTH
SSM_GROUP_DIM = 16
SSM_GROUPS = SSM_WIDTH // SSM_GROUP_DIM
SSM_STATE = 64
IN_WIDTH = ATTN_WIDTH + 2 * KV_WIDTH + SSM_WIDTH
D_FF = 3584
N_EXPERTS = 8
TOP_K = 2
NORM_EPS = 1e-6
NEG_INF = -1e30

LANES = 128
ROW_TILE = 512
SSM_CHUNK = 16
SSM_SEQS = 8
SSM_VEC = SSM_CHUNK * SSM_GROUP_DIM
SSM_LANE_GROUPS = LANES // SSM_GROUP_DIM
FF_CHUNK = 512
MOE_ROWS = 256
VMEM_LIMIT = 56 * 1024 * 1024
MOE_VMEM_LIMIT = 60 * 1024 * 1024

NT_DIMS = (((1,), (1,)), ((), ()))


def _cparams(sem, vmem=VMEM_LIMIT):
    return pltpu.CompilerParams(dimension_semantics=sem, vmem_limit_bytes=vmem)


def _const_spec(shape):
    n = len(shape)
    return pl.BlockSpec(shape, lambda *_: (0,) * n)


def _resident_spec(shape):
    n = len(shape)
    return pl.BlockSpec(shape, lambda *_: (0,) * n, pipeline_mode=pl.Buffered(1))


def _inproj_body(x_ref, g_ref, wqt_ref, wk_ref, wvt_ref, wu_ref, qg_ref, kg_ref, avg_ref,
                 qt_out, k_out, v3_out, u3_out, uscr):
    x = x_ref[...]
    ms = jnp.mean(x * x, axis=-1, keepdims=True)
    hn = (x * lax.rsqrt(ms + NORM_EPS) * g_ref[...]).astype(BF16)
    qt = lax.dot_general(wqt_ref[...], hn, NT_DIMS, preferred_element_type=F32)
    qms = jnp.dot(avg_ref[...], (qt * qt).astype(BF16), preferred_element_type=F32)
    qt_out[...] = (qt * lax.rsqrt(qms + NORM_EPS) * qg_ref[...]).astype(BF16)
    k = jnp.dot(hn, wk_ref[...], preferred_element_type=F32)
    kms = jnp.dot((k * k).astype(BF16), avg_ref[:KV_WIDTH, :KV_WIDTH], preferred_element_type=F32)
    k_out[...] = (k * lax.rsqrt(kms + NORM_EPS) * kg_ref[...]).astype(BF16)
    vt = lax.dot_general(wvt_ref[...], hn, NT_DIMS, preferred_element_type=F32)
    for b in range(ROW_TILE // ATTN_BLOCK):
        v3_out[b] = vt[:, b * ATTN_BLOCK:(b + 1) * ATTN_BLOCK].astype(BF16)
    u = jnp.dot(hn, wu_ref[...], preferred_element_type=F32)
    for lb in range(SSM_WIDTH // LANES):
        uscr[lb] = u[:, lb * LANES:(lb + 1) * LANES]
    for t in range(SSM_CHUNK):
        for lb in range(SSM_WIDTH // LANES):
            u3_out[t, :, lb * LANES:(lb + 1) * LANES] = (
                uscr[lb, pl.ds(t, ROW_TILE // SSM_CHUNK, stride=SSM_CHUNK), :].astype(BF16))


def _inproj(x2d, g, wqt, wk, wvt, wu, qg, kg, avg):
    t = x2d.shape[0]
    cpt = ROW_TILE // SSM_CHUNK
    return pl.pallas_call(
        _inproj_body,
        grid=(t // ROW_TILE,),
        in_specs=[pl.BlockSpec((ROW_TILE, D_MODEL), lambda i: (i, 0)),
                  _const_spec((1, D_MODEL)), _const_spec((ATTN_WIDTH, D_MODEL)),
                  _const_spec((D_MODEL, KV_WIDTH)), _const_spec((KV_WIDTH, D_MODEL)),
                  _const_spec((D_MODEL, SSM_WIDTH)), _const_spec((ATTN_WIDTH, 1)),
                  _const_spec((1, KV_WIDTH)), _const_spec((ATTN_WIDTH, ATTN_WIDTH))],
        out_specs=[pl.BlockSpec((ATTN_WIDTH, ROW_TILE), lambda i: (0, i)),
                   pl.BlockSpec((ROW_TILE, KV_WIDTH), lambda i: (i, 0)),
                   pl.BlockSpec((ROW_TILE // ATTN_BLOCK, KV_WIDTH, ATTN_BLOCK), lambda i: (i, 0, 0)),
                   pl.BlockSpec((SSM_CHUNK, cpt, SSM_WIDTH), lambda i: (0, i, 0))],
        out_shape=[jax.ShapeDtypeStruct((ATTN_WIDTH, t), BF16),
                   jax.ShapeDtypeStruct((t, KV_WIDTH), BF16),
                   jax.ShapeDtypeStruct((t // ATTN_BLOCK, KV_WIDTH, ATTN_BLOCK), BF16),
                   jax.ShapeDtypeStruct((SSM_CHUNK, t // SSM_CHUNK, SSM_WIDTH), BF16)],
        scratch_shapes=[pltpu.VMEM((SSM_WIDTH // LANES, ROW_TILE, LANES), F32)],
        compiler_params=_cparams(("parallel",)),
        name="inproj",
    )(x2d, g, wqt, wk, wvt, wu, qg, kg, avg)


def _attn_body(sink_ref, qt_ref, k_ref, v3_ref, g_ref, eye_ref, o_ref, at_scr, *, blocks_per_seq):
    i = pl.program_id(0)
    key = lax.broadcasted_iota(jnp.int32, (ATTN_BLOCK, ATTN_BLOCK), 0)
    qry = lax.broadcasted_iota(jnp.int32, (ATTN_BLOCK, ATTN_BLOCK), 1)
    cur_ok = key <= qry
    prev_ok = key > qry
    zpad = jnp.zeros((HEAD_DIM, ATTN_BLOCK), BF16)
    nblk = ROW_TILE // ATTN_BLOCK
    for blk in range(nblk):
        gblk = i * nblk + blk
        pblk = jnp.maximum(gblk - 1, 0)
        row0 = pl.multiple_of(gblk * ATTN_BLOCK, ATTN_BLOCK)
        prev0 = pl.multiple_of(pblk * ATTN_BLOCK, ATTN_BLOCK)
        has_prev = (gblk % blocks_per_seq) != 0
        kc = k_ref[pl.ds(row0, ATTN_BLOCK), :]
        kp = k_ref[pl.ds(prev0, ATTN_BLOCK), :]
        vc = v3_ref[gblk]
        vp = v3_ref[pblk]
        qb = qt_ref[:, blk * ATTN_BLOCK:(blk + 1) * ATTN_BLOCK]
        pmask = jnp.logical_and(prev_ok, has_prev)
        outs = []
        for h in range(N_HEADS):
            kv = h // GQA
            qh = qb[h * HEAD_DIM:(h + 1) * HEAD_DIM, :]
            qpad = jnp.concatenate([qh, zpad] if kv == 0 else [zpad, qh], axis=0)
            sc = jnp.dot(kc, qpad, preferred_element_type=F32)
            sp = jnp.dot(kp, qpad, preferred_element_type=F32)
            sc = jnp.where(cur_ok, sc, NEG_INF)
            sp = jnp.where(pmask, sp, NEG_INF)
            sink = sink_ref[h]
            m = jnp.maximum(jnp.maximum(jnp.max(sc, axis=0, keepdims=True),
                                        jnp.max(sp, axis=0, keepdims=True)), sink)
            pc = jnp.exp(sc - m)
            pp = jnp.exp(sp - m)
            den = (jnp.sum(pc, axis=0, keepdims=True) + jnp.sum(pp, axis=0, keepdims=True)
                   + jnp.exp(sink - m))
            vs = slice(kv * HEAD_DIM, (kv + 1) * HEAD_DIM)
            o = (jnp.dot(vc[vs, :], pc.astype(BF16), preferred_element_type=F32)
                 + jnp.dot(vp[vs, :], pp.astype(BF16), preferred_element_type=F32))
            outs.append(o / den)
        a = jnp.concatenate(outs, axis=0)
        ms = jnp.mean(a * a, axis=0, keepdims=True)
        at_scr[:, blk * ATTN_BLOCK:(blk + 1) * ATTN_BLOCK] = (
            a * lax.rsqrt(ms + NORM_EPS) * g_ref[...]).astype(BF16)
    o_ref[...] = lax.dot_general(eye_ref[...], at_scr[...], NT_DIMS,
                                 preferred_element_type=F32).astype(BF16)


def _attention(qt, k, v3, sinks, gcol, eye, seq):
    t = k.shape[0]
    return pl.pallas_call(
        functools.partial(_attn_body, blocks_per_seq=seq // ATTN_BLOCK),
        grid=(t // ROW_TILE,),
        in_specs=[pl.BlockSpec(memory_space=pltpu.SMEM),
                  pl.BlockSpec((ATTN_WIDTH, ROW_TILE), lambda i: (0, i)),
                  _const_spec((t, KV_WIDTH)), _const_spec((t // ATTN_BLOCK, KV_WIDTH, ATTN_BLOCK)),
                  _const_spec((ATTN_WIDTH, 1)), _const_spec((ROW_TILE, ROW_TILE))],
        out_specs=pl.BlockSpec((ROW_TILE, ATTN_WIDTH), lambda i: (i, 0)),
        out_shape=jax.ShapeDtypeStruct((t, ATTN_WIDTH), BF16),
        scratch_shapes=[pltpu.VMEM((ATTN_WIDTH, ROW_TILE), BF16)],
        compiler_params=_cparams(("parallel",)),
        name="swa_attention",
    )(sinks, qt, k, v3, gcol, eye)


def _ssm_tables(lam_re, lam_im, log_dt, b_re, b_im, c_re, c_im, d_skip, n_steps):
    hi = lax.Precision.HIGHEST
    g_, n_, c_ = SSM_GROUPS, SSM_STATE, SSM_GROUP_DIM
    lr, li = lam_re.astype(F32), lam_im.astype(F32)
    dt = jnp.exp(log_dt.astype(F32))[:, None]
    mag = jnp.exp(lr * dt)
    ang = li * dt
    ab_re, ab_im = mag * jnp.cos(ang), mag * jnp.sin(ang)
    nr = ab_re - 1.0
    den = lr * lr + li * li
    f_re = (nr * lr + ab_im * li) / den
    f_im = (ab_im * lr - nr * li) / den
    br, bi = b_re.astype(F32), b_im.astype(F32)
    bb_re = f_re[..., None] * br - f_im[..., None] * bi
    bb_im = f_re[..., None] * bi + f_im[..., None] * br
    cr, ci = c_re.astype(F32), c_im.astype(F32)

    def apow(p):
        p = p.astype(F32)[None, :, None]
        m = jnp.exp(lr[:, None, :] * dt[:, :, None] * p)
        an = li[:, None, :] * dt[:, :, None] * p
        return m * jnp.cos(an), m * jnp.sin(an)

    ch = SSM_CHUNK
    pr, pi = apow(jnp.arange(ch + 1))
    ca_re = cr[:, None] * pr[:, :ch, None, :] - ci[:, None] * pi[:, :ch, None, :]
    ca_im = cr[:, None] * pi[:, :ch, None, :] + ci[:, None] * pr[:, :ch, None, :]
    kk = (jnp.einsum('gtcn,gnd->gtcd', ca_re, bb_re, precision=hi)
          - jnp.einsum('gtcn,gnd->gtcd', ca_im, bb_im, precision=hi))
    tt = jnp.arange(ch)[:, None]
    jj = jnp.arange(ch)[None, :]
    tau = tt - jj
    m5 = jnp.where((tau >= 0)[None, :, :, None, None], kk[:, jnp.clip(tau, 0, ch - 1)], 0.0)
    m = m5.transpose(0, 1, 3, 2, 4).reshape(g_, ch * c_, ch * c_)
    rr, ri = pr[:, ch - 1::-1][:, :ch], pi[:, ch - 1::-1][:, :ch]
    bt_re, bt_im = bb_re.transpose(0, 2, 1)[:, None], bb_im.transpose(0, 2, 1)[:, None]
    w_re = (rr[:, :, None, :] * bt_re - ri[:, :, None, :] * bt_im).reshape(g_, ch * c_, n_)
    w_im = (rr[:, :, None, :] * bt_im + ri[:, :, None, :] * bt_re).reshape(g_, ch * c_, n_)
    w_re, w_im = w_re.transpose(0, 2, 1), w_im.transpose(0, 2, 1)
    v_re = (cr[:, None] * pr[:, 1:, None, :] - ci[:, None] * pi[:, 1:, None, :])
    v_im = -(cr[:, None] * pi[:, 1:, None, :] + ci[:, None] * pr[:, 1:, None, :])
    v_re = v_re.reshape(g_, ch * c_, n_)
    v_im = v_im.reshape(g_, ch * c_, n_)
    np_ = g_ // 2
    z = jnp.zeros_like(w_re[0::2])
    wt_re = jnp.concatenate([jnp.concatenate([w_re[0::2], z], axis=2),
                             jnp.concatenate([z, w_re[1::2]], axis=2)], axis=1)
    wt_im = jnp.concatenate([jnp.concatenate([w_im[0::2], z], axis=2),
                             jnp.concatenate([z, w_im[1::2]], axis=2)], axis=1)
    zv = jnp.zeros_like(v_re)
    even = (jnp.arange(g_) % 2 == 0)[:, None, None]
    vt_re = jnp.where(even, jnp.concatenate([v_re, zv], axis=2), jnp.concatenate([zv, v_re], axis=2))
    vt_im = jnp.where(even, jnp.concatenate([v_im, zv], axis=2), jnp.concatenate([zv, v_im], axis=2))
    pair = lambda x: x.reshape(np_, 2, *x.shape[1:]).swapaxes(1, 2).reshape(np_, x.shape[1], 2 * n_)
    qr, qi = apow(jnp.arange(n_steps) * ch)
    sr, si = apow(jnp.array([ch, ch * n_steps]))
    consts = jnp.stack([pair(sr), pair(si)], axis=1)
    apw = jnp.stack([pair(qr), pair(qi)], axis=1)
    dcol = jnp.tile(d_skip.astype(F32).reshape(g_, 1, c_), (1, ch, 1)).reshape(g_, ch * c_, 1)
    return (m.astype(BF16), wt_re.astype(BF16), wt_im.astype(BF16),
            vt_re.astype(BF16), vt_im.astype(BF16), consts, apw, dcol)


def _ssm_body(u_ref, eye_ref, m_ref, wre_ref, wim_ref, vre_ref, vim_ref, c_ref, apw_ref, d_ref,
              y_ref, ut, yt, pre, pim, sre, sim, *, n_steps, halves):
    lanes = 2 * SSM_STATE
    nk = u_ref.shape[1]
    gd = SSM_GROUP_DIM
    for t in range(SSM_CHUNK):
        xt = lax.dot_general(eye_ref[...], u_ref[t], NT_DIMS,
                             preferred_element_type=F32).astype(BF16)
        for g in range(SSM_LANE_GROUPS):
            ut[g, t * gd:(t + 1) * gd, :] = xt[g * gd:(g + 1) * gd, :]

    row = lax.broadcasted_iota(jnp.int32, (SSM_SEQS, lanes), 0)
    seq_start = (row % halves) == 0
    shift = lambda x: jnp.where(seq_start, 0.0, pltpu.roll(x, 1, axis=0))

    for pp in range(SSM_LANE_GROUPS // 2):
        ucat = jnp.concatenate([ut[2 * pp], ut[2 * pp + 1]], axis=0)
        pre[...] = jnp.dot(wre_ref[pp], ucat, preferred_element_type=F32).T
        pim[...] = jnp.dot(wim_ref[pp], ucat, preferred_element_type=F32).T
        ar = jnp.broadcast_to(c_ref[pp, 0, 0:1, :], (SSM_SEQS, lanes))
        ai = jnp.broadcast_to(c_ref[pp, 1, 0:1, :], (SSM_SEQS, lanes))

        def step(i, carry):
            s_re, s_im = carry
            idx = pl.ds(i, SSM_SEQS, stride=n_steps)
            sre[idx, :] = s_re
            sim[idx, :] = s_im
            p_r = pre[idx, :]
            p_i = pim[idx, :]
            return ar * s_re - ai * s_im + p_r, ar * s_im + ai * s_re + p_i

        zero = jnp.zeros((SSM_SEQS, lanes), F32)
        e_re, e_im = lax.fori_loop(0, n_steps, step, (zero, zero))

        br = jnp.broadcast_to(c_ref[pp, 0, 1:2, :], (SSM_SEQS, lanes))
        bi = jnp.broadcast_to(c_ref[pp, 1, 1:2, :], (SSM_SEQS, lanes))
        c_re = jnp.zeros_like(e_re)
        c_im = jnp.zeros_like(e_im)
        for _ in range(halves - 1):
            n_re = e_re + br * c_re - bi * c_im
            n_im = e_im + br * c_im + bi * c_re
            c_re, c_im = shift(n_re), shift(n_im)

        apr = apw_ref[pp, 0][None, :, :]
        api = apw_ref[pp, 1][None, :, :]
        s3r = sre[...].reshape(SSM_SEQS, n_steps, lanes)
        s3i = sim[...].reshape(SSM_SEQS, n_steps, lanes)
        t_re = (s3r + apr * c_re[:, None, :] - api * c_im[:, None, :]).reshape(nk, lanes)
        t_im = (s3i + apr * c_im[:, None, :] + api * c_re[:, None, :]).reshape(nk, lanes)
        tr_hi = t_re.astype(BF16)
        tr_lo = (t_re - tr_hi.astype(F32)).astype(BF16)
        ti_hi = t_im.astype(BF16)
        ti_lo = (t_im - ti_hi.astype(F32)).astype(BF16)
        for g in (2 * pp, 2 * pp + 1):
            ug = ut[g]
            y = jnp.dot(m_ref[g], ug, preferred_element_type=F32)
            y += lax.dot_general(vre_ref[g], tr_hi, NT_DIMS, preferred_element_type=F32)
            y += lax.dot_general(vre_ref[g], tr_lo, NT_DIMS, preferred_element_type=F32)
            y += lax.dot_general(vim_ref[g], ti_hi, NT_DIMS, preferred_element_type=F32)
            y += lax.dot_general(vim_ref[g], ti_lo, NT_DIMS, preferred_element_type=F32)
            yt[g] = y + d_ref[g] * ug.astype(F32)

    for t in range(SSM_CHUNK):
        rows = jnp.concatenate([yt[g, t * gd:(t + 1) * gd, :] for g in range(SSM_LANE_GROUPS)], axis=0)
        y_ref[t] = rows.T


def _ssm(u3, tables, eye, bsz, seq):
    m, wre, wim, vre, vim, consts, apw, dcol = tables
    nk = u3.shape[1]
    halves = SSM_SEQS // bsz
    n_steps = seq // (SSM_CHUNK * halves)
    lanes = 2 * SSM_STATE
    lg = SSM_LANE_GROUPS
    lead = lambda shape: pl.BlockSpec(shape, lambda b: (b,) + (0,) * (len(shape) - 1))
    return pl.pallas_call(
        functools.partial(_ssm_body, n_steps=n_steps, halves=halves),
        grid=(SSM_GROUPS // lg,),
        in_specs=[pl.BlockSpec((SSM_CHUNK, nk, LANES), lambda b: (0, 0, b)),
                  _const_spec((LANES, LANES)),
                  lead((lg, SSM_VEC, SSM_VEC)),
                  lead((lg // 2, lanes, 2 * SSM_VEC)), lead((lg // 2, lanes, 2 * SSM_VEC)),
                  lead((lg, SSM_VEC, lanes)), lead((lg, SSM_VEC, lanes)),
                  lead((lg // 2, 2, 2, lanes)), lead((lg // 2, 2, n_steps, lanes)),
                  lead((lg, SSM_VEC, 1))],
        out_specs=pl.BlockSpec((SSM_CHUNK, nk, LANES), lambda b: (0, 0, b)),
        out_shape=jax.ShapeDtypeStruct((SSM_CHUNK, nk, SSM_WIDTH), F32),
        scratch_shapes=[pltpu.VMEM((lg, SSM_VEC, nk), BF16), pltpu.VMEM((lg, SSM_VEC, nk), F32)]
                       + [pltpu.VMEM((nk, lanes), F32)] * 4,
        compiler_params=_cparams(("parallel",)),
        name="s5_ssm",
    )(u3, eye, m, wre, wim, vre, vim, consts, apw, dcol)


def _mix_body(*refs, with_router):
    if with_router:
        (x_ref, a_ref, y3_ref, wglu_ref, sg_ref, wo_ref, fg_ref, rw_hi_ref, rw_lo_ref,
         x1_out, lg_out, yscr) = refs
    else:
        x_ref, a_ref, y3_ref, wglu_ref, sg_ref, wo_ref, fg_ref, x1_out, hn_out, yscr = refs
    nlb = SSM_WIDTH // LANES
    for t in range(SSM_CHUNK):
        for lb in range(nlb):
            yscr[lb, pl.ds(t, ROW_TILE // SSM_CHUNK, stride=SSM_CHUNK), :] = (
                y3_ref[t, :, lb * LANES:(lb + 1) * LANES])
    y = jax.nn.gelu(jnp.concatenate([yscr[lb] for lb in range(nlb)], axis=1))
    z = y * jax.nn.sigmoid(jnp.dot(y.astype(BF16), wglu_ref[...], preferred_element_type=F32))
    ms = jnp.mean(z * z, axis=-1, keepdims=True)
    sn = (z * lax.rsqrt(ms + NORM_EPS) * sg_ref[...]).astype(BF16)
    x1 = (x_ref[...]
          + jnp.dot(a_ref[...], wo_ref[:ATTN_WIDTH, :], preferred_element_type=F32)
          + jnp.dot(sn, wo_ref[ATTN_WIDTH:, :], preferred_element_type=F32))
    x1_out[...] = x1
    ms1 = jnp.mean(x1 * x1, axis=-1, keepdims=True)
    hn = x1 * lax.rsqrt(ms1 + NORM_EPS) * fg_ref[...]
    hn_hi = hn.astype(BF16)
    if with_router:
        hn_lo = (hn - hn_hi.astype(F32)).astype(BF16)
        lg_out[...] = (jnp.dot(hn_hi, rw_hi_ref[...], preferred_element_type=F32)
                       + jnp.dot(hn_hi, rw_lo_ref[...], preferred_element_type=F32)
                       + jnp.dot(hn_lo, rw_hi_ref[...], preferred_element_type=F32))
    else:
        hn_out[...] = hn_hi


def _mix(x2d, attn, y3, wglu, sg, wo, fg, router=None):
    t = x2d.shape[0]
    cpt = ROW_TILE // SSM_CHUNK
    row = lambda w: pl.BlockSpec((ROW_TILE, w), lambda i: (i, 0))
    in_specs = [row(D_MODEL), row(ATTN_WIDTH),
                pl.BlockSpec((SSM_CHUNK, cpt, SSM_WIDTH), lambda i: (0, i, 0)),
                _const_spec((SSM_WIDTH, SSM_WIDTH)), _const_spec((1, SSM_WIDTH)),
                _const_spec((D_MODEL, D_MODEL)), _const_spec((1, D_MODEL))]
    args = [x2d, attn, y3, wglu, sg, wo, fg]
    if router is not None:
        ne = router[0].shape[1]
        in_specs += [_const_spec((D_MODEL, ne)), _const_spec((D_MODEL, ne))]
        args += list(router)
        second = (row(ne), jax.ShapeDtypeStruct((t, ne), F32))
    else:
        second = (row(D_MODEL), jax.ShapeDtypeStruct((t, D_MODEL), BF16))
    return pl.pallas_call(
        functools.partial(_mix_body, with_router=router is not None),
        grid=(t // ROW_TILE,),
        in_specs=in_specs, out_specs=[row(D_MODEL), second[0]],
        out_shape=[jax.ShapeDtypeStruct((t, D_MODEL), F32), second[1]],
        scratch_shapes=[pltpu.VMEM((SSM_WIDTH // LANES, ROW_TILE, LANES), F32)],
        compiler_params=_cparams(("parallel",)),
        name="mix_outproj",
    )(*args)


def _ffn_body(x_ref, h_ref, wg_ref, wu_ref, wd_ref, o_ref, acc):
    h = h_ref[...]
    acc[...] = x_ref[...]
    for c in range(D_FF // FF_CHUNK):
        sl = slice(c * FF_CHUNK, (c + 1) * FF_CHUNK)
        g = jnp.dot(h, wg_ref[:, sl], preferred_element_type=F32)
        u = jnp.dot(h, wu_ref[:, sl], preferred_element_type=F32)
        a = (jax.nn.silu(g) * u).astype(BF16)
        acc[...] += jnp.dot(a, wd_ref[sl, :], preferred_element_type=F32)
    o_ref[...] = acc[...]


def _dense_ffn(x1, hn, wg, wu, wd):
    t = x1.shape[0]
    row = lambda w: pl.BlockSpec((ROW_TILE, w), lambda i: (i, 0))
    return pl.pallas_call(
        _ffn_body,
        grid=(t // ROW_TILE,),
        in_specs=[row(D_MODEL), row(D_MODEL), _resident_spec((D_MODEL, D_FF)),
                  _resident_spec((D_MODEL, D_FF)), _resident_spec((D_FF, D_MODEL))],
        out_specs=row(D_MODEL),
        out_shape=jax.ShapeDtypeStruct((t, D_MODEL), F32),
        scratch_shapes=[pltpu.VMEM((ROW_TILE, D_MODEL), F32)],
        compiler_params=_cparams(("parallel",)),
        name="dense_ffn",
    )(x1, hn, wg, wu, wd)


def _expert_ffn_body(e_ref, nused_ref, nvalid_ref, tok_ref, dst_ref, x1_hbm, fg_ref, gate_ref,
                     wg_ref, wu_ref, wd_ref, out_hbm, xbuf, acc, ybuf, gsem, ssem):
    b = pl.program_id(0)
    n_used = nused_ref[0]
    n_ff = D_FF // FF_CHUNK
    rows_per_chunk = -(-MOE_ROWS // n_ff)

    def gather_row(blk, r):
        i = tok_ref[blk * MOE_ROWS + r]
        pltpu.make_async_copy(x1_hbm.at[pl.ds(i, 1)], xbuf.at[pl.ds(r, 1)], gsem).start()

    def scatter_row(blk, r):
        i = dst_ref[blk * MOE_ROWS + r]
        pltpu.make_async_copy(ybuf.at[pl.ds(r, 1)], out_hbm.at[pl.ds(i, 1)], ssem).start()

    def scatter_rows(blk, n):
        def body(r, carry):
            scatter_row(blk, r)
            return carry
        lax.fori_loop(0, n, body, 0)

    def wait_gather():
        pltpu.make_async_copy(x1_hbm.at[pl.ds(0, MOE_ROWS)], xbuf, gsem).wait()

    def wait_scatter(n):
        n8 = pl.multiple_of(n & -8, 8)

        @pl.when(n8 > 0)
        def _():
            pltpu.make_async_copy(ybuf.at[pl.ds(0, n8)], out_hbm.at[pl.ds(0, n8)], ssem).wait()

        def one(_, carry):
            pltpu.make_async_copy(ybuf.at[pl.ds(0, 1)], out_hbm.at[pl.ds(0, 1)], ssem).wait()
            return carry
        lax.fori_loop(0, n - n8, one, 0)

    def ffn(scatter_prev):
        wait_gather()
        x = xbuf[...]
        ms = jnp.mean(x * x, axis=-1, keepdims=True)
        h = (x * lax.rsqrt(ms + NORM_EPS) * fg_ref[...]).astype(BF16)
        for c in range(n_ff):
            for r in range(c * rows_per_chunk, min((c + 1) * rows_per_chunk, MOE_ROWS)):
                gather_row(b + 1, r)
                if scatter_prev:
                    scatter_row(b - 1, r)
            sl = slice(c * FF_CHUNK, (c + 1) * FF_CHUNK)
            g = jnp.dot(h, wg_ref[0, :, sl], preferred_element_type=F32)
            u = jnp.dot(h, wu_ref[0, :, sl], preferred_element_type=F32)
            a = (jax.nn.silu(g) * u).astype(BF16)
            part = jnp.dot(a, wd_ref[0, sl, :], preferred_element_type=F32)
            if c == 0:
                acc[...] = part
            else:
                acc[...] += part

    @pl.when(b < n_used)
    def _():
        @pl.when(b == 0)
        def _():
            def body(r, carry):
                gather_row(0, r)
                return carry
            lax.fori_loop(0, MOE_ROWS, body, 0, unroll=8)

        n_prev = jnp.where(b >= 1, nvalid_ref[jnp.maximum(b - 1, 0)], 0)

        @pl.when(n_prev == MOE_ROWS)
        def _():
            ffn(scatter_prev=True)

        @pl.when(n_prev != MOE_ROWS)
        def _():
            scatter_rows(b - 1, n_prev)
            ffn(scatter_prev=False)

        wait_scatter(n_prev)
        ybuf[...] = acc[...] * gate_ref[0]

        @pl.when(b == n_used - 1)
        def _():
            scatter_rows(b, nvalid_ref[b])
            wait_scatter(nvalid_ref[b])
            wait_gather()


def _expert_ffn(x1, fg, tok, dst, gate, block_e, n_used, n_valid, wg, wu, wd, nblk, out_rows):
    wspec = lambda shape: pl.BlockSpec(shape, lambda b, e, *_: (e[b], 0, 0))
    return pl.pallas_call(
        _expert_ffn_body,
        grid_spec=pltpu.PrefetchScalarGridSpec(
            num_scalar_prefetch=5, grid=(nblk,),
            in_specs=[pl.BlockSpec(memory_space=pl.ANY),
                      pl.BlockSpec((1, D_MODEL), lambda b, *_: (0, 0)),
                      pl.BlockSpec((1, MOE_ROWS, 1), lambda b, *_: (b, 0, 0)),
                      wspec((1, D_MODEL, D_FF)), wspec((1, D_MODEL, D_FF)), wspec((1, D_FF, D_MODEL))],
            out_specs=pl.BlockSpec(memory_space=pl.ANY),
            scratch_shapes=[pltpu.VMEM((MOE_ROWS, D_MODEL), F32)] * 3
                           + [pltpu.SemaphoreType.DMA(()), pltpu.SemaphoreType.DMA(())]),
        out_shape=jax.ShapeDtypeStruct((out_rows, D_MODEL), F32),
        compiler_params=pltpu.CompilerParams(dimension_semantics=("arbitrary",),
                                             vmem_limit_bytes=MOE_VMEM_LIMIT),
        name="moe_expert_ffn",
    )(block_e, n_used, n_valid, tok, dst, x1, fg, gate.reshape(nblk, MOE_ROWS, 1), wg, wu, wd)


def _combine_body(x_ref, y0_ref, y1_ref, o_ref):
    o_ref[...] = x_ref[...] + y0_ref[...] + y1_ref[...]


def _combine(x1, y2):
    t = x1.shape[0]
    nt = t // ROW_TILE
    return pl.pallas_call(
        _combine_body,
        grid=(nt,),
        in_specs=[pl.BlockSpec((ROW_TILE, D_MODEL), lambda i: (i, 0)),
                  pl.BlockSpec((ROW_TILE, D_MODEL), lambda i: (i, 0)),
                  pl.BlockSpec((ROW_TILE, D_MODEL), lambda i: (i + nt, 0))],
        out_specs=pl.BlockSpec((ROW_TILE, D_MODEL), lambda i: (i, 0)),
        out_shape=jax.ShapeDtypeStruct((t, D_MODEL), F32),
        compiler_params=_cparams(("parallel",)),
        name="moe_combine",
    )(x1, y2, y2)


def _moe(x1, fg, logits, wg, wu, wd):
    t = x1.shape[0]
    n_assign = t * TOP_K
    nblk = n_assign // MOE_ROWS + N_EXPERTS
    cap = nblk * MOE_ROWS
    top_v, top_i = lax.top_k(logits, TOP_K)
    gates = jax.nn.softmax(top_v, axis=-1).reshape(-1)
    flat_e = top_i.reshape(-1).astype(jnp.int32)
    order = jnp.argsort(flat_e, stable=True).astype(jnp.int32)
    experts = jnp.arange(N_EXPERTS, dtype=jnp.int32)
    counts = jnp.sum((flat_e[:, None] == experts[None, :]).astype(jnp.int32), axis=0)
    start = jnp.cumsum(counts) - counts
    padded = ((counts + MOE_ROWS - 1) // MOE_ROWS) * MOE_ROWS
    pend = jnp.cumsum(padded)
    pstart = pend - padded
    p = jnp.arange(cap, dtype=jnp.int32)
    e_p = jnp.minimum(jnp.sum((p[:, None] >= pend[None, :]).astype(jnp.int32), axis=1), N_EXPERTS - 1)
    rank = p - pstart[e_p]
    valid = jnp.logical_and(rank < counts[e_p], p < pend[-1])
    a_p = order[jnp.clip(start[e_p] + rank, 0, n_assign - 1)]
    tok = jnp.where(valid, a_p // TOP_K, 0).astype(jnp.int32)
    dst = jnp.where(valid, (a_p % TOP_K) * t + a_p // TOP_K, 0).astype(jnp.int32)
    gate = jnp.where(valid, gates[a_p], 0.0).astype(F32)
    block_e = e_p[::MOE_ROWS]
    n_used = (pend[-1:] // MOE_ROWS).astype(jnp.int32)
    n_valid = jnp.sum(valid.reshape(nblk, MOE_ROWS).astype(jnp.int32), axis=1)
    y2 = _expert_ffn(x1, fg, tok, dst, gate, block_e, n_used, n_valid, wg, wu, wd, nblk, n_assign)
    return _combine(x1, y2)


def kernel(x, attn_norm_g, w_in, q_norm_g, k_norm_g, sinks, lam_re, lam_im, log_dt, b_re, b_im,
           c_re, c_im, d_skip, w_glu, attn_out_g, ssm_out_g, w_o, ffn_norm_g, dense_wg, dense_wu,
           dense_wd, router_w, moe_wg, moe_wu, moe_wd):
    bsz, seq, _ = x.shape
    depth = w_in.shape[0]
    assert SSM_SEQS % bsz == 0 and seq % (SSM_CHUNK * (SSM_SEQS // bsz)) == 0
    assert seq % ROW_TILE == 0 and ROW_TILE % ATTN_BLOCK == 0
    x2d = x.reshape(bsz * seq, D_MODEL).astype(F32)
    head = jnp.arange(ATTN_WIDTH) // HEAD_DIM
    avg = jnp.where(head[:, None] == head[None, :], 1.0 / HEAD_DIM, 0.0).astype(BF16)
    eye = jnp.eye(ROW_TILE, dtype=BF16)
    n_steps = seq // (SSM_CHUNK * (SSM_SEQS // bsz))
    row = lambda v: v.astype(F32).reshape(1, -1)
    col = lambda v: v.astype(F32).reshape(-1, 1)
    ko, vo, uo = ATTN_WIDTH, ATTN_WIDTH + KV_WIDTH, ATTN_WIDTH + 2 * KV_WIDTH
    for l in range(depth):
        qg = jnp.tile(q_norm_g[l].astype(F32), N_HEADS) * (HEAD_DIM ** -0.5)
        kg = jnp.tile(k_norm_g[l].astype(F32), N_KV_HEADS)
        w = w_in[l].astype(BF16)
        qt, k, v3, u3 = _inproj(x2d, row(attn_norm_g[l]), w[:, :ko].T, w[:, ko:vo], w[:, vo:uo].T,
                                w[:, uo:], col(qg), row(kg), avg)
        attn = _attention(qt, k, v3, sinks[l].astype(F32), col(attn_out_g[l]), eye, seq)
        tables = _ssm_tables(lam_re[l], lam_im[l], log_dt[l], b_re[l], b_im[l], c_re[l], c_im[l],
                             d_skip[l], n_steps)
        y3 = _ssm(u3, tables, eye[:LANES, :LANES], bsz, seq)
        i = l // 2
        router = None
        if l % 2 == 1:
            rw = router_w[i].astype(F32)
            rw_hi = rw.astype(BF16)
            router = (rw_hi, (rw - rw_hi.astype(F32)).astype(BF16))
        x1, second = _mix(x2d, attn, y3, w_glu[l].astype(BF16), row(ssm_out_g[l]),
                          w_o[l].astype(BF16), row(ffn_norm_g[l]), router)
        if l % 2 == 0:
            x2d = _dense_ffn(x1, second, dense_wg[i].astype(BF16), dense_wu[i].astype(BF16),
                             dense_wd[i].astype(BF16))
        else:
            x2d = _moe(x1, row(ffn_norm_g[l]), second, moe_wg[i].astype(BF16), moe_wu[i].astype(BF16),
                       moe_wd[i].astype(BF16))
    return x2d.reshape(bsz, seq, D_MODEL)
```

```python
import functools
import math

import jax
import jax.numpy as jnp
from jax import lax
from jax.experimental import pallas as pl
from jax.experimental.pallas import tpu as pltpu

F32 = jnp.float32
BF16 = jnp.bfloat16

D_MODEL = 1024
HEAD_DIM = 64
N_HEADS = 8
N_KV_HEADS = 2
GQA = N_HEADS // N_KV_HEADS
ATTN_WIDTH = N_HEADS * HEAD_DIM
KV_WIDTH = N_KV_HEADS * HEAD_DIM
ATTN_BLOCK = 128
SSM_WIDTH = D_MODEL - ATTN_WIDTH
SSM_GROUP_DIM = 16
SSM_GROUPS = SSM_WIDTH // SSM_GROUP_DIM
SSM_STATE = 64
IN_WIDTH = ATTN_WIDTH + 2 * KV_WIDTH + SSM_WIDTH
D_FF = 3584
N_EXPERTS = 8
TOP_K = 2
NORM_EPS = 1e-6
NEG_INF = -1e30

LANES = 128
ROW_TILE = 512
SSM_CHUNK = 16
SSM_SEQS = 8
SSM_VEC = SSM_CHUNK * SSM_GROUP_DIM
SSM_LANE_GROUPS = LANES // SSM_GROUP_DIM
FF_CHUNK = 512
MOE_ROWS = 256
VMEM_LIMIT = 56 * 1024 * 1024
MOE_VMEM_LIMIT = 60 * 1024 * 1024

NT_DIMS = (((1,), (1,)), ((), ()))


def _cparams(sem, vmem=VMEM_LIMIT):
    return pltpu.CompilerParams(dimension_semantics=sem, vmem_limit_bytes=vmem)


def _const_spec(shape):
    n = len(shape)
    return pl.BlockSpec(shape, lambda *_: (0,) * n)


def _resident_spec(shape):
    n = len(shape)
    return pl.BlockSpec(shape, lambda *_: (0,) * n, pipeline_mode=pl.Buffered(1))


def _inproj_body(x_ref, g_ref, wqt_ref, wk_ref, wvt_ref, wu_ref, qg_ref, kg_ref, avg_ref,
                 qt_out, k_out, v3_out, u3_out, uscr):
    x = x_ref[...]
    ms = jnp.mean(x * x, axis=-1, keepdims=True)
    hn = (x * lax.rsqrt(ms + NORM_EPS) * g_ref[...]).astype(BF16)
    qt = lax.dot_general(wqt_ref[...], hn, NT_DIMS, preferred_element_type=F32)
    qms = jnp.dot(avg_ref[...], (qt * qt).astype(BF16), preferred_element_type=F32)
    qt_out[...] = (qt * lax.rsqrt(qms + NORM_EPS) * qg_ref[...]).astype(BF16)
    k = jnp.dot(hn, wk_ref[...], preferred_element_type=F32)
    kms = jnp.dot((k * k).astype(BF16), avg_ref[:KV_WIDTH, :KV_WIDTH], preferred_element_type=F32)
    k_out[...] = (k * lax.rsqrt(kms + NORM_EPS) * kg_ref[...]).astype(BF16)
    vt = lax.dot_general(wvt_ref[...], hn, NT_DIMS, preferred_element_type=F32)
    for b in range(ROW_TILE // ATTN_BLOCK):
        v3_out[b] = vt[:, b * ATTN_BLOCK:(b + 1) * ATTN_BLOCK].astype(BF16)
    u = jnp.dot(hn, wu_ref[...], preferred_element_type=F32)
    for lb in range(SSM_WIDTH // LANES):
        uscr[lb] = u[:, lb * LANES:(lb + 1) * LANES]
    for t in range(SSM_CHUNK):
        for lb in range(SSM_WIDTH // LANES):
            u3_out[t, :, lb * LANES:(lb + 1) * LANES] = (
                uscr[lb, pl.ds(t, ROW_TILE // SSM_CHUNK, stride=SSM_CHUNK), :].astype(BF16))


def _inproj(x2d, g, wqt, wk, wvt, wu, qg, kg, avg):
    t = x2d.shape[0]
    cpt = ROW_TILE // SSM_CHUNK
    return pl.pallas_call(
        _inproj_body,
        grid=(t // ROW_TILE,),
        in_specs=[pl.BlockSpec((ROW_TILE, D_MODEL), lambda i: (i, 0)),
                  _const_spec((1, D_MODEL)), _const_spec((ATTN_WIDTH, D_MODEL)),
                  _const_spec((D_MODEL, KV_WIDTH)), _const_spec((KV_WIDTH, D_MODEL)),
                  _const_spec((D_MODEL, SSM_WIDTH)), _const_spec((ATTN_WIDTH, 1)),
                  _const_spec((1, KV_WIDTH)), _const_spec((ATTN_WIDTH, ATTN_WIDTH))],
        out_specs=[pl.BlockSpec((ATTN_WIDTH, ROW_TILE), lambda i: (0, i)),
                   pl.BlockSpec((ROW_TILE, KV_WIDTH), lambda i: (i, 0)),
                   pl.BlockSpec((ROW_TILE // ATTN_BLOCK, KV_WIDTH, ATTN_BLOCK), lambda i: (i, 0, 0)),
                   pl.BlockSpec((SSM_CHUNK, cpt, SSM_WIDTH), lambda i: (0, i, 0))],
        out_shape=[jax.ShapeDtypeStruct((ATTN_WIDTH, t), BF16),
                   jax.ShapeDtypeStruct((t, KV_WIDTH), BF16),
                   jax.ShapeDtypeStruct((t // ATTN_BLOCK, KV_WIDTH, ATTN_BLOCK), BF16),
                   jax.ShapeDtypeStruct((SSM_CHUNK, t // SSM_CHUNK, SSM_WIDTH), BF16)],
        scratch_shapes=[pltpu.VMEM((SSM_WIDTH // LANES, ROW_TILE, LANES), F32)],
        compiler_params=_cparams(("parallel",)),
        name="inproj",
    )(x2d, g, wqt, wk, wvt, wu, qg, kg, avg)


def _attn_body(sink_ref, qt_ref, k_ref, v3_ref, g_ref, eye_ref, o_ref, at_scr, *, blocks_per_seq):
    i = pl.program_id(0)
    key = lax.broadcasted_iota(jnp.int32, (ATTN_BLOCK, ATTN_BLOCK), 0)
    qry = lax.broadcasted_iota(jnp.int32, (ATTN_BLOCK, ATTN_BLOCK), 1)
    cur_ok = key <= qry
    prev_ok = key > qry
    zpad = jnp.zeros((HEAD_DIM, ATTN_BLOCK), BF16)
    nblk = ROW_TILE // ATTN_BLOCK
    for blk in range(nblk):
        gblk = i * nblk + blk
        pblk = jnp.maximum(gblk - 1, 0)
        row0 = pl.multiple_of(gblk * ATTN_BLOCK, ATTN_BLOCK)
        prev0 = pl.multiple_of(pblk * ATTN_BLOCK, ATTN_BLOCK)
        has_prev = (gblk % blocks_per_seq) != 0
        kc = k_ref[pl.ds(row0, ATTN_BLOCK), :]
        kp = k_ref[pl.ds(prev0, ATTN_BLOCK), :]
        vc = v3_ref[gblk]
        vp = v3_ref[pblk]
        qb = qt_ref[:, blk * ATTN_BLOCK:(blk + 1) * ATTN_BLOCK]
        pmask = jnp.logical_and(prev_ok, has_prev)
        outs = []
        for h in range(N_HEADS):
            kv = h // GQA
            qh = qb[h * HEAD_DIM:(h + 1) * HEAD_DIM, :]
            qpad = jnp.concatenate([qh, zpad] if kv == 0 else [zpad, qh], axis=0)
            sc = jnp.dot(kc, qpad, preferred_element_type=F32)
            sp = jnp.dot(kp, qpad, preferred_element_type=F32)
            sc = jnp.where(cur_ok, sc, NEG_INF)
            sp = jnp.where(pmask, sp, NEG_INF)
            sink = sink_ref[h]
            m = jnp.maximum(jnp.maximum(jnp.max(sc, axis=0, keepdims=True),
                                        jnp.max(sp, axis=0, keepdims=True)), sink)
            pc = jnp.exp(sc - m)
            pp = jnp.exp(sp - m)
            den = (jnp.sum(pc, axis=0, keepdims=True) + jnp.sum(pp, axis=0, keepdims=True)
                   + jnp.exp(sink - m))
            vs = slice(kv * HEAD_DIM, (kv + 1) * HEAD_DIM)
            o = (jnp.dot(vc[vs, :], pc.astype(BF16), preferred_element_type=F32)
                 + jnp.dot(vp[vs, :], pp.astype(BF16), preferred_element_type=F32))
            outs.append(o / den)
        a = jnp.concatenate(outs, axis=0)
        ms = jnp.mean(a * a, axis=0, keepdims=True)
        at_scr[:, blk * ATTN_BLOCK:(blk + 1) * ATTN_BLOCK] = (
            a * lax.rsqrt(ms + NORM_EPS) * g_ref[...]).astype(BF16)
    o_ref[...] = lax.dot_general(eye_ref[...], at_scr[...], NT_DIMS,
                                 preferred_element_type=F32).astype(BF16)


def _attention(qt, k, v3, sinks, gcol, eye, seq):
    t = k.shape[0]
    return pl.pallas_call(
        functools.partial(_attn_body, blocks_per_seq=seq // ATTN_BLOCK),
        grid=(t // ROW_TILE,),
        in_specs=[pl.BlockSpec(memory_space=pltpu.SMEM),
                  pl.BlockSpec((ATTN_WIDTH, ROW_TILE), lambda i: (0, i)),
                  _const_spec((t, KV_WIDTH)), _const_spec((t // ATTN_BLOCK, KV_WIDTH, ATTN_BLOCK)),
                  _const_spec((ATTN_WIDTH, 1)), _const_spec((ROW_TILE, ROW_TILE))],
        out_specs=pl.BlockSpec((ROW_TILE, ATTN_WIDTH), lambda i: (i, 0)),
        out_shape=jax.ShapeDtypeStruct((t, ATTN_WIDTH), BF16),
        scratch_shapes=[pltpu.VMEM((ATTN_WIDTH, ROW_TILE), BF16)],
        compiler_params=_cparams(("parallel",)),
        name="swa_attention",
    )(sinks, qt, k, v3, gcol, eye)


def _ssm_tables_body(lam_ref, lamc_ref, b_ref, c_ref, erep_ref, etile_ref,
                     m_out, wre_out, wim_out, vre_out, vim_out, k_out, apw_out, *, n_steps):
    hi = lax.Precision.HIGHEST
    lanes = 2 * SSM_STATE
    ch, gd = SSM_CHUNK, SSM_GROUP_DIM
    lr, li, dt = lam_ref[0, 0:1, :], lam_ref[0, 1:2, :], lam_ref[0, 2:3, :]

    def apow(p):
        mag = jnp.exp(p * (lr * dt))
        ang = p * (li * dt)
        return mag * jnp.cos(ang), mag * jnp.sin(ang)

    tau = lax.broadcasted_iota(jnp.int32, (ch, lanes), 0).astype(F32)
    p0r, p0i = apow(tau)
    p1r, p1i = apow(tau + 1.0)
    step = lax.broadcasted_iota(jnp.int32, (n_steps, lanes), 0).astype(F32)
    qr, qi = apow(step * float(ch))
    apw_out[0, 0] = qr
    apw_out[0, 1] = qi
    sel = lax.broadcasted_iota(jnp.int32, (8, lanes), 0)
    kr, ki = apow(jnp.where(sel == 0, float(ch), float(ch * n_steps)))
    k_out[0, 0] = kr[0:2]
    k_out[0, 1] = ki[0:2]

    lrc, lic, dtc = lamc_ref[0, :, 0:1], lamc_ref[0, :, 1:2], lamc_ref[0, :, 2:3]
    magc = jnp.exp(lrc * dtc)
    ab_re, ab_im = magc * jnp.cos(lic * dtc), magc * jnp.sin(lic * dtc)
    nr = ab_re - 1.0
    den = lrc * lrc + lic * lic
    f_re = (nr * lrc + ab_im * lic) / den
    f_im = (ab_im * lrc - nr * lic) / den
    br, bi = b_ref[0, 0], b_ref[0, 1]
    bb_re = f_re * br - f_im * bi
    bb_im = f_re * bi + f_im * br
    bx_re = jnp.dot(bb_re, etile_ref[...], precision=hi, preferred_element_type=F32)
    bx_im = jnp.dot(bb_im, etile_ref[...], precision=hi, preferred_element_type=F32)
    jrev = float(ch - 1) - lax.broadcasted_iota(jnp.int32, (lanes, ch), 1).astype(F32)
    mg = jnp.exp(jrev * (lrc * dtc))
    an = jrev * (lic * dtc)
    px_re = jnp.dot(mg * jnp.cos(an), erep_ref[...], precision=hi, preferred_element_type=F32)
    px_im = jnp.dot(mg * jnp.sin(an), erep_ref[...], precision=hi, preferred_element_type=F32)
    w_re = px_re * bx_re - px_im * bx_im
    w_im = px_re * bx_im + px_im * bx_re
    first = lax.broadcasted_iota(jnp.int32, (lanes, SSM_VEC), 0) < SSM_STATE
    pack = lambda w: jnp.concatenate([jnp.where(first, w, 0.0), jnp.where(first, 0.0, w)], axis=1)
    wre_out[0] = pack(w_re).astype(BF16)
    wim_out[0] = pack(w_im).astype(BF16)

    blk = lax.broadcasted_iota(jnp.int32, (SSM_VEC, SSM_VEC), 1) // gd
    for g in range(2):
        cr, ci = c_ref[g, 0], c_ref[g, 1]
        stack = lambda x: x.reshape(ch * gd, lanes)
        ca_re = stack(cr[None] * p0r[:, None, :] - ci[None] * p0i[:, None, :])
        ca_im = stack(cr[None] * p0i[:, None, :] + ci[None] * p0r[:, None, :])
        kt = (jnp.dot(ca_re, bx_re, precision=hi, preferred_element_type=F32)
              - jnp.dot(ca_im, bx_im, precision=hi, preferred_element_type=F32))
        m = jnp.where(blk == 0, kt, 0.0)
        for j in range(1, ch):
            shifted = jnp.concatenate([jnp.zeros((j * gd, SSM_VEC), F32), kt[:SSM_VEC - j * gd]], axis=0)
            m = jnp.where(blk == j, shifted, m)
        m_out[g] = m.astype(BF16)
        vre_out[g] = stack(cr[None] * p1r[:, None, :] - ci[None] * p1i[:, None, :]).astype(BF16)
        vim_out[g] = (-stack(cr[None] * p1i[:, None, :] + ci[None] * p1r[:, None, :])).astype(BF16)


def _ssm_tables(lam_re, lam_im, log_dt, b_re, b_im, c_re, c_im, d_skip, n_steps):
    g_, n_, c_ = SSM_GROUPS, SSM_STATE, SSM_GROUP_DIM
    np_ = g_ // 2
    lanes = 2 * n_
    dt = jnp.broadcast_to(jnp.exp(log_dt.astype(F32))[:, None], (g_, n_))
    lam = jnp.stack([lam_re.astype(F32), lam_im.astype(F32), dt], axis=1)
    lam_row = lam.reshape(np_, 2, 3, n_).transpose(0, 2, 1, 3).reshape(np_, 3, lanes)
    lam_col = lam_row.transpose(0, 2, 1)
    b2 = jnp.stack([b_re.astype(F32).reshape(np_, lanes, c_), b_im.astype(F32).reshape(np_, lanes, c_)],
                   axis=1)
    c2 = jnp.stack([c_re.astype(F32), c_im.astype(F32)], axis=1)
    z = jnp.zeros_like(c2)
    even = (jnp.arange(g_) % 2 == 0)[:, None, None, None]
    c2 = jnp.where(even, jnp.concatenate([c2, z], axis=3), jnp.concatenate([z, c2], axis=3))
    col = jnp.arange(SSM_VEC)
    erep = (col[None, :] // c_ == jnp.arange(SSM_CHUNK)[:, None]).astype(F32)
    etile = (col[None, :] % c_ == jnp.arange(c_)[:, None]).astype(F32)
    lead = lambda shape: pl.BlockSpec(shape, lambda p: (p,) + (0,) * (len(shape) - 1))
    m, wre, wim, vre, vim, consts, apw = pl.pallas_call(
        functools.partial(_ssm_tables_body, n_steps=n_steps),
        grid=(np_,),
        in_specs=[lead((1, 3, lanes)), lead((1, lanes, 3)), lead((1, 2, lanes, c_)),
                  lead((2, 2, c_, lanes)), _const_spec((SSM_CHUNK, SSM_VEC)), _const_spec((c_, SSM_VEC))],
        out_specs=[lead((2, SSM_VEC, SSM_VEC)), lead((1, lanes, 2 * SSM_VEC)), lead((1, lanes, 2 * SSM_VEC)),
                   lead((2, SSM_VEC, lanes)), lead((2, SSM_VEC, lanes)),
                   lead((1, 2, 2, lanes)), lead((1, 2, n_steps, lanes))],
        out_shape=[jax.ShapeDtypeStruct((g_, SSM_VEC, SSM_VEC), BF16),
                   jax.ShapeDtypeStruct((np_, lanes, 2 * SSM_VEC), BF16),
                   jax.ShapeDtypeStruct((np_, lanes, 2 * SSM_VEC), BF16),
                   jax.ShapeDtypeStruct((g_, SSM_VEC, lanes), BF16),
                   jax.ShapeDtypeStruct((g_, SSM_VEC, lanes), BF16),
                   jax.ShapeDtypeStruct((np_, 2, 2, lanes), F32),
                   jax.ShapeDtypeStruct((np_, 2, n_steps, lanes), F32)],
        compiler_params=_cparams(("parallel",)),
        name="s5_operators",
    )(lam_row, lam_col, b2, c2, erep, etile)
    dcol = jnp.tile(d_skip.astype(F32).reshape(g_, 1, c_), (1, SSM_CHUNK, 1)).reshape(g_, SSM_VEC, 1)
    return m, wre, wim, vre, vim, consts, apw, dcol


def _ssm_body(u_ref, eye_ref, m_ref, wre_ref, wim_ref, vre_ref, vim_ref, c_ref, apw_ref, d_ref,
              y_ref, ut, yt, pre, pim, sre, sim, *, n_steps, halves):
    lanes = 2 * SSM_STATE
    nk = u_ref.shape[1]
    gd = SSM_GROUP_DIM
    for t in range(SSM_CHUNK):
        xt = lax.dot_general(eye_ref[...], u_ref[t], NT_DIMS,
                             preferred_element_type=F32).astype(BF16)
        for g in range(SSM_LANE_GROUPS):
            ut[g, t * gd:(t + 1) * gd, :] = xt[g * gd:(g + 1) * gd, :]

    row = lax.broadcasted_iota(jnp.int32, (SSM_SEQS, lanes), 0)
    seq_start = (row % halves) == 0
    shift = lambda x: jnp.where(seq_start, 0.0, pltpu.roll(x, 1, axis=0))

    for pp in range(SSM_LANE_GROUPS // 2):
        ucat = jnp.concatenate([ut[2 * pp], ut[2 * pp + 1]], axis=0)
        pre[...] = jnp.dot(wre_ref[pp], ucat, preferred_element_type=F32).T
        pim[...] = jnp.dot(wim_ref[pp], ucat, preferred_element_type=F32).T
        ar = jnp.broadcast_to(c_ref[pp, 0, 0:1, :], (SSM_SEQS, lanes))
        ai = jnp.broadcast_to(c_ref[pp, 1, 0:1, :], (SSM_SEQS, lanes))

        def step(i, carry):
            s_re, s_im = carry
            idx = pl.ds(i, SSM_SEQS, stride=n_steps)
            sre[idx, :] = s_re
            sim[idx, :] = s_im
            p_r = pre[idx, :]
            p_i = pim[idx, :]
            return ar * s_re - ai * s_im + p_r, ar * s_im + ai * s_re + p_i

        zero = jnp.zeros((SSM_SEQS, lanes), F32)
        e_re, e_im = lax.fori_loop(0, n_steps, step, (zero, zero))

        br = jnp.broadcast_to(c_ref[pp, 0, 1:2, :], (SSM_SEQS, lanes))
        bi = jnp.broadcast_to(c_ref[pp, 1, 1:2, :], (SSM_SEQS, lanes))
        c_re = jnp.zeros_like(e_re)
        c_im = jnp.zeros_like(e_im)
        for _ in range(halves - 1):
            n_re = e_re + br * c_re - bi * c_im
            n_im = e_im + br * c_im + bi * c_re
            c_re, c_im = shift(n_re), shift(n_im)

        apr = apw_ref[pp, 0][None, :, :]
        api = apw_ref[pp, 1][None, :, :]
        s3r = sre[...].reshape(SSM_SEQS, n_steps, lanes)
        s3i = sim[...].reshape(SSM_SEQS, n_steps, lanes)
        t_re = (s3r + apr * c_re[:, None, :] - api * c_im[:, None, :]).reshape(nk, lanes)
        t_im = (s3i + apr * c_im[:, None, :] + api * c_re[:, None, :]).reshape(nk, lanes)
        tr_hi = t_re.astype(BF16)
        tr_lo = (t_re - tr_hi.astype(F32)).astype(BF16)
        ti_hi = t_im.astype(BF16)
        ti_lo = (t_im - ti_hi.astype(F32)).astype(BF16)
        for g in (2 * pp, 2 * pp + 1):
            ug = ut[g]
            y = jnp.dot(m_ref[g], ug, preferred_element_type=F32)
            y += lax.dot_general(vre_ref[g], tr_hi, NT_DIMS, preferred_element_type=F32)
            y += lax.dot_general(vre_ref[g], tr_lo, NT_DIMS, preferred_element_type=F32)
            y += lax.dot_general(vim_ref[g], ti_hi, NT_DIMS, preferred_element_type=F32)
            y += lax.dot_general(vim_ref[g], ti_lo, NT_DIMS, preferred_element_type=F32)
            yt[g] = y + d_ref[g] * ug.astype(F32)

    for t in range(SSM_CHUNK):
        rows = jnp.concatenate([yt[g, t * gd:(t + 1) * gd, :] for g in range(SSM_LANE_GROUPS)], axis=0)
        y_ref[t] = rows.T


def _ssm(u3, tables, eye, bsz, seq):
    m, wre, wim, vre, vim, consts, apw, dcol = tables
    nk = u3.shape[1]
    halves = SSM_SEQS // bsz
    n_steps = seq // (SSM_CHUNK * halves)
    lanes = 2 * SSM_STATE
    lg = SSM_LANE_GROUPS
    lead = lambda shape: pl.BlockSpec(shape, lambda b: (b,) + (0,) * (len(shape) - 1))
    return pl.pallas_call(
        functools.partial(_ssm_body, n_steps=n_steps, halves=halves),
        grid=(SSM_GROUPS // lg,),
        in_specs=[pl.BlockSpec((SSM_CHUNK, nk, LANES), lambda b: (0, 0, b)),
                  _const_spec((LANES, LANES)),
                  lead((lg, SSM_VEC, SSM_VEC)),
                  lead((lg // 2, lanes, 2 * SSM_VEC)), lead((lg // 2, lanes, 2 * SSM_VEC)),
                  lead((lg, SSM_VEC, lanes)), lead((lg, SSM_VEC, lanes)),
                  lead((lg // 2, 2, 2, lanes)), lead((lg // 2, 2, n_steps, lanes)),
                  lead((lg, SSM_VEC, 1))],
        out_specs=pl.BlockSpec((SSM_CHUNK, nk, LANES), lambda b: (0, 0, b)),
        out_shape=jax.ShapeDtypeStruct((SSM_CHUNK, nk, SSM_WIDTH), F32),
        scratch_shapes=[pltpu.VMEM((lg, SSM_VEC, nk), BF16), pltpu.VMEM((lg, SSM_VEC, nk), F32)]
                       + [pltpu.VMEM((nk, lanes), F32)] * 4,
        compiler_params=_cparams(("parallel",)),
        name="s5_ssm",
    )(u3, eye, m, wre, wim, vre, vim, consts, apw, dcol)


def _mix_body(*refs, with_router):
    if with_router:
        (x_ref, a_ref, y3_ref, wglu_ref, sg_ref, wo_ref, fg_ref, rw_hi_ref, rw_lo_ref,
         x1_out, lg_out, yscr) = refs
    else:
        x_ref, a_ref, y3_ref, wglu_ref, sg_ref, wo_ref, fg_ref, x1_out, hn_out, yscr = refs
    nlb = SSM_WIDTH // LANES
    for t in range(SSM_CHUNK):
        for lb in range(nlb):
            yscr[lb, pl.ds(t, ROW_TILE // SSM_CHUNK, stride=SSM_CHUNK), :] = (
                y3_ref[t, :, lb * LANES:(lb + 1) * LANES])
    y = jax.nn.gelu(jnp.concatenate([yscr[lb] for lb in range(nlb)], axis=1))
    z = y * jax.nn.sigmoid(jnp.dot(y.astype(BF16), wglu_ref[...], preferred_element_type=F32))
    ms = jnp.mean(z * z, axis=-1, keepdims=True)
    sn = (z * lax.rsqrt(ms + NORM_EPS) * sg_ref[...]).astype(BF16)
    x1 = (x_ref[...]
          + jnp.dot(a_ref[...], wo_ref[:ATTN_WIDTH, :], preferred_element_type=F32)
          + jnp.dot(sn, wo_ref[ATTN_WIDTH:, :], preferred_element_type=F32))
    x1_out[...] = x1
    ms1 = jnp.mean(x1 * x1, axis=-1, keepdims=True)
    hn = x1 * lax.rsqrt(ms1 + NORM_EPS) * fg_ref[...]
    hn_hi = hn.astype(BF16)
    if with_router:
        hn_lo = (hn - hn_hi.astype(F32)).astype(BF16)
        lg_out[...] = (jnp.dot(hn_hi, rw_hi_ref[...], preferred_element_type=F32)
                       + jnp.dot(hn_hi, rw_lo_ref[...], preferred_element_type=F32)
                       + jnp.dot(hn_lo, rw_hi_ref[...], preferred_element_type=F32))
    else:
        hn_out[...] = hn_hi


def _mix(x2d, attn, y3, wglu, sg, wo, fg, router=None):
    t = x2d.shape[0]
    cpt = ROW_TILE // SSM_CHUNK
    row = lambda w: pl.BlockSpec((ROW_TILE, w), lambda i: (i, 0))
    in_specs = [row(D_MODEL), row(ATTN_WIDTH),
                pl.BlockSpec((SSM_CHUNK, cpt, SSM_WIDTH), lambda i: (0, i, 0)),
                _const_spec((SSM_WIDTH, SSM_WIDTH)), _const_spec((1, SSM_WIDTH)),
                _const_spec((D_MODEL, D_MODEL)), _const_spec((1, D_MODEL))]
    args = [x2d, attn, y3, wglu, sg, wo, fg]
    if router is not None:
        ne = router[0].shape[1]
        in_specs += [_const_spec((D_MODEL, ne)), _const_spec((D_MODEL, ne))]
        args += list(router)
        second = (row(ne), jax.ShapeDtypeStruct((t, ne), F32))
    else:
        second = (row(D_MODEL), jax.ShapeDtypeStruct((t, D_MODEL), BF16))
    return pl.pallas_call(
        functools.partial(_mix_body, with_router=router is not None),
        grid=(t // ROW_TILE,),
        in_specs=in_specs, out_specs=[row(D_MODEL), second[0]],
        out_shape=[jax.ShapeDtypeStruct((t, D_MODEL), F32), second[1]],
        scratch_shapes=[pltpu.VMEM((SSM_WIDTH // LANES, ROW_TILE, LANES), F32)],
        compiler_params=_cparams(("parallel",)),
        name="mix_outproj",
    )(*args)


def _ffn_body(x_ref, h_ref, wg_ref, wu_ref, wd_ref, o_ref, acc):
    h = h_ref[...]
    acc[...] = x_ref[...]
    for c in range(D_FF // FF_CHUNK):
        sl = slice(c * FF_CHUNK, (c + 1) * FF_CHUNK)
        g = jnp.dot(h, wg_ref[:, sl], preferred_element_type=F32)
        u = jnp.dot(h, wu_ref[:, sl], preferred_element_type=F32)
        a = (jax.nn.silu(g) * u).astype(BF16)
        acc[...] += jnp.dot(a, wd_ref[sl, :], preferred_element_type=F32)
    o_ref[...] = acc[...]


def _dense_ffn(x1, hn, wg, wu, wd):
    t = x1.shape[0]
    row = lambda w: pl.BlockSpec((ROW_TILE, w), lambda i: (i, 0))
    return pl.pallas_call(
        _ffn_body,
        grid=(t // ROW_TILE,),
        in_specs=[row(D_MODEL), row(D_MODEL), _resident_spec((D_MODEL, D_FF)),
                  _resident_spec((D_MODEL, D_FF)), _resident_spec((D_FF, D_MODEL))],
        out_specs=row(D_MODEL),
        out_shape=jax.ShapeDtypeStruct((t, D_MODEL), F32),
        scratch_shapes=[pltpu.VMEM((ROW_TILE, D_MODEL), F32)],
        compiler_params=_cparams(("parallel",)),
        name="dense_ffn",
    )(x1, hn, wg, wu, wd)


def _expert_ffn_body(e_ref, nused_ref, nvalid_ref, tok_ref, dst_ref, x1_hbm, fg_ref, gate_ref,
                     wg_ref, wu_ref, wd_ref, out_hbm, xbuf, acc, ybuf, gsem, ssem):
    b = pl.program_id(0)
    n_used = nused_ref[0]
    n_ff = D_FF // FF_CHUNK
    rows_per_chunk = -(-MOE_ROWS // n_ff)

    def gather_row(blk, r):
        i = tok_ref[blk * MOE_ROWS + r]
        pltpu.make_async_copy(x1_hbm.at[pl.ds(i, 1)], xbuf.at[pl.ds(r, 1)], gsem).start()

    def scatter_row(blk, r):
        i = dst_ref[blk * MOE_ROWS + r]
        pltpu.make_async_copy(ybuf.at[pl.ds(r, 1)], out_hbm.at[pl.ds(i, 1)], ssem).start()

    def scatter_rows(blk, n):
        def body(r, carry):
            scatter_row(blk, r)
            return carry
        lax.fori_loop(0, n, body, 0)

    def wait_gather():
        pltpu.make_async_copy(x1_hbm.at[pl.ds(0, MOE_ROWS)], xbuf, gsem).wait()

    def wait_scatter(n):
        n8 = pl.multiple_of(n & -8, 8)

        @pl.when(n8 > 0)
        def _():
            pltpu.make_async_copy(ybuf.at[pl.ds(0, n8)], out_hbm.at[pl.ds(0, n8)], ssem).wait()

        def one(_, carry):
            pltpu.make_async_copy(ybuf.at[pl.ds(0, 1)], out_hbm.at[pl.ds(0, 1)], ssem).wait()
            return carry
        lax.fori_loop(0, n - n8, one, 0)

    def ffn(scatter_prev):
        wait_gather()
        x = xbuf[...]
        ms = jnp.mean(x * x, axis=-1, keepdims=True)
        h = (x * lax.rsqrt(ms + NORM_EPS) * fg_ref[...]).astype(BF16)
        for c in range(n_ff):
            for r in range(c * rows_per_chunk, min((c + 1) * rows_per_chunk, MOE_ROWS)):
                gather_row(b + 1, r)
                if scatter_prev:
                    scatter_row(b - 1, r)
            sl = slice(c * FF_CHUNK, (c + 1) * FF_CHUNK)
            g = jnp.dot(h, wg_ref[0, :, sl], preferred_element_type=F32)
            u = jnp.dot(h, wu_ref[0, :, sl], preferred_element_type=F32)
            a = (jax.nn.silu(g) * u).astype(BF16)
            part = jnp.dot(a, wd_ref[0, sl, :], preferred_element_type=F32)
            if c == 0:
                acc[...] = part
            else:
                acc[...] += part

    @pl.when(b < n_used)
    def _():
        @pl.when(b == 0)
        def _():
            def body(r, carry):
                gather_row(0, r)
                return carry
            lax.fori_loop(0, MOE_ROWS, body, 0, unroll=8)

        n_prev = jnp.where(b >= 1, nvalid_ref[jnp.maximum(b - 1, 0)], 0)

        @pl.when(n_prev == MOE_ROWS)
        def _():
            ffn(scatter_prev=True)

        @pl.when(n_prev != MOE_ROWS)
        def _():
            scatter_rows(b - 1, n_prev)
            ffn(scatter_prev=False)

        wait_scatter(n_prev)
        ybuf[...] = acc[...] * gate_ref[0]

        @pl.when(b == n_used - 1)
        def _():
            scatter_rows(b, nvalid_ref[b])
            wait_scatter(nvalid_ref[b])
            wait_gather()


def _expert_ffn(x1, fg, tok, dst, gate, block_e, n_used, n_valid, wg, wu, wd, nblk, out_rows):
    wspec = lambda shape: pl.BlockSpec(shape, lambda b, e, *_: (e[b], 0, 0))
    return pl.pallas_call(
        _expert_ffn_body,
        grid_spec=pltpu.PrefetchScalarGridSpec(
            num_scalar_prefetch=5, grid=(nblk,),
            in_specs=[pl.BlockSpec(memory_space=pl.ANY),
                      pl.BlockSpec((1, D_MODEL), lambda b, *_: (0, 0)),
                      pl.BlockSpec((1, MOE_ROWS, 1), lambda b, *_: (b, 0, 0)),
                      wspec((1, D_MODEL, D_FF)), wspec((1, D_MODEL, D_FF)), wspec((1, D_FF, D_MODEL))],
            out_specs=pl.BlockSpec(memory_space=pl.ANY),
            scratch_shapes=[pltpu.VMEM((MOE_ROWS, D_MODEL), F32)] * 3
                           + [pltpu.SemaphoreType.DMA(()), pltpu.SemaphoreType.DMA(())]),
        out_shape=jax.ShapeDtypeStruct((out_rows, D_MODEL), F32),
        compiler_params=pltpu.CompilerParams(dimension_semantics=("arbitrary",),
                                             vmem_limit_bytes=MOE_VMEM_LIMIT),
        name="moe_expert_ffn",
    )(block_e, n_used, n_valid, tok, dst, x1, fg, gate.reshape(nblk, MOE_ROWS, 1), wg, wu, wd)


def _combine_body(x_ref, y0_ref, y1_ref, o_ref):
    o_ref[...] = x_ref[...] + y0_ref[...] + y1_ref[...]


def _combine(x1, y2):
    t = x1.shape[0]
    nt = t // ROW_TILE
    return pl.pallas_call(
        _combine_body,
        grid=(nt,),
        in_specs=[pl.BlockSpec((ROW_TILE, D_MODEL), lambda i: (i, 0)),
                  pl.BlockSpec((ROW_TILE, D_MODEL), lambda i: (i, 0)),
                  pl.BlockSpec((ROW_TILE, D_MODEL), lambda i: (i + nt, 0))],
        out_specs=pl.BlockSpec((ROW_TILE, D_MODEL), lambda i: (i, 0)),
        out_shape=jax.ShapeDtypeStruct((t, D_MODEL), F32),
        compiler_params=_cparams(("parallel",)),
        name="moe_combine",
    )(x1, y2, y2)


def _moe(x1, fg, logits, wg, wu, wd):
    t = x1.shape[0]
    n_assign = t * TOP_K
    nblk = n_assign // MOE_ROWS + N_EXPERTS
    cap = nblk * MOE_ROWS
    top_v, top_i = lax.top_k(logits, TOP_K)
    gates = jax.nn.softmax(top_v, axis=-1).reshape(-1)
    flat_e = top_i.reshape(-1).astype(jnp.int32)
    order = jnp.argsort(flat_e, stable=True).astype(jnp.int32)
    experts = jnp.arange(N_EXPERTS, dtype=jnp.int32)
    counts = jnp.sum((flat_e[:, None] == experts[None, :]).astype(jnp.int32), axis=0)
    start = jnp.cumsum(counts) - counts
    padded = ((counts + MOE_ROWS - 1) // MOE_ROWS) * MOE_ROWS
    pend = jnp.cumsum(padded)
    pstart = pend - padded
    p = jnp.arange(cap, dtype=jnp.int32)
    e_p = jnp.minimum(jnp.sum((p[:, None] >= pend[None, :]).astype(jnp.int32), axis=1), N_EXPERTS - 1)
    rank = p - pstart[e_p]
    valid = jnp.logical_and(rank < counts[e_p], p < pend[-1])
    a_p = order[jnp.clip(start[e_p] + rank, 0, n_assign - 1)]
    tok = jnp.where(valid, a_p // TOP_K, 0).astype(jnp.int32)
    dst = jnp.where(valid, (a_p % TOP_K) * t + a_p // TOP_K, 0).astype(jnp.int32)
    gate = jnp.where(valid, gates[a_p], 0.0).astype(F32)
    block_e = e_p[::MOE_ROWS]
    n_used = (pend[-1:] // MOE_ROWS).astype(jnp.int32)
    n_valid = jnp.sum(valid.reshape(nblk, MOE_ROWS).astype(jnp.int32), axis=1)
    y2 = _expert_ffn(x1, fg, tok, dst, gate, block_e, n_used, n_valid, wg, wu, wd, nblk, n_assign)
    return _combine(x1, y2)


def kernel(x, attn_norm_g, w_in, q_norm_g, k_norm_g, sinks, lam_re, lam_im, log_dt, b_re, b_im,
           c_re, c_im, d_skip, w_glu, attn_out_g, ssm_out_g, w_o, ffn_norm_g, dense_wg, dense_wu,
           dense_wd, router_w, moe_wg, moe_wu, moe_wd):
    bsz, seq, _ = x.shape
    depth = w_in.shape[0]
    assert SSM_SEQS % bsz == 0 and seq % (SSM_CHUNK * (SSM_SEQS // bsz)) == 0
    assert seq % ROW_TILE == 0 and ROW_TILE % ATTN_BLOCK == 0
    x2d = x.reshape(bsz * seq, D_MODEL).astype(F32)
    head = jnp.arange(ATTN_WIDTH) // HEAD_DIM
    avg = jnp.where(head[:, None] == head[None, :], 1.0 / HEAD_DIM, 0.0).astype(BF16)
    eye = jnp.eye(ROW_TILE, dtype=BF16)
    n_steps = seq // (SSM_CHUNK * (SSM_SEQS // bsz))
    row = lambda v: v.astype(F32).reshape(1, -1)
    col = lambda v: v.astype(F32).reshape(-1, 1)
    ko, vo, uo = ATTN_WIDTH, ATTN_WIDTH + KV_WIDTH, ATTN_WIDTH + 2 * KV_WIDTH
    for l in range(depth):
        qg = jnp.tile(q_norm_g[l].astype(F32), N_HEADS) * (HEAD_DIM ** -0.5)
        kg = jnp.tile(k_norm_g[l].astype(F32), N_KV_HEADS)
        w = w_in[l].astype(BF16)
        qt, k, v3, u3 = _inproj(x2d, row(attn_norm_g[l]), w[:, :ko].T, w[:, ko:vo], w[:, vo:uo].T,
                                w[:, uo:], col(qg), row(kg), avg)
        attn = _attention(qt, k, v3, sinks[l].astype(F32), col(attn_out_g[l]), eye, seq)
        tables = _ssm_tables(lam_re[l], lam_im[l], log_dt[l], b_re[l], b_im[l], c_re[l], c_im[l],
                             d_skip[l], n_steps)
        y3 = _ssm(u3, tables, eye[:LANES, :LANES], bsz, seq)
        i = l // 2
        router = None
        if l % 2 == 1:
            rw = router_w[i].astype(F32)
            rw_hi = rw.astype(BF16)
            router = (rw_hi, (rw - rw_hi.astype(F32)).astype(BF16))
        x1, second = _mix(x2d, attn, y3, w_glu[l].astype(BF16), row(ssm_out_g[l]),
                          w_o[l].astype(BF16), row(ffn_norm_g[l]), router)
        if l % 2 == 0:
            x2d = _dense_ffn(x1, second, dense_wg[i].astype(BF16), dense_wu[i].astype(BF16),
                             dense_wd[i].astype(BF16))
        else:
            x2d = _moe(x1, row(ffn_norm_g[l]), second, moe_wg[i].astype(BF16), moe_wu[i].astype(BF16),
                       moe_wd[i].astype(BF16))
    return x2d.reshape(bsz, seq, D_MODEL)
```

```python
import functools
import math

import jax
import jax.numpy as jnp
from jax import lax
from jax.experimental import pallas as pl
from jax.experimental.pallas import tpu as pltpu

F32 = jnp.float32
BF16 = jnp.bfloat16

D_MODEL = 1024
HEAD_DIM = 64
N_HEADS = 8
N_KV_HEADS = 2
GQA = N_HEADS // N_KV_HEADS
ATTN_WIDTH = N_HEADS * HEAD_DIM
KV_WIDTH = N_KV_HEADS * HEAD_DIM
ATTN_BLOCK = 128
SSM_WIDTH = D_MODEL - ATTN_WIDTH
SSM_GROUP_DIM = 16
SSM_GROUPS = SSM_WIDTH // SSM_GROUP_DIM
SSM_STATE = 64
IN_WIDTH = ATTN_WIDTH + 2 * KV_WIDTH + SSM_WIDTH
D_FF = 3584
N_EXPERTS = 8
TOP_K = 2
NORM_EPS = 1e-6
NEG_INF = -1e30

LANES = 128
ROW_TILE = 512
SSM_CHUNK = 16
SSM_SEQS = 8
SSM_VEC = SSM_CHUNK * SSM_GROUP_DIM
SSM_LANE_GROUPS = LANES // SSM_GROUP_DIM
FF_CHUNK = 512
MOE_ROWS = 256
VMEM_LIMIT = 56 * 1024 * 1024
MOE_VMEM_LIMIT = 60 * 1024 * 1024

NT_DIMS = (((1,), (1,)), ((), ()))


def _cparams(sem, vmem=VMEM_LIMIT):
    return pltpu.CompilerParams(dimension_semantics=sem, vmem_limit_bytes=vmem)


def _const_spec(shape):
    n = len(shape)
    return pl.BlockSpec(shape, lambda *_: (0,) * n)


def _resident_spec(shape):
    n = len(shape)
    return pl.BlockSpec(shape, lambda *_: (0,) * n, pipeline_mode=pl.Buffered(1))


def _inproj_body(x_ref, g_ref, wqt_ref, wk_ref, wvt_ref, wu_ref, qg_ref, kg_ref, avg_ref,
                 qt_out, k_out, v3_out, u3_out, uscr):
    x = x_ref[...]
    ms = jnp.mean(x * x, axis=-1, keepdims=True)
    hn = (x * lax.rsqrt(ms + NORM_EPS) * g_ref[...]).astype(BF16)
    qt = lax.dot_general(wqt_ref[...], hn, NT_DIMS, preferred_element_type=F32)
    qms = jnp.dot(avg_ref[...], (qt * qt).astype(BF16), preferred_element_type=F32)
    qt_out[...] = (qt * lax.rsqrt(qms + NORM_EPS) * qg_ref[...]).astype(BF16)
    k = jnp.dot(hn, wk_ref[...], preferred_element_type=F32)
    kms = jnp.dot((k * k).astype(BF16), avg_ref[:KV_WIDTH, :KV_WIDTH], preferred_element_type=F32)
    k_out[...] = (k * lax.rsqrt(kms + NORM_EPS) * kg_ref[...]).astype(BF16)
    vt = lax.dot_general(wvt_ref[...], hn, NT_DIMS, preferred_element_type=F32)
    for b in range(ROW_TILE // ATTN_BLOCK):
        v3_out[b] = vt[:, b * ATTN_BLOCK:(b + 1) * ATTN_BLOCK].astype(BF16)
    u = jnp.dot(hn, wu_ref[...], preferred_element_type=F32)
    for lb in range(SSM_WIDTH // LANES):
        uscr[lb] = u[:, lb * LANES:(lb + 1) * LANES]
    for t in range(SSM_CHUNK):
        for lb in range(SSM_WIDTH // LANES):
            u3_out[t, :, lb * LANES:(lb + 1) * LANES] = (
                uscr[lb, pl.ds(t, ROW_TILE // SSM_CHUNK, stride=SSM_CHUNK), :].astype(BF16))


def _inproj(x2d, g, wqt, wk, wvt, wu, qg, kg, avg):
    t = x2d.shape[0]
    cpt = ROW_TILE // SSM_CHUNK
    return pl.pallas_call(
        _inproj_body,
        grid=(t // ROW_TILE,),
        in_specs=[pl.BlockSpec((ROW_TILE, D_MODEL), lambda i: (i, 0)),
                  _const_spec((1, D_MODEL)), _const_spec((ATTN_WIDTH, D_MODEL)),
                  _const_spec((D_MODEL, KV_WIDTH)), _const_spec((KV_WIDTH, D_MODEL)),
                  _const_spec((D_MODEL, SSM_WIDTH)), _const_spec((ATTN_WIDTH, 1)),
                  _const_spec((1, KV_WIDTH)), _const_spec((ATTN_WIDTH, ATTN_WIDTH))],
        out_specs=[pl.BlockSpec((ATTN_WIDTH, ROW_TILE), lambda i: (0, i)),
                   pl.BlockSpec((ROW_TILE, KV_WIDTH), lambda i: (i, 0)),
                   pl.BlockSpec((ROW_TILE // ATTN_BLOCK, KV_WIDTH, ATTN_BLOCK), lambda i: (i, 0, 0)),
                   pl.BlockSpec((SSM_CHUNK, cpt, SSM_WIDTH), lambda i: (0, i, 0))],
        out_shape=[jax.ShapeDtypeStruct((ATTN_WIDTH, t), BF16),
                   jax.ShapeDtypeStruct((t, KV_WIDTH), BF16),
                   jax.ShapeDtypeStruct((t // ATTN_BLOCK, KV_WIDTH, ATTN_BLOCK), BF16),
                   jax.ShapeDtypeStruct((SSM_CHUNK, t // SSM_CHUNK, SSM_WIDTH), BF16)],
        scratch_shapes=[pltpu.VMEM((SSM_WIDTH // LANES, ROW_TILE, LANES), F32)],
        compiler_params=_cparams(("parallel",)),
        name="inproj",
    )(x2d, g, wqt, wk, wvt, wu, qg, kg, avg)


def _attn_body(sink_ref, qt_ref, k_ref, v3_ref, g_ref, eye_ref, o_ref, at_scr, *, blocks_per_seq):
    i = pl.program_id(0)
    key = lax.broadcasted_iota(jnp.int32, (ATTN_BLOCK, ATTN_BLOCK), 0)
    qry = lax.broadcasted_iota(jnp.int32, (ATTN_BLOCK, ATTN_BLOCK), 1)
    cur_ok = key <= qry
    prev_ok = key > qry
    zpad = jnp.zeros((HEAD_DIM, ATTN_BLOCK), BF16)
    nblk = ROW_TILE // ATTN_BLOCK
    for blk in range(nblk):
        gblk = i * nblk + blk
        pblk = jnp.maximum(gblk - 1, 0)
        row0 = pl.multiple_of(gblk * ATTN_BLOCK, ATTN_BLOCK)
        prev0 = pl.multiple_of(pblk * ATTN_BLOCK, ATTN_BLOCK)
        has_prev = (gblk % blocks_per_seq) != 0
        kc = k_ref[pl.ds(row0, ATTN_BLOCK), :]
        kp = k_ref[pl.ds(prev0, ATTN_BLOCK), :]
        vc = v3_ref[gblk]
        vp = v3_ref[pblk]
        qb = qt_ref[:, blk * ATTN_BLOCK:(blk + 1) * ATTN_BLOCK]
        pmask = jnp.logical_and(prev_ok, has_prev)
        outs = []
        for h in range(N_HEADS):
            kv = h // GQA
            qh = qb[h * HEAD_DIM:(h + 1) * HEAD_DIM, :]
            qpad = jnp.concatenate([qh, zpad] if kv == 0 else [zpad, qh], axis=0)
            sc = jnp.dot(kc, qpad, preferred_element_type=F32)
            sp = jnp.dot(kp, qpad, preferred_element_type=F32)
            sc = jnp.where(cur_ok, sc, NEG_INF)
            sp = jnp.where(pmask, sp, NEG_INF)
            sink = sink_ref[h]
            m = jnp.maximum(jnp.maximum(jnp.max(sc, axis=0, keepdims=True),
                                        jnp.max(sp, axis=0, keepdims=True)), sink)
            pc = jnp.exp(sc - m)
            pp = jnp.exp(sp - m)
            den = (jnp.sum(pc, axis=0, keepdims=True) + jnp.sum(pp, axis=0, keepdims=True)
                   + jnp.exp(sink - m))
            vs = slice(kv * HEAD_DIM, (kv + 1) * HEAD_DIM)
            o = (jnp.dot(vc[vs, :], pc.astype(BF16), preferred_element_type=F32)
                 + jnp.dot(vp[vs, :], pp.astype(BF16), preferred_element_type=F32))
            outs.append(o / den)
        a = jnp.concatenate(outs, axis=0)
        ms = jnp.mean(a * a, axis=0, keepdims=True)
        at_scr[:, blk * ATTN_BLOCK:(blk + 1) * ATTN_BLOCK] = (
            a * lax.rsqrt(ms + NORM_EPS) * g_ref[...]).astype(BF16)
    o_ref[...] = lax.dot_general(eye_ref[...], at_scr[...], NT_DIMS,
                                 preferred_element_type=F32).astype(BF16)


def _attention(qt, k, v3, sinks, gcol, eye, seq):
    t = k.shape[0]
    return pl.pallas_call(
        functools.partial(_attn_body, blocks_per_seq=seq // ATTN_BLOCK),
        grid=(t // ROW_TILE,),
        in_specs=[pl.BlockSpec(memory_space=pltpu.SMEM),
                  pl.BlockSpec((ATTN_WIDTH, ROW_TILE), lambda i: (0, i)),
                  _const_spec((t, KV_WIDTH)), _const_spec((t // ATTN_BLOCK, KV_WIDTH, ATTN_BLOCK)),
                  _const_spec((ATTN_WIDTH, 1)), _const_spec((ROW_TILE, ROW_TILE))],
        out_specs=pl.BlockSpec((ROW_TILE, ATTN_WIDTH), lambda i: (i, 0)),
        out_shape=jax.ShapeDtypeStruct((t, ATTN_WIDTH), BF16),
        scratch_shapes=[pltpu.VMEM((ATTN_WIDTH, ROW_TILE), BF16)],
        compiler_params=_cparams(("parallel",)),
        name="swa_attention",
    )(sinks, qt, k, v3, gcol, eye)


def _ssm_tables_body(lam_ref, lamc_ref, b_ref, c_ref, erep_ref, etile_ref,
                     m_out, wre_out, wim_out, vre_out, vim_out, k_out, apw_out, *, n_steps):
    hi = lax.Precision.HIGHEST
    lanes = 2 * SSM_STATE
    ch, gd = SSM_CHUNK, SSM_GROUP_DIM
    lr, li, dt = lam_ref[0, 0:1, :], lam_ref[0, 1:2, :], lam_ref[0, 2:3, :]

    def apow(p):
        mag = jnp.exp(p * (lr * dt))
        ang = p * (li * dt)
        return mag * jnp.cos(ang), mag * jnp.sin(ang)

    tau = lax.broadcasted_iota(jnp.int32, (ch, lanes), 0).astype(F32)
    p0r, p0i = apow(tau)
    p1r, p1i = apow(tau + 1.0)
    step = lax.broadcasted_iota(jnp.int32, (n_steps, lanes), 0).astype(F32)
    qr, qi = apow(step * float(ch))
    apw_out[0, 0] = qr
    apw_out[0, 1] = qi
    sel = lax.broadcasted_iota(jnp.int32, (8, lanes), 0)
    kr, ki = apow(jnp.where(sel == 0, float(ch), float(ch * n_steps)))
    k_out[0, 0] = kr[0:2]
    k_out[0, 1] = ki[0:2]

    lrc, lic, dtc = lamc_ref[0, :, 0:1], lamc_ref[0, :, 1:2], lamc_ref[0, :, 2:3]
    magc = jnp.exp(lrc * dtc)
    ab_re, ab_im = magc * jnp.cos(lic * dtc), magc * jnp.sin(lic * dtc)
    nr = ab_re - 1.0
    den = lrc * lrc + lic * lic
    f_re = (nr * lrc + ab_im * lic) / den
    f_im = (ab_im * lrc - nr * lic) / den
    br, bi = b_ref[0, 0], b_ref[0, 1]
    bb_re = f_re * br - f_im * bi
    bb_im = f_re * bi + f_im * br
    bx_re = jnp.dot(bb_re, etile_ref[...], precision=hi, preferred_element_type=F32)
    bx_im = jnp.dot(bb_im, etile_ref[...], precision=hi, preferred_element_type=F32)
    jrev = float(ch - 1) - lax.broadcasted_iota(jnp.int32, (lanes, ch), 1).astype(F32)
    mg = jnp.exp(jrev * (lrc * dtc))
    an = jrev * (lic * dtc)
    px_re = jnp.dot(mg * jnp.cos(an), erep_ref[...], precision=hi, preferred_element_type=F32)
    px_im = jnp.dot(mg * jnp.sin(an), erep_ref[...], precision=hi, preferred_element_type=F32)
    w_re = px_re * bx_re - px_im * bx_im
    w_im = px_re * bx_im + px_im * bx_re
    first = lax.broadcasted_iota(jnp.int32, (lanes, SSM_VEC), 0) < SSM_STATE
    pack = lambda w: jnp.concatenate([jnp.where(first, w, 0.0), jnp.where(first, 0.0, w)], axis=1)
    wre_out[0] = pack(w_re).astype(BF16)
    wim_out[0] = pack(w_im).astype(BF16)

    blk = lax.broadcasted_iota(jnp.int32, (SSM_VEC, SSM_VEC), 1) // gd
    for g in range(2):
        cr, ci = c_ref[g, 0], c_ref[g, 1]
        stack = lambda x: x.reshape(ch * gd, lanes)
        ca_re = stack(cr[None] * p0r[:, None, :] - ci[None] * p0i[:, None, :])
        ca_im = stack(cr[None] * p0i[:, None, :] + ci[None] * p0r[:, None, :])
        kt = (jnp.dot(ca_re, bx_re, precision=hi, preferred_element_type=F32)
              - jnp.dot(ca_im, bx_im, precision=hi, preferred_element_type=F32))
        m = jnp.where(blk == 0, kt, 0.0)
        for j in range(1, ch):
            shifted = jnp.concatenate([jnp.zeros((j * gd, SSM_VEC), F32), kt[:SSM_VEC - j * gd]], axis=0)
            m = jnp.where(blk == j, shifted, m)
        m_out[g] = m.astype(BF16)
        vre_out[g] = stack(cr[None] * p1r[:, None, :] - ci[None] * p1i[:, None, :]).astype(BF16)
        vim_out[g] = (-stack(cr[None] * p1i[:, None, :] + ci[None] * p1r[:, None, :])).astype(BF16)


def _ssm_tables(lam_re, lam_im, log_dt, b_re, b_im, c_re, c_im, d_skip, n_steps):
    g_, n_, c_ = SSM_GROUPS, SSM_STATE, SSM_GROUP_DIM
    np_ = g_ // 2
    lanes = 2 * n_
    dt = jnp.broadcast_to(jnp.exp(log_dt.astype(F32))[:, None], (g_, n_))
    lam = jnp.stack([lam_re.astype(F32), lam_im.astype(F32), dt], axis=1)
    lam_row = lam.reshape(np_, 2, 3, n_).transpose(0, 2, 1, 3).reshape(np_, 3, lanes)
    lam_col = lam_row.transpose(0, 2, 1)
    b2 = jnp.stack([b_re.astype(F32).reshape(np_, lanes, c_), b_im.astype(F32).reshape(np_, lanes, c_)],
                   axis=1)
    c2 = jnp.stack([c_re.astype(F32), c_im.astype(F32)], axis=1)
    z = jnp.zeros_like(c2)
    even = (jnp.arange(g_) % 2 == 0)[:, None, None, None]
    c2 = jnp.where(even, jnp.concatenate([c2, z], axis=3), jnp.concatenate([z, c2], axis=3))
    col = jnp.arange(SSM_VEC)
    erep = (col[None, :] // c_ == jnp.arange(SSM_CHUNK)[:, None]).astype(F32)
    etile = (col[None, :] % c_ == jnp.arange(c_)[:, None]).astype(F32)
    lead = lambda shape: pl.BlockSpec(shape, lambda p: (p,) + (0,) * (len(shape) - 1))
    m, wre, wim, vre, vim, consts, apw = pl.pallas_call(
        functools.partial(_ssm_tables_body, n_steps=n_steps),
        grid=(np_,),
        in_specs=[lead((1, 3, lanes)), lead((1, lanes, 3)), lead((1, 2, lanes, c_)),
                  lead((2, 2, c_, lanes)), _const_spec((SSM_CHUNK, SSM_VEC)), _const_spec((c_, SSM_VEC))],
        out_specs=[lead((2, SSM_VEC, SSM_VEC)), lead((1, lanes, 2 * SSM_VEC)), lead((1, lanes, 2 * SSM_VEC)),
                   lead((2, SSM_VEC, lanes)), lead((2, SSM_VEC, lanes)),
                   lead((1, 2, 2, lanes)), lead((1, 2, n_steps, lanes))],
        out_shape=[jax.ShapeDtypeStruct((g_, SSM_VEC, SSM_VEC), BF16),
                   jax.ShapeDtypeStruct((np_, lanes, 2 * SSM_VEC), BF16),
                   jax.ShapeDtypeStruct((np_, lanes, 2 * SSM_VEC), BF16),
                   jax.ShapeDtypeStruct((g_, SSM_VEC, lanes), BF16),
                   jax.ShapeDtypeStruct((g_, SSM_VEC, lanes), BF16),
                   jax.ShapeDtypeStruct((np_, 2, 2, lanes), F32),
                   jax.ShapeDtypeStruct((np_, 2, n_steps, lanes), F32)],
        compiler_params=_cparams(("parallel",)),
        name="s5_operators",
    )(lam_row, lam_col, b2, c2, erep, etile)
    dcol = jnp.tile(d_skip.astype(F32).reshape(g_, 1, c_), (1, SSM_CHUNK, 1)).reshape(g_, SSM_VEC, 1)
    return m, wre, wim, vre, vim, consts, apw, dcol


def _ssm_body(u_ref, eye_ref, m_ref, wre_ref, wim_ref, vre_ref, vim_ref, c_ref, apw_ref, d_ref,
              y_ref, ut, yt, pre, pim, sre, sim, *, n_steps, halves):
    lanes = 2 * SSM_STATE
    nk = u_ref.shape[1]
    gd = SSM_GROUP_DIM
    for t in range(SSM_CHUNK):
        xt = lax.dot_general(eye_ref[...], u_ref[t], NT_DIMS,
                             preferred_element_type=F32).astype(BF16)
        for g in range(SSM_LANE_GROUPS):
            ut[g, t * gd:(t + 1) * gd, :] = xt[g * gd:(g + 1) * gd, :]

    row = lax.broadcasted_iota(jnp.int32, (SSM_SEQS, lanes), 0)
    seq_start = (row % halves) == 0
    shift = lambda x: jnp.where(seq_start, 0.0, pltpu.roll(x, 1, axis=0))

    npairs = SSM_LANE_GROUPS // 2
    for pp in range(npairs):
        ucat = jnp.concatenate([ut[2 * pp], ut[2 * pp + 1]], axis=0)
        pre[pp] = jnp.dot(wre_ref[pp], ucat, preferred_element_type=F32).T
        pim[pp] = jnp.dot(wim_ref[pp], ucat, preferred_element_type=F32).T

    a16 = [(jnp.broadcast_to(c_ref[pp, 0, 0:1, :], (SSM_SEQS, lanes)),
            jnp.broadcast_to(c_ref[pp, 1, 0:1, :], (SSM_SEQS, lanes))) for pp in range(npairs)]

    def step(i, carry):
        idx = pl.ds(i, SSM_SEQS, stride=n_steps)
        out = []
        for pp in range(npairs):
            s_re, s_im = carry[pp]
            ar, ai = a16[pp]
            sre[pp, idx, :] = s_re
            sim[pp, idx, :] = s_im
            out.append((ar * s_re - ai * s_im + pre[pp, idx, :],
                        ar * s_im + ai * s_re + pim[pp, idx, :]))
        return tuple(out)

    zero = jnp.zeros((SSM_SEQS, lanes), F32)
    ends = lax.fori_loop(0, n_steps, step, ((zero, zero),) * npairs, unroll=2)

    for pp in range(npairs):
        e_re, e_im = ends[pp]
        br = jnp.broadcast_to(c_ref[pp, 0, 1:2, :], (SSM_SEQS, lanes))
        bi = jnp.broadcast_to(c_ref[pp, 1, 1:2, :], (SSM_SEQS, lanes))
        c_re = jnp.zeros_like(e_re)
        c_im = jnp.zeros_like(e_im)
        for _ in range(halves - 1):
            n_re = e_re + br * c_re - bi * c_im
            n_im = e_im + br * c_im + bi * c_re
            c_re, c_im = shift(n_re), shift(n_im)

        apr = apw_ref[pp, 0][None, :, :]
        api = apw_ref[pp, 1][None, :, :]
        s3r = sre[pp].reshape(SSM_SEQS, n_steps, lanes)
        s3i = sim[pp].reshape(SSM_SEQS, n_steps, lanes)
        t_re = (s3r + apr * c_re[:, None, :] - api * c_im[:, None, :]).reshape(nk, lanes)
        t_im = (s3i + apr * c_im[:, None, :] + api * c_re[:, None, :]).reshape(nk, lanes)
        tr_hi = t_re.astype(BF16)
        tr_lo = (t_re - tr_hi.astype(F32)).astype(BF16)
        ti_hi = t_im.astype(BF16)
        ti_lo = (t_im - ti_hi.astype(F32)).astype(BF16)
        for g in (2 * pp, 2 * pp + 1):
            ug = ut[g]
            y = jnp.dot(m_ref[g], ug, preferred_element_type=F32)
            y += lax.dot_general(vre_ref[g], tr_hi, NT_DIMS, preferred_element_type=F32)
            y += lax.dot_general(vre_ref[g], tr_lo, NT_DIMS, preferred_element_type=F32)
            y += lax.dot_general(vim_ref[g], ti_hi, NT_DIMS, preferred_element_type=F32)
            y += lax.dot_general(vim_ref[g], ti_lo, NT_DIMS, preferred_element_type=F32)
            yt[g] = y + d_ref[g] * ug.astype(F32)

    for t in range(SSM_CHUNK):
        rows = jnp.concatenate([yt[g, t * gd:(t + 1) * gd, :] for g in range(SSM_LANE_GROUPS)], axis=0)
        y_ref[t] = rows.T


def _ssm(u3, tables, eye, bsz, seq):
    m, wre, wim, vre, vim, consts, apw, dcol = tables
    nk = u3.shape[1]
    halves = SSM_SEQS // bsz
    n_steps = seq // (SSM_CHUNK * halves)
    lanes = 2 * SSM_STATE
    lg = SSM_LANE_GROUPS
    lead = lambda shape: pl.BlockSpec(shape, lambda b: (b,) + (0,) * (len(shape) - 1))
    return pl.pallas_call(
        functools.partial(_ssm_body, n_steps=n_steps, halves=halves),
        grid=(SSM_GROUPS // lg,),
        in_specs=[pl.BlockSpec((SSM_CHUNK, nk, LANES), lambda b: (0, 0, b)),
                  _const_spec((LANES, LANES)),
                  lead((lg, SSM_VEC, SSM_VEC)),
                  lead((lg // 2, lanes, 2 * SSM_VEC)), lead((lg // 2, lanes, 2 * SSM_VEC)),
                  lead((lg, SSM_VEC, lanes)), lead((lg, SSM_VEC, lanes)),
                  lead((lg // 2, 2, 2, lanes)), lead((lg // 2, 2, n_steps, lanes)),
                  lead((lg, SSM_VEC, 1))],
        out_specs=pl.BlockSpec((SSM_CHUNK, nk, LANES), lambda b: (0, 0, b)),
        out_shape=jax.ShapeDtypeStruct((SSM_CHUNK, nk, SSM_WIDTH), F32),
        scratch_shapes=[pltpu.VMEM((lg, SSM_VEC, nk), BF16), pltpu.VMEM((lg, SSM_VEC, nk), F32)]
                       + [pltpu.VMEM((lg // 2, nk, lanes), F32)] * 4,
        compiler_params=_cparams(("parallel",)),
        name="s5_ssm",
    )(u3, eye, m, wre, wim, vre, vim, consts, apw, dcol)


def _mix_body(*refs, with_router):
    if with_router:
        x_ref, a_ref, y3_ref, wglu_ref, sg_ref, wo_ref, fg_ref, rwt_ref, x1_out, lg_out, yscr = refs
    else:
        x_ref, a_ref, y3_ref, wglu_ref, sg_ref, wo_ref, fg_ref, x1_out, hn_out, yscr = refs
    nlb = SSM_WIDTH // LANES
    for t in range(SSM_CHUNK):
        for lb in range(nlb):
            yscr[lb, pl.ds(t, ROW_TILE // SSM_CHUNK, stride=SSM_CHUNK), :] = (
                y3_ref[t, :, lb * LANES:(lb + 1) * LANES])
    y = jax.nn.gelu(jnp.concatenate([yscr[lb] for lb in range(nlb)], axis=1))
    z = y * jax.nn.sigmoid(jnp.dot(y.astype(BF16), wglu_ref[...], preferred_element_type=F32))
    ms = jnp.mean(z * z, axis=-1, keepdims=True)
    sn = (z * lax.rsqrt(ms + NORM_EPS) * sg_ref[...]).astype(BF16)
    x1 = (x_ref[...]
          + jnp.dot(a_ref[...], wo_ref[:ATTN_WIDTH, :], preferred_element_type=F32)
          + jnp.dot(sn, wo_ref[ATTN_WIDTH:, :], preferred_element_type=F32))
    x1_out[...] = x1
    ms1 = jnp.mean(x1 * x1, axis=-1, keepdims=True)
    hn = x1 * lax.rsqrt(ms1 + NORM_EPS) * fg_ref[...]
    if with_router:
        lane = lax.broadcasted_iota(jnp.int32, lg_out.shape, 1)
        lg = jnp.zeros(lg_out.shape, F32)
        for e in range(lg_out.shape[1]):
            lg = jnp.where(lane == e, jnp.sum(hn * rwt_ref[e:e + 1, :], axis=-1, keepdims=True), lg)
        lg_out[...] = lg
    else:
        hn_out[...] = hn.astype(BF16)


def _mix(x2d, attn, y3, wglu, sg, wo, fg, router=None):
    t = x2d.shape[0]
    cpt = ROW_TILE // SSM_CHUNK
    row = lambda w: pl.BlockSpec((ROW_TILE, w), lambda i: (i, 0))
    in_specs = [row(D_MODEL), row(ATTN_WIDTH),
                pl.BlockSpec((SSM_CHUNK, cpt, SSM_WIDTH), lambda i: (0, i, 0)),
                _const_spec((SSM_WIDTH, SSM_WIDTH)), _const_spec((1, SSM_WIDTH)),
                _const_spec((D_MODEL, D_MODEL)), _const_spec((1, D_MODEL))]
    args = [x2d, attn, y3, wglu, sg, wo, fg]
    if router is not None:
        ne = router.shape[0]
        in_specs.append(_const_spec((ne, D_MODEL)))
        args.append(router)
        second = (row(ne), jax.ShapeDtypeStruct((t, ne), F32))
    else:
        second = (row(D_MODEL), jax.ShapeDtypeStruct((t, D_MODEL), BF16))
    return pl.pallas_call(
        functools.partial(_mix_body, with_router=router is not None),
        grid=(t // ROW_TILE,),
        in_specs=in_specs, out_specs=[row(D_MODEL), second[0]],
        out_shape=[jax.ShapeDtypeStruct((t, D_MODEL), F32), second[1]],
        scratch_shapes=[pltpu.VMEM((SSM_WIDTH // LANES, ROW_TILE, LANES), F32)],
        compiler_params=_cparams(("parallel",)),
        name="mix_outproj",
    )(*args)


def _ffn_body(x_ref, h_ref, wg_ref, wu_ref, wd_ref, o_ref, acc):
    h = h_ref[...]
    acc[...] = x_ref[...]
    for c in range(D_FF // FF_CHUNK):
        sl = slice(c * FF_CHUNK, (c + 1) * FF_CHUNK)
        g = jnp.dot(h, wg_ref[:, sl], preferred_element_type=F32)
        u = jnp.dot(h, wu_ref[:, sl], preferred_element_type=F32)
        a = (jax.nn.silu(g) * u).astype(BF16)
        acc[...] += jnp.dot(a, wd_ref[sl, :], preferred_element_type=F32)
    o_ref[...] = acc[...]


def _dense_ffn(x1, hn, wg, wu, wd):
    t = x1.shape[0]
    row = lambda w: pl.BlockSpec((ROW_TILE, w), lambda i: (i, 0))
    return pl.pallas_call(
        _ffn_body,
        grid=(t // ROW_TILE,),
        in_specs=[row(D_MODEL), row(D_MODEL), _resident_spec((D_MODEL, D_FF)),
                  _resident_spec((D_MODEL, D_FF)), _resident_spec((D_FF, D_MODEL))],
        out_specs=row(D_MODEL),
        out_shape=jax.ShapeDtypeStruct((t, D_MODEL), F32),
        scratch_shapes=[pltpu.VMEM((ROW_TILE, D_MODEL), F32)],
        compiler_params=_cparams(("parallel",)),
        name="dense_ffn",
    )(x1, hn, wg, wu, wd)


def _expert_ffn_body(e_ref, nused_ref, nvalid_ref, tok_ref, dst_ref, x1_hbm, fg_ref,
                     wg_ref, wu_ref, wd_ref, out_hbm, xbuf, acc, ybuf, gsem, ssem):
    b = pl.program_id(0)
    n_used = nused_ref[0]
    n_ff = D_FF // FF_CHUNK
    rows_per_chunk = -(-MOE_ROWS // n_ff)

    def gather_row(blk, r):
        i = tok_ref[blk * MOE_ROWS + r]
        pltpu.make_async_copy(x1_hbm.at[pl.ds(i, 1)], xbuf.at[pl.ds(r, 1)], gsem).start()

    def scatter_row(blk, r):
        i = dst_ref[blk * MOE_ROWS + r]
        pltpu.make_async_copy(ybuf.at[pl.ds(r, 1)], out_hbm.at[pl.ds(i, 1)], ssem).start()

    def scatter_rows(blk, n):
        def body(r, carry):
            scatter_row(blk, r)
            return carry
        lax.fori_loop(0, n, body, 0)

    def wait_gather():
        pltpu.make_async_copy(x1_hbm.at[pl.ds(0, MOE_ROWS)], xbuf, gsem).wait()

    def wait_scatter(n):
        n8 = pl.multiple_of(n & -8, 8)

        @pl.when(n8 > 0)
        def _():
            pltpu.make_async_copy(ybuf.at[pl.ds(0, n8)], out_hbm.at[pl.ds(0, n8)], ssem).wait()

        def one(_, carry):
            pltpu.make_async_copy(ybuf.at[pl.ds(0, 1)], out_hbm.at[pl.ds(0, 1)], ssem).wait()
            return carry
        lax.fori_loop(0, n - n8, one, 0)

    def ffn(scatter_prev):
        wait_gather()
        x = xbuf[...]
        ms = jnp.mean(x * x, axis=-1, keepdims=True)
        h = (x * lax.rsqrt(ms + NORM_EPS) * fg_ref[...]).astype(BF16)
        for c in range(n_ff):
            for r in range(c * rows_per_chunk, min((c + 1) * rows_per_chunk, MOE_ROWS)):
                gather_row(b + 1, r)
                if scatter_prev:
                    scatter_row(b - 1, r)
            sl = slice(c * FF_CHUNK, (c + 1) * FF_CHUNK)
            g = jnp.dot(h, wg_ref[0, :, sl], preferred_element_type=F32)
            u = jnp.dot(h, wu_ref[0, :, sl], preferred_element_type=F32)
            a = (jax.nn.silu(g) * u).astype(BF16)
            part = jnp.dot(a, wd_ref[0, sl, :], preferred_element_type=F32)
            if c == 0:
                acc[...] = part
            else:
                acc[...] += part

    @pl.when(b < n_used)
    def _():
        @pl.when(b == 0)
        def _():
            def body(r, carry):
                gather_row(0, r)
                return carry
            lax.fori_loop(0, MOE_ROWS, body, 0, unroll=8)

        n_prev = jnp.where(b >= 1, nvalid_ref[jnp.maximum(b - 1, 0)], 0)

        @pl.when(n_prev == MOE_ROWS)
        def _():
            ffn(scatter_prev=True)

        @pl.when(n_prev != MOE_ROWS)
        def _():
            scatter_rows(b - 1, n_prev)
            ffn(scatter_prev=False)

        wait_scatter(n_prev)
        ybuf[...] = acc[...]

        @pl.when(b == n_used - 1)
        def _():
            scatter_rows(b, nvalid_ref[b])
            wait_scatter(nvalid_ref[b])
            wait_gather()


def _expert_ffn(x1, fg, tok, dst, block_e, n_used, n_valid, wg, wu, wd, nblk, out_rows):
    wspec = lambda shape: pl.BlockSpec(shape, lambda b, e, *_: (e[b], 0, 0))
    return pl.pallas_call(
        _expert_ffn_body,
        grid_spec=pltpu.PrefetchScalarGridSpec(
            num_scalar_prefetch=5, grid=(nblk,),
            in_specs=[pl.BlockSpec(memory_space=pl.ANY),
                      pl.BlockSpec((1, D_MODEL), lambda b, *_: (0, 0)),
                      wspec((1, D_MODEL, D_FF)), wspec((1, D_MODEL, D_FF)), wspec((1, D_FF, D_MODEL))],
            out_specs=pl.BlockSpec(memory_space=pl.ANY),
            scratch_shapes=[pltpu.VMEM((MOE_ROWS, D_MODEL), F32)] * 3
                           + [pltpu.SemaphoreType.DMA(()), pltpu.SemaphoreType.DMA(())]),
        out_shape=jax.ShapeDtypeStruct((out_rows, D_MODEL), F32),
        compiler_params=pltpu.CompilerParams(dimension_semantics=("arbitrary",),
                                             vmem_limit_bytes=MOE_VMEM_LIMIT),
        name="moe_expert_ffn",
    )(block_e, n_used, n_valid, tok, dst, x1, fg, wg, wu, wd)


def _combine_body(x_ref, y0_ref, y1_ref, g_ref, o_ref):
    o_ref[...] = x_ref[...] + y0_ref[...] * g_ref[:, 0:1] + y1_ref[...] * g_ref[:, 1:2]


def _combine(x1, y2, gates):
    t = x1.shape[0]
    nt = t // ROW_TILE
    return pl.pallas_call(
        _combine_body,
        grid=(nt,),
        in_specs=[pl.BlockSpec((ROW_TILE, D_MODEL), lambda i: (i, 0)),
                  pl.BlockSpec((ROW_TILE, D_MODEL), lambda i: (i, 0)),
                  pl.BlockSpec((ROW_TILE, D_MODEL), lambda i: (i + nt, 0)),
                  pl.BlockSpec((ROW_TILE, TOP_K), lambda i: (i, 0))],
        out_specs=pl.BlockSpec((ROW_TILE, D_MODEL), lambda i: (i, 0)),
        out_shape=jax.ShapeDtypeStruct((t, D_MODEL), F32),
        compiler_params=_cparams(("parallel",)),
        name="moe_combine",
    )(x1, y2, y2, gates)


def _moe(x1, fg, logits, wg, wu, wd):
    t = x1.shape[0]
    n_assign = t * TOP_K
    nblk = n_assign // MOE_ROWS + N_EXPERTS
    cap = nblk * MOE_ROWS
    top_v, top_i = lax.top_k(logits, TOP_K)
    gates = jax.nn.softmax(top_v, axis=-1)
    flat_e = top_i.reshape(-1).astype(jnp.int32)
    order = jnp.argsort(flat_e, stable=True).astype(jnp.int32)
    experts = jnp.arange(N_EXPERTS, dtype=jnp.int32)
    counts = jnp.sum((flat_e[:, None] == experts[None, :]).astype(jnp.int32), axis=0)
    start = jnp.cumsum(counts) - counts
    padded = ((counts + MOE_ROWS - 1) // MOE_ROWS) * MOE_ROWS
    pend = jnp.cumsum(padded)
    pstart = pend - padded
    p = jnp.arange(cap, dtype=jnp.int32)
    e_p = jnp.minimum(jnp.sum((p[:, None] >= pend[None, :]).astype(jnp.int32), axis=1), N_EXPERTS - 1)
    rank = p - pstart[e_p]
    valid = jnp.logical_and(rank < counts[e_p], p < pend[-1])
    a_p = order[jnp.clip(start[e_p] + rank, 0, n_assign - 1)]
    tok = jnp.where(valid, a_p // TOP_K, 0).astype(jnp.int32)
    dst = jnp.where(valid, (a_p % TOP_K) * t + a_p // TOP_K, 0).astype(jnp.int32)
    block_e = e_p[::MOE_ROWS]
    n_used = (pend[-1:] // MOE_ROWS).astype(jnp.int32)
    n_valid = jnp.sum(valid.reshape(nblk, MOE_ROWS).astype(jnp.int32), axis=1)
    y2 = _expert_ffn(x1, fg, tok, dst, block_e, n_used, n_valid, wg, wu, wd, nblk, n_assign)
    return _combine(x1, y2, gates)


def kernel(x, attn_norm_g, w_in, q_norm_g, k_norm_g, sinks, lam_re, lam_im, log_dt, b_re, b_im,
           c_re, c_im, d_skip, w_glu, attn_out_g, ssm_out_g, w_o, ffn_norm_g, dense_wg, dense_wu,
           dense_wd, router_w, moe_wg, moe_wu, moe_wd):
    bsz, seq, _ = x.shape
    depth = w_in.shape[0]
    assert SSM_SEQS % bsz == 0 and seq % (SSM_CHUNK * (SSM_SEQS // bsz)) == 0
    assert seq % ROW_TILE == 0 and ROW_TILE % ATTN_BLOCK == 0
    x2d = x.reshape(bsz * seq, D_MODEL).astype(F32)
    head = jnp.arange(ATTN_WIDTH) // HEAD_DIM
    avg = jnp.where(head[:, None] == head[None, :], 1.0 / HEAD_DIM, 0.0).astype(BF16)
    eye = jnp.eye(ROW_TILE, dtype=BF16)
    n_steps = seq // (SSM_CHUNK * (SSM_SEQS // bsz))
    row = lambda v: v.astype(F32).reshape(1, -1)
    col = lambda v: v.astype(F32).reshape(-1, 1)
    ko, vo, uo = ATTN_WIDTH, ATTN_WIDTH + KV_WIDTH, ATTN_WIDTH + 2 * KV_WIDTH
    for l in range(depth):
        qg = jnp.tile(q_norm_g[l].astype(F32), N_HEADS) * (HEAD_DIM ** -0.5)
        kg = jnp.tile(k_norm_g[l].astype(F32), N_KV_HEADS)
        w = w_in[l].astype(BF16)
        qt, k, v3, u3 = _inproj(x2d, row(attn_norm_g[l]), w[:, :ko].T, w[:, ko:vo], w[:, vo:uo].T,
                                w[:, uo:], col(qg), row(kg), avg)
        attn = _attention(qt, k, v3, sinks[l].astype(F32), col(attn_out_g[l]), eye, seq)
        tables = _ssm_tables(lam_re[l], lam_im[l], log_dt[l], b_re[l], b_im[l], c_re[l], c_im[l],
                             d_skip[l], n_steps)
        y3 = _ssm(u3, tables, eye[:LANES, :LANES], bsz, seq)
        i = l // 2
        router = router_w[i].astype(F32).T if l % 2 == 1 else None
        x1, second = _mix(x2d, attn, y3, w_glu[l].astype(BF16), row(ssm_out_g[l]),
                          w_o[l].astype(BF16), row(ffn_norm_g[l]), router)
        if l % 2 == 0:
            x2d = _dense_ffn(x1, second, dense_wg[i].astype(BF16), dense_wu[i].astype(BF16),
                             dense_wd[i].astype(BF16))
        else:
            x2d = _moe(x1, row(ffn_norm_g[l]), second, moe_wg[i].astype(BF16), moe_wu[i].astype(BF16),
                       moe_wd[i].astype(BF16))
    return x2d.reshape(bsz, seq, D_MODEL)
```

```python
import functools
import math

import jax
import jax.numpy as jnp
from jax import lax
from jax.experimental import pallas as pl
from jax.experimental.pallas import tpu as pltpu

F32 = jnp.float32
BF16 = jnp.bfloat16

D_MODEL = 1024
HEAD_DIM = 64
N_HEADS = 8
N_KV_HEADS = 2
GQA = N_HEADS // N_KV_HEADS
ATTN_WIDTH = N_HEADS * HEAD_DIM
KV_WIDTH = N_KV_HEADS * HEAD_DIM
ATTN_BLOCK = 128
SSM_WIDTH = D_MODEL - ATTN_WIDTH
SSM_GROUP_DIM = 16
SSM_GROUPS = SSM_WIDTH // SSM_GROUP_DIM
SSM_STATE = 64
IN_WIDTH = ATTN_WIDTH + 2 * KV_WIDTH + SSM_WIDTH
D_FF = 3584
N_EXPERTS = 8
TOP_K = 2
NORM_EPS = 1e-6
NEG_INF = -1e30

LANES = 128
ROW_TILE = 512
SSM_CHUNK = 16
SSM_SEQS = 8
SSM_VEC = SSM_CHUNK * SSM_GROUP_DIM
SSM_LANE_GROUPS = LANES // SSM_GROUP_DIM
FF_CHUNK = 512
MOE_ROWS = 256
VMEM_LIMIT = 56 * 1024 * 1024
MOE_VMEM_LIMIT = 60 * 1024 * 1024

NT_DIMS = (((1,), (1,)), ((), ()))


def _cparams(sem, vmem=VMEM_LIMIT):
    return pltpu.CompilerParams(dimension_semantics=sem, vmem_limit_bytes=vmem)


def _const_spec(shape):
    n = len(shape)
    return pl.BlockSpec(shape, lambda *_: (0,) * n)


def _resident_spec(shape):
    n = len(shape)
    return pl.BlockSpec(shape, lambda *_: (0,) * n, pipeline_mode=pl.Buffered(1))


def _inproj_body(x_ref, g_ref, wqt_ref, wk_ref, wvt_ref, wu_ref, qg_ref, kg_ref, avg_ref,
                 qt_out, k_out, v3_out, u3_out, uscr):
    x = x_ref[...]
    ms = jnp.mean(x * x, axis=-1, keepdims=True)
    hn = (x * lax.rsqrt(ms + NORM_EPS) * g_ref[...]).astype(BF16)
    qt = lax.dot_general(wqt_ref[...], hn, NT_DIMS, preferred_element_type=F32)
    qms = jnp.dot(avg_ref[...], (qt * qt).astype(BF16), preferred_element_type=F32)
    qt_out[...] = (qt * lax.rsqrt(qms + NORM_EPS) * qg_ref[...]).astype(BF16)
    k = jnp.dot(hn, wk_ref[...], preferred_element_type=F32)
    kms = jnp.dot((k * k).astype(BF16), avg_ref[:KV_WIDTH, :KV_WIDTH], preferred_element_type=F32)
    k_out[...] = (k * lax.rsqrt(kms + NORM_EPS) * kg_ref[...]).astype(BF16)
    vt = lax.dot_general(wvt_ref[...], hn, NT_DIMS, preferred_element_type=F32)
    for b in range(ROW_TILE // ATTN_BLOCK):
        v3_out[b] = vt[:, b * ATTN_BLOCK:(b + 1) * ATTN_BLOCK].astype(BF16)
    u = jnp.dot(hn, wu_ref[...], preferred_element_type=F32)
    for lb in range(SSM_WIDTH // LANES):
        uscr[lb] = u[:, lb * LANES:(lb + 1) * LANES]
    for t in range(SSM_CHUNK):
        for lb in range(SSM_WIDTH // LANES):
            u3_out[t, :, lb * LANES:(lb + 1) * LANES] = (
                uscr[lb, pl.ds(t, ROW_TILE // SSM_CHUNK, stride=SSM_CHUNK), :].astype(BF16))


def _inproj(x2d, g, wqt, wk, wvt, wu, qg, kg, avg):
    t = x2d.shape[0]
    cpt = ROW_TILE // SSM_CHUNK
    return pl.pallas_call(
        _inproj_body,
        grid=(t // ROW_TILE,),
        in_specs=[pl.BlockSpec((ROW_TILE, D_MODEL), lambda i: (i, 0)),
                  _const_spec((1, D_MODEL)), _const_spec((ATTN_WIDTH, D_MODEL)),
                  _const_spec((D_MODEL, KV_WIDTH)), _const_spec((KV_WIDTH, D_MODEL)),
                  _const_spec((D_MODEL, SSM_WIDTH)), _const_spec((ATTN_WIDTH, 1)),
                  _const_spec((1, KV_WIDTH)), _const_spec((ATTN_WIDTH, ATTN_WIDTH))],
        out_specs=[pl.BlockSpec((ATTN_WIDTH, ROW_TILE), lambda i: (0, i)),
                   pl.BlockSpec((ROW_TILE, KV_WIDTH), lambda i: (i, 0)),
                   pl.BlockSpec((ROW_TILE // ATTN_BLOCK, KV_WIDTH, ATTN_BLOCK), lambda i: (i, 0, 0)),
                   pl.BlockSpec((SSM_CHUNK, cpt, SSM_WIDTH), lambda i: (0, i, 0))],
        out_shape=[jax.ShapeDtypeStruct((ATTN_WIDTH, t), BF16),
                   jax.ShapeDtypeStruct((t, KV_WIDTH), BF16),
                   jax.ShapeDtypeStruct((t // ATTN_BLOCK, KV_WIDTH, ATTN_BLOCK), BF16),
                   jax.ShapeDtypeStruct((SSM_CHUNK, t // SSM_CHUNK, SSM_WIDTH), BF16)],
        scratch_shapes=[pltpu.VMEM((SSM_WIDTH // LANES, ROW_TILE, LANES), F32)],
        compiler_params=_cparams(("parallel",)),
        name="inproj",
    )(x2d, g, wqt, wk, wvt, wu, qg, kg, avg)


def _attn_body(sink_ref, qt_ref, k_ref, v3_ref, g_ref, eye_ref, o_ref, at_scr, *, blocks_per_seq):
    i = pl.program_id(0)
    key = lax.broadcasted_iota(jnp.int32, (ATTN_BLOCK, ATTN_BLOCK), 0)
    qry = lax.broadcasted_iota(jnp.int32, (ATTN_BLOCK, ATTN_BLOCK), 1)
    cur_ok = key <= qry
    prev_ok = key > qry
    zpad = jnp.zeros((HEAD_DIM, ATTN_BLOCK), BF16)
    nblk = ROW_TILE // ATTN_BLOCK
    for blk in range(nblk):
        gblk = i * nblk + blk
        pblk = jnp.maximum(gblk - 1, 0)
        row0 = pl.multiple_of(gblk * ATTN_BLOCK, ATTN_BLOCK)
        prev0 = pl.multiple_of(pblk * ATTN_BLOCK, ATTN_BLOCK)
        has_prev = (gblk % blocks_per_seq) != 0
        kc = k_ref[pl.ds(row0, ATTN_BLOCK), :]
        kp = k_ref[pl.ds(prev0, ATTN_BLOCK), :]
        vc = v3_ref[gblk]
        vp = v3_ref[pblk]
        qb = qt_ref[:, blk * ATTN_BLOCK:(blk + 1) * ATTN_BLOCK]
        pmask = jnp.logical_and(prev_ok, has_prev)
        outs = []
        for h in range(N_HEADS):
            kv = h // GQA
            qh = qb[h * HEAD_DIM:(h + 1) * HEAD_DIM, :]
            qpad = jnp.concatenate([qh, zpad] if kv == 0 else [zpad, qh], axis=0)
            sc = jnp.dot(kc, qpad, preferred_element_type=F32)
            sp = jnp.dot(kp, qpad, preferred_element_type=F32)
            sc = jnp.where(cur_ok, sc, NEG_INF)
            sp = jnp.where(pmask, sp, NEG_INF)
            sink = sink_ref[h]
            m = jnp.maximum(jnp.maximum(jnp.max(sc, axis=0, keepdims=True),
                                        jnp.max(sp, axis=0, keepdims=True)), sink)
            pc = jnp.exp(sc - m)
            pp = jnp.exp(sp - m)
            den = (jnp.sum(pc, axis=0, keepdims=True) + jnp.sum(pp, axis=0, keepdims=True)
                   + jnp.exp(sink - m))
            vs = slice(kv * HEAD_DIM, (kv + 1) * HEAD_DIM)
            o = (jnp.dot(vc[vs, :], pc.astype(BF16), preferred_element_type=F32)
                 + jnp.dot(vp[vs, :], pp.astype(BF16), preferred_element_type=F32))
            outs.append(o / den)
        a = jnp.concatenate(outs, axis=0)
        ms = jnp.mean(a * a, axis=0, keepdims=True)
        at_scr[:, blk * ATTN_BLOCK:(blk + 1) * ATTN_BLOCK] = (
            a * lax.rsqrt(ms + NORM_EPS) * g_ref[...]).astype(BF16)
    o_ref[...] = lax.dot_general(eye_ref[...], at_scr[...], NT_DIMS,
                                 preferred_element_type=F32).astype(BF16)


def _attention(qt, k, v3, sinks, gcol, eye, seq):
    t = k.shape[0]
    return pl.pallas_call(
        functools.partial(_attn_body, blocks_per_seq=seq // ATTN_BLOCK),
        grid=(t // ROW_TILE,),
        in_specs=[pl.BlockSpec(memory_space=pltpu.SMEM),
                  pl.BlockSpec((ATTN_WIDTH, ROW_TILE), lambda i: (0, i)),
                  _const_spec((t, KV_WIDTH)), _const_spec((t // ATTN_BLOCK, KV_WIDTH, ATTN_BLOCK)),
                  _const_spec((ATTN_WIDTH, 1)), _const_spec((ROW_TILE, ROW_TILE))],
        out_specs=pl.BlockSpec((ROW_TILE, ATTN_WIDTH), lambda i: (i, 0)),
        out_shape=jax.ShapeDtypeStruct((t, ATTN_WIDTH), BF16),
        scratch_shapes=[pltpu.VMEM((ATTN_WIDTH, ROW_TILE), BF16)],
        compiler_params=_cparams(("parallel",)),
        name="swa_attention",
    )(sinks, qt, k, v3, gcol, eye)


def _ssm_tables_body(lam_ref, lamc_ref, b_ref, c_ref, erep_ref, etile_ref,
                     m_out, wre_out, wim_out, vre_out, vim_out, k_out, apw_out, *, n_steps):
    hi = lax.Precision.HIGHEST
    lanes = 2 * SSM_STATE
    ch, gd = SSM_CHUNK, SSM_GROUP_DIM
    lr, li, dt = lam_ref[0, 0:1, :], lam_ref[0, 1:2, :], lam_ref[0, 2:3, :]

    def apow(p):
        mag = jnp.exp(p * (lr * dt))
        ang = p * (li * dt)
        return mag * jnp.cos(ang), mag * jnp.sin(ang)

    tau = lax.broadcasted_iota(jnp.int32, (ch, lanes), 0).astype(F32)
    p0r, p0i = apow(tau)
    p1r, p1i = apow(tau + 1.0)
    step = lax.broadcasted_iota(jnp.int32, (n_steps, lanes), 0).astype(F32)
    qr, qi = apow(step * float(ch))
    apw_out[0, 0] = qr
    apw_out[0, 1] = qi
    sel = lax.broadcasted_iota(jnp.int32, (8, lanes), 0)
    kr, ki = apow(jnp.where(sel == 0, float(ch), float(ch * n_steps)))
    k_out[0, 0] = kr[0:2]
    k_out[0, 1] = ki[0:2]

    lrc, lic, dtc = lamc_ref[0, :, 0:1], lamc_ref[0, :, 1:2], lamc_ref[0, :, 2:3]
    magc = jnp.exp(lrc * dtc)
    ab_re, ab_im = magc * jnp.cos(lic * dtc), magc * jnp.sin(lic * dtc)
    nr = ab_re - 1.0
    den = lrc * lrc + lic * lic
    f_re = (nr * lrc + ab_im * lic) / den
    f_im = (ab_im * lrc - nr * lic) / den
    br, bi = b_ref[0, 0], b_ref[0, 1]
    bb_re = f_re * br - f_im * bi
    bb_im = f_re * bi + f_im * br
    bx_re = jnp.dot(bb_re, etile_ref[...], precision=hi, preferred_element_type=F32)
    bx_im = jnp.dot(bb_im, etile_ref[...], precision=hi, preferred_element_type=F32)
    jrev = float(ch - 1) - lax.broadcasted_iota(jnp.int32, (lanes, ch), 1).astype(F32)
    mg = jnp.exp(jrev * (lrc * dtc))
    an = jrev * (lic * dtc)
    px_re = jnp.dot(mg * jnp.cos(an), erep_ref[...], precision=hi, preferred_element_type=F32)
    px_im = jnp.dot(mg * jnp.sin(an), erep_ref[...], precision=hi, preferred_element_type=F32)
    w_re = px_re * bx_re - px_im * bx_im
    w_im = px_re * bx_im + px_im * bx_re
    first = lax.broadcasted_iota(jnp.int32, (lanes, SSM_VEC), 0) < SSM_STATE
    pack = lambda w: jnp.concatenate([jnp.where(first, w, 0.0), jnp.where(first, 0.0, w)], axis=1)
    wre_out[0] = pack(w_re).astype(BF16)
    wim_out[0] = pack(w_im).astype(BF16)

    blk = lax.broadcasted_iota(jnp.int32, (SSM_VEC, SSM_VEC), 1) // gd
    for g in range(2):
        cr, ci = c_ref[g, 0], c_ref[g, 1]
        stack = lambda x: x.reshape(ch * gd, lanes)
        ca_re = stack(cr[None] * p0r[:, None, :] - ci[None] * p0i[:, None, :])
        ca_im = stack(cr[None] * p0i[:, None, :] + ci[None] * p0r[:, None, :])
        kt = (jnp.dot(ca_re, bx_re, precision=hi, preferred_element_type=F32)
              - jnp.dot(ca_im, bx_im, precision=hi, preferred_element_type=F32))
        m = jnp.where(blk == 0, kt, 0.0)
        for j in range(1, ch):
            shifted = jnp.concatenate([jnp.zeros((j * gd, SSM_VEC), F32), kt[:SSM_VEC - j * gd]], axis=0)
            m = jnp.where(blk == j, shifted, m)
        m_out[g] = m.astype(BF16)
        vre_out[g] = stack(cr[None] * p1r[:, None, :] - ci[None] * p1i[:, None, :]).astype(BF16)
        vim_out[g] = (-stack(cr[None] * p1i[:, None, :] + ci[None] * p1r[:, None, :])).astype(BF16)


def _ssm_tables(lam_re, lam_im, log_dt, b_re, b_im, c_re, c_im, d_skip, n_steps):
    g_, n_, c_ = SSM_GROUPS, SSM_STATE, SSM_GROUP_DIM
    np_ = g_ // 2
    lanes = 2 * n_
    dt = jnp.broadcast_to(jnp.exp(log_dt.astype(F32))[:, None], (g_, n_))
    lam = jnp.stack([lam_re.astype(F32), lam_im.astype(F32), dt], axis=1)
    lam_row = lam.reshape(np_, 2, 3, n_).transpose(0, 2, 1, 3).reshape(np_, 3, lanes)
    lam_col = lam_row.transpose(0, 2, 1)
    b2 = jnp.stack([b_re.astype(F32).reshape(np_, lanes, c_), b_im.astype(F32).reshape(np_, lanes, c_)],
                   axis=1)
    c2 = jnp.stack([c_re.astype(F32), c_im.astype(F32)], axis=1)
    z = jnp.zeros_like(c2)
    even = (jnp.arange(g_) % 2 == 0)[:, None, None, None]
    c2 = jnp.where(even, jnp.concatenate([c2, z], axis=3), jnp.concatenate([z, c2], axis=3))
    col = jnp.arange(SSM_VEC)
    erep = (col[None, :] // c_ == jnp.arange(SSM_CHUNK)[:, None]).astype(F32)
    etile = (col[None, :] % c_ == jnp.arange(c_)[:, None]).astype(F32)
    lead = lambda shape: pl.BlockSpec(shape, lambda p: (p,) + (0,) * (len(shape) - 1))
    m, wre, wim, vre, vim, consts, apw = pl.pallas_call(
        functools.partial(_ssm_tables_body, n_steps=n_steps),
        grid=(np_,),
        in_specs=[lead((1, 3, lanes)), lead((1, lanes, 3)), lead((1, 2, lanes, c_)),
                  lead((2, 2, c_, lanes)), _const_spec((SSM_CHUNK, SSM_VEC)), _const_spec((c_, SSM_VEC))],
        out_specs=[lead((2, SSM_VEC, SSM_VEC)), lead((1, lanes, 2 * SSM_VEC)), lead((1, lanes, 2 * SSM_VEC)),
                   lead((2, SSM_VEC, lanes)), lead((2, SSM_VEC, lanes)),
                   lead((1, 2, 2, lanes)), lead((1, 2, n_steps, lanes))],
        out_shape=[jax.ShapeDtypeStruct((g_, SSM_VEC, SSM_VEC), BF16),
                   jax.ShapeDtypeStruct((np_, lanes, 2 * SSM_VEC), BF16),
                   jax.ShapeDtypeStruct((np_, lanes, 2 * SSM_VEC), BF16),
                   jax.ShapeDtypeStruct((g_, SSM_VEC, lanes), BF16),
                   jax.ShapeDtypeStruct((g_, SSM_VEC, lanes), BF16),
                   jax.ShapeDtypeStruct((np_, 2, 2, lanes), F32),
                   jax.ShapeDtypeStruct((np_, 2, n_steps, lanes), F32)],
        compiler_params=_cparams(("parallel",)),
        name="s5_operators",
    )(lam_row, lam_col, b2, c2, erep, etile)
    dcol = jnp.tile(d_skip.astype(F32).reshape(g_, 1, c_), (1, SSM_CHUNK, 1)).reshape(g_, SSM_VEC, 1)
    return m, wre, wim, vre, vim, consts, apw, dcol


def _ssm_body(u_ref, eye_ref, m_ref, wre_ref, wim_ref, vre_ref, vim_ref, c_ref, apw_ref, d_ref,
              y_ref, ut, yt, pre, pim, sre, sim, *, n_steps, halves, pitch):
    lanes = 2 * SSM_STATE
    nk = u_ref.shape[1]
    gd = SSM_GROUP_DIM
    for t in range(SSM_CHUNK):
        xt = lax.dot_general(eye_ref[...], u_ref[t], NT_DIMS,
                             preferred_element_type=F32).astype(BF16)
        for g in range(SSM_LANE_GROUPS):
            ut[g, t * gd:(t + 1) * gd, :] = xt[g * gd:(g + 1) * gd, :]

    row = lax.broadcasted_iota(jnp.int32, (SSM_SEQS, lanes), 0)
    seq_start = (row % halves) == 0
    shift = lambda x: jnp.where(seq_start, 0.0, pltpu.roll(x, 1, axis=0))

    npairs = SSM_LANE_GROUPS // 2
    for pp in range(npairs):
        ucat = jnp.concatenate([ut[2 * pp], ut[2 * pp + 1]], axis=0)
        p_re = jnp.dot(wre_ref[pp], ucat, preferred_element_type=F32).T
        p_im = jnp.dot(wim_ref[pp], ucat, preferred_element_type=F32).T
        if pitch == n_steps:
            pre[pp] = p_re
            pim[pp] = p_im
        else:
            for s in range(SSM_SEQS):
                pre[pp, s * pitch:s * pitch + n_steps, :] = p_re[s * n_steps:(s + 1) * n_steps]
                pim[pp, s * pitch:s * pitch + n_steps, :] = p_im[s * n_steps:(s + 1) * n_steps]

    a16 = [(jnp.broadcast_to(c_ref[pp, 0, 0:1, :], (SSM_SEQS, lanes)),
            jnp.broadcast_to(c_ref[pp, 1, 0:1, :], (SSM_SEQS, lanes))) for pp in range(npairs)]

    def step(i, carry):
        idx = pl.ds(i, SSM_SEQS, stride=pitch)
        out = []
        for pp in range(npairs):
            s_re, s_im = carry[pp]
            ar, ai = a16[pp]
            sre[pp, idx, :] = s_re
            sim[pp, idx, :] = s_im
            out.append((ar * s_re - ai * s_im + pre[pp, idx, :],
                        ar * s_im + ai * s_re + pim[pp, idx, :]))
        return tuple(out)

    zero = jnp.zeros((SSM_SEQS, lanes), F32)
    ends = lax.fori_loop(0, n_steps, step, ((zero, zero),) * npairs, unroll=2)

    for pp in range(npairs):
        e_re, e_im = ends[pp]
        br = jnp.broadcast_to(c_ref[pp, 0, 1:2, :], (SSM_SEQS, lanes))
        bi = jnp.broadcast_to(c_ref[pp, 1, 1:2, :], (SSM_SEQS, lanes))
        c_re = jnp.zeros_like(e_re)
        c_im = jnp.zeros_like(e_im)
        for _ in range(halves - 1):
            n_re = e_re + br * c_re - bi * c_im
            n_im = e_im + br * c_im + bi * c_re
            c_re, c_im = shift(n_re), shift(n_im)

        apr = apw_ref[pp, 0][None, :, :]
        api = apw_ref[pp, 1][None, :, :]
        if pitch == n_steps:
            s3r = sre[pp].reshape(SSM_SEQS, n_steps, lanes)
            s3i = sim[pp].reshape(SSM_SEQS, n_steps, lanes)
        else:
            s3r = jnp.stack([sre[pp, s * pitch:s * pitch + n_steps, :] for s in range(SSM_SEQS)])
            s3i = jnp.stack([sim[pp, s * pitch:s * pitch + n_steps, :] for s in range(SSM_SEQS)])
        t_re = (s3r + apr * c_re[:, None, :] - api * c_im[:, None, :]).reshape(nk, lanes)
        t_im = (s3i + apr * c_im[:, None, :] + api * c_re[:, None, :]).reshape(nk, lanes)
        tr_hi = t_re.astype(BF16)
        tr_lo = (t_re - tr_hi.astype(F32)).astype(BF16)
        ti_hi = t_im.astype(BF16)
        ti_lo = (t_im - ti_hi.astype(F32)).astype(BF16)
        for g in (2 * pp, 2 * pp + 1):
            ug = ut[g]
            y = jnp.dot(m_ref[g], ug, preferred_element_type=F32)
            y += lax.dot_general(vre_ref[g], tr_hi, NT_DIMS, preferred_element_type=F32)
            y += lax.dot_general(vre_ref[g], tr_lo, NT_DIMS, preferred_element_type=F32)
            y += lax.dot_general(vim_ref[g], ti_hi, NT_DIMS, preferred_element_type=F32)
            y += lax.dot_general(vim_ref[g], ti_lo, NT_DIMS, preferred_element_type=F32)
            yt[g] = y + d_ref[g] * ug.astype(F32)

    for t in range(SSM_CHUNK):
        rows = jnp.concatenate([yt[g, t * gd:(t + 1) * gd, :] for g in range(SSM_LANE_GROUPS)], axis=0)
        y_ref[t] = rows.T


def _ssm(u3, tables, eye, bsz, seq):
    m, wre, wim, vre, vim, consts, apw, dcol = tables
    nk = u3.shape[1]
    halves = SSM_SEQS // bsz
    n_steps = seq // (SSM_CHUNK * halves)
    lanes = 2 * SSM_STATE
    lg = SSM_LANE_GROUPS
    pitch = n_steps + 8 if (n_steps // 8) % 2 == 0 else n_steps
    lead = lambda shape: pl.BlockSpec(shape, lambda b: (b,) + (0,) * (len(shape) - 1))
    return pl.pallas_call(
        functools.partial(_ssm_body, n_steps=n_steps, halves=halves, pitch=pitch),
        grid=(SSM_GROUPS // lg,),
        in_specs=[pl.BlockSpec((SSM_CHUNK, nk, LANES), lambda b: (0, 0, b)),
                  _const_spec((LANES, LANES)),
                  lead((lg, SSM_VEC, SSM_VEC)),
                  lead((lg // 2, lanes, 2 * SSM_VEC)), lead((lg // 2, lanes, 2 * SSM_VEC)),
                  lead((lg, SSM_VEC, lanes)), lead((lg, SSM_VEC, lanes)),
                  lead((lg // 2, 2, 2, lanes)), lead((lg // 2, 2, n_steps, lanes)),
                  lead((lg, SSM_VEC, 1))],
        out_specs=pl.BlockSpec((SSM_CHUNK, nk, LANES), lambda b: (0, 0, b)),
        out_shape=jax.ShapeDtypeStruct((SSM_CHUNK, nk, SSM_WIDTH), F32),
        scratch_shapes=[pltpu.VMEM((lg, SSM_VEC, nk), BF16), pltpu.VMEM((lg, SSM_VEC, nk), F32)]
                       + [pltpu.VMEM((lg // 2, SSM_SEQS * pitch, lanes), F32)] * 4,
        compiler_params=_cparams(("parallel",)),
        name="s5_ssm",
    )(u3, eye, m, wre, wim, vre, vim, consts, apw, dcol)


def _mix_math(x_ref, a_ref, y3_ref, wglu_ref, sg_ref, wo_ref, fg_ref, yscr):
    nlb = SSM_WIDTH // LANES
    for t in range(SSM_CHUNK):
        for lb in range(nlb):
            yscr[lb, pl.ds(t, ROW_TILE // SSM_CHUNK, stride=SSM_CHUNK), :] = (
                y3_ref[t, :, lb * LANES:(lb + 1) * LANES])
    y = jax.nn.gelu(jnp.concatenate([yscr[lb] for lb in range(nlb)], axis=1))
    z = y * jax.nn.sigmoid(jnp.dot(y.astype(BF16), wglu_ref[...], preferred_element_type=F32))
    ms = jnp.mean(z * z, axis=-1, keepdims=True)
    sn = (z * lax.rsqrt(ms + NORM_EPS) * sg_ref[...]).astype(BF16)
    x1 = (x_ref[...]
          + jnp.dot(a_ref[...], wo_ref[:ATTN_WIDTH, :], preferred_element_type=F32)
          + jnp.dot(sn, wo_ref[ATTN_WIDTH:, :], preferred_element_type=F32))
    ms1 = jnp.mean(x1 * x1, axis=-1, keepdims=True)
    return x1, x1 * lax.rsqrt(ms1 + NORM_EPS) * fg_ref[...]


def _mix_router_body(x_ref, a_ref, y3_ref, wglu_ref, sg_ref, wo_ref, fg_ref, rwt_ref,
                     x1_out, lg_out, yscr):
    x1, hn = _mix_math(x_ref, a_ref, y3_ref, wglu_ref, sg_ref, wo_ref, fg_ref, yscr)
    x1_out[...] = x1
    lane = lax.broadcasted_iota(jnp.int32, lg_out.shape, 1)
    lg = jnp.zeros(lg_out.shape, F32)
    for e in range(lg_out.shape[1]):
        lg = jnp.where(lane == e, jnp.sum(hn * rwt_ref[e:e + 1, :], axis=-1, keepdims=True), lg)
    lg_out[...] = lg


def _mix_ffn_body(x_ref, a_ref, y3_ref, wglu_ref, sg_ref, wo_ref, fg_ref, wg_ref, wu_ref, wd_ref,
                  o_ref, yscr):
    x1, hn = _mix_math(x_ref, a_ref, y3_ref, wglu_ref, sg_ref, wo_ref, fg_ref, yscr)
    h = hn.astype(BF16)
    o_ref[...] = x1
    for c in range(D_FF // FF_CHUNK):
        sl = slice(c * FF_CHUNK, (c + 1) * FF_CHUNK)
        g = jnp.dot(h, wg_ref[:, sl], preferred_element_type=F32)
        u = jnp.dot(h, wu_ref[:, sl], preferred_element_type=F32)
        a = (jax.nn.silu(g) * u).astype(BF16)
        o_ref[...] += jnp.dot(a, wd_ref[sl, :], preferred_element_type=F32)


def _mix_specs():
    row = lambda w: pl.BlockSpec((ROW_TILE, w), lambda i: (i, 0))
    in_specs = [row(D_MODEL), row(ATTN_WIDTH),
                pl.BlockSpec((SSM_CHUNK, ROW_TILE // SSM_CHUNK, SSM_WIDTH), lambda i: (0, i, 0)),
                _const_spec((SSM_WIDTH, SSM_WIDTH)), _const_spec((1, SSM_WIDTH)),
                _const_spec((D_MODEL, D_MODEL)), _const_spec((1, D_MODEL))]
    return row, in_specs, pltpu.VMEM((SSM_WIDTH // LANES, ROW_TILE, LANES), F32)


def _mix_router(x2d, attn, y3, wglu, sg, wo, fg, rwt):
    t = x2d.shape[0]
    ne = rwt.shape[0]
    row, in_specs, yscr = _mix_specs()
    return pl.pallas_call(
        _mix_router_body,
        grid=(t // ROW_TILE,),
        in_specs=in_specs + [_const_spec((ne, D_MODEL))],
        out_specs=[row(D_MODEL), row(ne)],
        out_shape=[jax.ShapeDtypeStruct((t, D_MODEL), F32), jax.ShapeDtypeStruct((t, ne), F32)],
        scratch_shapes=[yscr],
        compiler_params=_cparams(("parallel",)),
        name="mix_router",
    )(x2d, attn, y3, wglu, sg, wo, fg, rwt)


def _mix_ffn(x2d, attn, y3, wglu, sg, wo, fg, wg, wu, wd):
    t = x2d.shape[0]
    row, in_specs, yscr = _mix_specs()
    return pl.pallas_call(
        _mix_ffn_body,
        grid=(t // ROW_TILE,),
        in_specs=in_specs + [_resident_spec((D_MODEL, D_FF)), _resident_spec((D_MODEL, D_FF)),
                             _resident_spec((D_FF, D_MODEL))],
        out_specs=row(D_MODEL),
        out_shape=jax.ShapeDtypeStruct((t, D_MODEL), F32),
        scratch_shapes=[yscr],
        compiler_params=_cparams(("parallel",)),
        name="mix_dense_ffn",
    )(x2d, attn, y3, wglu, sg, wo, fg, wg, wu, wd)


def _expert_ffn_body(e_ref, nused_ref, nvalid_ref, tok_ref, dst_ref, x1_hbm, fg_ref,
                     wg_ref, wu_ref, wd_ref, out_hbm, xbuf, acc, ybuf, gsem, ssem):
    b = pl.program_id(0)
    n_used = nused_ref[0]
    n_ff = D_FF // FF_CHUNK
    rows_per_chunk = -(-MOE_ROWS // n_ff)

    def gather_row(blk, r):
        i = tok_ref[blk * MOE_ROWS + r]
        pltpu.make_async_copy(x1_hbm.at[pl.ds(i, 1)], xbuf.at[pl.ds(r, 1)], gsem).start()

    def scatter_row(blk, r):
        i = dst_ref[blk * MOE_ROWS + r]
        pltpu.make_async_copy(ybuf.at[pl.ds(r, 1)], out_hbm.at[pl.ds(i, 1)], ssem).start()

    def scatter_rows(blk, n):
        def body(r, carry):
            scatter_row(blk, r)
            return carry
        lax.fori_loop(0, n, body, 0)

    def wait_gather():
        pltpu.make_async_copy(x1_hbm.at[pl.ds(0, MOE_ROWS)], xbuf, gsem).wait()

    def wait_scatter(n):
        n8 = pl.multiple_of(n & -8, 8)

        @pl.when(n8 > 0)
        def _():
            pltpu.make_async_copy(ybuf.at[pl.ds(0, n8)], out_hbm.at[pl.ds(0, n8)], ssem).wait()

        def one(_, carry):
            pltpu.make_async_copy(ybuf.at[pl.ds(0, 1)], out_hbm.at[pl.ds(0, 1)], ssem).wait()
            return carry
        lax.fori_loop(0, n - n8, one, 0)

    def ffn(scatter_prev):
        wait_gather()
        x = xbuf[...]
        ms = jnp.mean(x * x, axis=-1, keepdims=True)
        h = (x * lax.rsqrt(ms + NORM_EPS) * fg_ref[...]).astype(BF16)
        for c in range(n_ff):
            for r in range(c * rows_per_chunk, min((c + 1) * rows_per_chunk, MOE_ROWS)):
                gather_row(b + 1, r)
                if scatter_prev:
                    scatter_row(b - 1, r)
            sl = slice(c * FF_CHUNK, (c + 1) * FF_CHUNK)
            g = jnp.dot(h, wg_ref[0, :, sl], preferred_element_type=F32)
            u = jnp.dot(h, wu_ref[0, :, sl], preferred_element_type=F32)
            a = (jax.nn.silu(g) * u).astype(BF16)
            part = jnp.dot(a, wd_ref[0, sl, :], preferred_element_type=F32)
            if c == 0:
                acc[...] = part
            else:
                acc[...] += part

    @pl.when(b < n_used)
    def _():
        @pl.when(b == 0)
        def _():
            def body(r, carry):
                gather_row(0, r)
                return carry
            lax.fori_loop(0, MOE_ROWS, body, 0, unroll=8)

        n_prev = jnp.where(b >= 1, nvalid_ref[jnp.maximum(b - 1, 0)], 0)

        @pl.when(n_prev == MOE_ROWS)
        def _():
            ffn(scatter_prev=True)

        @pl.when(n_prev != MOE_ROWS)
        def _():
            scatter_rows(b - 1, n_prev)
            ffn(scatter_prev=False)

        wait_scatter(n_prev)
        ybuf[...] = acc[...]

        @pl.when(b == n_used - 1)
        def _():
            scatter_rows(b, nvalid_ref[b])
            wait_scatter(nvalid_ref[b])
            wait_gather()


def _expert_ffn(x1, fg, tok, dst, block_e, n_used, n_valid, wg, wu, wd, nblk, out_rows):
    wspec = lambda shape: pl.BlockSpec(shape, lambda b, e, *_: (e[b], 0, 0))
    return pl.pallas_call(
        _expert_ffn_body,
        grid_spec=pltpu.PrefetchScalarGridSpec(
            num_scalar_prefetch=5, grid=(nblk,),
            in_specs=[pl.BlockSpec(memory_space=pl.ANY),
                      pl.BlockSpec((1, D_MODEL), lambda b, *_: (0, 0)),
                      wspec((1, D_MODEL, D_FF)), wspec((1, D_MODEL, D_FF)), wspec((1, D_FF, D_MODEL))],
            out_specs=pl.BlockSpec(memory_space=pl.ANY),
            scratch_shapes=[pltpu.VMEM((MOE_ROWS, D_MODEL), F32)] * 3
                           + [pltpu.SemaphoreType.DMA(()), pltpu.SemaphoreType.DMA(())]),
        out_shape=jax.ShapeDtypeStruct((out_rows, D_MODEL), F32),
        compiler_params=pltpu.CompilerParams(dimension_semantics=("arbitrary",),
                                             vmem_limit_bytes=MOE_VMEM_LIMIT),
        name="moe_expert_ffn",
    )(block_e, n_used, n_valid, tok, dst, x1, fg, wg, wu, wd)


def _combine_body(x_ref, y0_ref, y1_ref, g_ref, o_ref):
    o_ref[...] = x_ref[...] + y0_ref[...] * g_ref[:, 0:1] + y1_ref[...] * g_ref[:, 1:2]


def _combine(x1, y2, gates):
    t = x1.shape[0]
    nt = t // ROW_TILE
    return pl.pallas_call(
        _combine_body,
        grid=(nt,),
        in_specs=[pl.BlockSpec((ROW_TILE, D_MODEL), lambda i: (i, 0)),
                  pl.BlockSpec((ROW_TILE, D_MODEL), lambda i: (i, 0)),
                  pl.BlockSpec((ROW_TILE, D_MODEL), lambda i: (i + nt, 0)),
                  pl.BlockSpec((ROW_TILE, TOP_K), lambda i: (i, 0))],
        out_specs=pl.BlockSpec((ROW_TILE, D_MODEL), lambda i: (i, 0)),
        out_shape=jax.ShapeDtypeStruct((t, D_MODEL), F32),
        compiler_params=_cparams(("parallel",)),
        name="moe_combine",
    )(x1, y2, y2, gates)


def _moe(x1, fg, logits, wg, wu, wd):
    t = x1.shape[0]
    n_assign = t * TOP_K
    nblk = n_assign // MOE_ROWS + N_EXPERTS
    cap = nblk * MOE_ROWS
    top_v, top_i = lax.top_k(logits, TOP_K)
    gates = jax.nn.softmax(top_v, axis=-1)
    flat_e = top_i.reshape(-1).astype(jnp.int32)
    order = jnp.argsort(flat_e, stable=True).astype(jnp.int32)
    experts = jnp.arange(N_EXPERTS, dtype=jnp.int32)
    counts = jnp.sum((flat_e[:, None] == experts[None, :]).astype(jnp.int32), axis=0)
    start = jnp.cumsum(counts) - counts
    padded = ((counts + MOE_ROWS - 1) // MOE_ROWS) * MOE_ROWS
    pend = jnp.cumsum(padded)
    pstart = pend - padded
    p = jnp.arange(cap, dtype=jnp.int32)
    e_p = jnp.minimum(jnp.sum((p[:, None] >= pend[None, :]).astype(jnp.int32), axis=1), N_EXPERTS - 1)
    rank = p - pstart[e_p]
    valid = jnp.logical_and(rank < counts[e_p], p < pend[-1])
    a_p = order[jnp.clip(start[e_p] + rank, 0, n_assign - 1)]
    tok = jnp.where(valid, a_p // TOP_K, 0).astype(jnp.int32)
    dst = jnp.where(valid, (a_p % TOP_K) * t + a_p // TOP_K, 0).astype(jnp.int32)
    block_e = e_p[::MOE_ROWS]
    n_used = (pend[-1:] // MOE_ROWS).astype(jnp.int32)
    n_valid = jnp.sum(valid.reshape(nblk, MOE_ROWS).astype(jnp.int32), axis=1)
    y2 = _expert_ffn(x1, fg, tok, dst, block_e, n_used, n_valid, wg, wu, wd, nblk, n_assign)
    return _combine(x1, y2, gates)


def kernel(x, attn_norm_g, w_in, q_norm_g, k_norm_g, sinks, lam_re, lam_im, log_dt, b_re, b_im,
           c_re, c_im, d_skip, w_glu, attn_out_g, ssm_out_g, w_o, ffn_norm_g, dense_wg, dense_wu,
           dense_wd, router_w, moe_wg, moe_wu, moe_wd):
    bsz, seq, _ = x.shape
    depth = w_in.shape[0]
    assert SSM_SEQS % bsz == 0 and seq % (SSM_CHUNK * (SSM_SEQS // bsz)) == 0
    assert seq % ROW_TILE == 0 and ROW_TILE % ATTN_BLOCK == 0
    x2d = x.reshape(bsz * seq, D_MODEL).astype(F32)
    head = jnp.arange(ATTN_WIDTH) // HEAD_DIM
    avg = jnp.where(head[:, None] == head[None, :], 1.0 / HEAD_DIM, 0.0).astype(BF16)
    eye = jnp.eye(ROW_TILE, dtype=BF16)
    n_steps = seq // (SSM_CHUNK * (SSM_SEQS // bsz))
    row = lambda v: v.astype(F32).reshape(1, -1)
    col = lambda v: v.astype(F32).reshape(-1, 1)
    ko, vo, uo = ATTN_WIDTH, ATTN_WIDTH + KV_WIDTH, ATTN_WIDTH + 2 * KV_WIDTH
    for l in range(depth):
        qg = jnp.tile(q_norm_g[l].astype(F32), N_HEADS) * (HEAD_DIM ** -0.5)
        kg = jnp.tile(k_norm_g[l].astype(F32), N_KV_HEADS)
        w = w_in[l].astype(BF16)
        qt, k, v3, u3 = _inproj(x2d, row(attn_norm_g[l]), w[:, :ko].T, w[:, ko:vo], w[:, vo:uo].T,
                                w[:, uo:], col(qg), row(kg), avg)
        attn = _attention(qt, k, v3, sinks[l].astype(F32), col(attn_out_g[l]), eye, seq)
        tables = _ssm_tables(lam_re[l], lam_im[l], log_dt[l], b_re[l], b_im[l], c_re[l], c_im[l],
                             d_skip[l], n_steps)
        y3 = _ssm(u3, tables, eye[:LANES, :LANES], bsz, seq)
        i = l // 2
        mix_args = (x2d, attn, y3, w_glu[l].astype(BF16), row(ssm_out_g[l]), w_o[l].astype(BF16),
                    row(ffn_norm_g[l]))
        if l % 2 == 0:
            x2d = _mix_ffn(*mix_args, dense_wg[i].astype(BF16), dense_wu[i].astype(BF16),
                           dense_wd[i].astype(BF16))
        else:
            x1, logits = _mix_router(*mix_args, router_w[i].astype(F32).T)
            x2d = _moe(x1, row(ffn_norm_g[l]), logits, moe_wg[i].astype(BF16), moe_wu[i].astype(BF16),
                       moe_wd[i].astype(BF16))
    return x2d.reshape(bsz, seq, D_MODEL)
```

```python
import functools
import math

import jax
import jax.numpy as jnp
from jax import lax
from jax.experimental import pallas as pl
from jax.experimental.pallas import tpu as pltpu

F32 = jnp.float32
BF16 = jnp.bfloat16

D_MODEL = 1024
HEAD_DIM = 64
N_HEADS = 8
N_KV_HEADS = 2
GQA = N_HEADS // N_KV_HEADS
ATTN_WIDTH = N_HEADS * HEAD_DIM
KV_WIDTH = N_KV_HEADS * HEAD_DIM
ATTN_BLOCK = 128
SSM_WIDTH = D_MODEL - ATTN_WIDTH
SSM_GROUP_DIM = 16
SSM_GROUPS = SSM_WIDTH // SSM_GROUP_DIM
SSM_STATE = 64
IN_WIDTH = ATTN_WIDTH + 2 * KV_WIDTH + SSM_WIDTH
D_FF = 3584
N_EXPERTS = 8
TOP_K = 2
NORM_EPS = 1e-6
NEG_INF = -1e30

LANES = 128
ROW_TILE = 512
SSM_CHUNK = 16
SSM_SEQS = 8
SSM_VEC = SSM_CHUNK * SSM_GROUP_DIM
SSM_LANE_GROUPS = LANES // SSM_GROUP_DIM
FF_CHUNK = 512
MOE_ROWS = 512
VMEM_LIMIT = 56 * 1024 * 1024
MOE_VMEM_LIMIT = 60 * 1024 * 1024

NT_DIMS = (((1,), (1,)), ((), ()))


def _cparams(sem, vmem=VMEM_LIMIT):
    return pltpu.CompilerParams(dimension_semantics=sem, vmem_limit_bytes=vmem)


def _const_spec(shape):
    n = len(shape)
    return pl.BlockSpec(shape, lambda *_: (0,) * n)


def _resident_spec(shape):
    n = len(shape)
    return pl.BlockSpec(shape, lambda *_: (0,) * n, pipeline_mode=pl.Buffered(1))


def _inproj_body(x_ref, g_ref, wqt_ref, wk_ref, wvt_ref, wu_ref, qg_ref, kg_ref, avg_ref,
                 qt_out, k_out, v3_out, u3_out, uscr):
    x = x_ref[...]
    ms = jnp.mean(x * x, axis=-1, keepdims=True)
    hn = (x * lax.rsqrt(ms + NORM_EPS) * g_ref[...]).astype(BF16)
    qt = lax.dot_general(wqt_ref[...], hn, NT_DIMS, preferred_element_type=F32)
    qms = jnp.dot(avg_ref[...], (qt * qt).astype(BF16), preferred_element_type=F32)
    qt_out[...] = (qt * lax.rsqrt(qms + NORM_EPS) * qg_ref[...]).astype(BF16)
    k = jnp.dot(hn, wk_ref[...], preferred_element_type=F32)
    kms = jnp.dot((k * k).astype(BF16), avg_ref[:KV_WIDTH, :KV_WIDTH], preferred_element_type=F32)
    k_out[...] = (k * lax.rsqrt(kms + NORM_EPS) * kg_ref[...]).astype(BF16)
    vt = lax.dot_general(wvt_ref[...], hn, NT_DIMS, preferred_element_type=F32)
    for b in range(ROW_TILE // ATTN_BLOCK):
        v3_out[b] = vt[:, b * ATTN_BLOCK:(b + 1) * ATTN_BLOCK].astype(BF16)
    u = jnp.dot(hn, wu_ref[...], preferred_element_type=F32)
    for lb in range(SSM_WIDTH // LANES):
        uscr[lb] = u[:, lb * LANES:(lb + 1) * LANES]
    for t in range(SSM_CHUNK):
        for lb in range(SSM_WIDTH // LANES):
            u3_out[t, :, lb * LANES:(lb + 1) * LANES] = (
                uscr[lb, pl.ds(t, ROW_TILE // SSM_CHUNK, stride=SSM_CHUNK), :].astype(BF16))


def _inproj(x2d, g, wqt, wk, wvt, wu, qg, kg, avg):
    t = x2d.shape[0]
    cpt = ROW_TILE // SSM_CHUNK
    return pl.pallas_call(
        _inproj_body,
        grid=(t // ROW_TILE,),
        in_specs=[pl.BlockSpec((ROW_TILE, D_MODEL), lambda i: (i, 0)),
                  _const_spec((1, D_MODEL)), _const_spec((ATTN_WIDTH, D_MODEL)),
                  _const_spec((D_MODEL, KV_WIDTH)), _const_spec((KV_WIDTH, D_MODEL)),
                  _const_spec((D_MODEL, SSM_WIDTH)), _const_spec((ATTN_WIDTH, 1)),
                  _const_spec((1, KV_WIDTH)), _const_spec((ATTN_WIDTH, ATTN_WIDTH))],
        out_specs=[pl.BlockSpec((ATTN_WIDTH, ROW_TILE), lambda i: (0, i)),
                   pl.BlockSpec((ROW_TILE, KV_WIDTH), lambda i: (i, 0)),
                   pl.BlockSpec((ROW_TILE // ATTN_BLOCK, KV_WIDTH, ATTN_BLOCK), lambda i: (i, 0, 0)),
                   pl.BlockSpec((SSM_CHUNK, cpt, SSM_WIDTH), lambda i: (0, i, 0))],
        out_shape=[jax.ShapeDtypeStruct((ATTN_WIDTH, t), BF16),
                   jax.ShapeDtypeStruct((t, KV_WIDTH), BF16),
                   jax.ShapeDtypeStruct((t // ATTN_BLOCK, KV_WIDTH, ATTN_BLOCK), BF16),
                   jax.ShapeDtypeStruct((SSM_CHUNK, t // SSM_CHUNK, SSM_WIDTH), BF16)],
        scratch_shapes=[pltpu.VMEM((SSM_WIDTH // LANES, ROW_TILE, LANES), F32)],
        compiler_params=_cparams(("parallel",)),
        name="inproj",
    )(x2d, g, wqt, wk, wvt, wu, qg, kg, avg)


def _attn_body(sink_ref, qt_ref, k_ref, v3_ref, g_ref, eye_ref, o_ref, at_scr, *, blocks_per_seq):
    i = pl.program_id(0)
    key = lax.broadcasted_iota(jnp.int32, (ATTN_BLOCK, ATTN_BLOCK), 0)
    qry = lax.broadcasted_iota(jnp.int32, (ATTN_BLOCK, ATTN_BLOCK), 1)
    cur_ok = key <= qry
    prev_ok = key > qry
    zpad = jnp.zeros((HEAD_DIM, ATTN_BLOCK), BF16)
    nblk = ROW_TILE // ATTN_BLOCK
    for blk in range(nblk):
        gblk = i * nblk + blk
        pblk = jnp.maximum(gblk - 1, 0)
        row0 = pl.multiple_of(gblk * ATTN_BLOCK, ATTN_BLOCK)
        prev0 = pl.multiple_of(pblk * ATTN_BLOCK, ATTN_BLOCK)
        has_prev = (gblk % blocks_per_seq) != 0
        kc = k_ref[pl.ds(row0, ATTN_BLOCK), :]
        kp = k_ref[pl.ds(prev0, ATTN_BLOCK), :]
        vc = v3_ref[gblk]
        vp = v3_ref[pblk]
        qb = qt_ref[:, blk * ATTN_BLOCK:(blk + 1) * ATTN_BLOCK]
        pmask = jnp.logical_and(prev_ok, has_prev)
        outs = []
        for h in range(N_HEADS):
            kv = h // GQA
            qh = qb[h * HEAD_DIM:(h + 1) * HEAD_DIM, :]
            qpad = jnp.concatenate([qh, zpad] if kv == 0 else [zpad, qh], axis=0)
            sc = jnp.dot(kc, qpad, preferred_element_type=F32)
            sp = jnp.dot(kp, qpad, preferred_element_type=F32)
            sc = jnp.where(cur_ok, sc, NEG_INF)
            sp = jnp.where(pmask, sp, NEG_INF)
            sink = sink_ref[h]
            m = jnp.maximum(jnp.maximum(jnp.max(sc, axis=0, keepdims=True),
                                        jnp.max(sp, axis=0, keepdims=True)), sink)
            pc = jnp.exp(sc - m)
            pp = jnp.exp(sp - m)
            den = (jnp.sum(pc, axis=0, keepdims=True) + jnp.sum(pp, axis=0, keepdims=True)
                   + jnp.exp(sink - m))
            vs = slice(kv * HEAD_DIM, (kv + 1) * HEAD_DIM)
            o = (jnp.dot(vc[vs, :], pc.astype(BF16), preferred_element_type=F32)
                 + jnp.dot(vp[vs, :], pp.astype(BF16), preferred_element_type=F32))
            outs.append(o / den)
        a = jnp.concatenate(outs, axis=0)
        ms = jnp.mean(a * a, axis=0, keepdims=True)
        at_scr[:, blk * ATTN_BLOCK:(blk + 1) * ATTN_BLOCK] = (
            a * lax.rsqrt(ms + NORM_EPS) * g_ref[...]).astype(BF16)
    o_ref[...] = lax.dot_general(eye_ref[...], at_scr[...], NT_DIMS,
                                 preferred_element_type=F32).astype(BF16)


def _attention(qt, k, v3, sinks, gcol, eye, seq):
    t = k.shape[0]
    return pl.pallas_call(
        functools.partial(_attn_body, blocks_per_seq=seq // ATTN_BLOCK),
        grid=(t // ROW_TILE,),
        in_specs=[pl.BlockSpec(memory_space=pltpu.SMEM),
                  pl.BlockSpec((ATTN_WIDTH, ROW_TILE), lambda i: (0, i)),
                  _const_spec((t, KV_WIDTH)), _const_spec((t // ATTN_BLOCK, KV_WIDTH, ATTN_BLOCK)),
                  _const_spec((ATTN_WIDTH, 1)), _const_spec((ROW_TILE, ROW_TILE))],
        out_specs=pl.BlockSpec((ROW_TILE, ATTN_WIDTH), lambda i: (i, 0)),
        out_shape=jax.ShapeDtypeStruct((t, ATTN_WIDTH), BF16),
        scratch_shapes=[pltpu.VMEM((ATTN_WIDTH, ROW_TILE), BF16)],
        compiler_params=_cparams(("parallel",)),
        name="swa_attention",
    )(sinks, qt, k, v3, gcol, eye)


def _ssm_tables_body(lam_ref, lamc_ref, b_ref, c_ref, erep_ref, etile_ref,
                     m_out, wre_out, wim_out, vre_out, vim_out, k_out, apw_out, *, n_steps):
    hi = lax.Precision.HIGHEST
    lanes = 2 * SSM_STATE
    ch, gd = SSM_CHUNK, SSM_GROUP_DIM
    lr, li, dt = lam_ref[0, 0:1, :], lam_ref[0, 1:2, :], lam_ref[0, 2:3, :]

    def apow(p):
        mag = jnp.exp(p * (lr * dt))
        ang = p * (li * dt)
        return mag * jnp.cos(ang), mag * jnp.sin(ang)

    tau = lax.broadcasted_iota(jnp.int32, (ch, lanes), 0).astype(F32)
    p0r, p0i = apow(tau)
    p1r, p1i = apow(tau + 1.0)
    step = lax.broadcasted_iota(jnp.int32, (n_steps, lanes), 0).astype(F32)
    qr, qi = apow(step * float(ch))
    apw_out[0, 0] = qr
    apw_out[0, 1] = qi
    sel = lax.broadcasted_iota(jnp.int32, (8, lanes), 0)
    kr, ki = apow(jnp.where(sel == 0, float(ch), float(ch * n_steps)))
    k_out[0, 0] = kr[0:2]
    k_out[0, 1] = ki[0:2]

    lrc, lic, dtc = lamc_ref[0, :, 0:1], lamc_ref[0, :, 1:2], lamc_ref[0, :, 2:3]
    magc = jnp.exp(lrc * dtc)
    ab_re, ab_im = magc * jnp.cos(lic * dtc), magc * jnp.sin(lic * dtc)
    nr = ab_re - 1.0
    den = lrc * lrc + lic * lic
    f_re = (nr * lrc + ab_im * lic) / den
    f_im = (ab_im * lrc - nr * lic) / den
    br, bi = b_ref[0, 0], b_ref[0, 1]
    bb_re = f_re * br - f_im * bi
    bb_im = f_re * bi + f_im * br
    bx_re = jnp.dot(bb_re, etile_ref[...], precision=hi, preferred_element_type=F32)
    bx_im = jnp.dot(bb_im, etile_ref[...], precision=hi, preferred_element_type=F32)
    jrev = float(ch - 1) - lax.broadcasted_iota(jnp.int32, (lanes, ch), 1).astype(F32)
    mg = jnp.exp(jrev * (lrc * dtc))
    an = jrev * (lic * dtc)
    px_re = jnp.dot(mg * jnp.cos(an), erep_ref[...], precision=hi, preferred_element_type=F32)
    px_im = jnp.dot(mg * jnp.sin(an), erep_ref[...], precision=hi, preferred_element_type=F32)
    w_re = px_re * bx_re - px_im * bx_im
    w_im = px_re * bx_im + px_im * bx_re
    first = lax.broadcasted_iota(jnp.int32, (lanes, SSM_VEC), 0) < SSM_STATE
    pack = lambda w: jnp.concatenate([jnp.where(first, w, 0.0), jnp.where(first, 0.0, w)], axis=1)
    wre_out[0] = pack(w_re).astype(BF16)
    wim_out[0] = pack(w_im).astype(BF16)

    blk = lax.broadcasted_iota(jnp.int32, (SSM_VEC, SSM_VEC), 1) // gd
    for g in range(2):
        cr, ci = c_ref[g, 0], c_ref[g, 1]
        stack = lambda x: x.reshape(ch * gd, lanes)
        ca_re = stack(cr[None] * p0r[:, None, :] - ci[None] * p0i[:, None, :])
        ca_im = stack(cr[None] * p0i[:, None, :] + ci[None] * p0r[:, None, :])
        kt = (jnp.dot(ca_re, bx_re, precision=hi, preferred_element_type=F32)
              - jnp.dot(ca_im, bx_im, precision=hi, preferred_element_type=F32))
        m = jnp.where(blk == 0, kt, 0.0)
        for j in range(1, ch):
            shifted = jnp.concatenate([jnp.zeros((j * gd, SSM_VEC), F32), kt[:SSM_VEC - j * gd]], axis=0)
            m = jnp.where(blk == j, shifted, m)
        m_out[g] = m.astype(BF16)
        vre_out[g] = stack(cr[None] * p1r[:, None, :] - ci[None] * p1i[:, None, :]).astype(BF16)
        vim_out[g] = (-stack(cr[None] * p1i[:, None, :] + ci[None] * p1r[:, None, :])).astype(BF16)


def _ssm_tables(lam_re, lam_im, log_dt, b_re, b_im, c_re, c_im, d_skip, n_steps):
    g_, n_, c_ = SSM_GROUPS, SSM_STATE, SSM_GROUP_DIM
    np_ = g_ // 2
    lanes = 2 * n_
    dt = jnp.broadcast_to(jnp.exp(log_dt.astype(F32))[:, None], (g_, n_))
    lam = jnp.stack([lam_re.astype(F32), lam_im.astype(F32), dt], axis=1)
    lam_row = lam.reshape(np_, 2, 3, n_).transpose(0, 2, 1, 3).reshape(np_, 3, lanes)
    lam_col = lam_row.transpose(0, 2, 1)
    b2 = jnp.stack([b_re.astype(F32).reshape(np_, lanes, c_), b_im.astype(F32).reshape(np_, lanes, c_)],
                   axis=1)
    c2 = jnp.stack([c_re.astype(F32), c_im.astype(F32)], axis=1)
    z = jnp.zeros_like(c2)
    even = (jnp.arange(g_) % 2 == 0)[:, None, None, None]
    c2 = jnp.where(even, jnp.concatenate([c2, z], axis=3), jnp.concatenate([z, c2], axis=3))
    col = jnp.arange(SSM_VEC)
    erep = (col[None, :] // c_ == jnp.arange(SSM_CHUNK)[:, None]).astype(F32)
    etile = (col[None, :] % c_ == jnp.arange(c_)[:, None]).astype(F32)
    lead = lambda shape: pl.BlockSpec(shape, lambda p: (p,) + (0,) * (len(shape) - 1))
    m, wre, wim, vre, vim, consts, apw = pl.pallas_call(
        functools.partial(_ssm_tables_body, n_steps=n_steps),
        grid=(np_,),
        in_specs=[lead((1, 3, lanes)), lead((1, lanes, 3)), lead((1, 2, lanes, c_)),
                  lead((2, 2, c_, lanes)), _const_spec((SSM_CHUNK, SSM_VEC)), _const_spec((c_, SSM_VEC))],
        out_specs=[lead((2, SSM_VEC, SSM_VEC)), lead((1, lanes, 2 * SSM_VEC)), lead((1, lanes, 2 * SSM_VEC)),
                   lead((2, SSM_VEC, lanes)), lead((2, SSM_VEC, lanes)),
                   lead((1, 2, 2, lanes)), lead((1, 2, n_steps, lanes))],
        out_shape=[jax.ShapeDtypeStruct((g_, SSM_VEC, SSM_VEC), BF16),
                   jax.ShapeDtypeStruct((np_, lanes, 2 * SSM_VEC), BF16),
                   jax.ShapeDtypeStruct((np_, lanes, 2 * SSM_VEC), BF16),
                   jax.ShapeDtypeStruct((g_, SSM_VEC, lanes), BF16),
                   jax.ShapeDtypeStruct((g_, SSM_VEC, lanes), BF16),
                   jax.ShapeDtypeStruct((np_, 2, 2, lanes), F32),
                   jax.ShapeDtypeStruct((np_, 2, n_steps, lanes), F32)],
        compiler_params=_cparams(("parallel",)),
        name="s5_operators",
    )(lam_row, lam_col, b2, c2, erep, etile)
    dcol = jnp.tile(d_skip.astype(F32).reshape(g_, 1, c_), (1, SSM_CHUNK, 1)).reshape(g_, SSM_VEC, 1)
    return m, wre, wim, vre, vim, consts, apw, dcol


def _ssm_body(u_ref, eye_ref, m_ref, wre_ref, wim_ref, vre_ref, vim_ref, c_ref, apw_ref, d_ref,
              y_ref, ut, yt, pre, pim, sre, sim, *, n_steps, halves, pitch):
    lanes = 2 * SSM_STATE
    nk = u_ref.shape[1]
    gd = SSM_GROUP_DIM
    for t in range(SSM_CHUNK):
        xt = lax.dot_general(eye_ref[...], u_ref[t], NT_DIMS,
                             preferred_element_type=F32).astype(BF16)
        for g in range(SSM_LANE_GROUPS):
            ut[g, t * gd:(t + 1) * gd, :] = xt[g * gd:(g + 1) * gd, :]

    row = lax.broadcasted_iota(jnp.int32, (SSM_SEQS, lanes), 0)
    seq_start = (row % halves) == 0
    shift = lambda x: jnp.where(seq_start, 0.0, pltpu.roll(x, 1, axis=0))

    npairs = SSM_LANE_GROUPS // 2
    for pp in range(npairs):
        ucat = jnp.concatenate([ut[2 * pp], ut[2 * pp + 1]], axis=0)
        p_re = jnp.dot(wre_ref[pp], ucat, preferred_element_type=F32).T
        p_im = jnp.dot(wim_ref[pp], ucat, preferred_element_type=F32).T
        if pitch == n_steps:
            pre[pp] = p_re
            pim[pp] = p_im
        else:
            for s in range(SSM_SEQS):
                pre[pp, s * pitch:s * pitch + n_steps, :] = p_re[s * n_steps:(s + 1) * n_steps]
                pim[pp, s * pitch:s * pitch + n_steps, :] = p_im[s * n_steps:(s + 1) * n_steps]

    a16 = [(jnp.broadcast_to(c_ref[pp, 0, 0:1, :], (SSM_SEQS, lanes)),
            jnp.broadcast_to(c_ref[pp, 1, 0:1, :], (SSM_SEQS, lanes))) for pp in range(npairs)]

    def step(i, carry):
        idx = pl.ds(i, SSM_SEQS, stride=pitch)
        out = []
        for pp in range(npairs):
            s_re, s_im = carry[pp]
            ar, ai = a16[pp]
            sre[pp, idx, :] = s_re
            sim[pp, idx, :] = s_im
            out.append((ar * s_re - ai * s_im + pre[pp, idx, :],
                        ar * s_im + ai * s_re + pim[pp, idx, :]))
        return tuple(out)

    zero = jnp.zeros((SSM_SEQS, lanes), F32)
    ends = lax.fori_loop(0, n_steps, step, ((zero, zero),) * npairs, unroll=2)

    for pp in range(npairs):
        e_re, e_im = ends[pp]
        br = jnp.broadcast_to(c_ref[pp, 0, 1:2, :], (SSM_SEQS, lanes))
        bi = jnp.broadcast_to(c_ref[pp, 1, 1:2, :], (SSM_SEQS, lanes))
        c_re = jnp.zeros_like(e_re)
        c_im = jnp.zeros_like(e_im)
        for _ in range(halves - 1):
            n_re = e_re + br * c_re - bi * c_im
            n_im = e_im + br * c_im + bi * c_re
            c_re, c_im = shift(n_re), shift(n_im)

        apr = apw_ref[pp, 0][None, :, :]
        api = apw_ref[pp, 1][None, :, :]
        if pitch == n_steps:
            s3r = sre[pp].reshape(SSM_SEQS, n_steps, lanes)
            s3i = sim[pp].reshape(SSM_SEQS, n_steps, lanes)
        else:
            s3r = jnp.stack([sre[pp, s * pitch:s * pitch + n_steps, :] for s in range(SSM_SEQS)])
            s3i = jnp.stack([sim[pp, s * pitch:s * pitch + n_steps, :] for s in range(SSM_SEQS)])
        t_re = (s3r + apr * c_re[:, None, :] - api * c_im[:, None, :]).reshape(nk, lanes)
        t_im = (s3i + apr * c_im[:, None, :] + api * c_re[:, None, :]).reshape(nk, lanes)
        tr_hi = t_re.astype(BF16)
        tr_lo = (t_re - tr_hi.astype(F32)).astype(BF16)
        ti_hi = t_im.astype(BF16)
        ti_lo = (t_im - ti_hi.astype(F32)).astype(BF16)
        for g in (2 * pp, 2 * pp + 1):
            ug = ut[g]
            y = jnp.dot(m_ref[g], ug, preferred_element_type=F32)
            y += lax.dot_general(vre_ref[g], tr_hi, NT_DIMS, preferred_element_type=F32)
            y += lax.dot_general(vre_ref[g], tr_lo, NT_DIMS, preferred_element_type=F32)
            y += lax.dot_general(vim_ref[g], ti_hi, NT_DIMS, preferred_element_type=F32)
            y += lax.dot_general(vim_ref[g], ti_lo, NT_DIMS, preferred_element_type=F32)
            yt[g] = y + d_ref[g] * ug.astype(F32)

    for t in range(SSM_CHUNK):
        rows = jnp.concatenate([yt[g, t * gd:(t + 1) * gd, :] for g in range(SSM_LANE_GROUPS)], axis=0)
        y_ref[t] = rows.T


def _ssm(u3, tables, eye, bsz, seq):
    m, wre, wim, vre, vim, consts, apw, dcol = tables
    nk = u3.shape[1]
    halves = SSM_SEQS // bsz
    n_steps = seq // (SSM_CHUNK * halves)
    lanes = 2 * SSM_STATE
    lg = SSM_LANE_GROUPS
    pitch = n_steps + 8 if (n_steps // 8) % 2 == 0 else n_steps
    lead = lambda shape: pl.BlockSpec(shape, lambda b: (b,) + (0,) * (len(shape) - 1))
    return pl.pallas_call(
        functools.partial(_ssm_body, n_steps=n_steps, halves=halves, pitch=pitch),
        grid=(SSM_GROUPS // lg,),
        in_specs=[pl.BlockSpec((SSM_CHUNK, nk, LANES), lambda b: (0, 0, b)),
                  _const_spec((LANES, LANES)),
                  lead((lg, SSM_VEC, SSM_VEC)),
                  lead((lg // 2, lanes, 2 * SSM_VEC)), lead((lg // 2, lanes, 2 * SSM_VEC)),
                  lead((lg, SSM_VEC, lanes)), lead((lg, SSM_VEC, lanes)),
                  lead((lg // 2, 2, 2, lanes)), lead((lg // 2, 2, n_steps, lanes)),
                  lead((lg, SSM_VEC, 1))],
        out_specs=pl.BlockSpec((SSM_CHUNK, nk, LANES), lambda b: (0, 0, b)),
        out_shape=jax.ShapeDtypeStruct((SSM_CHUNK, nk, SSM_WIDTH), F32),
        scratch_shapes=[pltpu.VMEM((lg, SSM_VEC, nk), BF16), pltpu.VMEM((lg, SSM_VEC, nk), F32)]
                       + [pltpu.VMEM((lg // 2, SSM_SEQS * pitch, lanes), F32)] * 4,
        compiler_params=_cparams(("parallel",)),
        name="s5_ssm",
    )(u3, eye, m, wre, wim, vre, vim, consts, apw, dcol)


def _mix_math(x_ref, a_ref, y3_ref, wglu_ref, sg_ref, wo_ref, fg_ref, yscr):
    nlb = SSM_WIDTH // LANES
    for t in range(SSM_CHUNK):
        for lb in range(nlb):
            yscr[lb, pl.ds(t, ROW_TILE // SSM_CHUNK, stride=SSM_CHUNK), :] = (
                y3_ref[t, :, lb * LANES:(lb + 1) * LANES])
    y = jax.nn.gelu(jnp.concatenate([yscr[lb] for lb in range(nlb)], axis=1))
    z = y * jax.nn.sigmoid(jnp.dot(y.astype(BF16), wglu_ref[...], preferred_element_type=F32))
    ms = jnp.mean(z * z, axis=-1, keepdims=True)
    sn = (z * lax.rsqrt(ms + NORM_EPS) * sg_ref[...]).astype(BF16)
    x1 = (x_ref[...]
          + jnp.dot(a_ref[...], wo_ref[:ATTN_WIDTH, :], preferred_element_type=F32)
          + jnp.dot(sn, wo_ref[ATTN_WIDTH:, :], preferred_element_type=F32))
    ms1 = jnp.mean(x1 * x1, axis=-1, keepdims=True)
    return x1, x1 * lax.rsqrt(ms1 + NORM_EPS) * fg_ref[...]


def _mix_router_body(x_ref, a_ref, y3_ref, wglu_ref, sg_ref, wo_ref, fg_ref, rwt_ref,
                     x1_out, lg_out, yscr):
    x1, hn = _mix_math(x_ref, a_ref, y3_ref, wglu_ref, sg_ref, wo_ref, fg_ref, yscr)
    x1_out[...] = x1
    lane = lax.broadcasted_iota(jnp.int32, lg_out.shape, 1)
    lg = jnp.zeros(lg_out.shape, F32)
    for e in range(lg_out.shape[1]):
        lg = jnp.where(lane == e, jnp.sum(hn * rwt_ref[e:e + 1, :], axis=-1, keepdims=True), lg)
    lg_out[...] = lg


def _mix_ffn_body(x_ref, a_ref, y3_ref, wglu_ref, sg_ref, wo_ref, fg_ref, wg_ref, wu_ref, wd_ref,
                  o_ref, yscr):
    x1, hn = _mix_math(x_ref, a_ref, y3_ref, wglu_ref, sg_ref, wo_ref, fg_ref, yscr)
    h = hn.astype(BF16)
    o_ref[...] = x1
    for c in range(D_FF // FF_CHUNK):
        sl = slice(c * FF_CHUNK, (c + 1) * FF_CHUNK)
        g = jnp.dot(h, wg_ref[:, sl], preferred_element_type=F32)
        u = jnp.dot(h, wu_ref[:, sl], preferred_element_type=F32)
        a = (jax.nn.silu(g) * u).astype(BF16)
        o_ref[...] += jnp.dot(a, wd_ref[sl, :], preferred_element_type=F32)


def _mix_specs():
    row = lambda w: pl.BlockSpec((ROW_TILE, w), lambda i: (i, 0))
    in_specs = [row(D_MODEL), row(ATTN_WIDTH),
                pl.BlockSpec((SSM_CHUNK, ROW_TILE // SSM_CHUNK, SSM_WIDTH), lambda i: (0, i, 0)),
                _const_spec((SSM_WIDTH, SSM_WIDTH)), _const_spec((1, SSM_WIDTH)),
                _const_spec((D_MODEL, D_MODEL)), _const_spec((1, D_MODEL))]
    return row, in_specs, pltpu.VMEM((SSM_WIDTH // LANES, ROW_TILE, LANES), F32)


def _mix_router(x2d, attn, y3, wglu, sg, wo, fg, rwt):
    t = x2d.shape[0]
    ne = rwt.shape[0]
    row, in_specs, yscr = _mix_specs()
    return pl.pallas_call(
        _mix_router_body,
        grid=(t // ROW_TILE,),
        in_specs=in_specs + [_const_spec((ne, D_MODEL))],
        out_specs=[row(D_MODEL), row(ne)],
        out_shape=[jax.ShapeDtypeStruct((t, D_MODEL), F32), jax.ShapeDtypeStruct((t, ne), F32)],
        scratch_shapes=[yscr],
        compiler_params=_cparams(("parallel",)),
        name="mix_router",
    )(x2d, attn, y3, wglu, sg, wo, fg, rwt)


def _mix_ffn(x2d, attn, y3, wglu, sg, wo, fg, wg, wu, wd):
    t = x2d.shape[0]
    row, in_specs, yscr = _mix_specs()
    return pl.pallas_call(
        _mix_ffn_body,
        grid=(t // ROW_TILE,),
        in_specs=in_specs + [_resident_spec((D_MODEL, D_FF)), _resident_spec((D_MODEL, D_FF)),
                             _resident_spec((D_FF, D_MODEL))],
        out_specs=row(D_MODEL),
        out_shape=jax.ShapeDtypeStruct((t, D_MODEL), F32),
        scratch_shapes=[yscr],
        compiler_params=_cparams(("parallel",)),
        name="mix_dense_ffn",
    )(x2d, attn, y3, wglu, sg, wo, fg, wg, wu, wd)


def _expert_ffn_body(e_ref, nused_ref, nvalid_ref, tok_ref, dst_ref, x1_hbm, fg_ref,
                     wg_ref, wu_ref, wd_ref, out_hbm, xbuf, acc, ybuf, gsem, ssem):
    b = pl.program_id(0)
    n_used = nused_ref[0]
    n_ff = D_FF // FF_CHUNK
    rows_per_chunk = -(-MOE_ROWS // n_ff)

    def gather_row(blk, r):
        i = tok_ref[blk * MOE_ROWS + r]
        pltpu.make_async_copy(x1_hbm.at[pl.ds(i, 1)], xbuf.at[pl.ds(r, 1)], gsem).start()

    def scatter_row(blk, r):
        i = dst_ref[blk * MOE_ROWS + r]
        pltpu.make_async_copy(ybuf.at[pl.ds(r, 1)], out_hbm.at[pl.ds(i, 1)], ssem).start()

    def scatter_rows(blk, n):
        def body(r, carry):
            scatter_row(blk, r)
            return carry
        lax.fori_loop(0, n, body, 0)

    def wait_gather():
        pltpu.make_async_copy(x1_hbm.at[pl.ds(0, MOE_ROWS)], xbuf, gsem).wait()

    def wait_scatter(n):
        n8 = pl.multiple_of(n & -8, 8)

        @pl.when(n8 > 0)
        def _():
            pltpu.make_async_copy(ybuf.at[pl.ds(0, n8)], out_hbm.at[pl.ds(0, n8)], ssem).wait()

        def one(_, carry):
            pltpu.make_async_copy(ybuf.at[pl.ds(0, 1)], out_hbm.at[pl.ds(0, 1)], ssem).wait()
            return carry
        lax.fori_loop(0, n - n8, one, 0)

    def ffn(scatter_prev):
        wait_gather()
        x = xbuf[...]
        ms = jnp.mean(x * x, axis=-1, keepdims=True)
        h = (x * lax.rsqrt(ms + NORM_EPS) * fg_ref[...]).astype(BF16)
        for c in range(n_ff):
            for r in range(c * rows_per_chunk, min((c + 1) * rows_per_chunk, MOE_ROWS)):
                gather_row(b + 1, r)
                if scatter_prev:
                    scatter_row(b - 1, r)
            sl = slice(c * FF_CHUNK, (c + 1) * FF_CHUNK)
            g = jnp.dot(h, wg_ref[0, :, sl], preferred_element_type=F32)
            u = jnp.dot(h, wu_ref[0, :, sl], preferred_element_type=F32)
            a = (jax.nn.silu(g) * u).astype(BF16)
            part = jnp.dot(a, wd_ref[0, sl, :], preferred_element_type=F32)
            if c == 0:
                acc[...] = part
            else:
                acc[...] += part

    @pl.when(b < n_used)
    def _():
        @pl.when(b == 0)
        def _():
            def body(r, carry):
                gather_row(0, r)
                return carry
            lax.fori_loop(0, MOE_ROWS, body, 0, unroll=8)

        n_prev = jnp.where(b >= 1, nvalid_ref[jnp.maximum(b - 1, 0)], 0)

        @pl.when(n_prev == MOE_ROWS)
        def _():
            ffn(scatter_prev=True)

        @pl.when(n_prev != MOE_ROWS)
        def _():
            scatter_rows(b - 1, n_prev)
            ffn(scatter_prev=False)

        wait_scatter(n_prev)
        ybuf[...] = acc[...]

        @pl.when(b == n_used - 1)
        def _():
            scatter_rows(b, nvalid_ref[b])
            wait_scatter(nvalid_ref[b])
            wait_gather()


def _expert_ffn(x1, fg, tok, dst, block_e, n_used, n_valid, wg, wu, wd, nblk, out_rows):
    wspec = lambda shape: pl.BlockSpec(shape, lambda b, e, *_: (e[b], 0, 0))
    return pl.pallas_call(
        _expert_ffn_body,
        grid_spec=pltpu.PrefetchScalarGridSpec(
            num_scalar_prefetch=5, grid=(nblk,),
            in_specs=[pl.BlockSpec(memory_space=pl.ANY),
                      pl.BlockSpec((1, D_MODEL), lambda b, *_: (0, 0)),
                      wspec((1, D_MODEL, D_FF)), wspec((1, D_MODEL, D_FF)), wspec((1, D_FF, D_MODEL))],
            out_specs=pl.BlockSpec(memory_space=pl.ANY),
            scratch_shapes=[pltpu.VMEM((MOE_ROWS, D_MODEL), F32)] * 3
                           + [pltpu.SemaphoreType.DMA(()), pltpu.SemaphoreType.DMA(())]),
        out_shape=jax.ShapeDtypeStruct((out_rows, D_MODEL), F32),
        compiler_params=pltpu.CompilerParams(dimension_semantics=("arbitrary",),
                                             vmem_limit_bytes=MOE_VMEM_LIMIT),
        name="moe_expert_ffn",
    )(block_e, n_used, n_valid, tok, dst, x1, fg, wg, wu, wd)


def _combine_body(x_ref, y0_ref, y1_ref, g_ref, o_ref):
    o_ref[...] = x_ref[...] + y0_ref[...] * g_ref[:, 0:1] + y1_ref[...] * g_ref[:, 1:2]


def _combine(x1, y2, gates):
    t = x1.shape[0]
    nt = t // ROW_TILE
    return pl.pallas_call(
        _combine_body,
        grid=(nt,),
        in_specs=[pl.BlockSpec((ROW_TILE, D_MODEL), lambda i: (i, 0)),
                  pl.BlockSpec((ROW_TILE, D_MODEL), lambda i: (i, 0)),
                  pl.BlockSpec((ROW_TILE, D_MODEL), lambda i: (i + nt, 0)),
                  pl.BlockSpec((ROW_TILE, TOP_K), lambda i: (i, 0))],
        out_specs=pl.BlockSpec((ROW_TILE, D_MODEL), lambda i: (i, 0)),
        out_shape=jax.ShapeDtypeStruct((t, D_MODEL), F32),
        compiler_params=_cparams(("parallel",)),
        name="moe_combine",
    )(x1, y2, y2, gates)


def _moe(x1, fg, logits, wg, wu, wd):
    t = x1.shape[0]
    n_assign = t * TOP_K
    nblk = n_assign // MOE_ROWS + N_EXPERTS
    cap = nblk * MOE_ROWS
    top_v, top_i = lax.top_k(logits, TOP_K)
    gates = jax.nn.softmax(top_v, axis=-1)
    flat_e = top_i.reshape(-1).astype(jnp.int32)
    order = jnp.argsort(flat_e, stable=True).astype(jnp.int32)
    experts = jnp.arange(N_EXPERTS, dtype=jnp.int32)
    counts = jnp.sum((flat_e[:, None] == experts[None, :]).astype(jnp.int32), axis=0)
    start = jnp.cumsum(counts) - counts
    padded = ((counts + MOE_ROWS - 1) // MOE_ROWS) * MOE_ROWS
    pend = jnp.cumsum(padded)
    pstart = pend - padded
    p = jnp.arange(cap, dtype=jnp.int32)
    e_p = jnp.minimum(jnp.sum((p[:, None] >= pend[None, :]).astype(jnp.int32), axis=1), N_EXPERTS - 1)
    rank = p - pstart[e_p]
    valid = jnp.logical_and(rank < counts[e_p], p < pend[-1])
    a_p = order[jnp.clip(start[e_p] + rank, 0, n_assign - 1)]
    tok = jnp.where(valid, a_p // TOP_K, 0).astype(jnp.int32)
    dst = jnp.where(valid, (a_p % TOP_K) * t + a_p // TOP_K, 0).astype(jnp.int32)
    block_e = e_p[::MOE_ROWS]
    n_used = (pend[-1:] // MOE_ROWS).astype(jnp.int32)
    n_valid = jnp.sum(valid.reshape(nblk, MOE_ROWS).astype(jnp.int32), axis=1)
    y2 = _expert_ffn(x1, fg, tok, dst, block_e, n_used, n_valid, wg, wu, wd, nblk, n_assign)
    return _combine(x1, y2, gates)


def kernel(x, attn_norm_g, w_in, q_norm_g, k_norm_g, sinks, lam_re, lam_im, log_dt, b_re, b_im,
           c_re, c_im, d_skip, w_glu, attn_out_g, ssm_out_g, w_o, ffn_norm_g, dense_wg, dense_wu,
           dense_wd, router_w, moe_wg, moe_wu, moe_wd):
    bsz, seq, _ = x.shape
    depth = w_in.shape[0]
    assert SSM_SEQS % bsz == 0 and seq % (SSM_CHUNK * (SSM_SEQS // bsz)) == 0
    assert seq % ROW_TILE == 0 and ROW_TILE % ATTN_BLOCK == 0
    x2d = x.reshape(bsz * seq, D_MODEL).astype(F32)
    head = jnp.arange(ATTN_WIDTH) // HEAD_DIM
    avg = jnp.where(head[:, None] == head[None, :], 1.0 / HEAD_DIM, 0.0).astype(BF16)
    eye = jnp.eye(ROW_TILE, dtype=BF16)
    n_steps = seq // (SSM_CHUNK * (SSM_SEQS // bsz))
    row = lambda v: v.astype(F32).reshape(1, -1)
    col = lambda v: v.astype(F32).reshape(-1, 1)
    ko, vo, uo = ATTN_WIDTH, ATTN_WIDTH + KV_WIDTH, ATTN_WIDTH + 2 * KV_WIDTH
    for l in range(depth):
        qg = jnp.tile(q_norm_g[l].astype(F32), N_HEADS) * (HEAD_DIM ** -0.5)
        kg = jnp.tile(k_norm_g[l].astype(F32), N_KV_HEADS)
        w = w_in[l].astype(BF16)
        qt, k, v3, u3 = _inproj(x2d, row(attn_norm_g[l]), w[:, :ko].T, w[:, ko:vo], w[:, vo:uo].T,
                                w[:, uo:], col(qg), row(kg), avg)
        attn = _attention(qt, k, v3, sinks[l].astype(F32), col(attn_out_g[l]), eye, seq)
        tables = _ssm_tables(lam_re[l], lam_im[l], log_dt[l], b_re[l], b_im[l], c_re[l], c_im[l],
                             d_skip[l], n_steps)
        y3 = _ssm(u3, tables, eye[:LANES, :LANES], bsz, seq)
        i = l // 2
        mix_args = (x2d, attn, y3, w_glu[l].astype(BF16), row(ssm_out_g[l]), w_o[l].astype(BF16),
                    row(ffn_norm_g[l]))
        if l % 2 == 0:
            x2d = _mix_ffn(*mix_args, dense_wg[i].astype(BF16), dense_wu[i].astype(BF16),
                           dense_wd[i].astype(BF16))
        else:
            x1, logits = _mix_router(*mix_args, router_w[i].astype(F32).T)
            x2d = _moe(x1, row(ffn_norm_g[l]), logits, moe_wg[i].astype(BF16), moe_wu[i].astype(BF16),
                       moe_wd[i].astype(BF16))
    return x2d.reshape(bsz, seq, D_MODEL)
```

```python
import functools
import math

import jax
import jax.numpy as jnp
from jax import lax
from jax.experimental import pallas as pl
from jax.experimental.pallas import tpu as pltpu

F32 = jnp.float32
BF16 = jnp.bfloat16

D_MODEL = 1024
HEAD_DIM = 64
N_HEADS = 8
N_KV_HEADS = 2
GQA = N_HEADS // N_KV_HEADS
ATTN_WIDTH = N_HEADS * HEAD_DIM
KV_WIDTH = N_KV_HEADS * HEAD_DIM
ATTN_BLOCK = 128
SSM_WIDTH = D_MODEL - ATTN_WIDTH
SSM_GROUP_DIM = 16
SSM_GROUPS = SSM_WIDTH // SSM_GROUP_DIM
SSM_STATE = 64
IN_WIDTH = ATTN_WIDTH + 2 * KV_WIDTH + SSM_WIDTH
D_FF = 3584
N_EXPERTS = 8
TOP_K = 2
NORM_EPS = 1e-6
NEG_INF = -1e30

LANES = 128
ROW_TILE = 512
SSM_CHUNK = 16
SSM_SEQS = 8
SSM_VEC = SSM_CHUNK * SSM_GROUP_DIM
SSM_LANE_GROUPS = LANES // SSM_GROUP_DIM
FF_CHUNK = 512
MOE_ROWS = 512
VMEM_LIMIT = 56 * 1024 * 1024
MOE_VMEM_LIMIT = 60 * 1024 * 1024

NT_DIMS = (((1,), (1,)), ((), ()))


def _cparams(sem, vmem=VMEM_LIMIT):
    return pltpu.CompilerParams(dimension_semantics=sem, vmem_limit_bytes=vmem)


def _const_spec(shape):
    n = len(shape)
    return pl.BlockSpec(shape, lambda *_: (0,) * n)


def _resident_spec(shape):
    n = len(shape)
    return pl.BlockSpec(shape, lambda *_: (0,) * n, pipeline_mode=pl.Buffered(1))


def _row_tiled_call(body, steps, in_specs, out_specs, out_shape, scratch_shapes, name, args, casts=()):
    n_in, n_out, n_c = len(in_specs), len(out_specs), len(casts)

    def hosted(*refs):
        ins, cast_in = refs[:n_in], refs[n_in:n_in + n_c]
        outs = refs[n_in + n_c:n_in + n_c + n_out]
        cast_out = refs[n_in + n_c + n_out:n_in + 2 * n_c + n_out]
        body(*ins, *outs, *refs[n_in + 2 * n_c + n_out:])
        for src, dst in zip(cast_in, cast_out):
            dst[...] = src[...].astype(BF16)

    for w in casts:
        assert w.ndim == 2 and w.shape[0] % (steps * 16) == 0, w.shape
    cast_specs = [pl.BlockSpec((w.shape[0] // steps, w.shape[1]), lambda i: (i, 0)) for w in casts]
    res = pl.pallas_call(
        hosted,
        grid=(steps,),
        in_specs=list(in_specs) + cast_specs,
        out_specs=list(out_specs) + cast_specs,
        out_shape=list(out_shape) + [jax.ShapeDtypeStruct(w.shape, BF16) for w in casts],
        scratch_shapes=scratch_shapes,
        compiler_params=_cparams(("parallel",)),
        name=name,
    )(*args, *casts)
    return res[:n_out], res[n_out:]


def _inproj_body(x_ref, g_ref, wqt_ref, wk_ref, wvt_ref, wu_ref, qg_ref, kg_ref, avg_ref,
                 qt_out, k_out, v3_out, u3_out, uscr):
    x = x_ref[...]
    ms = jnp.mean(x * x, axis=-1, keepdims=True)
    hn = (x * lax.rsqrt(ms + NORM_EPS) * g_ref[...]).astype(BF16)
    qt = lax.dot_general(wqt_ref[...], hn, NT_DIMS, preferred_element_type=F32)
    qms = jnp.dot(avg_ref[...], (qt * qt).astype(BF16), preferred_element_type=F32)
    qt_out[...] = (qt * lax.rsqrt(qms + NORM_EPS) * qg_ref[...]).astype(BF16)
    k = jnp.dot(hn, wk_ref[...], preferred_element_type=F32)
    kms = jnp.dot((k * k).astype(BF16), avg_ref[:KV_WIDTH, :KV_WIDTH], preferred_element_type=F32)
    k_out[...] = (k * lax.rsqrt(kms + NORM_EPS) * kg_ref[...]).astype(BF16)
    vt = lax.dot_general(wvt_ref[...], hn, NT_DIMS, preferred_element_type=F32)
    for b in range(ROW_TILE // ATTN_BLOCK):
        v3_out[b] = vt[:, b * ATTN_BLOCK:(b + 1) * ATTN_BLOCK].astype(BF16)
    u = jnp.dot(hn, wu_ref[...], preferred_element_type=F32)
    for lb in range(SSM_WIDTH // LANES):
        uscr[lb] = u[:, lb * LANES:(lb + 1) * LANES]
    for t in range(SSM_CHUNK):
        for lb in range(SSM_WIDTH // LANES):
            u3_out[t, :, lb * LANES:(lb + 1) * LANES] = (
                uscr[lb, pl.ds(t, ROW_TILE // SSM_CHUNK, stride=SSM_CHUNK), :].astype(BF16))


def _inproj(x2d, g, wqt, wk, wvt, wu, qg, kg, avg, casts=()):
    t = x2d.shape[0]
    cpt = ROW_TILE // SSM_CHUNK
    return _row_tiled_call(
        _inproj_body, t // ROW_TILE,
        in_specs=[pl.BlockSpec((ROW_TILE, D_MODEL), lambda i: (i, 0)),
                  _const_spec((1, D_MODEL)), _const_spec((ATTN_WIDTH, D_MODEL)),
                  _const_spec((D_MODEL, KV_WIDTH)), _const_spec((KV_WIDTH, D_MODEL)),
                  _const_spec((D_MODEL, SSM_WIDTH)), _const_spec((ATTN_WIDTH, 1)),
                  _const_spec((1, KV_WIDTH)), _const_spec((ATTN_WIDTH, ATTN_WIDTH))],
        out_specs=[pl.BlockSpec((ATTN_WIDTH, ROW_TILE), lambda i: (0, i)),
                   pl.BlockSpec((ROW_TILE, KV_WIDTH), lambda i: (i, 0)),
                   pl.BlockSpec((ROW_TILE // ATTN_BLOCK, KV_WIDTH, ATTN_BLOCK), lambda i: (i, 0, 0)),
                   pl.BlockSpec((SSM_CHUNK, cpt, SSM_WIDTH), lambda i: (0, i, 0))],
        out_shape=[jax.ShapeDtypeStruct((ATTN_WIDTH, t), BF16),
                   jax.ShapeDtypeStruct((t, KV_WIDTH), BF16),
                   jax.ShapeDtypeStruct((t // ATTN_BLOCK, KV_WIDTH, ATTN_BLOCK), BF16),
                   jax.ShapeDtypeStruct((SSM_CHUNK, t // SSM_CHUNK, SSM_WIDTH), BF16)],
        scratch_shapes=[pltpu.VMEM((SSM_WIDTH // LANES, ROW_TILE, LANES), F32)],
        name="inproj", args=(x2d, g, wqt, wk, wvt, wu, qg, kg, avg), casts=casts)


def _attn_body(sink_ref, qt_ref, k_ref, v3_ref, g_ref, eye_ref, o_ref, at_scr, *, blocks_per_seq):
    i = pl.program_id(0)
    key = lax.broadcasted_iota(jnp.int32, (ATTN_BLOCK, ATTN_BLOCK), 0)
    qry = lax.broadcasted_iota(jnp.int32, (ATTN_BLOCK, ATTN_BLOCK), 1)
    cur_ok = key <= qry
    prev_ok = key > qry
    zpad = jnp.zeros((HEAD_DIM, ATTN_BLOCK), BF16)
    nblk = ROW_TILE // ATTN_BLOCK
    for blk in range(nblk):
        gblk = i * nblk + blk
        pblk = jnp.maximum(gblk - 1, 0)
        row0 = pl.multiple_of(gblk * ATTN_BLOCK, ATTN_BLOCK)
        prev0 = pl.multiple_of(pblk * ATTN_BLOCK, ATTN_BLOCK)
        has_prev = (gblk % blocks_per_seq) != 0
        kc = k_ref[pl.ds(row0, ATTN_BLOCK), :]
        kp = k_ref[pl.ds(prev0, ATTN_BLOCK), :]
        vc = v3_ref[gblk]
        vp = v3_ref[pblk]
        qb = qt_ref[:, blk * ATTN_BLOCK:(blk + 1) * ATTN_BLOCK]
        pmask = jnp.logical_and(prev_ok, has_prev)
        outs = []
        for h in range(N_HEADS):
            kv = h // GQA
            qh = qb[h * HEAD_DIM:(h + 1) * HEAD_DIM, :]
            qpad = jnp.concatenate([qh, zpad] if kv == 0 else [zpad, qh], axis=0)
            sc = jnp.dot(kc, qpad, preferred_element_type=F32)
            sp = jnp.dot(kp, qpad, preferred_element_type=F32)
            sc = jnp.where(cur_ok, sc, NEG_INF)
            sp = jnp.where(pmask, sp, NEG_INF)
            sink = sink_ref[h]
            m = jnp.maximum(jnp.maximum(jnp.max(sc, axis=0, keepdims=True),
                                        jnp.max(sp, axis=0, keepdims=True)), sink)
            pc = jnp.exp(sc - m)
            pp = jnp.exp(sp - m)
            den = (jnp.sum(pc, axis=0, keepdims=True) + jnp.sum(pp, axis=0, keepdims=True)
                   + jnp.exp(sink - m))
            vs = slice(kv * HEAD_DIM, (kv + 1) * HEAD_DIM)
            o = (jnp.dot(vc[vs, :], pc.astype(BF16), preferred_element_type=F32)
                 + jnp.dot(vp[vs, :], pp.astype(BF16), preferred_element_type=F32))
            outs.append(o / den)
        a = jnp.concatenate(outs, axis=0)
        ms = jnp.mean(a * a, axis=0, keepdims=True)
        at_scr[:, blk * ATTN_BLOCK:(blk + 1) * ATTN_BLOCK] = (
            a * lax.rsqrt(ms + NORM_EPS) * g_ref[...]).astype(BF16)
    o_ref[...] = lax.dot_general(eye_ref[...], at_scr[...], NT_DIMS,
                                 preferred_element_type=F32).astype(BF16)


def _attention(qt, k, v3, sinks, gcol, eye, seq, casts=()):
    t = k.shape[0]
    return _row_tiled_call(
        functools.partial(_attn_body, blocks_per_seq=seq // ATTN_BLOCK), t // ROW_TILE,
        in_specs=[pl.BlockSpec(memory_space=pltpu.SMEM),
                  pl.BlockSpec((ATTN_WIDTH, ROW_TILE), lambda i: (0, i)),
                  _const_spec((t, KV_WIDTH)), _const_spec((t // ATTN_BLOCK, KV_WIDTH, ATTN_BLOCK)),
                  _const_spec((ATTN_WIDTH, 1)), _const_spec((ROW_TILE, ROW_TILE))],
        out_specs=[pl.BlockSpec((ROW_TILE, ATTN_WIDTH), lambda i: (i, 0))],
        out_shape=[jax.ShapeDtypeStruct((t, ATTN_WIDTH), BF16)],
        scratch_shapes=[pltpu.VMEM((ATTN_WIDTH, ROW_TILE), BF16)],
        name="swa_attention", args=(sinks, qt, k, v3, gcol, eye), casts=casts)


def _ssm_tables_body(lam_ref, lamc_ref, b_ref, c_ref, erep_ref, etile_ref,
                     m_out, wre_out, wim_out, vre_out, vim_out, k_out, apw_out, *, n_steps):
    hi = lax.Precision.HIGHEST
    lanes = 2 * SSM_STATE
    ch, gd = SSM_CHUNK, SSM_GROUP_DIM
    lr, li, dt = lam_ref[0, 0:1, :], lam_ref[0, 1:2, :], lam_ref[0, 2:3, :]

    def apow(p):
        mag = jnp.exp(p * (lr * dt))
        ang = p * (li * dt)
        return mag * jnp.cos(ang), mag * jnp.sin(ang)

    tau = lax.broadcasted_iota(jnp.int32, (ch, lanes), 0).astype(F32)
    p0r, p0i = apow(tau)
    p1r, p1i = apow(tau + 1.0)
    step = lax.broadcasted_iota(jnp.int32, (n_steps, lanes), 0).astype(F32)
    qr, qi = apow(step * float(ch))
    apw_out[0, 0] = qr
    apw_out[0, 1] = qi
    sel = lax.broadcasted_iota(jnp.int32, (8, lanes), 0)
    kr, ki = apow(jnp.where(sel == 0, float(ch), float(ch * n_steps)))
    k_out[0, 0] = kr[0:2]
    k_out[0, 1] = ki[0:2]

    lrc, lic, dtc = lamc_ref[0, :, 0:1], lamc_ref[0, :, 1:2], lamc_ref[0, :, 2:3]
    magc = jnp.exp(lrc * dtc)
    ab_re, ab_im = magc * jnp.cos(lic * dtc), magc * jnp.sin(lic * dtc)
    nr = ab_re - 1.0
    den = lrc * lrc + lic * lic
    f_re = (nr * lrc + ab_im * lic) / den
    f_im = (ab_im * lrc - nr * lic) / den
    br, bi = b_ref[0, 0], b_ref[0, 1]
    bb_re = f_re * br - f_im * bi
    bb_im = f_re * bi + f_im * br
    bx_re = jnp.dot(bb_re, etile_ref[...], precision=hi, preferred_element_type=F32)
    bx_im = jnp.dot(bb_im, etile_ref[...], precision=hi, preferred_element_type=F32)
    jrev = float(ch - 1) - lax.broadcasted_iota(jnp.int32, (lanes, ch), 1).astype(F32)
    mg = jnp.exp(jrev * (lrc * dtc))
    an = jrev * (lic * dtc)
    px_re = jnp.dot(mg * jnp.cos(an), erep_ref[...], precision=hi, preferred_element_type=F32)
    px_im = jnp.dot(mg * jnp.sin(an), erep_ref[...], precision=hi, preferred_element_type=F32)
    w_re = px_re * bx_re - px_im * bx_im
    w_im = px_re * bx_im + px_im * bx_re
    first = lax.broadcasted_iota(jnp.int32, (lanes, SSM_VEC), 0) < SSM_STATE
    pack = lambda w: jnp.concatenate([jnp.where(first, w, 0.0), jnp.where(first, 0.0, w)], axis=1)
    wre_out[0] = pack(w_re).astype(BF16)
    wim_out[0] = pack(w_im).astype(BF16)

    blk = lax.broadcasted_iota(jnp.int32, (SSM_VEC, SSM_VEC), 1) // gd
    for g in range(2):
        cr, ci = c_ref[g, 0], c_ref[g, 1]
        stack = lambda x: x.reshape(ch * gd, lanes)
        ca_re = stack(cr[None] * p0r[:, None, :] - ci[None] * p0i[:, None, :])
        ca_im = stack(cr[None] * p0i[:, None, :] + ci[None] * p0r[:, None, :])
        kt = (jnp.dot(ca_re, bx_re, precision=hi, preferred_element_type=F32)
              - jnp.dot(ca_im, bx_im, precision=hi, preferred_element_type=F32))
        m = jnp.where(blk == 0, kt, 0.0)
        for j in range(1, ch):
            shifted = jnp.concatenate([jnp.zeros((j * gd, SSM_VEC), F32), kt[:SSM_VEC - j * gd]], axis=0)
            m = jnp.where(blk == j, shifted, m)
        m_out[g] = m.astype(BF16)
        vre_out[g] = stack(cr[None] * p1r[:, None, :] - ci[None] * p1i[:, None, :]).astype(BF16)
        vim_out[g] = (-stack(cr[None] * p1i[:, None, :] + ci[None] * p1r[:, None, :])).astype(BF16)


def _ssm_tables(lam_re, lam_im, log_dt, b_re, b_im, c_re, c_im, d_skip, n_steps):
    g_, n_, c_ = SSM_GROUPS, SSM_STATE, SSM_GROUP_DIM
    np_ = g_ // 2
    lanes = 2 * n_
    dt = jnp.broadcast_to(jnp.exp(log_dt.astype(F32))[:, None], (g_, n_))
    lam = jnp.stack([lam_re.astype(F32), lam_im.astype(F32), dt], axis=1)
    lam_row = lam.reshape(np_, 2, 3, n_).transpose(0, 2, 1, 3).reshape(np_, 3, lanes)
    lam_col = lam_row.transpose(0, 2, 1)
    b2 = jnp.stack([b_re.astype(F32).reshape(np_, lanes, c_), b_im.astype(F32).reshape(np_, lanes, c_)],
                   axis=1)
    c2 = jnp.stack([c_re.astype(F32), c_im.astype(F32)], axis=1)
    z = jnp.zeros_like(c2)
    even = (jnp.arange(g_) % 2 == 0)[:, None, None, None]
    c2 = jnp.where(even, jnp.concatenate([c2, z], axis=3), jnp.concatenate([z, c2], axis=3))
    col = jnp.arange(SSM_VEC)
    erep = (col[None, :] // c_ == jnp.arange(SSM_CHUNK)[:, None]).astype(F32)
    etile = (col[None, :] % c_ == jnp.arange(c_)[:, None]).astype(F32)
    lead = lambda shape: pl.BlockSpec(shape, lambda p: (p,) + (0,) * (len(shape) - 1))
    m, wre, wim, vre, vim, consts, apw = pl.pallas_call(
        functools.partial(_ssm_tables_body, n_steps=n_steps),
        grid=(np_,),
        in_specs=[lead((1, 3, lanes)), lead((1, lanes, 3)), lead((1, 2, lanes, c_)),
                  lead((2, 2, c_, lanes)), _const_spec((SSM_CHUNK, SSM_VEC)), _const_spec((c_, SSM_VEC))],
        out_specs=[lead((2, SSM_VEC, SSM_VEC)), lead((1, lanes, 2 * SSM_VEC)), lead((1, lanes, 2 * SSM_VEC)),
                   lead((2, SSM_VEC, lanes)), lead((2, SSM_VEC, lanes)),
                   lead((1, 2, 2, lanes)), lead((1, 2, n_steps, lanes))],
        out_shape=[jax.ShapeDtypeStruct((g_, SSM_VEC, SSM_VEC), BF16),
                   jax.ShapeDtypeStruct((np_, lanes, 2 * SSM_VEC), BF16),
                   jax.ShapeDtypeStruct((np_, lanes, 2 * SSM_VEC), BF16),
                   jax.ShapeDtypeStruct((g_, SSM_VEC, lanes), BF16),
                   jax.ShapeDtypeStruct((g_, SSM_VEC, lanes), BF16),
                   jax.ShapeDtypeStruct((np_, 2, 2, lanes), F32),
                   jax.ShapeDtypeStruct((np_, 2, n_steps, lanes), F32)],
        compiler_params=_cparams(("parallel",)),
        name="s5_operators",
    )(lam_row, lam_col, b2, c2, erep, etile)
    dcol = jnp.tile(d_skip.astype(F32).reshape(g_, 1, c_), (1, SSM_CHUNK, 1)).reshape(g_, SSM_VEC, 1)
    return m, wre, wim, vre, vim, consts, apw, dcol


def _ssm_body(u_ref, eye_ref, m_ref, wre_ref, wim_ref, vre_ref, vim_ref, c_ref, apw_ref, d_ref,
              y_ref, ut, yt, pre, pim, sre, sim, *, n_steps, halves, pitch):
    lanes = 2 * SSM_STATE
    nk = u_ref.shape[1]
    gd = SSM_GROUP_DIM
    for t in range(SSM_CHUNK):
        xt = lax.dot_general(eye_ref[...], u_ref[t], NT_DIMS,
                             preferred_element_type=F32).astype(BF16)
        for g in range(SSM_LANE_GROUPS):
            ut[g, t * gd:(t + 1) * gd, :] = xt[g * gd:(g + 1) * gd, :]

    row = lax.broadcasted_iota(jnp.int32, (SSM_SEQS, lanes), 0)
    seq_start = (row % halves) == 0
    shift = lambda x: jnp.where(seq_start, 0.0, pltpu.roll(x, 1, axis=0))

    npairs = SSM_LANE_GROUPS // 2
    for pp in range(npairs):
        ucat = jnp.concatenate([ut[2 * pp], ut[2 * pp + 1]], axis=0)
        p_re = jnp.dot(wre_ref[pp], ucat, preferred_element_type=F32).T
        p_im = jnp.dot(wim_ref[pp], ucat, preferred_element_type=F32).T
        if pitch == n_steps:
            pre[pp] = p_re
            pim[pp] = p_im
        else:
            for s in range(SSM_SEQS):
                pre[pp, s * pitch:s * pitch + n_steps, :] = p_re[s * n_steps:(s + 1) * n_steps]
                pim[pp, s * pitch:s * pitch + n_steps, :] = p_im[s * n_steps:(s + 1) * n_steps]

    a16 = [(jnp.broadcast_to(c_ref[pp, 0, 0:1, :], (SSM_SEQS, lanes)),
            jnp.broadcast_to(c_ref[pp, 1, 0:1, :], (SSM_SEQS, lanes))) for pp in range(npairs)]

    def step(i, carry):
        idx = pl.ds(i, SSM_SEQS, stride=pitch)
        out = []
        for pp in range(npairs):
            s_re, s_im = carry[pp]
            ar, ai = a16[pp]
            sre[pp, idx, :] = s_re
            sim[pp, idx, :] = s_im
            out.append((ar * s_re - ai * s_im + pre[pp, idx, :],
                        ar * s_im + ai * s_re + pim[pp, idx, :]))
        return tuple(out)

    zero = jnp.zeros((SSM_SEQS, lanes), F32)
    ends = lax.fori_loop(0, n_steps, step, ((zero, zero),) * npairs, unroll=2)

    for pp in range(npairs):
        e_re, e_im = ends[pp]
        br = jnp.broadcast_to(c_ref[pp, 0, 1:2, :], (SSM_SEQS, lanes))
        bi = jnp.broadcast_to(c_ref[pp, 1, 1:2, :], (SSM_SEQS, lanes))
        c_re = jnp.zeros_like(e_re)
        c_im = jnp.zeros_like(e_im)
        for _ in range(halves - 1):
            n_re = e_re + br * c_re - bi * c_im
            n_im = e_im + br * c_im + bi * c_re
            c_re, c_im = shift(n_re), shift(n_im)

        apr = apw_ref[pp, 0][None, :, :]
        api = apw_ref[pp, 1][None, :, :]
        if pitch == n_steps:
            s3r = sre[pp].reshape(SSM_SEQS, n_steps, lanes)
            s3i = sim[pp].reshape(SSM_SEQS, n_steps, lanes)
        else:
            s3r = jnp.stack([sre[pp, s * pitch:s * pitch + n_steps, :] for s in range(SSM_SEQS)])
            s3i = jnp.stack([sim[pp, s * pitch:s * pitch + n_steps, :] for s in range(SSM_SEQS)])
        t_re = (s3r + apr * c_re[:, None, :] - api * c_im[:, None, :]).reshape(nk, lanes)
        t_im = (s3i + apr * c_im[:, None, :] + api * c_re[:, None, :]).reshape(nk, lanes)
        tr_hi = t_re.astype(BF16)
        tr_lo = (t_re - tr_hi.astype(F32)).astype(BF16)
        ti_hi = t_im.astype(BF16)
        ti_lo = (t_im - ti_hi.astype(F32)).astype(BF16)
        for g in (2 * pp, 2 * pp + 1):
            ug = ut[g]
            y = jnp.dot(m_ref[g], ug, preferred_element_type=F32)
            y += lax.dot_general(vre_ref[g], tr_hi, NT_DIMS, preferred_element_type=F32)
            y += lax.dot_general(vre_ref[g], tr_lo, NT_DIMS, preferred_element_type=F32)
            y += lax.dot_general(vim_ref[g], ti_hi, NT_DIMS, preferred_element_type=F32)
            y += lax.dot_general(vim_ref[g], ti_lo, NT_DIMS, preferred_element_type=F32)
            yt[g] = y + d_ref[g] * ug.astype(F32)

    for t in range(SSM_CHUNK):
        rows = jnp.concatenate([yt[g, t * gd:(t + 1) * gd, :] for g in range(SSM_LANE_GROUPS)], axis=0)
        y_ref[t] = rows.T


def _ssm(u3, tables, eye, bsz, seq):
    m, wre, wim, vre, vim, consts, apw, dcol = tables
    nk = u3.shape[1]
    halves = SSM_SEQS // bsz
    n_steps = seq // (SSM_CHUNK * halves)
    lanes = 2 * SSM_STATE
    lg = SSM_LANE_GROUPS
    pitch = n_steps + 8 if (n_steps // 8) % 2 == 0 else n_steps
    lead = lambda shape: pl.BlockSpec(shape, lambda b: (b,) + (0,) * (len(shape) - 1))
    return pl.pallas_call(
        functools.partial(_ssm_body, n_steps=n_steps, halves=halves, pitch=pitch),
        grid=(SSM_GROUPS // lg,),
        in_specs=[pl.BlockSpec((SSM_CHUNK, nk, LANES), lambda b: (0, 0, b)),
                  _const_spec((LANES, LANES)),
                  lead((lg, SSM_VEC, SSM_VEC)),
                  lead((lg // 2, lanes, 2 * SSM_VEC)), lead((lg // 2, lanes, 2 * SSM_VEC)),
                  lead((lg, SSM_VEC, lanes)), lead((lg, SSM_VEC, lanes)),
                  lead((lg // 2, 2, 2, lanes)), lead((lg // 2, 2, n_steps, lanes)),
                  lead((lg, SSM_VEC, 1))],
        out_specs=pl.BlockSpec((SSM_CHUNK, nk, LANES), lambda b: (0, 0, b)),
        out_shape=jax.ShapeDtypeStruct((SSM_CHUNK, nk, SSM_WIDTH), F32),
        scratch_shapes=[pltpu.VMEM((lg, SSM_VEC, nk), BF16), pltpu.VMEM((lg, SSM_VEC, nk), F32)]
                       + [pltpu.VMEM((lg // 2, SSM_SEQS * pitch, lanes), F32)] * 4,
        compiler_params=_cparams(("parallel",)),
        name="s5_ssm",
    )(u3, eye, m, wre, wim, vre, vim, consts, apw, dcol)


def _mix_math(x_ref, a_ref, y3_ref, wglu_ref, sg_ref, wo_ref, fg_ref, yscr):
    nlb = SSM_WIDTH // LANES
    for t in range(SSM_CHUNK):
        for lb in range(nlb):
            yscr[lb, pl.ds(t, ROW_TILE // SSM_CHUNK, stride=SSM_CHUNK), :] = (
                y3_ref[t, :, lb * LANES:(lb + 1) * LANES])
    y = jax.nn.gelu(jnp.concatenate([yscr[lb] for lb in range(nlb)], axis=1))
    z = y * jax.nn.sigmoid(jnp.dot(y.astype(BF16), wglu_ref[...], preferred_element_type=F32))
    ms = jnp.mean(z * z, axis=-1, keepdims=True)
    sn = (z * lax.rsqrt(ms + NORM_EPS) * sg_ref[...]).astype(BF16)
    x1 = (x_ref[...]
          + jnp.dot(a_ref[...], wo_ref[:ATTN_WIDTH, :], preferred_element_type=F32)
          + jnp.dot(sn, wo_ref[ATTN_WIDTH:, :], preferred_element_type=F32))
    ms1 = jnp.mean(x1 * x1, axis=-1, keepdims=True)
    return x1, x1 * lax.rsqrt(ms1 + NORM_EPS) * fg_ref[...]


def _mix_router_body(x_ref, a_ref, y3_ref, wglu_ref, sg_ref, wo_ref, fg_ref, rwt_ref,
                     x1_out, lg_out, yscr):
    x1, hn = _mix_math(x_ref, a_ref, y3_ref, wglu_ref, sg_ref, wo_ref, fg_ref, yscr)
    x1_out[...] = x1
    lane = lax.broadcasted_iota(jnp.int32, lg_out.shape, 1)
    lg = jnp.zeros(lg_out.shape, F32)
    for e in range(lg_out.shape[1]):
        lg = jnp.where(lane == e, jnp.sum(hn * rwt_ref[e:e + 1, :], axis=-1, keepdims=True), lg)
    lg_out[...] = lg


def _mix_ffn_body(x_ref, a_ref, y3_ref, wglu_ref, sg_ref, wo_ref, fg_ref, wg_ref, wu_ref, wd_ref,
                  o_ref, yscr):
    x1, hn = _mix_math(x_ref, a_ref, y3_ref, wglu_ref, sg_ref, wo_ref, fg_ref, yscr)
    h = hn.astype(BF16)
    o_ref[...] = x1
    for c in range(D_FF // FF_CHUNK):
        sl = slice(c * FF_CHUNK, (c + 1) * FF_CHUNK)
        g = jnp.dot(h, wg_ref[:, sl], preferred_element_type=F32)
        u = jnp.dot(h, wu_ref[:, sl], preferred_element_type=F32)
        a = (jax.nn.silu(g) * u).astype(BF16)
        o_ref[...] += jnp.dot(a, wd_ref[sl, :], preferred_element_type=F32)


def _mix_specs():
    row = lambda w: pl.BlockSpec((ROW_TILE, w), lambda i: (i, 0))
    in_specs = [row(D_MODEL), row(ATTN_WIDTH),
                pl.BlockSpec((SSM_CHUNK, ROW_TILE // SSM_CHUNK, SSM_WIDTH), lambda i: (0, i, 0)),
                _const_spec((SSM_WIDTH, SSM_WIDTH)), _const_spec((1, SSM_WIDTH)),
                _const_spec((D_MODEL, D_MODEL)), _const_spec((1, D_MODEL))]
    return row, in_specs, pltpu.VMEM((SSM_WIDTH // LANES, ROW_TILE, LANES), F32)


def _mix_router(x2d, attn, y3, wglu, sg, wo, fg, rwt):
    t = x2d.shape[0]
    ne = rwt.shape[0]
    row, in_specs, yscr = _mix_specs()
    return pl.pallas_call(
        _mix_router_body,
        grid=(t // ROW_TILE,),
        in_specs=in_specs + [_const_spec((ne, D_MODEL))],
        out_specs=[row(D_MODEL), row(ne)],
        out_shape=[jax.ShapeDtypeStruct((t, D_MODEL), F32), jax.ShapeDtypeStruct((t, ne), F32)],
        scratch_shapes=[yscr],
        compiler_params=_cparams(("parallel",)),
        name="mix_router",
    )(x2d, attn, y3, wglu, sg, wo, fg, rwt)


def _mix_ffn(x2d, attn, y3, wglu, sg, wo, fg, wg, wu, wd, casts=()):
    t = x2d.shape[0]
    row, in_specs, yscr = _mix_specs()
    return _row_tiled_call(
        _mix_ffn_body, t // ROW_TILE,
        in_specs=in_specs + [_resident_spec((D_MODEL, D_FF)), _resident_spec((D_MODEL, D_FF)),
                             _resident_spec((D_FF, D_MODEL))],
        out_specs=[row(D_MODEL)],
        out_shape=[jax.ShapeDtypeStruct((t, D_MODEL), F32)],
        scratch_shapes=[yscr],
        name="mix_dense_ffn", args=(x2d, attn, y3, wglu, sg, wo, fg, wg, wu, wd), casts=casts)


def _expert_ffn_body(e_ref, nused_ref, nvalid_ref, tok_ref, dst_ref, x1_hbm, fg_ref,
                     wg_ref, wu_ref, wd_ref, out_hbm, xbuf, acc, ybuf, gsem, ssem):
    b = pl.program_id(0)
    n_used = nused_ref[0]
    n_ff = D_FF // FF_CHUNK
    rows_per_chunk = -(-MOE_ROWS // n_ff)

    def gather_row(blk, r):
        i = tok_ref[blk * MOE_ROWS + r]
        pltpu.make_async_copy(x1_hbm.at[pl.ds(i, 1)], xbuf.at[pl.ds(r, 1)], gsem).start()

    def scatter_row(blk, r):
        i = dst_ref[blk * MOE_ROWS + r]
        pltpu.make_async_copy(ybuf.at[pl.ds(r, 1)], out_hbm.at[pl.ds(i, 1)], ssem).start()

    def scatter_rows(blk, n):
        def body(r, carry):
            scatter_row(blk, r)
            return carry
        lax.fori_loop(0, n, body, 0)

    def wait_gather():
        pltpu.make_async_copy(x1_hbm.at[pl.ds(0, MOE_ROWS)], xbuf, gsem).wait()

    def wait_scatter(n):
        n8 = pl.multiple_of(n & -8, 8)

        @pl.when(n8 > 0)
        def _():
            pltpu.make_async_copy(ybuf.at[pl.ds(0, n8)], out_hbm.at[pl.ds(0, n8)], ssem).wait()

        def one(_, carry):
            pltpu.make_async_copy(ybuf.at[pl.ds(0, 1)], out_hbm.at[pl.ds(0, 1)], ssem).wait()
            return carry
        lax.fori_loop(0, n - n8, one, 0)

    def ffn(scatter_prev):
        wait_gather()
        x = xbuf[...]
        ms = jnp.mean(x * x, axis=-1, keepdims=True)
        h = (x * lax.rsqrt(ms + NORM_EPS) * fg_ref[...]).astype(BF16)
        for c in range(n_ff):
            for r in range(c * rows_per_chunk, min((c + 1) * rows_per_chunk, MOE_ROWS)):
                gather_row(b + 1, r)
                if scatter_prev:
                    scatter_row(b - 1, r)
            sl = slice(c * FF_CHUNK, (c + 1) * FF_CHUNK)
            g = jnp.dot(h, wg_ref[0, :, sl], preferred_element_type=F32)
            u = jnp.dot(h, wu_ref[0, :, sl], preferred_element_type=F32)
            a = (jax.nn.silu(g) * u).astype(BF16)
            part = jnp.dot(a, wd_ref[0, sl, :], preferred_element_type=F32)
            if c == 0:
                acc[...] = part
            else:
                acc[...] += part

    @pl.when(b < n_used)
    def _():
        @pl.when(b == 0)
        def _():
            def body(r, carry):
                gather_row(0, r)
                return carry
            lax.fori_loop(0, MOE_ROWS, body, 0, unroll=8)

        n_prev = jnp.where(b >= 1, nvalid_ref[jnp.maximum(b - 1, 0)], 0)

        @pl.when(n_prev == MOE_ROWS)
        def _():
            ffn(scatter_prev=True)

        @pl.when(n_prev != MOE_ROWS)
        def _():
            scatter_rows(b - 1, n_prev)
            ffn(scatter_prev=False)

        wait_scatter(n_prev)
        ybuf[...] = acc[...]

        @pl.when(b == n_used - 1)
        def _():
            scatter_rows(b, nvalid_ref[b])
            wait_scatter(nvalid_ref[b])
            wait_gather()


def _expert_ffn(x1, fg, tok, dst, block_e, n_used, n_valid, wg, wu, wd, nblk, out_rows):
    wspec = lambda shape: pl.BlockSpec(shape, lambda b, e, *_: (e[b], 0, 0))
    return pl.pallas_call(
        _expert_ffn_body,
        grid_spec=pltpu.PrefetchScalarGridSpec(
            num_scalar_prefetch=5, grid=(nblk,),
            in_specs=[pl.BlockSpec(memory_space=pl.ANY),
                      pl.BlockSpec((1, D_MODEL), lambda b, *_: (0, 0)),
                      wspec((1, D_MODEL, D_FF)), wspec((1, D_MODEL, D_FF)), wspec((1, D_FF, D_MODEL))],
            out_specs=pl.BlockSpec(memory_space=pl.ANY),
            scratch_shapes=[pltpu.VMEM((MOE_ROWS, D_MODEL), F32)] * 3
                           + [pltpu.SemaphoreType.DMA(()), pltpu.SemaphoreType.DMA(())]),
        out_shape=jax.ShapeDtypeStruct((out_rows, D_MODEL), F32),
        compiler_params=pltpu.CompilerParams(dimension_semantics=("arbitrary",),
                                             vmem_limit_bytes=MOE_VMEM_LIMIT),
        name="moe_expert_ffn",
    )(block_e, n_used, n_valid, tok, dst, x1, fg, wg, wu, wd)


def _combine_body(x_ref, y0_ref, y1_ref, g_ref, o_ref):
    o_ref[...] = x_ref[...] + y0_ref[...] * g_ref[:, 0:1] + y1_ref[...] * g_ref[:, 1:2]


def _combine(x1, y2, gates):
    t = x1.shape[0]
    nt = t // ROW_TILE
    return pl.pallas_call(
        _combine_body,
        grid=(nt,),
        in_specs=[pl.BlockSpec((ROW_TILE, D_MODEL), lambda i: (i, 0)),
                  pl.BlockSpec((ROW_TILE, D_MODEL), lambda i: (i, 0)),
                  pl.BlockSpec((ROW_TILE, D_MODEL), lambda i: (i + nt, 0)),
                  pl.BlockSpec((ROW_TILE, TOP_K), lambda i: (i, 0))],
        out_specs=pl.BlockSpec((ROW_TILE, D_MODEL), lambda i: (i, 0)),
        out_shape=jax.ShapeDtypeStruct((t, D_MODEL), F32),
        compiler_params=_cparams(("parallel",)),
        name="moe_combine",
    )(x1, y2, y2, gates)


def _moe(x1, fg, logits, wg, wu, wd):
    t = x1.shape[0]
    n_assign = t * TOP_K
    nblk = n_assign // MOE_ROWS + N_EXPERTS
    cap = nblk * MOE_ROWS
    top_v, top_i = lax.top_k(logits, TOP_K)
    gates = jax.nn.softmax(top_v, axis=-1)
    flat_e = top_i.reshape(-1).astype(jnp.int32)
    order = jnp.argsort(flat_e, stable=True).astype(jnp.int32)
    experts = jnp.arange(N_EXPERTS, dtype=jnp.int32)
    counts = jnp.sum((flat_e[:, None] == experts[None, :]).astype(jnp.int32), axis=0)
    start = jnp.cumsum(counts) - counts
    padded = ((counts + MOE_ROWS - 1) // MOE_ROWS) * MOE_ROWS
    pend = jnp.cumsum(padded)
    pstart = pend - padded
    p = jnp.arange(cap, dtype=jnp.int32)
    e_p = jnp.minimum(jnp.sum((p[:, None] >= pend[None, :]).astype(jnp.int32), axis=1), N_EXPERTS - 1)
    rank = p - pstart[e_p]
    valid = jnp.logical_and(rank < counts[e_p], p < pend[-1])
    a_p = order[jnp.clip(start[e_p] + rank, 0, n_assign - 1)]
    tok = jnp.where(valid, a_p // TOP_K, 0).astype(jnp.int32)
    dst = jnp.where(valid, (a_p % TOP_K) * t + a_p // TOP_K, 0).astype(jnp.int32)
    block_e = e_p[::MOE_ROWS]
    n_used = (pend[-1:] // MOE_ROWS).astype(jnp.int32)
    n_valid = jnp.sum(valid.reshape(nblk, MOE_ROWS).astype(jnp.int32), axis=1)
    y2 = _expert_ffn(x1, fg, tok, dst, block_e, n_used, n_valid, wg, wu, wd, nblk, n_assign)
    return _combine(x1, y2, gates)


def kernel(x, attn_norm_g, w_in, q_norm_g, k_norm_g, sinks, lam_re, lam_im, log_dt, b_re, b_im,
           c_re, c_im, d_skip, w_glu, attn_out_g, ssm_out_g, w_o, ffn_norm_g, dense_wg, dense_wu,
           dense_wd, router_w, moe_wg, moe_wu, moe_wd):
    bsz, seq, _ = x.shape
    depth = w_in.shape[0]
    assert SSM_SEQS % bsz == 0 and seq % (SSM_CHUNK * (SSM_SEQS // bsz)) == 0
    assert seq % ROW_TILE == 0 and ROW_TILE % ATTN_BLOCK == 0
    x2d = x.reshape(bsz * seq, D_MODEL).astype(F32)
    head = jnp.arange(ATTN_WIDTH) // HEAD_DIM
    avg = jnp.where(head[:, None] == head[None, :], 1.0 / HEAD_DIM, 0.0).astype(BF16)
    eye = jnp.eye(ROW_TILE, dtype=BF16)
    n_steps = seq // (SSM_CHUNK * (SSM_SEQS // bsz))
    row = lambda v: v.astype(F32).reshape(1, -1)
    col = lambda v: v.astype(F32).reshape(-1, 1)
    ko, vo, uo = ATTN_WIDTH, ATTN_WIDTH + KV_WIDTH, ATTN_WIDTH + 2 * KV_WIDTH
    flat = lambda w: w.astype(F32).reshape(-1, w.shape[-1])
    moe_wg_bf = None
    for l in range(depth):
        i = l // 2
        dense = l % 2 == 0
        qg = jnp.tile(q_norm_g[l].astype(F32), N_HEADS) * (HEAD_DIM ** -0.5)
        kg = jnp.tile(k_norm_g[l].astype(F32), N_KV_HEADS)
        w = w_in[l].astype(BF16)
        early = [flat(dense_wg[i]), flat(dense_wu[i])] if dense else [flat(moe_wu[i])]
        late = [flat(dense_wd[i])] if dense else [flat(moe_wd[i])]
        (qt, k, v3, u3), early_bf = _inproj(x2d, row(attn_norm_g[l]), w[:, :ko].T, w[:, ko:vo],
                                            w[:, vo:uo].T, w[:, uo:], col(qg), row(kg), avg, casts=early)
        (attn,), late_bf = _attention(qt, k, v3, sinks[l].astype(F32), col(attn_out_g[l]), eye, seq,
                                      casts=late)
        tables = _ssm_tables(lam_re[l], lam_im[l], log_dt[l], b_re[l], b_im[l], c_re[l], c_im[l],
                             d_skip[l], n_steps)
        y3 = _ssm(u3, tables, eye[:LANES, :LANES], bsz, seq)
        mix_args = (x2d, attn, y3, w_glu[l].astype(BF16), row(ssm_out_g[l]), w_o[l].astype(BF16),
                    row(ffn_norm_g[l]))
        if dense:
            ahead = [flat(moe_wg[i])] if l + 1 < depth else []
            (x2d,), ahead_bf = _mix_ffn(*mix_args, early_bf[0], early_bf[1], late_bf[0], casts=ahead)
            if ahead:
                moe_wg_bf = ahead_bf[0].reshape(moe_wg[i].shape)
        else:
            x1, logits = _mix_router(*mix_args, router_w[i].astype(F32).T)
            x2d = _moe(x1, row(ffn_norm_g[l]), logits, moe_wg_bf,
                       early_bf[0].reshape(moe_wu[i].shape), late_bf[0].reshape(moe_wd[i].shape))
    return x2d.reshape(bsz, seq, D_MODEL)
```

```python
import functools
import math

import jax
import jax.numpy as jnp
from jax import lax
from jax.experimental import pallas as pl
from jax.experimental.pallas import tpu as pltpu

F32 = jnp.float32
BF16 = jnp.bfloat16

D_MODEL = 1024
HEAD_DIM = 64
N_HEADS = 8
N_KV_HEADS = 2
GQA = N_HEADS // N_KV_HEADS
ATTN_WIDTH = N_HEADS * HEAD_DIM
KV_WIDTH = N_KV_HEADS * HEAD_DIM
ATTN_BLOCK = 128
SSM_WIDTH = D_MODEL - ATTN_WIDTH
SSM_GROUP_DIM = 16
SSM_GROUPS = SSM_WIDTH // SSM_GROUP_DIM
SSM_STATE = 64
IN_WIDTH = ATTN_WIDTH + 2 * KV_WIDTH + SSM_WIDTH
D_FF = 3584
N_EXPERTS = 8
TOP_K = 2
NORM_EPS = 1e-6
NEG_INF = -1e30

LANES = 128
ROW_TILE = 512
SSM_CHUNK = 16
SSM_SEQS = 8
SSM_VEC = SSM_CHUNK * SSM_GROUP_DIM
SSM_LANE_GROUPS = LANES // SSM_GROUP_DIM
FF_CHUNK = 512
MOE_ROWS = 512
VMEM_LIMIT = 56 * 1024 * 1024
MOE_VMEM_LIMIT = 60 * 1024 * 1024

NT_DIMS = (((1,), (1,)), ((), ()))


def _cparams(sem, vmem=VMEM_LIMIT):
    return pltpu.CompilerParams(dimension_semantics=sem, vmem_limit_bytes=vmem)


def _const_spec(shape):
    n = len(shape)
    return pl.BlockSpec(shape, lambda *_: (0,) * n)


def _resident_spec(shape):
    n = len(shape)
    return pl.BlockSpec(shape, lambda *_: (0,) * n, pipeline_mode=pl.Buffered(1))


def _row_tiled_call(body, steps, in_specs, out_specs, out_shape, scratch_shapes, name, args, casts=()):
    n_in, n_out, n_c = len(in_specs), len(out_specs), len(casts)

    def hosted(*refs):
        ins, cast_in = refs[:n_in], refs[n_in:n_in + n_c]
        outs = refs[n_in + n_c:n_in + n_c + n_out]
        cast_out = refs[n_in + n_c + n_out:n_in + 2 * n_c + n_out]
        body(*ins, *outs, *refs[n_in + 2 * n_c + n_out:])
        for src, dst in zip(cast_in, cast_out):
            dst[...] = src[...].astype(BF16)

    for w in casts:
        assert w.ndim == 2 and w.shape[0] % (steps * 16) == 0, w.shape
    cast_specs = [pl.BlockSpec((w.shape[0] // steps, w.shape[1]), lambda i: (i, 0)) for w in casts]
    res = pl.pallas_call(
        hosted,
        grid=(steps,),
        in_specs=list(in_specs) + cast_specs,
        out_specs=list(out_specs) + cast_specs,
        out_shape=list(out_shape) + [jax.ShapeDtypeStruct(w.shape, BF16) for w in casts],
        scratch_shapes=scratch_shapes,
        compiler_params=_cparams(("parallel",)),
        name=name,
    )(*args, *casts)
    return res[:n_out], res[n_out:]


def _inproj_body(x_ref, g_ref, wqvt_ref, wku_ref, qg_ref, kg_ref, avg_ref,
                 qt_out, k_out, v3_out, u3_out, uscr):
    x = x_ref[...]
    ms = jnp.mean(x * x, axis=-1, keepdims=True)
    hn = (x * lax.rsqrt(ms + NORM_EPS) * g_ref[...]).astype(BF16)
    qvt = lax.dot_general(wqvt_ref[...], hn, NT_DIMS, preferred_element_type=F32)
    ku = jnp.dot(hn, wku_ref[...], preferred_element_type=F32)
    qt = qvt[:ATTN_WIDTH]
    qms = jnp.dot(avg_ref[...], (qt * qt).astype(BF16), preferred_element_type=F32)
    qt_out[...] = (qt * lax.rsqrt(qms + NORM_EPS) * qg_ref[...]).astype(BF16)
    k = ku[:, :KV_WIDTH]
    kms = jnp.dot((k * k).astype(BF16), avg_ref[:KV_WIDTH, :KV_WIDTH], preferred_element_type=F32)
    k_out[...] = (k * lax.rsqrt(kms + NORM_EPS) * kg_ref[...]).astype(BF16)
    vt = qvt[ATTN_WIDTH:]
    for b in range(ROW_TILE // ATTN_BLOCK):
        v3_out[b] = vt[:, b * ATTN_BLOCK:(b + 1) * ATTN_BLOCK].astype(BF16)
    u = ku[:, KV_WIDTH:]
    for lb in range(SSM_WIDTH // LANES):
        uscr[lb] = u[:, lb * LANES:(lb + 1) * LANES]
    for t in range(SSM_CHUNK):
        for lb in range(SSM_WIDTH // LANES):
            u3_out[t, :, lb * LANES:(lb + 1) * LANES] = (
                uscr[lb, pl.ds(t, ROW_TILE // SSM_CHUNK, stride=SSM_CHUNK), :].astype(BF16))


def _inproj(x2d, g, wqvt, wku, qg, kg, avg, casts=()):
    t = x2d.shape[0]
    cpt = ROW_TILE // SSM_CHUNK
    return _row_tiled_call(
        _inproj_body, t // ROW_TILE,
        in_specs=[pl.BlockSpec((ROW_TILE, D_MODEL), lambda i: (i, 0)),
                  _const_spec((1, D_MODEL)), _const_spec((ATTN_WIDTH + KV_WIDTH, D_MODEL)),
                  _const_spec((D_MODEL, KV_WIDTH + SSM_WIDTH)), _const_spec((ATTN_WIDTH, 1)),
                  _const_spec((1, KV_WIDTH)), _const_spec((ATTN_WIDTH, ATTN_WIDTH))],
        out_specs=[pl.BlockSpec((ATTN_WIDTH, ROW_TILE), lambda i: (0, i)),
                   pl.BlockSpec((ROW_TILE, KV_WIDTH), lambda i: (i, 0)),
                   pl.BlockSpec((ROW_TILE // ATTN_BLOCK, KV_WIDTH, ATTN_BLOCK), lambda i: (i, 0, 0)),
                   pl.BlockSpec((SSM_CHUNK, cpt, SSM_WIDTH), lambda i: (0, i, 0))],
        out_shape=[jax.ShapeDtypeStruct((ATTN_WIDTH, t), BF16),
                   jax.ShapeDtypeStruct((t, KV_WIDTH), BF16),
                   jax.ShapeDtypeStruct((t // ATTN_BLOCK, KV_WIDTH, ATTN_BLOCK), BF16),
                   jax.ShapeDtypeStruct((SSM_CHUNK, t // SSM_CHUNK, SSM_WIDTH), BF16)],
        scratch_shapes=[pltpu.VMEM((SSM_WIDTH // LANES, ROW_TILE, LANES), F32)],
        name="inproj", args=(x2d, g, wqvt, wku, qg, kg, avg), casts=casts)


def _attn_body(sink_ref, qt_ref, k_ref, v3_ref, g_ref, eye_ref, o_ref, at_scr, *, blocks_per_seq):
    i = pl.program_id(0)
    key = lax.broadcasted_iota(jnp.int32, (ATTN_BLOCK, ATTN_BLOCK), 0)
    qry = lax.broadcasted_iota(jnp.int32, (ATTN_BLOCK, ATTN_BLOCK), 1)
    cur_ok = key <= qry
    zpad = jnp.zeros((HEAD_DIM, ATTN_BLOCK), BF16)
    nblk = ROW_TILE // ATTN_BLOCK
    for blk in range(nblk):
        gblk = i * nblk + blk
        pblk = jnp.maximum(gblk - 1, 0)
        row0 = pl.multiple_of(gblk * ATTN_BLOCK, ATTN_BLOCK)
        prev0 = pl.multiple_of(pblk * ATTN_BLOCK, ATTN_BLOCK)
        has_prev = (gblk % blocks_per_seq) != 0
        kc = k_ref[pl.ds(row0, ATTN_BLOCK), :]
        kp = k_ref[pl.ds(prev0, ATTN_BLOCK), :]
        vc = v3_ref[gblk]
        vp = v3_ref[pblk]
        qb = qt_ref[:, blk * ATTN_BLOCK:(blk + 1) * ATTN_BLOCK]
        keep = jnp.logical_or(cur_ok, has_prev)
        outs = []
        for h in range(N_HEADS):
            kv = h // GQA
            qh = qb[h * HEAD_DIM:(h + 1) * HEAD_DIM, :]
            qpad = jnp.concatenate([qh, zpad] if kv == 0 else [zpad, qh], axis=0)
            sc = jnp.dot(kc, qpad, preferred_element_type=F32)
            sp = jnp.dot(kp, qpad, preferred_element_type=F32)
            s = jnp.where(keep, jnp.where(cur_ok, sc, sp), NEG_INF)
            sink = sink_ref[h]
            m = jnp.maximum(jnp.max(s, axis=0, keepdims=True), sink)
            p = jnp.exp(s - m)
            den = jnp.sum(p, axis=0, keepdims=True) + jnp.exp(sink - m)
            pb = p.astype(BF16)
            zero = jnp.zeros_like(pb)
            vs = slice(kv * HEAD_DIM, (kv + 1) * HEAD_DIM)
            o = (jnp.dot(vc[vs, :], jnp.where(cur_ok, pb, zero), preferred_element_type=F32)
                 + jnp.dot(vp[vs, :], jnp.where(cur_ok, zero, pb), preferred_element_type=F32))
            outs.append(o / den)
        a = jnp.concatenate(outs, axis=0)
        ms = jnp.mean(a * a, axis=0, keepdims=True)
        at_scr[:, blk * ATTN_BLOCK:(blk + 1) * ATTN_BLOCK] = (
            a * lax.rsqrt(ms + NORM_EPS) * g_ref[...]).astype(BF16)
    o_ref[...] = lax.dot_general(eye_ref[...], at_scr[...], NT_DIMS,
                                 preferred_element_type=F32).astype(BF16)


def _attention(qt, k, v3, sinks, gcol, eye, seq, casts=()):
    t = k.shape[0]
    return _row_tiled_call(
        functools.partial(_attn_body, blocks_per_seq=seq // ATTN_BLOCK), t // ROW_TILE,
        in_specs=[pl.BlockSpec(memory_space=pltpu.SMEM),
                  pl.BlockSpec((ATTN_WIDTH, ROW_TILE), lambda i: (0, i)),
                  _const_spec((t, KV_WIDTH)), _const_spec((t // ATTN_BLOCK, KV_WIDTH, ATTN_BLOCK)),
                  _const_spec((ATTN_WIDTH, 1)), _const_spec((ROW_TILE, ROW_TILE))],
        out_specs=[pl.BlockSpec((ROW_TILE, ATTN_WIDTH), lambda i: (i, 0))],
        out_shape=[jax.ShapeDtypeStruct((t, ATTN_WIDTH), BF16)],
        scratch_shapes=[pltpu.VMEM((ATTN_WIDTH, ROW_TILE), BF16)],
        name="swa_attention", args=(sinks, qt, k, v3, gcol, eye), casts=casts)


def _ssm_tables_body(lam_ref, lamc_ref, b_ref, c_ref, erep_ref, etile_ref,
                     m_out, wre_out, wim_out, vre_out, vim_out, k_out, apw_out, *, n_steps):
    hi = lax.Precision.HIGHEST
    lanes = 2 * SSM_STATE
    ch, gd = SSM_CHUNK, SSM_GROUP_DIM
    lr, li, dt = lam_ref[0, 0:1, :], lam_ref[0, 1:2, :], lam_ref[0, 2:3, :]

    def apow(p):
        mag = jnp.exp(p * (lr * dt))
        ang = p * (li * dt)
        return mag * jnp.cos(ang), mag * jnp.sin(ang)

    tau = lax.broadcasted_iota(jnp.int32, (ch, lanes), 0).astype(F32)
    p0r, p0i = apow(tau)
    p1r, p1i = apow(tau + 1.0)
    step = lax.broadcasted_iota(jnp.int32, (n_steps, lanes), 0).astype(F32)
    qr, qi = apow(step * float(ch))
    apw_out[0, 0] = qr
    apw_out[0, 1] = qi
    sel = lax.broadcasted_iota(jnp.int32, (8, lanes), 0)
    kr, ki = apow(jnp.where(sel == 0, float(ch), float(ch * n_steps)))
    k_out[0, 0] = kr[0:2]
    k_out[0, 1] = ki[0:2]

    lrc, lic, dtc = lamc_ref[0, :, 0:1], lamc_ref[0, :, 1:2], lamc_ref[0, :, 2:3]
    magc = jnp.exp(lrc * dtc)
    ab_re, ab_im = magc * jnp.cos(lic * dtc), magc * jnp.sin(lic * dtc)
    nr = ab_re - 1.0
    den = lrc * lrc + lic * lic
    f_re = (nr * lrc + ab_im * lic) / den
    f_im = (ab_im * lrc - nr * lic) / den
    br, bi = b_ref[0, 0], b_ref[0, 1]
    bb_re = f_re * br - f_im * bi
    bb_im = f_re * bi + f_im * br
    bx_re = jnp.dot(bb_re, etile_ref[...], precision=hi, preferred_element_type=F32)
    bx_im = jnp.dot(bb_im, etile_ref[...], precision=hi, preferred_element_type=F32)
    jrev = float(ch - 1) - lax.broadcasted_iota(jnp.int32, (lanes, ch), 1).astype(F32)
    mg = jnp.exp(jrev * (lrc * dtc))
    an = jrev * (lic * dtc)
    px_re = jnp.dot(mg * jnp.cos(an), erep_ref[...], precision=hi, preferred_element_type=F32)
    px_im = jnp.dot(mg * jnp.sin(an), erep_ref[...], precision=hi, preferred_element_type=F32)
    w_re = px_re * bx_re - px_im * bx_im
    w_im = px_re * bx_im + px_im * bx_re
    first = lax.broadcasted_iota(jnp.int32, (lanes, SSM_VEC), 0) < SSM_STATE
    pack = lambda w: jnp.concatenate([jnp.where(first, w, 0.0), jnp.where(first, 0.0, w)], axis=1)
    wre_out[0] = pack(w_re).astype(BF16)
    wim_out[0] = pack(w_im).astype(BF16)

    blk = lax.broadcasted_iota(jnp.int32, (SSM_VEC, SSM_VEC), 1) // gd
    for g in range(2):
        cr, ci = c_ref[g, 0], c_ref[g, 1]
        stack = lambda x: x.reshape(ch * gd, lanes)
        ca_re = stack(cr[None] * p0r[:, None, :] - ci[None] * p0i[:, None, :])
        ca_im = stack(cr[None] * p0i[:, None, :] + ci[None] * p0r[:, None, :])
        kt = (jnp.dot(ca_re, bx_re, precision=hi, preferred_element_type=F32)
              - jnp.dot(ca_im, bx_im, precision=hi, preferred_element_type=F32))
        m = jnp.where(blk == 0, kt, 0.0)
        for j in range(1, ch):
            shifted = jnp.concatenate([jnp.zeros((j * gd, SSM_VEC), F32), kt[:SSM_VEC - j * gd]], axis=0)
            m = jnp.where(blk == j, shifted, m)
        m_out[g] = m.astype(BF16)
        vre_out[g] = stack(cr[None] * p1r[:, None, :] - ci[None] * p1i[:, None, :]).astype(BF16)
        vim_out[g] = (-stack(cr[None] * p1i[:, None, :] + ci[None] * p1r[:, None, :])).astype(BF16)


def _ssm_tables(lam_re, lam_im, log_dt, b_re, b_im, c_re, c_im, d_skip, n_steps):
    g_, n_, c_ = SSM_GROUPS, SSM_STATE, SSM_GROUP_DIM
    np_ = g_ // 2
    lanes = 2 * n_
    dt = jnp.broadcast_to(jnp.exp(log_dt.astype(F32))[:, None], (g_, n_))
    lam = jnp.stack([lam_re.astype(F32), lam_im.astype(F32), dt], axis=1)
    lam_row = lam.reshape(np_, 2, 3, n_).transpose(0, 2, 1, 3).reshape(np_, 3, lanes)
    lam_col = lam_row.transpose(0, 2, 1)
    b2 = jnp.stack([b_re.astype(F32).reshape(np_, lanes, c_), b_im.astype(F32).reshape(np_, lanes, c_)],
                   axis=1)
    c2 = jnp.stack([c_re.astype(F32), c_im.astype(F32)], axis=1)
    z = jnp.zeros_like(c2)
    even = (jnp.arange(g_) % 2 == 0)[:, None, None, None]
    c2 = jnp.where(even, jnp.concatenate([c2, z], axis=3), jnp.concatenate([z, c2], axis=3))
    col = jnp.arange(SSM_VEC)
    erep = (col[None, :] // c_ == jnp.arange(SSM_CHUNK)[:, None]).astype(F32)
    etile = (col[None, :] % c_ == jnp.arange(c_)[:, None]).astype(F32)
    lead = lambda shape: pl.BlockSpec(shape, lambda p: (p,) + (0,) * (len(shape) - 1))
    m, wre, wim, vre, vim, consts, apw = pl.pallas_call(
        functools.partial(_ssm_tables_body, n_steps=n_steps),
        grid=(np_,),
        in_specs=[lead((1, 3, lanes)), lead((1, lanes, 3)), lead((1, 2, lanes, c_)),
                  lead((2, 2, c_, lanes)), _const_spec((SSM_CHUNK, SSM_VEC)), _const_spec((c_, SSM_VEC))],
        out_specs=[lead((2, SSM_VEC, SSM_VEC)), lead((1, lanes, 2 * SSM_VEC)), lead((1, lanes, 2 * SSM_VEC)),
                   lead((2, SSM_VEC, lanes)), lead((2, SSM_VEC, lanes)),
                   lead((1, 2, 2, lanes)), lead((1, 2, n_steps, lanes))],
        out_shape=[jax.ShapeDtypeStruct((g_, SSM_VEC, SSM_VEC), BF16),
                   jax.ShapeDtypeStruct((np_, lanes, 2 * SSM_VEC), BF16),
                   jax.ShapeDtypeStruct((np_, lanes, 2 * SSM_VEC), BF16),
                   jax.ShapeDtypeStruct((g_, SSM_VEC, lanes), BF16),
                   jax.ShapeDtypeStruct((g_, SSM_VEC, lanes), BF16),
                   jax.ShapeDtypeStruct((np_, 2, 2, lanes), F32),
                   jax.ShapeDtypeStruct((np_, 2, n_steps, lanes), F32)],
        compiler_params=_cparams(("parallel",)),
        name="s5_operators",
    )(lam_row, lam_col, b2, c2, erep, etile)
    dcol = jnp.tile(d_skip.astype(F32).reshape(g_, 1, c_), (1, SSM_CHUNK, 1)).reshape(g_, SSM_VEC, 1)
    return m, wre, wim, vre, vim, consts, apw, dcol


def _ssm_body(u_ref, eye_ref, m_ref, wre_ref, wim_ref, vre_ref, vim_ref, c_ref, apw_ref, d_ref,
              y_ref, ut, yt, pre, pim, sre, sim, *, n_steps, halves, pitch):
    lanes = 2 * SSM_STATE
    nk = u_ref.shape[1]
    gd = SSM_GROUP_DIM
    for t in range(SSM_CHUNK):
        xt = lax.dot_general(eye_ref[...], u_ref[t], NT_DIMS,
                             preferred_element_type=F32).astype(BF16)
        for g in range(SSM_LANE_GROUPS):
            ut[g, t * gd:(t + 1) * gd, :] = xt[g * gd:(g + 1) * gd, :]

    row = lax.broadcasted_iota(jnp.int32, (SSM_SEQS, lanes), 0)
    seq_start = (row % halves) == 0
    shift = lambda x: jnp.where(seq_start, 0.0, pltpu.roll(x, 1, axis=0))

    npairs = SSM_LANE_GROUPS // 2
    for pp in range(npairs):
        ucat = jnp.concatenate([ut[2 * pp], ut[2 * pp + 1]], axis=0)
        p_re = jnp.dot(wre_ref[pp], ucat, preferred_element_type=F32).T
        p_im = jnp.dot(wim_ref[pp], ucat, preferred_element_type=F32).T
        if pitch == n_steps:
            pre[pp] = p_re
            pim[pp] = p_im
        else:
            for s in range(SSM_SEQS):
                pre[pp, s * pitch:s * pitch + n_steps, :] = p_re[s * n_steps:(s + 1) * n_steps]
                pim[pp, s * pitch:s * pitch + n_steps, :] = p_im[s * n_steps:(s + 1) * n_steps]

    a16 = [(jnp.broadcast_to(c_ref[pp, 0, 0:1, :], (SSM_SEQS, lanes)),
            jnp.broadcast_to(c_ref[pp, 1, 0:1, :], (SSM_SEQS, lanes))) for pp in range(npairs)]

    def step(i, carry):
        idx = pl.ds(i, SSM_SEQS, stride=pitch)
        out = []
        for pp in range(npairs):
            s_re, s_im = carry[pp]
            ar, ai = a16[pp]
            sre[pp, idx, :] = s_re
            sim[pp, idx, :] = s_im
            out.append((ar * s_re - ai * s_im + pre[pp, idx, :],
                        ar * s_im + ai * s_re + pim[pp, idx, :]))
        return tuple(out)

    zero = jnp.zeros((SSM_SEQS, lanes), F32)
    ends = lax.fori_loop(0, n_steps, step, ((zero, zero),) * npairs, unroll=2)

    for pp in range(npairs):
        e_re, e_im = ends[pp]
        br = jnp.broadcast_to(c_ref[pp, 0, 1:2, :], (SSM_SEQS, lanes))
        bi = jnp.broadcast_to(c_ref[pp, 1, 1:2, :], (SSM_SEQS, lanes))
        c_re = jnp.zeros_like(e_re)
        c_im = jnp.zeros_like(e_im)
        for _ in range(halves - 1):
            n_re = e_re + br * c_re - bi * c_im
            n_im = e_im + br * c_im + bi * c_re
            c_re, c_im = shift(n_re), shift(n_im)

        apr = apw_ref[pp, 0][None, :, :]
        api = apw_ref[pp, 1][None, :, :]
        if pitch == n_steps:
            s3r = sre[pp].reshape(SSM_SEQS, n_steps, lanes)
            s3i = sim[pp].reshape(SSM_SEQS, n_steps, lanes)
        else:
            s3r = jnp.stack([sre[pp, s * pitch:s * pitch + n_steps, :] for s in range(SSM_SEQS)])
            s3i = jnp.stack([sim[pp, s * pitch:s * pitch + n_steps, :] for s in range(SSM_SEQS)])
        t_re = (s3r + apr * c_re[:, None, :] - api * c_im[:, None, :]).reshape(nk, lanes)
        t_im = (s3i + apr * c_im[:, None, :] + api * c_re[:, None, :]).reshape(nk, lanes)
        tr_hi = t_re.astype(BF16)
        tr_lo = (t_re - tr_hi.astype(F32)).astype(BF16)
        ti_hi = t_im.astype(BF16)
        ti_lo = (t_im - ti_hi.astype(F32)).astype(BF16)
        for g in (2 * pp, 2 * pp + 1):
            ug = ut[g]
            y = jnp.dot(m_ref[g], ug, preferred_element_type=F32)
            y += lax.dot_general(vre_ref[g], tr_hi, NT_DIMS, preferred_element_type=F32)
            y += lax.dot_general(vre_ref[g], tr_lo, NT_DIMS, preferred_element_type=F32)
            y += lax.dot_general(vim_ref[g], ti_hi, NT_DIMS, preferred_element_type=F32)
            y += lax.dot_general(vim_ref[g], ti_lo, NT_DIMS, preferred_element_type=F32)
            yt[g] = y + d_ref[g] * ug.astype(F32)

    for t in range(SSM_CHUNK):
        rows = jnp.concatenate([yt[g, t * gd:(t + 1) * gd, :] for g in range(SSM_LANE_GROUPS)], axis=0)
        y_ref[t] = rows.T


def _ssm(u3, tables, eye, bsz, seq):
    m, wre, wim, vre, vim, consts, apw, dcol = tables
    nk = u3.shape[1]
    halves = SSM_SEQS // bsz
    n_steps = seq // (SSM_CHUNK * halves)
    lanes = 2 * SSM_STATE
    lg = SSM_LANE_GROUPS
    pitch = n_steps + 8 if (n_steps // 8) % 2 == 0 else n_steps
    lead = lambda shape: pl.BlockSpec(shape, lambda b: (b,) + (0,) * (len(shape) - 1))
    return pl.pallas_call(
        functools.partial(_ssm_body, n_steps=n_steps, halves=halves, pitch=pitch),
        grid=(SSM_GROUPS // lg,),
        in_specs=[pl.BlockSpec((SSM_CHUNK, nk, LANES), lambda b: (0, 0, b)),
                  _const_spec((LANES, LANES)),
                  lead((lg, SSM_VEC, SSM_VEC)),
                  lead((lg // 2, lanes, 2 * SSM_VEC)), lead((lg // 2, lanes, 2 * SSM_VEC)),
                  lead((lg, SSM_VEC, lanes)), lead((lg, SSM_VEC, lanes)),
                  lead((lg // 2, 2, 2, lanes)), lead((lg // 2, 2, n_steps, lanes)),
                  lead((lg, SSM_VEC, 1))],
        out_specs=pl.BlockSpec((SSM_CHUNK, nk, LANES), lambda b: (0, 0, b)),
        out_shape=jax.ShapeDtypeStruct((SSM_CHUNK, nk, SSM_WIDTH), F32),
        scratch_shapes=[pltpu.VMEM((lg, SSM_VEC, nk), BF16), pltpu.VMEM((lg, SSM_VEC, nk), F32)]
                       + [pltpu.VMEM((lg // 2, SSM_SEQS * pitch, lanes), F32)] * 4,
        compiler_params=_cparams(("parallel",)),
        name="s5_ssm",
    )(u3, eye, m, wre, wim, vre, vim, consts, apw, dcol)


def _mix_math(x_ref, a_ref, y3_ref, wglu_ref, sg_ref, wo_ref, fg_ref, yscr):
    nlb = SSM_WIDTH // LANES
    for t in range(SSM_CHUNK):
        for lb in range(nlb):
            yscr[lb, pl.ds(t, ROW_TILE // SSM_CHUNK, stride=SSM_CHUNK), :] = (
                y3_ref[t, :, lb * LANES:(lb + 1) * LANES])
    y = jax.nn.gelu(jnp.concatenate([yscr[lb] for lb in range(nlb)], axis=1))
    z = y * jax.nn.sigmoid(jnp.dot(y.astype(BF16), wglu_ref[...], preferred_element_type=F32))
    ms = jnp.mean(z * z, axis=-1, keepdims=True)
    sn = (z * lax.rsqrt(ms + NORM_EPS) * sg_ref[...]).astype(BF16)
    x1 = (x_ref[...]
          + jnp.dot(a_ref[...], wo_ref[:ATTN_WIDTH, :], preferred_element_type=F32)
          + jnp.dot(sn, wo_ref[ATTN_WIDTH:, :], preferred_element_type=F32))
    ms1 = jnp.mean(x1 * x1, axis=-1, keepdims=True)
    return x1, x1 * lax.rsqrt(ms1 + NORM_EPS) * fg_ref[...]


def _mix_router_body(x_ref, a_ref, y3_ref, wglu_ref, sg_ref, wo_ref, fg_ref, rwt_ref,
                     x1_out, lg_out, yscr):
    x1, hn = _mix_math(x_ref, a_ref, y3_ref, wglu_ref, sg_ref, wo_ref, fg_ref, yscr)
    x1_out[...] = x1
    lane = lax.broadcasted_iota(jnp.int32, lg_out.shape, 1)
    lg = jnp.zeros(lg_out.shape, F32)
    for e in range(lg_out.shape[1]):
        lg = jnp.where(lane == e, jnp.sum(hn * rwt_ref[e:e + 1, :], axis=-1, keepdims=True), lg)
    lg_out[...] = lg


def _mix_ffn_body(x_ref, a_ref, y3_ref, wglu_ref, sg_ref, wo_ref, fg_ref, wg_ref, wu_ref, wd_ref,
                  o_ref, yscr):
    x1, hn = _mix_math(x_ref, a_ref, y3_ref, wglu_ref, sg_ref, wo_ref, fg_ref, yscr)
    h = hn.astype(BF16)
    o_ref[...] = x1
    for c in range(D_FF // FF_CHUNK):
        sl = slice(c * FF_CHUNK, (c + 1) * FF_CHUNK)
        g = jnp.dot(h, wg_ref[:, sl], preferred_element_type=F32)
        u = jnp.dot(h, wu_ref[:, sl], preferred_element_type=F32)
        a = (jax.nn.silu(g) * u).astype(BF16)
        o_ref[...] += jnp.dot(a, wd_ref[sl, :], preferred_element_type=F32)


def _mix_specs():
    row = lambda w: pl.BlockSpec((ROW_TILE, w), lambda i: (i, 0))
    in_specs = [row(D_MODEL), row(ATTN_WIDTH),
                pl.BlockSpec((SSM_CHUNK, ROW_TILE // SSM_CHUNK, SSM_WIDTH), lambda i: (0, i, 0)),
                _const_spec((SSM_WIDTH, SSM_WIDTH)), _const_spec((1, SSM_WIDTH)),
                _const_spec((D_MODEL, D_MODEL)), _const_spec((1, D_MODEL))]
    return row, in_specs, pltpu.VMEM((SSM_WIDTH // LANES, ROW_TILE, LANES), F32)


def _mix_router(x2d, attn, y3, wglu, sg, wo, fg, rwt):
    t = x2d.shape[0]
    ne = rwt.shape[0]
    row, in_specs, yscr = _mix_specs()
    return pl.pallas_call(
        _mix_router_body,
        grid=(t // ROW_TILE,),
        in_specs=in_specs + [_const_spec((ne, D_MODEL))],
        out_specs=[row(D_MODEL), row(ne)],
        out_shape=[jax.ShapeDtypeStruct((t, D_MODEL), F32), jax.ShapeDtypeStruct((t, ne), F32)],
        scratch_shapes=[yscr],
        compiler_params=_cparams(("parallel",)),
        name="mix_router",
    )(x2d, attn, y3, wglu, sg, wo, fg, rwt)


def _mix_ffn(x2d, attn, y3, wglu, sg, wo, fg, wg, wu, wd, casts=()):
    t = x2d.shape[0]
    row, in_specs, yscr = _mix_specs()
    return _row_tiled_call(
        _mix_ffn_body, t // ROW_TILE,
        in_specs=in_specs + [_resident_spec((D_MODEL, D_FF)), _resident_spec((D_MODEL, D_FF)),
                             _resident_spec((D_FF, D_MODEL))],
        out_specs=[row(D_MODEL)],
        out_shape=[jax.ShapeDtypeStruct((t, D_MODEL), F32)],
        scratch_shapes=[yscr],
        name="mix_dense_ffn", args=(x2d, attn, y3, wglu, sg, wo, fg, wg, wu, wd), casts=casts)


def _expert_ffn_body(e_ref, nused_ref, nvalid_ref, tok_ref, dst_ref, x1_hbm, fg_ref,
                     wg_ref, wu_ref, wd_ref, out_hbm, xbuf, acc, ybuf, gsem, ssem):
    b = pl.program_id(0)
    n_used = nused_ref[0]
    n_ff = D_FF // FF_CHUNK
    rows_per_chunk = -(-MOE_ROWS // n_ff)

    def gather_row(blk, r):
        i = tok_ref[blk * MOE_ROWS + r]
        pltpu.make_async_copy(x1_hbm.at[pl.ds(i, 1)], xbuf.at[pl.ds(r, 1)], gsem).start()

    def scatter_row(blk, r):
        i = dst_ref[blk * MOE_ROWS + r]
        pltpu.make_async_copy(ybuf.at[pl.ds(r, 1)], out_hbm.at[pl.ds(i, 1)], ssem).start()

    def scatter_rows(blk, n):
        def body(r, carry):
            scatter_row(blk, r)
            return carry
        lax.fori_loop(0, n, body, 0)

    def wait_gather():
        pltpu.make_async_copy(x1_hbm.at[pl.ds(0, MOE_ROWS)], xbuf, gsem).wait()

    def wait_scatter(n):
        n8 = pl.multiple_of(n & -8, 8)

        @pl.when(n8 > 0)
        def _():
            pltpu.make_async_copy(ybuf.at[pl.ds(0, n8)], out_hbm.at[pl.ds(0, n8)], ssem).wait()

        def one(_, carry):
            pltpu.make_async_copy(ybuf.at[pl.ds(0, 1)], out_hbm.at[pl.ds(0, 1)], ssem).wait()
            return carry
        lax.fori_loop(0, n - n8, one, 0)

    def ffn(scatter_prev):
        wait_gather()
        x = xbuf[...]
        ms = jnp.mean(x * x, axis=-1, keepdims=True)
        h = (x * lax.rsqrt(ms + NORM_EPS) * fg_ref[...]).astype(BF16)
        for c in range(n_ff):
            for r in range(c * rows_per_chunk, min((c + 1) * rows_per_chunk, MOE_ROWS)):
                gather_row(b + 1, r)
                if scatter_prev:
                    scatter_row(b - 1, r)
            sl = slice(c * FF_CHUNK, (c + 1) * FF_CHUNK)
            g = jnp.dot(h, wg_ref[0, :, sl], preferred_element_type=F32)
            u = jnp.dot(h, wu_ref[0, :, sl], preferred_element_type=F32)
            a = (jax.nn.silu(g) * u).astype(BF16)
            part = jnp.dot(a, wd_ref[0, sl, :], preferred_element_type=F32)
            if c == 0:
                acc[...] = part
            else:
                acc[...] += part

    @pl.when(b < n_used)
    def _():
        @pl.when(b == 0)
        def _():
            def body(r, carry):
                gather_row(0, r)
                return carry
            lax.fori_loop(0, MOE_ROWS, body, 0, unroll=8)

        n_prev = jnp.where(b >= 1, nvalid_ref[jnp.maximum(b - 1, 0)], 0)

        @pl.when(n_prev == MOE_ROWS)
        def _():
            ffn(scatter_prev=True)

        @pl.when(n_prev != MOE_ROWS)
        def _():
            scatter_rows(b - 1, n_prev)
            ffn(scatter_prev=False)

        wait_scatter(n_prev)
        ybuf[...] = acc[...]

        @pl.when(b == n_used - 1)
        def _():
            scatter_rows(b, nvalid_ref[b])
            wait_scatter(nvalid_ref[b])
            wait_gather()


def _expert_ffn(x1, fg, tok, dst, block_e, n_used, n_valid, wg, wu, wd, nblk, out_rows):
    wspec = lambda shape: pl.BlockSpec(shape, lambda b, e, *_: (e[b], 0, 0))
    return pl.pallas_call(
        _expert_ffn_body,
        grid_spec=pltpu.PrefetchScalarGridSpec(
            num_scalar_prefetch=5, grid=(nblk,),
            in_specs=[pl.BlockSpec(memory_space=pl.ANY),
                      pl.BlockSpec((1, D_MODEL), lambda b, *_: (0, 0)),
                      wspec((1, D_MODEL, D_FF)), wspec((1, D_MODEL, D_FF)), wspec((1, D_FF, D_MODEL))],
            out_specs=pl.BlockSpec(memory_space=pl.ANY),
            scratch_shapes=[pltpu.VMEM((MOE_ROWS, D_MODEL), F32)] * 3
                           + [pltpu.SemaphoreType.DMA(()), pltpu.SemaphoreType.DMA(())]),
        out_shape=jax.ShapeDtypeStruct((out_rows, D_MODEL), F32),
        compiler_params=pltpu.CompilerParams(dimension_semantics=("arbitrary",),
                                             vmem_limit_bytes=MOE_VMEM_LIMIT),
        name="moe_expert_ffn",
    )(block_e, n_used, n_valid, tok, dst, x1, fg, wg, wu, wd)


def _combine_body(x_ref, y0_ref, y1_ref, g_ref, o_ref):
    o_ref[...] = x_ref[...] + y0_ref[...] * g_ref[:, 0:1] + y1_ref[...] * g_ref[:, 1:2]


def _combine(x1, y2, gates):
    t = x1.shape[0]
    nt = t // ROW_TILE
    return pl.pallas_call(
        _combine_body,
        grid=(nt,),
        in_specs=[pl.BlockSpec((ROW_TILE, D_MODEL), lambda i: (i, 0)),
                  pl.BlockSpec((ROW_TILE, D_MODEL), lambda i: (i, 0)),
                  pl.BlockSpec((ROW_TILE, D_MODEL), lambda i: (i + nt, 0)),
                  pl.BlockSpec((ROW_TILE, TOP_K), lambda i: (i, 0))],
        out_specs=pl.BlockSpec((ROW_TILE, D_MODEL), lambda i: (i, 0)),
        out_shape=jax.ShapeDtypeStruct((t, D_MODEL), F32),
        compiler_params=_cparams(("parallel",)),
        name="moe_combine",
    )(x1, y2, y2, gates)


def _moe(x1, fg, logits, wg, wu, wd):
    t = x1.shape[0]
    n_assign = t * TOP_K
    nblk = n_assign // MOE_ROWS + N_EXPERTS
    cap = nblk * MOE_ROWS
    top_v, top_i = lax.top_k(logits, TOP_K)
    gates = jax.nn.softmax(top_v, axis=-1)
    flat_e = top_i.reshape(-1).astype(jnp.int32)
    order = jnp.argsort(flat_e, stable=True).astype(jnp.int32)
    experts = jnp.arange(N_EXPERTS, dtype=jnp.int32)
    counts = jnp.sum((flat_e[:, None] == experts[None, :]).astype(jnp.int32), axis=0)
    start = jnp.cumsum(counts) - counts
    padded = ((counts + MOE_ROWS - 1) // MOE_ROWS) * MOE_ROWS
    pend = jnp.cumsum(padded)
    pstart = pend - padded
    p = jnp.arange(cap, dtype=jnp.int32)
    e_p = jnp.minimum(jnp.sum((p[:, None] >= pend[None, :]).astype(jnp.int32), axis=1), N_EXPERTS - 1)
    rank = p - pstart[e_p]
    valid = jnp.logical_and(rank < counts[e_p], p < pend[-1])
    a_p = order[jnp.clip(start[e_p] + rank, 0, n_assign - 1)]
    tok = jnp.where(valid, a_p // TOP_K, 0).astype(jnp.int32)
    dst = jnp.where(valid, (a_p % TOP_K) * t + a_p // TOP_K, 0).astype(jnp.int32)
    block_e = e_p[::MOE_ROWS]
    n_used = (pend[-1:] // MOE_ROWS).astype(jnp.int32)
    n_valid = jnp.sum(valid.reshape(nblk, MOE_ROWS).astype(jnp.int32), axis=1)
    y2 = _expert_ffn(x1, fg, tok, dst, block_e, n_used, n_valid, wg, wu, wd, nblk, n_assign)
    return _combine(x1, y2, gates)


def kernel(x, attn_norm_g, w_in, q_norm_g, k_norm_g, sinks, lam_re, lam_im, log_dt, b_re, b_im,
           c_re, c_im, d_skip, w_glu, attn_out_g, ssm_out_g, w_o, ffn_norm_g, dense_wg, dense_wu,
           dense_wd, router_w, moe_wg, moe_wu, moe_wd):
    bsz, seq, _ = x.shape
    depth = w_in.shape[0]
    assert SSM_SEQS % bsz == 0 and seq % (SSM_CHUNK * (SSM_SEQS // bsz)) == 0
    assert seq % ROW_TILE == 0 and ROW_TILE % ATTN_BLOCK == 0
    x2d = x.reshape(bsz * seq, D_MODEL).astype(F32)
    head = jnp.arange(ATTN_WIDTH) // HEAD_DIM
    avg = jnp.where(head[:, None] == head[None, :], 1.0 / HEAD_DIM, 0.0).astype(BF16)
    eye = jnp.eye(ROW_TILE, dtype=BF16)
    n_steps = seq // (SSM_CHUNK * (SSM_SEQS // bsz))
    row = lambda v: v.astype(F32).reshape(1, -1)
    col = lambda v: v.astype(F32).reshape(-1, 1)
    ko, vo, uo = ATTN_WIDTH, ATTN_WIDTH + KV_WIDTH, ATTN_WIDTH + 2 * KV_WIDTH
    flat = lambda w: w.astype(F32).reshape(-1, w.shape[-1])
    moe_wg_bf = None
    for l in range(depth):
        i = l // 2
        dense = l % 2 == 0
        qg = jnp.tile(q_norm_g[l].astype(F32), N_HEADS) * (HEAD_DIM ** -0.5)
        kg = jnp.tile(k_norm_g[l].astype(F32), N_KV_HEADS)
        w = w_in[l].astype(BF16)
        early = [flat(dense_wg[i]), flat(dense_wu[i])] if dense else [flat(moe_wu[i])]
        late = [flat(dense_wd[i])] if dense else [flat(moe_wd[i])]
        wqvt = jnp.concatenate([w[:, :ko], w[:, vo:uo]], axis=1).T
        wku = jnp.concatenate([w[:, ko:vo], w[:, uo:]], axis=1)
        (qt, k, v3, u3), early_bf = _inproj(x2d, row(attn_norm_g[l]), wqvt, wku, col(qg), row(kg), avg,
                                            casts=early)
        (attn,), late_bf = _attention(qt, k, v3, sinks[l].astype(F32), col(attn_out_g[l]), eye, seq,
                                      casts=late)
        tables = _ssm_tables(lam_re[l], lam_im[l], log_dt[l], b_re[l], b_im[l], c_re[l], c_im[l],
                             d_skip[l], n_steps)
        y3 = _ssm(u3, tables, eye[:LANES, :LANES], bsz, seq)
        mix_args = (x2d, attn, y3, w_glu[l].astype(BF16), row(ssm_out_g[l]), w_o[l].astype(BF16),
                    row(ffn_norm_g[l]))
        if dense:
            ahead = [flat(moe_wg[i])] if l + 1 < depth else []
            (x2d,), ahead_bf = _mix_ffn(*mix_args, early_bf[0], early_bf[1], late_bf[0], casts=ahead)
            if ahead:
                moe_wg_bf = ahead_bf[0].reshape(moe_wg[i].shape)
        else:
            x1, logits = _mix_router(*mix_args, router_w[i].astype(F32).T)
            x2d = _moe(x1, row(ffn_norm_g[l]), logits, moe_wg_bf,
                       early_bf[0].reshape(moe_wu[i].shape), late_bf[0].reshape(moe_wd[i].shape))
    return x2d.reshape(bsz, seq, D_MODEL)
```

```python
import functools
import math

import jax
import jax.numpy as jnp
from jax import lax
from jax.experimental import pallas as pl
from jax.experimental.pallas import tpu as pltpu

F32 = jnp.float32
BF16 = jnp.bfloat16

D_MODEL = 1024
HEAD_DIM = 64
N_HEADS = 8
N_KV_HEADS = 2
GQA = N_HEADS // N_KV_HEADS
ATTN_WIDTH = N_HEADS * HEAD_DIM
KV_WIDTH = N_KV_HEADS * HEAD_DIM
ATTN_BLOCK = 128
SSM_WIDTH = D_MODEL - ATTN_WIDTH
SSM_GROUP_DIM = 16
SSM_GROUPS = SSM_WIDTH // SSM_GROUP_DIM
SSM_STATE = 64
IN_WIDTH = ATTN_WIDTH + 2 * KV_WIDTH + SSM_WIDTH
D_FF = 3584
N_EXPERTS = 8
TOP_K = 2
NORM_EPS = 1e-6
NEG_INF = -1e30

LANES = 128
ROW_TILE = 512
SSM_CHUNK = 16
SSM_SEQS = 8
SSM_VEC = SSM_CHUNK * SSM_GROUP_DIM
SSM_LANE_GROUPS = LANES // SSM_GROUP_DIM
FF_CHUNK = 512
MOE_ROWS = 512
VMEM_LIMIT = 56 * 1024 * 1024
MOE_VMEM_LIMIT = 60 * 1024 * 1024

NT_DIMS = (((1,), (1,)), ((), ()))


def _cparams(sem, vmem=VMEM_LIMIT):
    return pltpu.CompilerParams(dimension_semantics=sem, vmem_limit_bytes=vmem)


def _const_spec(shape):
    n = len(shape)
    return pl.BlockSpec(shape, lambda *_: (0,) * n)


def _resident_spec(shape):
    n = len(shape)
    return pl.BlockSpec(shape, lambda *_: (0,) * n, pipeline_mode=pl.Buffered(1))


def _row_tiled_call(body, steps, in_specs, out_specs, out_shape, scratch_shapes, name, args, casts=()):
    n_in, n_out, n_c = len(in_specs), len(out_specs), len(casts)

    def hosted(*refs):
        ins, cast_in = refs[:n_in], refs[n_in:n_in + n_c]
        outs = refs[n_in + n_c:n_in + n_c + n_out]
        cast_out = refs[n_in + n_c + n_out:n_in + 2 * n_c + n_out]
        body(*ins, *outs, *refs[n_in + 2 * n_c + n_out:])
        for src, dst in zip(cast_in, cast_out):
            dst[...] = src[...].astype(BF16)

    for w in casts:
        assert w.ndim == 2 and w.shape[0] % (steps * 16) == 0, w.shape
    cast_specs = [pl.BlockSpec((w.shape[0] // steps, w.shape[1]), lambda i: (i, 0)) for w in casts]
    res = pl.pallas_call(
        hosted,
        grid=(steps,),
        in_specs=list(in_specs) + cast_specs,
        out_specs=list(out_specs) + cast_specs,
        out_shape=list(out_shape) + [jax.ShapeDtypeStruct(w.shape, BF16) for w in casts],
        scratch_shapes=scratch_shapes,
        compiler_params=_cparams(("parallel",)),
        name=name,
    )(*args, *casts)
    return res[:n_out], res[n_out:]


def _inproj_body(x_ref, g_ref, wqvt_ref, wku_ref, qg_ref, kg_ref, avg_ref,
                 qt_out, k_out, v3_out, u3_out, uscr):
    x = x_ref[...]
    ms = jnp.mean(x * x, axis=-1, keepdims=True)
    hn = (x * lax.rsqrt(ms + NORM_EPS) * g_ref[...]).astype(BF16)
    qvt = lax.dot_general(wqvt_ref[...], hn, NT_DIMS, preferred_element_type=F32)
    ku = jnp.dot(hn, wku_ref[...], preferred_element_type=F32)
    qt = qvt[:ATTN_WIDTH]
    qms = jnp.dot(avg_ref[...], (qt * qt).astype(BF16), preferred_element_type=F32)
    qt_out[...] = (qt * lax.rsqrt(qms + NORM_EPS) * qg_ref[...]).astype(BF16)
    k = ku[:, :KV_WIDTH]
    kms = jnp.dot((k * k).astype(BF16), avg_ref[:KV_WIDTH, :KV_WIDTH], preferred_element_type=F32)
    k_out[...] = (k * lax.rsqrt(kms + NORM_EPS) * kg_ref[...]).astype(BF16)
    vt = qvt[ATTN_WIDTH:]
    for b in range(ROW_TILE // ATTN_BLOCK):
        v3_out[b] = vt[:, b * ATTN_BLOCK:(b + 1) * ATTN_BLOCK].astype(BF16)
    u = ku[:, KV_WIDTH:]
    for lb in range(SSM_WIDTH // LANES):
        uscr[lb] = u[:, lb * LANES:(lb + 1) * LANES]
    for t in range(SSM_CHUNK):
        for lb in range(SSM_WIDTH // LANES):
            u3_out[t, :, lb * LANES:(lb + 1) * LANES] = (
                uscr[lb, pl.ds(t, ROW_TILE // SSM_CHUNK, stride=SSM_CHUNK), :].astype(BF16))


def _inproj(x2d, g, wqvt, wku, qg, kg, avg, casts=()):
    t = x2d.shape[0]
    cpt = ROW_TILE // SSM_CHUNK
    return _row_tiled_call(
        _inproj_body, t // ROW_TILE,
        in_specs=[pl.BlockSpec((ROW_TILE, D_MODEL), lambda i: (i, 0)),
                  _const_spec((1, D_MODEL)), _const_spec((ATTN_WIDTH + KV_WIDTH, D_MODEL)),
                  _const_spec((D_MODEL, KV_WIDTH + SSM_WIDTH)), _const_spec((ATTN_WIDTH, 1)),
                  _const_spec((1, KV_WIDTH)), _const_spec((ATTN_WIDTH, ATTN_WIDTH))],
        out_specs=[pl.BlockSpec((ATTN_WIDTH, ROW_TILE), lambda i: (0, i)),
                   pl.BlockSpec((ROW_TILE, KV_WIDTH), lambda i: (i, 0)),
                   pl.BlockSpec((ROW_TILE // ATTN_BLOCK, KV_WIDTH, ATTN_BLOCK), lambda i: (i, 0, 0)),
                   pl.BlockSpec((SSM_CHUNK, cpt, SSM_WIDTH), lambda i: (0, i, 0))],
        out_shape=[jax.ShapeDtypeStruct((ATTN_WIDTH, t), BF16),
                   jax.ShapeDtypeStruct((t, KV_WIDTH), BF16),
                   jax.ShapeDtypeStruct((t // ATTN_BLOCK, KV_WIDTH, ATTN_BLOCK), BF16),
                   jax.ShapeDtypeStruct((SSM_CHUNK, t // SSM_CHUNK, SSM_WIDTH), BF16)],
        scratch_shapes=[pltpu.VMEM((SSM_WIDTH // LANES, ROW_TILE, LANES), F32)],
        name="inproj", args=(x2d, g, wqvt, wku, qg, kg, avg), casts=casts)


def _attn_body(sink_ref, qt_ref, k_ref, v3_ref, g_ref, eye_ref, o_ref, at_scr, *, blocks_per_seq):
    i = pl.program_id(0)
    key = lax.broadcasted_iota(jnp.int32, (ATTN_BLOCK, ATTN_BLOCK), 0)
    qry = lax.broadcasted_iota(jnp.int32, (ATTN_BLOCK, ATTN_BLOCK), 1)
    cur_ok = key <= qry
    zpad = jnp.zeros((HEAD_DIM, ATTN_BLOCK), BF16)
    nblk = ROW_TILE // ATTN_BLOCK
    for blk in range(nblk):
        gblk = i * nblk + blk
        pblk = jnp.maximum(gblk - 1, 0)
        row0 = pl.multiple_of(gblk * ATTN_BLOCK, ATTN_BLOCK)
        prev0 = pl.multiple_of(pblk * ATTN_BLOCK, ATTN_BLOCK)
        has_prev = (gblk % blocks_per_seq) != 0
        kc = k_ref[pl.ds(row0, ATTN_BLOCK), :]
        kp = k_ref[pl.ds(prev0, ATTN_BLOCK), :]
        vc = v3_ref[gblk]
        vp = v3_ref[pblk]
        qb = qt_ref[:, blk * ATTN_BLOCK:(blk + 1) * ATTN_BLOCK]
        keep = jnp.logical_or(cur_ok, has_prev)
        outs = []
        for h in range(N_HEADS):
            kv = h // GQA
            qh = qb[h * HEAD_DIM:(h + 1) * HEAD_DIM, :]
            qpad = jnp.concatenate([qh, zpad] if kv == 0 else [zpad, qh], axis=0)
            sc = jnp.dot(kc, qpad, preferred_element_type=F32)
            sp = jnp.dot(kp, qpad, preferred_element_type=F32)
            s = jnp.where(keep, jnp.where(cur_ok, sc, sp), NEG_INF)
            sink = sink_ref[h]
            m = jnp.maximum(jnp.max(s, axis=0, keepdims=True), sink)
            p = jnp.exp(s - m)
            den = jnp.sum(p, axis=0, keepdims=True) + jnp.exp(sink - m)
            pb = p.astype(BF16)
            zero = jnp.zeros_like(pb)
            vs = slice(kv * HEAD_DIM, (kv + 1) * HEAD_DIM)
            o = (jnp.dot(vc[vs, :], jnp.where(cur_ok, pb, zero), preferred_element_type=F32)
                 + jnp.dot(vp[vs, :], jnp.where(cur_ok, zero, pb), preferred_element_type=F32))
            outs.append(o / den)
        a = jnp.concatenate(outs, axis=0)
        ms = jnp.mean(a * a, axis=0, keepdims=True)
        at_scr[:, blk * ATTN_BLOCK:(blk + 1) * ATTN_BLOCK] = (
            a * lax.rsqrt(ms + NORM_EPS) * g_ref[...]).astype(BF16)
    o_ref[...] = lax.dot_general(eye_ref[...], at_scr[...], NT_DIMS,
                                 preferred_element_type=F32).astype(BF16)


def _attention(qt, k, v3, sinks, gcol, eye, seq, casts=()):
    t = k.shape[0]
    return _row_tiled_call(
        functools.partial(_attn_body, blocks_per_seq=seq // ATTN_BLOCK), t // ROW_TILE,
        in_specs=[pl.BlockSpec(memory_space=pltpu.SMEM),
                  pl.BlockSpec((ATTN_WIDTH, ROW_TILE), lambda i: (0, i)),
                  _const_spec((t, KV_WIDTH)), _const_spec((t // ATTN_BLOCK, KV_WIDTH, ATTN_BLOCK)),
                  _const_spec((ATTN_WIDTH, 1)), _const_spec((ROW_TILE, ROW_TILE))],
        out_specs=[pl.BlockSpec((ROW_TILE, ATTN_WIDTH), lambda i: (i, 0))],
        out_shape=[jax.ShapeDtypeStruct((t, ATTN_WIDTH), BF16)],
        scratch_shapes=[pltpu.VMEM((ATTN_WIDTH, ROW_TILE), BF16)],
        name="swa_attention", args=(sinks, qt, k, v3, gcol, eye), casts=casts)


def _ssm_tables_body(lam_ref, lamc_ref, b_ref, c_ref, erep_ref, etile_ref, eye_ref,
                     m_out, wre_out, wim_out, vre_out, vim_out, k_out, apw_out, *, n_steps):
    hi = lax.Precision.HIGHEST
    lanes = 2 * SSM_STATE
    ch, gd = SSM_CHUNK, SSM_GROUP_DIM
    lr, li, dt = lam_ref[0, 0:1, :], lam_ref[0, 1:2, :], lam_ref[0, 2:3, :]

    def apow(p):
        mag = jnp.exp(p * (lr * dt))
        ang = p * (li * dt)
        return mag * jnp.cos(ang), mag * jnp.sin(ang)

    tau = lax.broadcasted_iota(jnp.int32, (ch, lanes), 0).astype(F32)
    p0r, p0i = apow(tau)
    p1r, p1i = apow(tau + 1.0)
    base = 8
    doublings = []
    while base << len(doublings) < n_steps:
        doublings.append(ch * (base << len(doublings)))
    assert base << len(doublings) == n_steps and len(doublings) <= 6, n_steps
    sel = lax.broadcasted_iota(jnp.int32, (8, lanes), 0)
    pw = jnp.zeros((8, lanes), F32)
    for r, p in enumerate([ch, ch * n_steps] + doublings):
        pw = jnp.where(sel == r, float(p), pw)
    kr, ki = apow(pw)
    k_out[0, 0] = kr[0:2]
    k_out[0, 1] = ki[0:2]
    qr, qi = apow(sel.astype(F32) * float(ch))
    for m in range(len(doublings)):
        dr, di = kr[2 + m:3 + m], ki[2 + m:3 + m]
        qr, qi = (jnp.concatenate([qr, qr * dr - qi * di], axis=0),
                  jnp.concatenate([qi, qr * di + qi * dr], axis=0))
    apw_out[0, 0] = qr
    apw_out[0, 1] = qi

    to_col = lambda p: lax.dot_general(eye_ref[...], p, NT_DIMS, precision=hi,
                                       preferred_element_type=F32)
    c0r, c0i, c1r, c1i = to_col(p0r), to_col(p0i), to_col(p1r), to_col(p1i)
    lrc, lic = lamc_ref[0, :, 0:1], lamc_ref[0, :, 1:2]
    ab_re, ab_im = c1r[:, 0:1], c1i[:, 0:1]
    nr = ab_re - 1.0
    den = lrc * lrc + lic * lic
    f_re = (nr * lrc + ab_im * lic) / den
    f_im = (ab_im * lrc - nr * lic) / den
    br, bi = b_ref[0, 0], b_ref[0, 1]
    bb_re = f_re * br - f_im * bi
    bb_im = f_re * bi + f_im * br
    bx_re = jnp.dot(bb_re, etile_ref[...], precision=hi, preferred_element_type=F32)
    bx_im = jnp.dot(bb_im, etile_ref[...], precision=hi, preferred_element_type=F32)
    px_re = jnp.dot(c0r, erep_ref[...], precision=hi, preferred_element_type=F32)
    px_im = jnp.dot(c0i, erep_ref[...], precision=hi, preferred_element_type=F32)
    w_re = px_re * bx_re - px_im * bx_im
    w_im = px_re * bx_im + px_im * bx_re
    first = lax.broadcasted_iota(jnp.int32, (lanes, SSM_VEC), 0) < SSM_STATE
    pack = lambda w: jnp.concatenate([jnp.where(first, w, 0.0), jnp.where(first, 0.0, w)], axis=1)
    wre_out[0] = pack(w_re).astype(BF16)
    wim_out[0] = pack(w_im).astype(BF16)

    blk = lax.broadcasted_iota(jnp.int32, (SSM_VEC, SSM_VEC), 1) // gd
    for g in range(2):
        cr, ci = c_ref[g, 0], c_ref[g, 1]
        stack = lambda x: x.reshape(ch * gd, lanes)
        ca_re = stack(cr[None] * p0r[:, None, :] - ci[None] * p0i[:, None, :])
        ca_im = stack(cr[None] * p0i[:, None, :] + ci[None] * p0r[:, None, :])
        kt = (jnp.dot(ca_re, bx_re, precision=hi, preferred_element_type=F32)
              - jnp.dot(ca_im, bx_im, precision=hi, preferred_element_type=F32))
        m = jnp.where(blk == 0, kt, 0.0)
        for j in range(1, ch):
            shifted = jnp.concatenate([jnp.zeros((j * gd, SSM_VEC), F32), kt[:SSM_VEC - j * gd]], axis=0)
            m = jnp.where(blk == j, shifted, m)
        m_out[g] = m.astype(BF16)
        vre_out[g] = stack(cr[None] * p1r[:, None, :] - ci[None] * p1i[:, None, :]).astype(BF16)
        vim_out[g] = (-stack(cr[None] * p1i[:, None, :] + ci[None] * p1r[:, None, :])).astype(BF16)


def _ssm_tables(lam_re, lam_im, log_dt, b_re, b_im, c_re, c_im, d_skip, n_steps):
    g_, n_, c_ = SSM_GROUPS, SSM_STATE, SSM_GROUP_DIM
    np_ = g_ // 2
    lanes = 2 * n_
    dt = jnp.broadcast_to(jnp.exp(log_dt.astype(F32))[:, None], (g_, n_))
    lam = jnp.stack([lam_re.astype(F32), lam_im.astype(F32), dt], axis=1)
    lam_row = lam.reshape(np_, 2, 3, n_).transpose(0, 2, 1, 3).reshape(np_, 3, lanes)
    lam_col = lam_row.transpose(0, 2, 1)
    b2 = jnp.stack([b_re.astype(F32).reshape(np_, lanes, c_), b_im.astype(F32).reshape(np_, lanes, c_)],
                   axis=1)
    c2 = jnp.stack([c_re.astype(F32), c_im.astype(F32)], axis=1)
    z = jnp.zeros_like(c2)
    even = (jnp.arange(g_) % 2 == 0)[:, None, None, None]
    c2 = jnp.where(even, jnp.concatenate([c2, z], axis=3), jnp.concatenate([z, c2], axis=3))
    col = jnp.arange(SSM_VEC)
    erep = (SSM_CHUNK - 1 - col[None, :] // c_ == jnp.arange(SSM_CHUNK)[:, None]).astype(F32)
    etile = (col[None, :] % c_ == jnp.arange(c_)[:, None]).astype(F32)
    lead = lambda shape: pl.BlockSpec(shape, lambda p: (p,) + (0,) * (len(shape) - 1))
    m, wre, wim, vre, vim, consts, apw = pl.pallas_call(
        functools.partial(_ssm_tables_body, n_steps=n_steps),
        grid=(np_,),
        in_specs=[lead((1, 3, lanes)), lead((1, lanes, 3)), lead((1, 2, lanes, c_)),
                  lead((2, 2, c_, lanes)), _const_spec((SSM_CHUNK, SSM_VEC)), _const_spec((c_, SSM_VEC)),
                  _const_spec((lanes, lanes))],
        out_specs=[lead((2, SSM_VEC, SSM_VEC)), lead((1, lanes, 2 * SSM_VEC)), lead((1, lanes, 2 * SSM_VEC)),
                   lead((2, SSM_VEC, lanes)), lead((2, SSM_VEC, lanes)),
                   lead((1, 2, 2, lanes)), lead((1, 2, n_steps, lanes))],
        out_shape=[jax.ShapeDtypeStruct((g_, SSM_VEC, SSM_VEC), BF16),
                   jax.ShapeDtypeStruct((np_, lanes, 2 * SSM_VEC), BF16),
                   jax.ShapeDtypeStruct((np_, lanes, 2 * SSM_VEC), BF16),
                   jax.ShapeDtypeStruct((g_, SSM_VEC, lanes), BF16),
                   jax.ShapeDtypeStruct((g_, SSM_VEC, lanes), BF16),
                   jax.ShapeDtypeStruct((np_, 2, 2, lanes), F32),
                   jax.ShapeDtypeStruct((np_, 2, n_steps, lanes), F32)],
        compiler_params=_cparams(("parallel",)),
        name="s5_operators",
    )(lam_row, lam_col, b2, c2, erep, etile, jnp.eye(lanes, dtype=F32))
    dcol = jnp.tile(d_skip.astype(F32).reshape(g_, 1, c_), (1, SSM_CHUNK, 1)).reshape(g_, SSM_VEC, 1)
    return m, wre, wim, vre, vim, consts, apw, dcol


def _ssm_body(u_ref, eye_ref, m_ref, wre_ref, wim_ref, vre_ref, vim_ref, c_ref, apw_ref, d_ref,
              y_ref, ut, yt, pre, pim, sre, sim, *, n_steps, halves, pitch):
    lanes = 2 * SSM_STATE
    nk = u_ref.shape[1]
    gd = SSM_GROUP_DIM
    for t in range(SSM_CHUNK):
        xt = lax.dot_general(eye_ref[...], u_ref[t], NT_DIMS,
                             preferred_element_type=F32).astype(BF16)
        for g in range(SSM_LANE_GROUPS):
            ut[g, t * gd:(t + 1) * gd, :] = xt[g * gd:(g + 1) * gd, :]

    row = lax.broadcasted_iota(jnp.int32, (SSM_SEQS, lanes), 0)
    seq_start = (row % halves) == 0
    shift = lambda x: jnp.where(seq_start, 0.0, pltpu.roll(x, 1, axis=0))

    npairs = SSM_LANE_GROUPS // 2
    for pp in range(npairs):
        ucat = jnp.concatenate([ut[2 * pp], ut[2 * pp + 1]], axis=0)
        p_re = jnp.dot(wre_ref[pp], ucat, preferred_element_type=F32).T
        p_im = jnp.dot(wim_ref[pp], ucat, preferred_element_type=F32).T
        if pitch == n_steps:
            pre[pp] = p_re
            pim[pp] = p_im
        else:
            for s in range(SSM_SEQS):
                pre[pp, s * pitch:s * pitch + n_steps, :] = p_re[s * n_steps:(s + 1) * n_steps]
                pim[pp, s * pitch:s * pitch + n_steps, :] = p_im[s * n_steps:(s + 1) * n_steps]

    a16 = [(jnp.broadcast_to(c_ref[pp, 0, 0:1, :], (SSM_SEQS, lanes)),
            jnp.broadcast_to(c_ref[pp, 1, 0:1, :], (SSM_SEQS, lanes))) for pp in range(npairs)]

    def step(i, carry):
        idx = pl.ds(i, SSM_SEQS, stride=pitch)
        out = []
        for pp in range(npairs):
            s_re, s_im = carry[pp]
            ar, ai = a16[pp]
            sre[pp, idx, :] = s_re
            sim[pp, idx, :] = s_im
            out.append((ar * s_re - ai * s_im + pre[pp, idx, :],
                        ar * s_im + ai * s_re + pim[pp, idx, :]))
        return tuple(out)

    zero = jnp.zeros((SSM_SEQS, lanes), F32)
    ends = lax.fori_loop(0, n_steps, step, ((zero, zero),) * npairs, unroll=2)

    for pp in range(npairs):
        e_re, e_im = ends[pp]
        br = jnp.broadcast_to(c_ref[pp, 0, 1:2, :], (SSM_SEQS, lanes))
        bi = jnp.broadcast_to(c_ref[pp, 1, 1:2, :], (SSM_SEQS, lanes))
        c_re = jnp.zeros_like(e_re)
        c_im = jnp.zeros_like(e_im)
        for _ in range(halves - 1):
            n_re = e_re + br * c_re - bi * c_im
            n_im = e_im + br * c_im + bi * c_re
            c_re, c_im = shift(n_re), shift(n_im)

        apr = apw_ref[pp, 0][None, :, :]
        api = apw_ref[pp, 1][None, :, :]
        if pitch == n_steps:
            s3r = sre[pp].reshape(SSM_SEQS, n_steps, lanes)
            s3i = sim[pp].reshape(SSM_SEQS, n_steps, lanes)
        else:
            s3r = jnp.stack([sre[pp, s * pitch:s * pitch + n_steps, :] for s in range(SSM_SEQS)])
            s3i = jnp.stack([sim[pp, s * pitch:s * pitch + n_steps, :] for s in range(SSM_SEQS)])
        t_re = (s3r + apr * c_re[:, None, :] - api * c_im[:, None, :]).reshape(nk, lanes)
        t_im = (s3i + apr * c_im[:, None, :] + api * c_re[:, None, :]).reshape(nk, lanes)
        tr_hi = t_re.astype(BF16)
        tr_lo = (t_re - tr_hi.astype(F32)).astype(BF16)
        ti_hi = t_im.astype(BF16)
        ti_lo = (t_im - ti_hi.astype(F32)).astype(BF16)
        for g in (2 * pp, 2 * pp + 1):
            ug = ut[g]
            y = jnp.dot(m_ref[g], ug, preferred_element_type=F32)
            y += lax.dot_general(vre_ref[g], tr_hi, NT_DIMS, preferred_element_type=F32)
            y += lax.dot_general(vre_ref[g], tr_lo, NT_DIMS, preferred_element_type=F32)
            y += lax.dot_general(vim_ref[g], ti_hi, NT_DIMS, preferred_element_type=F32)
            y += lax.dot_general(vim_ref[g], ti_lo, NT_DIMS, preferred_element_type=F32)
            yt[g] = y + d_ref[g] * ug.astype(F32)

    for t in range(SSM_CHUNK):
        rows = jnp.concatenate([yt[g, t * gd:(t + 1) * gd, :] for g in range(SSM_LANE_GROUPS)], axis=0)
        y_ref[t] = rows.T


def _ssm(u3, tables, eye, bsz, seq):
    m, wre, wim, vre, vim, consts, apw, dcol = tables
    nk = u3.shape[1]
    halves = SSM_SEQS // bsz
    n_steps = seq // (SSM_CHUNK * halves)
    lanes = 2 * SSM_STATE
    lg = SSM_LANE_GROUPS
    pitch = n_steps + 8 if (n_steps // 8) % 2 == 0 else n_steps
    lead = lambda shape: pl.BlockSpec(shape, lambda b: (b,) + (0,) * (len(shape) - 1))
    return pl.pallas_call(
        functools.partial(_ssm_body, n_steps=n_steps, halves=halves, pitch=pitch),
        grid=(SSM_GROUPS // lg,),
        in_specs=[pl.BlockSpec((SSM_CHUNK, nk, LANES), lambda b: (0, 0, b)),
                  _const_spec((LANES, LANES)),
                  lead((lg, SSM_VEC, SSM_VEC)),
                  lead((lg // 2, lanes, 2 * SSM_VEC)), lead((lg // 2, lanes, 2 * SSM_VEC)),
                  lead((lg, SSM_VEC, lanes)), lead((lg, SSM_VEC, lanes)),
                  lead((lg // 2, 2, 2, lanes)), lead((lg // 2, 2, n_steps, lanes)),
                  lead((lg, SSM_VEC, 1))],
        out_specs=pl.BlockSpec((SSM_CHUNK, nk, LANES), lambda b: (0, 0, b)),
        out_shape=jax.ShapeDtypeStruct((SSM_CHUNK, nk, SSM_WIDTH), F32),
        scratch_shapes=[pltpu.VMEM((lg, SSM_VEC, nk), BF16), pltpu.VMEM((lg, SSM_VEC, nk), F32)]
                       + [pltpu.VMEM((lg // 2, SSM_SEQS * pitch, lanes), F32)] * 4,
        compiler_params=_cparams(("parallel",)),
        name="s5_ssm",
    )(u3, eye, m, wre, wim, vre, vim, consts, apw, dcol)


def _mix_math(x_ref, a_ref, y3_ref, wglu_ref, sg_ref, wo_ref, fg_ref, yscr):
    nlb = SSM_WIDTH // LANES
    for t in range(SSM_CHUNK):
        for lb in range(nlb):
            yscr[lb, pl.ds(t, ROW_TILE // SSM_CHUNK, stride=SSM_CHUNK), :] = (
                y3_ref[t, :, lb * LANES:(lb + 1) * LANES])
    y = jax.nn.gelu(jnp.concatenate([yscr[lb] for lb in range(nlb)], axis=1))
    z = y * jax.nn.sigmoid(jnp.dot(y.astype(BF16), wglu_ref[...], preferred_element_type=F32))
    ms = jnp.mean(z * z, axis=-1, keepdims=True)
    sn = (z * lax.rsqrt(ms + NORM_EPS) * sg_ref[...]).astype(BF16)
    x1 = (x_ref[...]
          + jnp.dot(a_ref[...], wo_ref[:ATTN_WIDTH, :], preferred_element_type=F32)
          + jnp.dot(sn, wo_ref[ATTN_WIDTH:, :], preferred_element_type=F32))
    ms1 = jnp.mean(x1 * x1, axis=-1, keepdims=True)
    return x1, x1 * lax.rsqrt(ms1 + NORM_EPS) * fg_ref[...]


def _mix_router_body(x_ref, a_ref, y3_ref, wglu_ref, sg_ref, wo_ref, fg_ref, rwt_ref,
                     x1_out, lg_out, yscr):
    x1, hn = _mix_math(x_ref, a_ref, y3_ref, wglu_ref, sg_ref, wo_ref, fg_ref, yscr)
    x1_out[...] = x1
    lane = lax.broadcasted_iota(jnp.int32, lg_out.shape, 1)
    lg = jnp.zeros(lg_out.shape, F32)
    for e in range(lg_out.shape[1]):
        lg = jnp.where(lane == e, jnp.sum(hn * rwt_ref[e:e + 1, :], axis=-1, keepdims=True), lg)
    lg_out[...] = lg


def _mix_ffn_body(x_ref, a_ref, y3_ref, wglu_ref, sg_ref, wo_ref, fg_ref, wg_ref, wu_ref, wd_ref,
                  o_ref, yscr):
    x1, hn = _mix_math(x_ref, a_ref, y3_ref, wglu_ref, sg_ref, wo_ref, fg_ref, yscr)
    h = hn.astype(BF16)
    o_ref[...] = x1
    for c in range(D_FF // FF_CHUNK):
        sl = slice(c * FF_CHUNK, (c + 1) * FF_CHUNK)
        g = jnp.dot(h, wg_ref[:, sl], preferred_element_type=F32)
        u = jnp.dot(h, wu_ref[:, sl], preferred_element_type=F32)
        a = (jax.nn.silu(g) * u).astype(BF16)
        o_ref[...] += jnp.dot(a, wd_ref[sl, :], preferred_element_type=F32)


def _mix_specs():
    row = lambda w: pl.BlockSpec((ROW_TILE, w), lambda i: (i, 0))
    in_specs = [row(D_MODEL), row(ATTN_WIDTH),
                pl.BlockSpec((SSM_CHUNK, ROW_TILE // SSM_CHUNK, SSM_WIDTH), lambda i: (0, i, 0)),
                _const_spec((SSM_WIDTH, SSM_WIDTH)), _const_spec((1, SSM_WIDTH)),
                _const_spec((D_MODEL, D_MODEL)), _const_spec((1, D_MODEL))]
    return row, in_specs, pltpu.VMEM((SSM_WIDTH // LANES, ROW_TILE, LANES), F32)


def _mix_router(x2d, attn, y3, wglu, sg, wo, fg, rwt):
    t = x2d.shape[0]
    ne = rwt.shape[0]
    row, in_specs, yscr = _mix_specs()
    return pl.pallas_call(
        _mix_router_body,
        grid=(t // ROW_TILE,),
        in_specs=in_specs + [_const_spec((ne, D_MODEL))],
        out_specs=[row(D_MODEL), row(ne)],
        out_shape=[jax.ShapeDtypeStruct((t, D_MODEL), F32), jax.ShapeDtypeStruct((t, ne), F32)],
        scratch_shapes=[yscr],
        compiler_params=_cparams(("parallel",)),
        name="mix_router",
    )(x2d, attn, y3, wglu, sg, wo, fg, rwt)


def _mix_ffn(x2d, attn, y3, wglu, sg, wo, fg, wg, wu, wd, casts=()):
    t = x2d.shape[0]
    row, in_specs, yscr = _mix_specs()
    return _row_tiled_call(
        _mix_ffn_body, t // ROW_TILE,
        in_specs=in_specs + [_resident_spec((D_MODEL, D_FF)), _resident_spec((D_MODEL, D_FF)),
                             _resident_spec((D_FF, D_MODEL))],
        out_specs=[row(D_MODEL)],
        out_shape=[jax.ShapeDtypeStruct((t, D_MODEL), F32)],
        scratch_shapes=[yscr],
        name="mix_dense_ffn", args=(x2d, attn, y3, wglu, sg, wo, fg, wg, wu, wd), casts=casts)


def _expert_ffn_body(e_ref, nused_ref, nvalid_ref, tok_ref, dst_ref, x1_hbm, fg_ref,
                     wg_ref, wu_ref, wd_ref, out_hbm, xbuf, acc, ybuf, gsem, ssem):
    b = pl.program_id(0)
    n_used = nused_ref[0]
    n_ff = D_FF // FF_CHUNK
    rows_per_chunk = -(-MOE_ROWS // n_ff)

    def gather_row(blk, r):
        i = tok_ref[blk * MOE_ROWS + r]
        pltpu.make_async_copy(x1_hbm.at[pl.ds(i, 1)], xbuf.at[pl.ds(r, 1)], gsem).start()

    def scatter_row(blk, r):
        i = dst_ref[blk * MOE_ROWS + r]
        pltpu.make_async_copy(ybuf.at[pl.ds(r, 1)], out_hbm.at[pl.ds(i, 1)], ssem).start()

    def scatter_rows(blk, n):
        def body(r, carry):
            scatter_row(blk, r)
            return carry
        lax.fori_loop(0, n, body, 0)

    def wait_gather():
        pltpu.make_async_copy(x1_hbm.at[pl.ds(0, MOE_ROWS)], xbuf, gsem).wait()

    def wait_scatter(n):
        n8 = pl.multiple_of(n & -8, 8)

        @pl.when(n8 > 0)
        def _():
            pltpu.make_async_copy(ybuf.at[pl.ds(0, n8)], out_hbm.at[pl.ds(0, n8)], ssem).wait()

        def one(_, carry):
            pltpu.make_async_copy(ybuf.at[pl.ds(0, 1)], out_hbm.at[pl.ds(0, 1)], ssem).wait()
            return carry
        lax.fori_loop(0, n - n8, one, 0)

    def ffn(scatter_prev):
        wait_gather()
        x = xbuf[...]
        ms = jnp.mean(x * x, axis=-1, keepdims=True)
        h = (x * lax.rsqrt(ms + NORM_EPS) * fg_ref[...]).astype(BF16)
        for c in range(n_ff):
            for r in range(c * rows_per_chunk, min((c + 1) * rows_per_chunk, MOE_ROWS)):
                gather_row(b + 1, r)
                if scatter_prev:
                    scatter_row(b - 1, r)
            sl = slice(c * FF_CHUNK, (c + 1) * FF_CHUNK)
            g = jnp.dot(h, wg_ref[0, :, sl], preferred_element_type=F32)
            u = jnp.dot(h, wu_ref[0, :, sl], preferred_element_type=F32)
            a = (jax.nn.silu(g) * u).astype(BF16)
            part = jnp.dot(a, wd_ref[0, sl, :], preferred_element_type=F32)
            if c == 0:
                acc[...] = part
            else:
                acc[...] += part

    @pl.when(b < n_used)
    def _():
        @pl.when(b == 0)
        def _():
            def body(r, carry):
                gather_row(0, r)
                return carry
            lax.fori_loop(0, MOE_ROWS, body, 0, unroll=8)

        n_prev = jnp.where(b >= 1, nvalid_ref[jnp.maximum(b - 1, 0)], 0)

        @pl.when(n_prev == MOE_ROWS)
        def _():
            ffn(scatter_prev=True)

        @pl.when(n_prev != MOE_ROWS)
        def _():
            scatter_rows(b - 1, n_prev)
            ffn(scatter_prev=False)

        wait_scatter(n_prev)
        ybuf[...] = acc[...]

        @pl.when(b == n_used - 1)
        def _():
            scatter_rows(b, nvalid_ref[b])
            wait_scatter(nvalid_ref[b])
            wait_gather()


def _expert_ffn(x1, fg, tok, dst, block_e, n_used, n_valid, wg, wu, wd, nblk, out_rows):
    wspec = lambda shape: pl.BlockSpec(shape, lambda b, e, *_: (e[b], 0, 0))
    return pl.pallas_call(
        _expert_ffn_body,
        grid_spec=pltpu.PrefetchScalarGridSpec(
            num_scalar_prefetch=5, grid=(nblk,),
            in_specs=[pl.BlockSpec(memory_space=pl.ANY),
                      pl.BlockSpec((1, D_MODEL), lambda b, *_: (0, 0)),
                      wspec((1, D_MODEL, D_FF)), wspec((1, D_MODEL, D_FF)), wspec((1, D_FF, D_MODEL))],
            out_specs=pl.BlockSpec(memory_space=pl.ANY),
            scratch_shapes=[pltpu.VMEM((MOE_ROWS, D_MODEL), F32)] * 3
                           + [pltpu.SemaphoreType.DMA(()), pltpu.SemaphoreType.DMA(())]),
        out_shape=jax.ShapeDtypeStruct((out_rows, D_MODEL), F32),
        compiler_params=pltpu.CompilerParams(dimension_semantics=("arbitrary",),
                                             vmem_limit_bytes=MOE_VMEM_LIMIT),
        name="moe_expert_ffn",
    )(block_e, n_used, n_valid, tok, dst, x1, fg, wg, wu, wd)


def _combine_body(x_ref, y0_ref, y1_ref, g_ref, o_ref):
    o_ref[...] = x_ref[...] + y0_ref[...] * g_ref[:, 0:1] + y1_ref[...] * g_ref[:, 1:2]


def _combine(x1, y2, gates):
    t = x1.shape[0]
    nt = t // ROW_TILE
    return pl.pallas_call(
        _combine_body,
        grid=(nt,),
        in_specs=[pl.BlockSpec((ROW_TILE, D_MODEL), lambda i: (i, 0)),
                  pl.BlockSpec((ROW_TILE, D_MODEL), lambda i: (i, 0)),
                  pl.BlockSpec((ROW_TILE, D_MODEL), lambda i: (i + nt, 0)),
                  pl.BlockSpec((ROW_TILE, TOP_K), lambda i: (i, 0))],
        out_specs=pl.BlockSpec((ROW_TILE, D_MODEL), lambda i: (i, 0)),
        out_shape=jax.ShapeDtypeStruct((t, D_MODEL), F32),
        compiler_params=_cparams(("parallel",)),
        name="moe_combine",
    )(x1, y2, y2, gates)


def _moe(x1, fg, logits, wg, wu, wd):
    t = x1.shape[0]
    n_assign = t * TOP_K
    nblk = n_assign // MOE_ROWS + N_EXPERTS
    cap = nblk * MOE_ROWS
    top_v, top_i = lax.top_k(logits, TOP_K)
    gates = jax.nn.softmax(top_v, axis=-1)
    flat_e = top_i.reshape(-1).astype(jnp.int32)
    order = jnp.argsort(flat_e, stable=True).astype(jnp.int32)
    experts = jnp.arange(N_EXPERTS, dtype=jnp.int32)
    counts = jnp.sum((flat_e[:, None] == experts[None, :]).astype(jnp.int32), axis=0)
    start = jnp.cumsum(counts) - counts
    padded = ((counts + MOE_ROWS - 1) // MOE_ROWS) * MOE_ROWS
    pend = jnp.cumsum(padded)
    pstart = pend - padded
    p = jnp.arange(cap, dtype=jnp.int32)
    e_p = jnp.minimum(jnp.sum((p[:, None] >= pend[None, :]).astype(jnp.int32), axis=1), N_EXPERTS - 1)
    rank = p - pstart[e_p]
    valid = jnp.logical_and(rank < counts[e_p], p < pend[-1])
    a_p = order[jnp.clip(start[e_p] + rank, 0, n_assign - 1)]
    tok = jnp.where(valid, a_p // TOP_K, 0).astype(jnp.int32)
    dst = jnp.where(valid, (a_p % TOP_K) * t + a_p // TOP_K, 0).astype(jnp.int32)
    block_e = e_p[::MOE_ROWS]
    n_used = (pend[-1:] // MOE_ROWS).astype(jnp.int32)
    n_valid = jnp.sum(valid.reshape(nblk, MOE_ROWS).astype(jnp.int32), axis=1)
    y2 = _expert_ffn(x1, fg, tok, dst, block_e, n_used, n_valid, wg, wu, wd, nblk, n_assign)
    return _combine(x1, y2, gates)


def kernel(x, attn_norm_g, w_in, q_norm_g, k_norm_g, sinks, lam_re, lam_im, log_dt, b_re, b_im,
           c_re, c_im, d_skip, w_glu, attn_out_g, ssm_out_g, w_o, ffn_norm_g, dense_wg, dense_wu,
           dense_wd, router_w, moe_wg, moe_wu, moe_wd):
    bsz, seq, _ = x.shape
    depth = w_in.shape[0]
    assert SSM_SEQS % bsz == 0 and seq % (SSM_CHUNK * (SSM_SEQS // bsz)) == 0
    assert seq % ROW_TILE == 0 and ROW_TILE % ATTN_BLOCK == 0
    x2d = x.reshape(bsz * seq, D_MODEL).astype(F32)
    head = jnp.arange(ATTN_WIDTH) // HEAD_DIM
    avg = jnp.where(head[:, None] == head[None, :], 1.0 / HEAD_DIM, 0.0).astype(BF16)
    eye = jnp.eye(ROW_TILE, dtype=BF16)
    n_steps = seq // (SSM_CHUNK * (SSM_SEQS // bsz))
    row = lambda v: v.astype(F32).reshape(1, -1)
    col = lambda v: v.astype(F32).reshape(-1, 1)
    ko, vo, uo = ATTN_WIDTH, ATTN_WIDTH + KV_WIDTH, ATTN_WIDTH + 2 * KV_WIDTH
    flat = lambda w: w.astype(F32).reshape(-1, w.shape[-1])
    moe_wg_bf = None
    for l in range(depth):
        i = l // 2
        dense = l % 2 == 0
        qg = jnp.tile(q_norm_g[l].astype(F32), N_HEADS) * (HEAD_DIM ** -0.5)
        kg = jnp.tile(k_norm_g[l].astype(F32), N_KV_HEADS)
        w = w_in[l].astype(BF16)
        early = [flat(dense_wg[i]), flat(dense_wu[i])] if dense else [flat(moe_wu[i])]
        late = [flat(dense_wd[i])] if dense else [flat(moe_wd[i])]
        wqvt = jnp.concatenate([w[:, :ko], w[:, vo:uo]], axis=1).T
        wku = jnp.concatenate([w[:, ko:vo], w[:, uo:]], axis=1)
        (qt, k, v3, u3), early_bf = _inproj(x2d, row(attn_norm_g[l]), wqvt, wku, col(qg), row(kg), avg,
                                            casts=early)
        (attn,), late_bf = _attention(qt, k, v3, sinks[l].astype(F32), col(attn_out_g[l]), eye, seq,
                                      casts=late)
        tables = _ssm_tables(lam_re[l], lam_im[l], log_dt[l], b_re[l], b_im[l], c_re[l], c_im[l],
                             d_skip[l], n_steps)
        y3 = _ssm(u3, tables, eye[:LANES, :LANES], bsz, seq)
        mix_args = (x2d, attn, y3, w_glu[l].astype(BF16), row(ssm_out_g[l]), w_o[l].astype(BF16),
                    row(ffn_norm_g[l]))
        if dense:
            ahead = [flat(moe_wg[i])] if l + 1 < depth else []
            (x2d,), ahead_bf = _mix_ffn(*mix_args, early_bf[0], early_bf[1], late_bf[0], casts=ahead)
            if ahead:
                moe_wg_bf = ahead_bf[0].reshape(moe_wg[i].shape)
        else:
            x1, logits = _mix_router(*mix_args, router_w[i].astype(F32).T)
            x2d = _moe(x1, row(ffn_norm_g[l]), logits, moe_wg_bf,
                       early_bf[0].reshape(moe_wu[i].shape), late_bf[0].reshape(moe_wd[i].shape))
    return x2d.reshape(bsz, seq, D_MODEL)
```

```python
import functools
import math

import jax
import jax.numpy as jnp
from jax import lax
from jax.experimental import pallas as pl
from jax.experimental.pallas import tpu as pltpu

F32 = jnp.float32
BF16 = jnp.bfloat16

D_MODEL = 1024
HEAD_DIM = 64
N_HEADS = 8
N_KV_HEADS = 2
GQA = N_HEADS // N_KV_HEADS
ATTN_WIDTH = N_HEADS * HEAD_DIM
KV_WIDTH = N_KV_HEADS * HEAD_DIM
ATTN_BLOCK = 128
SSM_WIDTH = D_MODEL - ATTN_WIDTH
SSM_GROUP_DIM = 16
SSM_GROUPS = SSM_WIDTH // SSM_GROUP_DIM
SSM_STATE = 64
IN_WIDTH = ATTN_WIDTH + 2 * KV_WIDTH + SSM_WIDTH
D_FF = 3584
N_EXPERTS = 8
TOP_K = 2
NORM_EPS = 1e-6
NEG_INF = -1e30

LANES = 128
ROW_TILE = 512
SSM_CHUNK = 16
SSM_SEQS = 8
SSM_VEC = SSM_CHUNK * SSM_GROUP_DIM
SSM_LANE_GROUPS = LANES // SSM_GROUP_DIM
FF_CHUNK = 256
MOE_ROWS = 512
VMEM_LIMIT = 56 * 1024 * 1024
MOE_VMEM_LIMIT = 60 * 1024 * 1024

NT_DIMS = (((1,), (1,)), ((), ()))


def _cparams(sem, vmem=VMEM_LIMIT):
    return pltpu.CompilerParams(dimension_semantics=sem, vmem_limit_bytes=vmem)


def _const_spec(shape):
    n = len(shape)
    return pl.BlockSpec(shape, lambda *_: (0,) * n)


def _resident_spec(shape):
    n = len(shape)
    return pl.BlockSpec(shape, lambda *_: (0,) * n, pipeline_mode=pl.Buffered(1))


def _row_tiled_call(body, steps, in_specs, out_specs, out_shape, scratch_shapes, name, args, casts=()):
    n_in, n_out, n_c = len(in_specs), len(out_specs), len(casts)

    def hosted(*refs):
        ins, cast_in = refs[:n_in], refs[n_in:n_in + n_c]
        outs = refs[n_in + n_c:n_in + n_c + n_out]
        cast_out = refs[n_in + n_c + n_out:n_in + 2 * n_c + n_out]
        body(*ins, *outs, *refs[n_in + 2 * n_c + n_out:])
        for src, dst in zip(cast_in, cast_out):
            dst[...] = src[...].astype(BF16)

    for w in casts:
        assert w.ndim == 2 and w.shape[0] % (steps * 16) == 0, w.shape
    cast_specs = [pl.BlockSpec((w.shape[0] // steps, w.shape[1]), lambda i: (i, 0)) for w in casts]
    res = pl.pallas_call(
        hosted,
        grid=(steps,),
        in_specs=list(in_specs) + cast_specs,
        out_specs=list(out_specs) + cast_specs,
        out_shape=list(out_shape) + [jax.ShapeDtypeStruct(w.shape, BF16) for w in casts],
        scratch_shapes=scratch_shapes,
        compiler_params=_cparams(("parallel",)),
        name=name,
    )(*args, *casts)
    return res[:n_out], res[n_out:]


def _inproj_body(x_ref, g_ref, wqvt_ref, wku_ref, qg_ref, kg_ref, avg_ref,
                 qt_out, k_out, v3_out, u3_out, uscr):
    x = x_ref[...]
    ms = jnp.mean(x * x, axis=-1, keepdims=True)
    hn = (x * lax.rsqrt(ms + NORM_EPS) * g_ref[...]).astype(BF16)
    qvt = lax.dot_general(wqvt_ref[...], hn, NT_DIMS, preferred_element_type=F32)
    ku = jnp.dot(hn, wku_ref[...], preferred_element_type=F32)
    qt = qvt[:ATTN_WIDTH]
    qms = jnp.dot(avg_ref[...], (qt * qt).astype(BF16), preferred_element_type=F32)
    qt_out[...] = (qt * lax.rsqrt(qms + NORM_EPS) * qg_ref[...]).astype(BF16)
    k = ku[:, :KV_WIDTH]
    kms = jnp.dot((k * k).astype(BF16), avg_ref[:KV_WIDTH, :KV_WIDTH], preferred_element_type=F32)
    k_out[...] = (k * lax.rsqrt(kms + NORM_EPS) * kg_ref[...]).astype(BF16)
    vt = qvt[ATTN_WIDTH:]
    for b in range(ROW_TILE // ATTN_BLOCK):
        v3_out[b] = vt[:, b * ATTN_BLOCK:(b + 1) * ATTN_BLOCK].astype(BF16)
    u = ku[:, KV_WIDTH:]
    for lb in range(SSM_WIDTH // LANES):
        uscr[lb] = u[:, lb * LANES:(lb + 1) * LANES]
    for t in range(SSM_CHUNK):
        for lb in range(SSM_WIDTH // LANES):
            u3_out[t, :, lb * LANES:(lb + 1) * LANES] = (
                uscr[lb, pl.ds(t, ROW_TILE // SSM_CHUNK, stride=SSM_CHUNK), :].astype(BF16))


def _inproj(x2d, g, wqvt, wku, qg, kg, avg, casts=()):
    t = x2d.shape[0]
    cpt = ROW_TILE // SSM_CHUNK
    return _row_tiled_call(
        _inproj_body, t // ROW_TILE,
        in_specs=[pl.BlockSpec((ROW_TILE, D_MODEL), lambda i: (i, 0)),
                  _const_spec((1, D_MODEL)), _const_spec((ATTN_WIDTH + KV_WIDTH, D_MODEL)),
                  _const_spec((D_MODEL, KV_WIDTH + SSM_WIDTH)), _const_spec((ATTN_WIDTH, 1)),
                  _const_spec((1, KV_WIDTH)), _const_spec((ATTN_WIDTH, ATTN_WIDTH))],
        out_specs=[pl.BlockSpec((ATTN_WIDTH, ROW_TILE), lambda i: (0, i)),
                   pl.BlockSpec((ROW_TILE, KV_WIDTH), lambda i: (i, 0)),
                   pl.BlockSpec((ROW_TILE // ATTN_BLOCK, KV_WIDTH, ATTN_BLOCK), lambda i: (i, 0, 0)),
                   pl.BlockSpec((SSM_CHUNK, cpt, SSM_WIDTH), lambda i: (0, i, 0))],
        out_shape=[jax.ShapeDtypeStruct((ATTN_WIDTH, t), BF16),
                   jax.ShapeDtypeStruct((t, KV_WIDTH), BF16),
                   jax.ShapeDtypeStruct((t // ATTN_BLOCK, KV_WIDTH, ATTN_BLOCK), BF16),
                   jax.ShapeDtypeStruct((SSM_CHUNK, t // SSM_CHUNK, SSM_WIDTH), BF16)],
        scratch_shapes=[pltpu.VMEM((SSM_WIDTH // LANES, ROW_TILE, LANES), F32)],
        name="inproj", args=(x2d, g, wqvt, wku, qg, kg, avg), casts=casts)


def _attn_body(sink_ref, qt_ref, k_ref, v3_ref, g_ref, eye_ref, o_ref, at_scr, *, blocks_per_seq):
    i = pl.program_id(0)
    key = lax.broadcasted_iota(jnp.int32, (ATTN_BLOCK, ATTN_BLOCK), 0)
    qry = lax.broadcasted_iota(jnp.int32, (ATTN_BLOCK, ATTN_BLOCK), 1)
    cur_ok = key <= qry
    zpad = jnp.zeros((HEAD_DIM, ATTN_BLOCK), BF16)
    nblk = ROW_TILE // ATTN_BLOCK
    for blk in range(nblk):
        gblk = i * nblk + blk
        pblk = jnp.maximum(gblk - 1, 0)
        row0 = pl.multiple_of(gblk * ATTN_BLOCK, ATTN_BLOCK)
        prev0 = pl.multiple_of(pblk * ATTN_BLOCK, ATTN_BLOCK)
        has_prev = (gblk % blocks_per_seq) != 0
        kc = k_ref[pl.ds(row0, ATTN_BLOCK), :]
        kp = k_ref[pl.ds(prev0, ATTN_BLOCK), :]
        vc = v3_ref[gblk]
        vp = v3_ref[pblk]
        qb = qt_ref[:, blk * ATTN_BLOCK:(blk + 1) * ATTN_BLOCK]
        keep = jnp.logical_or(cur_ok, has_prev)
        outs = []
        for h in range(N_HEADS):
            kv = h // GQA
            qh = qb[h * HEAD_DIM:(h + 1) * HEAD_DIM, :]
            qpad = jnp.concatenate([qh, zpad] if kv == 0 else [zpad, qh], axis=0)
            sc = jnp.dot(kc, qpad, preferred_element_type=F32)
            sp = jnp.dot(kp, qpad, preferred_element_type=F32)
            s = jnp.where(keep, jnp.where(cur_ok, sc, sp), NEG_INF)
            sink = sink_ref[h]
            m = jnp.maximum(jnp.max(s, axis=0, keepdims=True), sink)
            p = jnp.exp(s - m)
            den = jnp.sum(p, axis=0, keepdims=True) + jnp.exp(sink - m)
            pb = p.astype(BF16)
            zero = jnp.zeros_like(pb)
            vs = slice(kv * HEAD_DIM, (kv + 1) * HEAD_DIM)
            o = (jnp.dot(vc[vs, :], jnp.where(cur_ok, pb, zero), preferred_element_type=F32)
                 + jnp.dot(vp[vs, :], jnp.where(cur_ok, zero, pb), preferred_element_type=F32))
            outs.append(o / den)
        a = jnp.concatenate(outs, axis=0)
        ms = jnp.mean(a * a, axis=0, keepdims=True)
        at_scr[:, blk * ATTN_BLOCK:(blk + 1) * ATTN_BLOCK] = (
            a * lax.rsqrt(ms + NORM_EPS) * g_ref[...]).astype(BF16)
    o_ref[...] = lax.dot_general(eye_ref[...], at_scr[...], NT_DIMS,
                                 preferred_element_type=F32).astype(BF16)


def _attention(qt, k, v3, sinks, gcol, eye, seq, casts=()):
    t = k.shape[0]
    return _row_tiled_call(
        functools.partial(_attn_body, blocks_per_seq=seq // ATTN_BLOCK), t // ROW_TILE,
        in_specs=[pl.BlockSpec(memory_space=pltpu.SMEM),
                  pl.BlockSpec((ATTN_WIDTH, ROW_TILE), lambda i: (0, i)),
                  _const_spec((t, KV_WIDTH)), _const_spec((t // ATTN_BLOCK, KV_WIDTH, ATTN_BLOCK)),
                  _const_spec((ATTN_WIDTH, 1)), _const_spec((ROW_TILE, ROW_TILE))],
        out_specs=[pl.BlockSpec((ROW_TILE, ATTN_WIDTH), lambda i: (i, 0))],
        out_shape=[jax.ShapeDtypeStruct((t, ATTN_WIDTH), BF16)],
        scratch_shapes=[pltpu.VMEM((ATTN_WIDTH, ROW_TILE), BF16)],
        name="swa_attention", args=(sinks, qt, k, v3, gcol, eye), casts=casts)


def _ssm_tables_body(lam_ref, lamc_ref, b_ref, c_ref, erep_ref, etile_ref, eye_ref,
                     m_out, wre_out, wim_out, vre_out, vim_out, k_out, apw_out, *, n_steps):
    hi = lax.Precision.HIGHEST
    lanes = 2 * SSM_STATE
    ch, gd = SSM_CHUNK, SSM_GROUP_DIM
    lr, li, dt = lam_ref[0, 0:1, :], lam_ref[0, 1:2, :], lam_ref[0, 2:3, :]

    def apow(p):
        mag = jnp.exp(p * (lr * dt))
        ang = p * (li * dt)
        return mag * jnp.cos(ang), mag * jnp.sin(ang)

    tau = lax.broadcasted_iota(jnp.int32, (ch, lanes), 0).astype(F32)
    p0r, p0i = apow(tau)
    p1r, p1i = apow(tau + 1.0)
    base = 8
    doublings = []
    while base << len(doublings) < n_steps:
        doublings.append(ch * (base << len(doublings)))
    assert base << len(doublings) == n_steps and len(doublings) <= 6, n_steps
    sel = lax.broadcasted_iota(jnp.int32, (8, lanes), 0)
    pw = jnp.zeros((8, lanes), F32)
    for r, p in enumerate([ch, ch * n_steps] + doublings):
        pw = jnp.where(sel == r, float(p), pw)
    kr, ki = apow(pw)
    k_out[0, 0] = kr[0:2]
    k_out[0, 1] = ki[0:2]
    qr, qi = apow(sel.astype(F32) * float(ch))
    for m in range(len(doublings)):
        dr, di = kr[2 + m:3 + m], ki[2 + m:3 + m]
        qr, qi = (jnp.concatenate([qr, qr * dr - qi * di], axis=0),
                  jnp.concatenate([qi, qr * di + qi * dr], axis=0))
    apw_out[0, 0] = qr
    apw_out[0, 1] = qi

    to_col = lambda p: lax.dot_general(eye_ref[...], p, NT_DIMS, precision=hi,
                                       preferred_element_type=F32)
    c0r, c0i, c1r, c1i = to_col(p0r), to_col(p0i), to_col(p1r), to_col(p1i)
    lrc, lic = lamc_ref[0, :, 0:1], lamc_ref[0, :, 1:2]
    ab_re, ab_im = c1r[:, 0:1], c1i[:, 0:1]
    nr = ab_re - 1.0
    den = lrc * lrc + lic * lic
    f_re = (nr * lrc + ab_im * lic) / den
    f_im = (ab_im * lrc - nr * lic) / den
    br, bi = b_ref[0, 0], b_ref[0, 1]
    bb_re = f_re * br - f_im * bi
    bb_im = f_re * bi + f_im * br
    bx_re = jnp.dot(bb_re, etile_ref[...], precision=hi, preferred_element_type=F32)
    bx_im = jnp.dot(bb_im, etile_ref[...], precision=hi, preferred_element_type=F32)
    px_re = jnp.dot(c0r, erep_ref[...], precision=hi, preferred_element_type=F32)
    px_im = jnp.dot(c0i, erep_ref[...], precision=hi, preferred_element_type=F32)
    w_re = px_re * bx_re - px_im * bx_im
    w_im = px_re * bx_im + px_im * bx_re
    first = lax.broadcasted_iota(jnp.int32, (lanes, SSM_VEC), 0) < SSM_STATE
    pack = lambda w: jnp.concatenate([jnp.where(first, w, 0.0), jnp.where(first, 0.0, w)], axis=1)
    wre_out[0] = pack(w_re).astype(BF16)
    wim_out[0] = pack(w_im).astype(BF16)

    blk = lax.broadcasted_iota(jnp.int32, (SSM_VEC, SSM_VEC), 1) // gd
    for g in range(2):
        cr, ci = c_ref[g, 0], c_ref[g, 1]
        stack = lambda x: x.reshape(ch * gd, lanes)
        ca_re = stack(cr[None] * p0r[:, None, :] - ci[None] * p0i[:, None, :])
        ca_im = stack(cr[None] * p0i[:, None, :] + ci[None] * p0r[:, None, :])
        kt = (jnp.dot(ca_re, bx_re, precision=hi, preferred_element_type=F32)
              - jnp.dot(ca_im, bx_im, precision=hi, preferred_element_type=F32))
        m = jnp.where(blk == 0, kt, 0.0)
        for j in range(1, ch):
            shifted = jnp.concatenate([jnp.zeros((j * gd, SSM_VEC), F32), kt[:SSM_VEC - j * gd]], axis=0)
            m = jnp.where(blk == j, shifted, m)
        m_out[g] = m.astype(BF16)
        vre_out[g] = stack(cr[None] * p1r[:, None, :] - ci[None] * p1i[:, None, :]).astype(BF16)
        vim_out[g] = (-stack(cr[None] * p1i[:, None, :] + ci[None] * p1r[:, None, :])).astype(BF16)


def _ssm_tables(lam_re, lam_im, log_dt, b_re, b_im, c_re, c_im, d_skip, n_steps):
    g_, n_, c_ = SSM_GROUPS, SSM_STATE, SSM_GROUP_DIM
    np_ = g_ // 2
    lanes = 2 * n_
    dt = jnp.broadcast_to(jnp.exp(log_dt.astype(F32))[:, None], (g_, n_))
    lam = jnp.stack([lam_re.astype(F32), lam_im.astype(F32), dt], axis=1)
    lam_row = lam.reshape(np_, 2, 3, n_).transpose(0, 2, 1, 3).reshape(np_, 3, lanes)
    lam_col = lam_row.transpose(0, 2, 1)
    b2 = jnp.stack([b_re.astype(F32).reshape(np_, lanes, c_), b_im.astype(F32).reshape(np_, lanes, c_)],
                   axis=1)
    c2 = jnp.stack([c_re.astype(F32), c_im.astype(F32)], axis=1)
    z = jnp.zeros_like(c2)
    even = (jnp.arange(g_) % 2 == 0)[:, None, None, None]
    c2 = jnp.where(even, jnp.concatenate([c2, z], axis=3), jnp.concatenate([z, c2], axis=3))
    col = jnp.arange(SSM_VEC)
    erep = (SSM_CHUNK - 1 - col[None, :] // c_ == jnp.arange(SSM_CHUNK)[:, None]).astype(F32)
    etile = (col[None, :] % c_ == jnp.arange(c_)[:, None]).astype(F32)
    lead = lambda shape: pl.BlockSpec(shape, lambda p: (p,) + (0,) * (len(shape) - 1))
    m, wre, wim, vre, vim, consts, apw = pl.pallas_call(
        functools.partial(_ssm_tables_body, n_steps=n_steps),
        grid=(np_,),
        in_specs=[lead((1, 3, lanes)), lead((1, lanes, 3)), lead((1, 2, lanes, c_)),
                  lead((2, 2, c_, lanes)), _const_spec((SSM_CHUNK, SSM_VEC)), _const_spec((c_, SSM_VEC)),
                  _const_spec((lanes, lanes))],
        out_specs=[lead((2, SSM_VEC, SSM_VEC)), lead((1, lanes, 2 * SSM_VEC)), lead((1, lanes, 2 * SSM_VEC)),
                   lead((2, SSM_VEC, lanes)), lead((2, SSM_VEC, lanes)),
                   lead((1, 2, 2, lanes)), lead((1, 2, n_steps, lanes))],
        out_shape=[jax.ShapeDtypeStruct((g_, SSM_VEC, SSM_VEC), BF16),
                   jax.ShapeDtypeStruct((np_, lanes, 2 * SSM_VEC), BF16),
                   jax.ShapeDtypeStruct((np_, lanes, 2 * SSM_VEC), BF16),
                   jax.ShapeDtypeStruct((g_, SSM_VEC, lanes), BF16),
                   jax.ShapeDtypeStruct((g_, SSM_VEC, lanes), BF16),
                   jax.ShapeDtypeStruct((np_, 2, 2, lanes), F32),
                   jax.ShapeDtypeStruct((np_, 2, n_steps, lanes), F32)],
        compiler_params=_cparams(("parallel",)),
        name="s5_operators",
    )(lam_row, lam_col, b2, c2, erep, etile, jnp.eye(lanes, dtype=F32))
    dcol = jnp.tile(d_skip.astype(F32).reshape(g_, 1, c_), (1, SSM_CHUNK, 1)).reshape(g_, SSM_VEC, 1)
    return m, wre, wim, vre, vim, consts, apw, dcol


def _ssm_body(u_ref, eye_ref, m_ref, wre_ref, wim_ref, vre_ref, vim_ref, c_ref, apw_ref, d_ref,
              y_ref, ut, yt, pre, pim, sre, sim, *, n_steps, halves, pitch):
    lanes = 2 * SSM_STATE
    nk = u_ref.shape[1]
    gd = SSM_GROUP_DIM
    for t in range(SSM_CHUNK):
        xt = lax.dot_general(eye_ref[...], u_ref[t], NT_DIMS,
                             preferred_element_type=F32).astype(BF16)
        for g in range(SSM_LANE_GROUPS):
            ut[g, t * gd:(t + 1) * gd, :] = xt[g * gd:(g + 1) * gd, :]

    row = lax.broadcasted_iota(jnp.int32, (SSM_SEQS, lanes), 0)
    seq_start = (row % halves) == 0
    shift = lambda x: jnp.where(seq_start, 0.0, pltpu.roll(x, 1, axis=0))

    npairs = SSM_LANE_GROUPS // 2
    for pp in range(npairs):
        ucat = jnp.concatenate([ut[2 * pp], ut[2 * pp + 1]], axis=0)
        p_re = jnp.dot(wre_ref[pp], ucat, preferred_element_type=F32).T
        p_im = jnp.dot(wim_ref[pp], ucat, preferred_element_type=F32).T
        if pitch == n_steps:
            pre[pp] = p_re
            pim[pp] = p_im
        else:
            for s in range(SSM_SEQS):
                pre[pp, s * pitch:s * pitch + n_steps, :] = p_re[s * n_steps:(s + 1) * n_steps]
                pim[pp, s * pitch:s * pitch + n_steps, :] = p_im[s * n_steps:(s + 1) * n_steps]

    a16 = [(jnp.broadcast_to(c_ref[pp, 0, 0:1, :], (SSM_SEQS, lanes)),
            jnp.broadcast_to(c_ref[pp, 1, 0:1, :], (SSM_SEQS, lanes))) for pp in range(npairs)]

    def step(i, carry):
        idx = pl.ds(i, SSM_SEQS, stride=pitch)
        out = []
        for pp in range(npairs):
            s_re, s_im = carry[pp]
            ar, ai = a16[pp]
            sre[pp, idx, :] = s_re
            sim[pp, idx, :] = s_im
            out.append((ar * s_re - ai * s_im + pre[pp, idx, :],
                        ar * s_im + ai * s_re + pim[pp, idx, :]))
        return tuple(out)

    zero = jnp.zeros((SSM_SEQS, lanes), F32)
    ends = lax.fori_loop(0, n_steps, step, ((zero, zero),) * npairs, unroll=2)

    for pp in range(npairs):
        e_re, e_im = ends[pp]
        br = jnp.broadcast_to(c_ref[pp, 0, 1:2, :], (SSM_SEQS, lanes))
        bi = jnp.broadcast_to(c_ref[pp, 1, 1:2, :], (SSM_SEQS, lanes))
        c_re = jnp.zeros_like(e_re)
        c_im = jnp.zeros_like(e_im)
        for _ in range(halves - 1):
            n_re = e_re + br * c_re - bi * c_im
            n_im = e_im + br * c_im + bi * c_re
            c_re, c_im = shift(n_re), shift(n_im)

        apr = apw_ref[pp, 0][None, :, :]
        api = apw_ref[pp, 1][None, :, :]
        if pitch == n_steps:
            s3r = sre[pp].reshape(SSM_SEQS, n_steps, lanes)
            s3i = sim[pp].reshape(SSM_SEQS, n_steps, lanes)
        else:
            s3r = jnp.stack([sre[pp, s * pitch:s * pitch + n_steps, :] for s in range(SSM_SEQS)])
            s3i = jnp.stack([sim[pp, s * pitch:s * pitch + n_steps, :] for s in range(SSM_SEQS)])
        t_re = (s3r + apr * c_re[:, None, :] - api * c_im[:, None, :]).reshape(nk, lanes)
        t_im = (s3i + apr * c_im[:, None, :] + api * c_re[:, None, :]).reshape(nk, lanes)
        tr_hi = t_re.astype(BF16)
        tr_lo = (t_re - tr_hi.astype(F32)).astype(BF16)
        ti_hi = t_im.astype(BF16)
        ti_lo = (t_im - ti_hi.astype(F32)).astype(BF16)
        for g in (2 * pp, 2 * pp + 1):
            ug = ut[g]
            y = jnp.dot(m_ref[g], ug, preferred_element_type=F32)
            y += lax.dot_general(vre_ref[g], tr_hi, NT_DIMS, preferred_element_type=F32)
            y += lax.dot_general(vre_ref[g], tr_lo, NT_DIMS, preferred_element_type=F32)
            y += lax.dot_general(vim_ref[g], ti_hi, NT_DIMS, preferred_element_type=F32)
            y += lax.dot_general(vim_ref[g], ti_lo, NT_DIMS, preferred_element_type=F32)
            yt[g] = y + d_ref[g] * ug.astype(F32)

    for t in range(SSM_CHUNK):
        rows = jnp.concatenate([yt[g, t * gd:(t + 1) * gd, :] for g in range(SSM_LANE_GROUPS)], axis=0)
        y_ref[t] = rows.T


def _ssm(u3, tables, eye, bsz, seq):
    m, wre, wim, vre, vim, consts, apw, dcol = tables
    nk = u3.shape[1]
    halves = SSM_SEQS // bsz
    n_steps = seq // (SSM_CHUNK * halves)
    lanes = 2 * SSM_STATE
    lg = SSM_LANE_GROUPS
    pitch = n_steps + 8 if (n_steps // 8) % 2 == 0 else n_steps
    lead = lambda shape: pl.BlockSpec(shape, lambda b: (b,) + (0,) * (len(shape) - 1))
    return pl.pallas_call(
        functools.partial(_ssm_body, n_steps=n_steps, halves=halves, pitch=pitch),
        grid=(SSM_GROUPS // lg,),
        in_specs=[pl.BlockSpec((SSM_CHUNK, nk, LANES), lambda b: (0, 0, b)),
                  _const_spec((LANES, LANES)),
                  lead((lg, SSM_VEC, SSM_VEC)),
                  lead((lg // 2, lanes, 2 * SSM_VEC)), lead((lg // 2, lanes, 2 * SSM_VEC)),
                  lead((lg, SSM_VEC, lanes)), lead((lg, SSM_VEC, lanes)),
                  lead((lg // 2, 2, 2, lanes)), lead((lg // 2, 2, n_steps, lanes)),
                  lead((lg, SSM_VEC, 1))],
        out_specs=pl.BlockSpec((SSM_CHUNK, nk, LANES), lambda b: (0, 0, b)),
        out_shape=jax.ShapeDtypeStruct((SSM_CHUNK, nk, SSM_WIDTH), F32),
        scratch_shapes=[pltpu.VMEM((lg, SSM_VEC, nk), BF16), pltpu.VMEM((lg, SSM_VEC, nk), F32)]
                       + [pltpu.VMEM((lg // 2, SSM_SEQS * pitch, lanes), F32)] * 4,
        compiler_params=_cparams(("parallel",)),
        name="s5_ssm",
    )(u3, eye, m, wre, wim, vre, vim, consts, apw, dcol)


def _mix_math(x_ref, a_ref, y3_ref, wglu_ref, sg_ref, wo_ref, fg_ref, yscr):
    nlb = SSM_WIDTH // LANES
    for t in range(SSM_CHUNK):
        for lb in range(nlb):
            yscr[lb, pl.ds(t, ROW_TILE // SSM_CHUNK, stride=SSM_CHUNK), :] = (
                y3_ref[t, :, lb * LANES:(lb + 1) * LANES])
    y = jax.nn.gelu(jnp.concatenate([yscr[lb] for lb in range(nlb)], axis=1))
    z = y * jax.nn.sigmoid(jnp.dot(y.astype(BF16), wglu_ref[...], preferred_element_type=F32))
    ms = jnp.mean(z * z, axis=-1, keepdims=True)
    sn = (z * lax.rsqrt(ms + NORM_EPS) * sg_ref[...]).astype(BF16)
    x1 = (x_ref[...]
          + jnp.dot(a_ref[...], wo_ref[:ATTN_WIDTH, :], preferred_element_type=F32)
          + jnp.dot(sn, wo_ref[ATTN_WIDTH:, :], preferred_element_type=F32))
    ms1 = jnp.mean(x1 * x1, axis=-1, keepdims=True)
    return x1, x1 * lax.rsqrt(ms1 + NORM_EPS) * fg_ref[...]


def _mix_router_body(x_ref, a_ref, y3_ref, wglu_ref, sg_ref, wo_ref, fg_ref, rwt_ref,
                     x1_out, lg_out, yscr):
    x1, hn = _mix_math(x_ref, a_ref, y3_ref, wglu_ref, sg_ref, wo_ref, fg_ref, yscr)
    x1_out[...] = x1
    lane = lax.broadcasted_iota(jnp.int32, lg_out.shape, 1)
    lg = jnp.zeros(lg_out.shape, F32)
    for e in range(lg_out.shape[1]):
        lg = jnp.where(lane == e, jnp.sum(hn * rwt_ref[e:e + 1, :], axis=-1, keepdims=True), lg)
    lg_out[...] = lg


def _mix_ffn_body(x_ref, a_ref, y3_ref, wglu_ref, sg_ref, wo_ref, fg_ref, wg_ref, wu_ref, wd_ref,
                  o_ref, yscr):
    x1, hn = _mix_math(x_ref, a_ref, y3_ref, wglu_ref, sg_ref, wo_ref, fg_ref, yscr)
    h = hn.astype(BF16)
    o_ref[...] = x1
    for c in range(D_FF // FF_CHUNK):
        sl = slice(c * FF_CHUNK, (c + 1) * FF_CHUNK)
        g = jnp.dot(h, wg_ref[:, sl], preferred_element_type=F32)
        u = jnp.dot(h, wu_ref[:, sl], preferred_element_type=F32)
        a = (jax.nn.silu(g) * u).astype(BF16)
        o_ref[...] += jnp.dot(a, wd_ref[sl, :], preferred_element_type=F32)


def _mix_specs():
    row = lambda w: pl.BlockSpec((ROW_TILE, w), lambda i: (i, 0))
    in_specs = [row(D_MODEL), row(ATTN_WIDTH),
                pl.BlockSpec((SSM_CHUNK, ROW_TILE // SSM_CHUNK, SSM_WIDTH), lambda i: (0, i, 0)),
                _const_spec((SSM_WIDTH, SSM_WIDTH)), _const_spec((1, SSM_WIDTH)),
                _const_spec((D_MODEL, D_MODEL)), _const_spec((1, D_MODEL))]
    return row, in_specs, pltpu.VMEM((SSM_WIDTH // LANES, ROW_TILE, LANES), F32)


def _mix_router(x2d, attn, y3, wglu, sg, wo, fg, rwt):
    t = x2d.shape[0]
    ne = rwt.shape[0]
    row, in_specs, yscr = _mix_specs()
    return pl.pallas_call(
        _mix_router_body,
        grid=(t // ROW_TILE,),
        in_specs=in_specs + [_const_spec((ne, D_MODEL))],
        out_specs=[row(D_MODEL), row(ne)],
        out_shape=[jax.ShapeDtypeStruct((t, D_MODEL), F32), jax.ShapeDtypeStruct((t, ne), F32)],
        scratch_shapes=[yscr],
        compiler_params=_cparams(("parallel",)),
        name="mix_router",
    )(x2d, attn, y3, wglu, sg, wo, fg, rwt)


def _mix_ffn(x2d, attn, y3, wglu, sg, wo, fg, wg, wu, wd, casts=()):
    t = x2d.shape[0]
    row, in_specs, yscr = _mix_specs()
    return _row_tiled_call(
        _mix_ffn_body, t // ROW_TILE,
        in_specs=in_specs + [_resident_spec((D_MODEL, D_FF)), _resident_spec((D_MODEL, D_FF)),
                             _resident_spec((D_FF, D_MODEL))],
        out_specs=[row(D_MODEL)],
        out_shape=[jax.ShapeDtypeStruct((t, D_MODEL), F32)],
        scratch_shapes=[yscr],
        name="mix_dense_ffn", args=(x2d, attn, y3, wglu, sg, wo, fg, wg, wu, wd), casts=casts)


def _expert_ffn_body(e_ref, nused_ref, nvalid_ref, tok_ref, dst_ref, x1_hbm, fg_ref,
                     wg_ref, wu_ref, wd_ref, out_hbm, xbuf, acc, ybuf, gsem, ssem):
    b = pl.program_id(0)
    n_used = nused_ref[0]
    n_ff = D_FF // FF_CHUNK
    rows_per_chunk = -(-MOE_ROWS // n_ff)

    def gather_row(blk, r):
        i = tok_ref[blk * MOE_ROWS + r]
        pltpu.make_async_copy(x1_hbm.at[pl.ds(i, 1)], xbuf.at[pl.ds(r, 1)], gsem).start()

    def scatter_row(blk, r):
        i = dst_ref[blk * MOE_ROWS + r]
        pltpu.make_async_copy(ybuf.at[pl.ds(r, 1)], out_hbm.at[pl.ds(i, 1)], ssem).start()

    def scatter_rows(blk, n):
        def body(r, carry):
            scatter_row(blk, r)
            return carry
        lax.fori_loop(0, n, body, 0)

    def wait_gather():
        pltpu.make_async_copy(x1_hbm.at[pl.ds(0, MOE_ROWS)], xbuf, gsem).wait()

    def wait_scatter(n):
        n8 = pl.multiple_of(n & -8, 8)

        @pl.when(n8 > 0)
        def _():
            pltpu.make_async_copy(ybuf.at[pl.ds(0, n8)], out_hbm.at[pl.ds(0, n8)], ssem).wait()

        def one(_, carry):
            pltpu.make_async_copy(ybuf.at[pl.ds(0, 1)], out_hbm.at[pl.ds(0, 1)], ssem).wait()
            return carry
        lax.fori_loop(0, n - n8, one, 0)

    def ffn(scatter_prev):
        wait_gather()
        x = xbuf[...]
        ms = jnp.mean(x * x, axis=-1, keepdims=True)
        h = (x * lax.rsqrt(ms + NORM_EPS) * fg_ref[...]).astype(BF16)
        for c in range(n_ff):
            for r in range(c * rows_per_chunk, min((c + 1) * rows_per_chunk, MOE_ROWS)):
                gather_row(b + 1, r)
                if scatter_prev:
                    scatter_row(b - 1, r)
            sl = slice(c * FF_CHUNK, (c + 1) * FF_CHUNK)
            g = jnp.dot(h, wg_ref[0, :, sl], preferred_element_type=F32)
            u = jnp.dot(h, wu_ref[0, :, sl], preferred_element_type=F32)
            a = (jax.nn.silu(g) * u).astype(BF16)
            part = jnp.dot(a, wd_ref[0, sl, :], preferred_element_type=F32)
            if c == 0:
                acc[...] = part
            else:
                acc[...] += part

    @pl.when(b < n_used)
    def _():
        @pl.when(b == 0)
        def _():
            def body(r, carry):
                gather_row(0, r)
                return carry
            lax.fori_loop(0, MOE_ROWS, body, 0, unroll=8)

        n_prev = jnp.where(b >= 1, nvalid_ref[jnp.maximum(b - 1, 0)], 0)

        @pl.when(n_prev == MOE_ROWS)
        def _():
            ffn(scatter_prev=True)

        @pl.when(n_prev != MOE_ROWS)
        def _():
            scatter_rows(b - 1, n_prev)
            ffn(scatter_prev=False)

        wait_scatter(n_prev)
        ybuf[...] = acc[...]

        @pl.when(b == n_used - 1)
        def _():
            scatter_rows(b, nvalid_ref[b])
            wait_scatter(nvalid_ref[b])
            wait_gather()


def _expert_ffn(x1, fg, tok, dst, block_e, n_used, n_valid, wg, wu, wd, nblk, out_rows):
    wspec = lambda shape: pl.BlockSpec(shape, lambda b, e, *_: (e[b], 0, 0))
    return pl.pallas_call(
        _expert_ffn_body,
        grid_spec=pltpu.PrefetchScalarGridSpec(
            num_scalar_prefetch=5, grid=(nblk,),
            in_specs=[pl.BlockSpec(memory_space=pl.ANY),
                      pl.BlockSpec((1, D_MODEL), lambda b, *_: (0, 0)),
                      wspec((1, D_MODEL, D_FF)), wspec((1, D_MODEL, D_FF)), wspec((1, D_FF, D_MODEL))],
            out_specs=pl.BlockSpec(memory_space=pl.ANY),
            scratch_shapes=[pltpu.VMEM((MOE_ROWS, D_MODEL), F32)] * 3
                           + [pltpu.SemaphoreType.DMA(()), pltpu.SemaphoreType.DMA(())]),
        out_shape=jax.ShapeDtypeStruct((out_rows, D_MODEL), F32),
        compiler_params=pltpu.CompilerParams(dimension_semantics=("arbitrary",),
                                             vmem_limit_bytes=MOE_VMEM_LIMIT),
        name="moe_expert_ffn",
    )(block_e, n_used, n_valid, tok, dst, x1, fg, wg, wu, wd)


def _combine_body(x_ref, y0_ref, y1_ref, g_ref, o_ref):
    o_ref[...] = x_ref[...] + y0_ref[...] * g_ref[:, 0:1] + y1_ref[...] * g_ref[:, 1:2]


def _combine(x1, y2, gates):
    t = x1.shape[0]
    rows = 2 * ROW_TILE if t % (2 * ROW_TILE) == 0 else ROW_TILE
    nt = t // rows
    return pl.pallas_call(
        _combine_body,
        grid=(nt,),
        in_specs=[pl.BlockSpec((rows, D_MODEL), lambda i: (i, 0)),
                  pl.BlockSpec((rows, D_MODEL), lambda i: (i, 0)),
                  pl.BlockSpec((rows, D_MODEL), lambda i: (i + nt, 0)),
                  pl.BlockSpec((rows, TOP_K), lambda i: (i, 0))],
        out_specs=pl.BlockSpec((rows, D_MODEL), lambda i: (i, 0)),
        out_shape=jax.ShapeDtypeStruct((t, D_MODEL), F32),
        compiler_params=_cparams(("parallel",)),
        name="moe_combine",
    )(x1, y2, y2, gates)


def _moe(x1, fg, logits, wg, wu, wd):
    t = x1.shape[0]
    n_assign = t * TOP_K
    nblk = n_assign // MOE_ROWS + N_EXPERTS
    cap = nblk * MOE_ROWS
    top_v, top_i = lax.top_k(logits, TOP_K)
    gates = jax.nn.softmax(top_v, axis=-1)
    flat_e = top_i.reshape(-1).astype(jnp.int32)
    order = jnp.argsort(flat_e, stable=True).astype(jnp.int32)
    experts = jnp.arange(N_EXPERTS, dtype=jnp.int32)
    counts = jnp.sum((flat_e[:, None] == experts[None, :]).astype(jnp.int32), axis=0)
    start = jnp.cumsum(counts) - counts
    padded = ((counts + MOE_ROWS - 1) // MOE_ROWS) * MOE_ROWS
    pend = jnp.cumsum(padded)
    pstart = pend - padded
    p = jnp.arange(cap, dtype=jnp.int32)
    e_p = jnp.minimum(jnp.sum((p[:, None] >= pend[None, :]).astype(jnp.int32), axis=1), N_EXPERTS - 1)
    rank = p - pstart[e_p]
    valid = jnp.logical_and(rank < counts[e_p], p < pend[-1])
    a_p = order[jnp.clip(start[e_p] + rank, 0, n_assign - 1)]
    tok = jnp.where(valid, a_p // TOP_K, 0).astype(jnp.int32)
    dst = jnp.where(valid, (a_p % TOP_K) * t + a_p // TOP_K, 0).astype(jnp.int32)
    block_e = e_p[::MOE_ROWS]
    n_used = (pend[-1:] // MOE_ROWS).astype(jnp.int32)
    n_valid = jnp.sum(valid.reshape(nblk, MOE_ROWS).astype(jnp.int32), axis=1)
    y2 = _expert_ffn(x1, fg, tok, dst, block_e, n_used, n_valid, wg, wu, wd, nblk, n_assign)
    return _combine(x1, y2, gates)


def kernel(x, attn_norm_g, w_in, q_norm_g, k_norm_g, sinks, lam_re, lam_im, log_dt, b_re, b_im,
           c_re, c_im, d_skip, w_glu, attn_out_g, ssm_out_g, w_o, ffn_norm_g, dense_wg, dense_wu,
           dense_wd, router_w, moe_wg, moe_wu, moe_wd):
    bsz, seq, _ = x.shape
    depth = w_in.shape[0]
    assert SSM_SEQS % bsz == 0 and seq % (SSM_CHUNK * (SSM_SEQS // bsz)) == 0
    assert seq % ROW_TILE == 0 and ROW_TILE % ATTN_BLOCK == 0
    x2d = x.reshape(bsz * seq, D_MODEL).astype(F32)
    head = jnp.arange(ATTN_WIDTH) // HEAD_DIM
    avg = jnp.where(head[:, None] == head[None, :], 1.0 / HEAD_DIM, 0.0).astype(BF16)
    eye = jnp.eye(ROW_TILE, dtype=BF16)
    n_steps = seq // (SSM_CHUNK * (SSM_SEQS // bsz))
    row = lambda v: v.astype(F32).reshape(1, -1)
    col = lambda v: v.astype(F32).reshape(-1, 1)
    ko, vo, uo = ATTN_WIDTH, ATTN_WIDTH + KV_WIDTH, ATTN_WIDTH + 2 * KV_WIDTH
    flat = lambda w: w.astype(F32).reshape(-1, w.shape[-1])
    moe_wg_bf = None
    for l in range(depth):
        i = l // 2
        dense = l % 2 == 0
        qg = jnp.tile(q_norm_g[l].astype(F32), N_HEADS) * (HEAD_DIM ** -0.5)
        kg = jnp.tile(k_norm_g[l].astype(F32), N_KV_HEADS)
        w = w_in[l].astype(BF16)
        early = [flat(dense_wg[i]), flat(dense_wu[i])] if dense else [flat(moe_wu[i])]
        late = [flat(dense_wd[i])] if dense else [flat(moe_wd[i])]
        wqvt = jnp.concatenate([w[:, :ko], w[:, vo:uo]], axis=1).T
        wku = jnp.concatenate([w[:, ko:vo], w[:, uo:]], axis=1)
        (qt, k, v3, u3), early_bf = _inproj(x2d, row(attn_norm_g[l]), wqvt, wku, col(qg), row(kg), avg,
                                            casts=early)
        (attn,), late_bf = _attention(qt, k, v3, sinks[l].astype(F32), col(attn_out_g[l]), eye, seq,
                                      casts=late)
        tables = _ssm_tables(lam_re[l], lam_im[l], log_dt[l], b_re[l], b_im[l], c_re[l], c_im[l],
                             d_skip[l], n_steps)
        y3 = _ssm(u3, tables, eye[:LANES, :LANES], bsz, seq)
        mix_args = (x2d, attn, y3, w_glu[l].astype(BF16), row(ssm_out_g[l]), w_o[l].astype(BF16),
                    row(ffn_norm_g[l]))
        if dense:
            ahead = [flat(moe_wg[i])] if l + 1 < depth else []
            (x2d,), ahead_bf = _mix_ffn(*mix_args, early_bf[0], early_bf[1], late_bf[0], casts=ahead)
            if ahead:
                moe_wg_bf = ahead_bf[0].reshape(moe_wg[i].shape)
        else:
            x1, logits = _mix_router(*mix_args, router_w[i].astype(F32).T)
            x2d = _moe(x1, row(ffn_norm_g[l]), logits, moe_wg_bf,
                       early_bf[0].reshape(moe_wu[i].shape), late_bf[0].reshape(moe_wd[i].shape))
    return x2d.reshape(bsz, seq, D_MODEL)
```

```python
import functools
import math

import jax
import jax.numpy as jnp
from jax import lax
from jax.experimental import pallas as pl
from jax.experimental.pallas import tpu as pltpu

F32 = jnp.float32
BF16 = jnp.bfloat16

D_MODEL = 1024
HEAD_DIM = 64
N_HEADS = 8
N_KV_HEADS = 2
GQA = N_HEADS // N_KV_HEADS
ATTN_WIDTH = N_HEADS * HEAD_DIM
KV_WIDTH = N_KV_HEADS * HEAD_DIM
ATTN_BLOCK = 128
SSM_WIDTH = D_MODEL - ATTN_WIDTH
SSM_GROUP_DIM = 16
SSM_GROUPS = SSM_WIDTH // SSM_GROUP_DIM
SSM_STATE = 64
IN_WIDTH = ATTN_WIDTH + 2 * KV_WIDTH + SSM_WIDTH
D_FF = 3584
N_EXPERTS = 8
TOP_K = 2
NORM_EPS = 1e-6
NEG_INF = -1e30

LANES = 128
ROW_TILE = 512
SSM_CHUNK = 16
SSM_SEQS = 8
SSM_VEC = SSM_CHUNK * SSM_GROUP_DIM
SSM_LANE_GROUPS = LANES // SSM_GROUP_DIM
FF_CHUNK = 256
MOE_ROWS = 512
VMEM_LIMIT = 56 * 1024 * 1024
BIG_VMEM_LIMIT = 60 * 1024 * 1024

NT_DIMS = (((1,), (1,)), ((), ()))


def _cparams(sem, vmem=VMEM_LIMIT):
    return pltpu.CompilerParams(dimension_semantics=sem, vmem_limit_bytes=vmem)


def _const_spec(shape):
    n = len(shape)
    return pl.BlockSpec(shape, lambda *_: (0,) * n)


def _resident_spec(shape):
    n = len(shape)
    return pl.BlockSpec(shape, lambda *_: (0,) * n, pipeline_mode=pl.Buffered(1))


def _row_tiled_call(body, steps, in_specs, out_specs, out_shape, scratch_shapes, name, args, casts=(),
                    vmem=VMEM_LIMIT):
    n_in, n_out, n_c = len(in_specs), len(out_specs), len(casts)

    def hosted(*refs):
        ins, cast_in = refs[:n_in], refs[n_in:n_in + n_c]
        outs = refs[n_in + n_c:n_in + n_c + n_out]
        cast_out = refs[n_in + n_c + n_out:n_in + 2 * n_c + n_out]
        body(*ins, *outs, *refs[n_in + 2 * n_c + n_out:])
        for src, dst in zip(cast_in, cast_out):
            dst[...] = src[...].astype(BF16)

    for w in casts:
        assert w.ndim == 2 and w.shape[0] % (steps * 16) == 0, w.shape
    cast_specs = [pl.BlockSpec((w.shape[0] // steps, w.shape[1]), lambda i: (i, 0)) for w in casts]
    res = pl.pallas_call(
        hosted,
        grid=(steps,),
        in_specs=list(in_specs) + cast_specs,
        out_specs=list(out_specs) + cast_specs,
        out_shape=list(out_shape) + [jax.ShapeDtypeStruct(w.shape, BF16) for w in casts],
        scratch_shapes=scratch_shapes,
        compiler_params=_cparams(("parallel",), vmem),
        name=name,
    )(*args, *casts)
    return res[:n_out], res[n_out:]


def _inproj_body(x_ref, g_ref, wqvt_ref, wku_ref, qg_ref, kg_ref, avg_ref,
                 qt_out, k_out, v3_out, u3_out, uscr):
    x = x_ref[...]
    ms = jnp.mean(x * x, axis=-1, keepdims=True)
    hn = (x * lax.rsqrt(ms + NORM_EPS) * g_ref[...]).astype(BF16)
    qvt = lax.dot_general(wqvt_ref[...], hn, NT_DIMS, preferred_element_type=F32)
    ku = jnp.dot(hn, wku_ref[...], preferred_element_type=F32)
    qt = qvt[:ATTN_WIDTH]
    qms = jnp.dot(avg_ref[...], (qt * qt).astype(BF16), preferred_element_type=F32)
    qt_out[...] = (qt * lax.rsqrt(qms + NORM_EPS) * qg_ref[...]).astype(BF16)
    k = ku[:, :KV_WIDTH]
    kms = jnp.dot((k * k).astype(BF16), avg_ref[:KV_WIDTH, :KV_WIDTH], preferred_element_type=F32)
    k_out[...] = (k * lax.rsqrt(kms + NORM_EPS) * kg_ref[...]).astype(BF16)
    vt = qvt[ATTN_WIDTH:]
    for b in range(ROW_TILE // ATTN_BLOCK):
        v3_out[b] = vt[:, b * ATTN_BLOCK:(b + 1) * ATTN_BLOCK].astype(BF16)
    u = ku[:, KV_WIDTH:]
    for lb in range(SSM_WIDTH // LANES):
        uscr[lb] = u[:, lb * LANES:(lb + 1) * LANES]
    for t in range(SSM_CHUNK):
        for lb in range(SSM_WIDTH // LANES):
            u3_out[t, :, lb * LANES:(lb + 1) * LANES] = (
                uscr[lb, pl.ds(t, ROW_TILE // SSM_CHUNK, stride=SSM_CHUNK), :].astype(BF16))


def _inproj(x2d, g, wqvt, wku, qg, kg, avg, casts=()):
    t = x2d.shape[0]
    cpt = ROW_TILE // SSM_CHUNK
    return _row_tiled_call(
        _inproj_body, t // ROW_TILE,
        in_specs=[pl.BlockSpec((ROW_TILE, D_MODEL), lambda i: (i, 0)),
                  _const_spec((1, D_MODEL)), _const_spec((ATTN_WIDTH + KV_WIDTH, D_MODEL)),
                  _const_spec((D_MODEL, KV_WIDTH + SSM_WIDTH)), _const_spec((ATTN_WIDTH, 1)),
                  _const_spec((1, KV_WIDTH)), _const_spec((ATTN_WIDTH, ATTN_WIDTH))],
        out_specs=[pl.BlockSpec((ATTN_WIDTH, ROW_TILE), lambda i: (0, i)),
                   pl.BlockSpec((ROW_TILE, KV_WIDTH), lambda i: (i, 0)),
                   pl.BlockSpec((ROW_TILE // ATTN_BLOCK, KV_WIDTH, ATTN_BLOCK), lambda i: (i, 0, 0)),
                   pl.BlockSpec((SSM_CHUNK, cpt, SSM_WIDTH), lambda i: (0, i, 0))],
        out_shape=[jax.ShapeDtypeStruct((ATTN_WIDTH, t), BF16),
                   jax.ShapeDtypeStruct((t, KV_WIDTH), BF16),
                   jax.ShapeDtypeStruct((t // ATTN_BLOCK, KV_WIDTH, ATTN_BLOCK), BF16),
                   jax.ShapeDtypeStruct((SSM_CHUNK, t // SSM_CHUNK, SSM_WIDTH), BF16)],
        scratch_shapes=[pltpu.VMEM((SSM_WIDTH // LANES, ROW_TILE, LANES), F32)],
        name="inproj", args=(x2d, g, wqvt, wku, qg, kg, avg), casts=casts)


def _attn_body(sink_ref, qt_ref, k_ref, v3_ref, g_ref, eye_ref, o_ref, at_scr, *, blocks_per_seq):
    i = pl.program_id(0)
    key = lax.broadcasted_iota(jnp.int32, (ATTN_BLOCK, ATTN_BLOCK), 0)
    qry = lax.broadcasted_iota(jnp.int32, (ATTN_BLOCK, ATTN_BLOCK), 1)
    cur_ok = key <= qry
    zpad = jnp.zeros((HEAD_DIM, ATTN_BLOCK), BF16)
    nblk = ROW_TILE // ATTN_BLOCK
    for blk in range(nblk):
        gblk = i * nblk + blk
        pblk = jnp.maximum(gblk - 1, 0)
        row0 = pl.multiple_of(gblk * ATTN_BLOCK, ATTN_BLOCK)
        prev0 = pl.multiple_of(pblk * ATTN_BLOCK, ATTN_BLOCK)
        has_prev = (gblk % blocks_per_seq) != 0
        kc = k_ref[pl.ds(row0, ATTN_BLOCK), :]
        kp = k_ref[pl.ds(prev0, ATTN_BLOCK), :]
        vc = v3_ref[gblk]
        vp = v3_ref[pblk]
        qb = qt_ref[:, blk * ATTN_BLOCK:(blk + 1) * ATTN_BLOCK]
        keep = jnp.logical_or(cur_ok, has_prev)
        outs = []
        for h in range(N_HEADS):
            kv = h // GQA
            qh = qb[h * HEAD_DIM:(h + 1) * HEAD_DIM, :]
            qpad = jnp.concatenate([qh, zpad] if kv == 0 else [zpad, qh], axis=0)
            sc = jnp.dot(kc, qpad, preferred_element_type=F32)
            sp = jnp.dot(kp, qpad, preferred_element_type=F32)
            s = jnp.where(keep, jnp.where(cur_ok, sc, sp), NEG_INF)
            sink = sink_ref[h]
            m = jnp.maximum(jnp.max(s, axis=0, keepdims=True), sink)
            p = jnp.exp(s - m)
            den = jnp.sum(p, axis=0, keepdims=True) + jnp.exp(sink - m)
            pb = p.astype(BF16)
            zero = jnp.zeros_like(pb)
            vs = slice(kv * HEAD_DIM, (kv + 1) * HEAD_DIM)
            o = (jnp.dot(vc[vs, :], jnp.where(cur_ok, pb, zero), preferred_element_type=F32)
                 + jnp.dot(vp[vs, :], jnp.where(cur_ok, zero, pb), preferred_element_type=F32))
            outs.append(o / den)
        a = jnp.concatenate(outs, axis=0)
        ms = jnp.mean(a * a, axis=0, keepdims=True)
        at_scr[:, blk * ATTN_BLOCK:(blk + 1) * ATTN_BLOCK] = (
            a * lax.rsqrt(ms + NORM_EPS) * g_ref[...]).astype(BF16)
    o_ref[...] = lax.dot_general(eye_ref[...], at_scr[...], NT_DIMS,
                                 preferred_element_type=F32).astype(BF16)


def _attention(qt, k, v3, sinks, gcol, eye, seq, casts=()):
    t = k.shape[0]
    return _row_tiled_call(
        functools.partial(_attn_body, blocks_per_seq=seq // ATTN_BLOCK), t // ROW_TILE,
        in_specs=[pl.BlockSpec(memory_space=pltpu.SMEM),
                  pl.BlockSpec((ATTN_WIDTH, ROW_TILE), lambda i: (0, i)),
                  _const_spec((t, KV_WIDTH)), _const_spec((t // ATTN_BLOCK, KV_WIDTH, ATTN_BLOCK)),
                  _const_spec((ATTN_WIDTH, 1)), _const_spec((ROW_TILE, ROW_TILE))],
        out_specs=[pl.BlockSpec((ROW_TILE, ATTN_WIDTH), lambda i: (i, 0))],
        out_shape=[jax.ShapeDtypeStruct((t, ATTN_WIDTH), BF16)],
        scratch_shapes=[pltpu.VMEM((ATTN_WIDTH, ROW_TILE), BF16)],
        name="swa_attention", args=(sinks, qt, k, v3, gcol, eye), casts=casts)


def _ssm_tables_body(lam_ref, lamc_ref, b_ref, c_ref, erep_ref, etile_ref, eye_ref,
                     m_out, wre_out, wim_out, vre_out, vim_out, k_out, apw_out, *, n_steps):
    hi = lax.Precision.HIGHEST
    lanes = 2 * SSM_STATE
    ch, gd = SSM_CHUNK, SSM_GROUP_DIM
    lr, li, dt = lam_ref[0, 0:1, :], lam_ref[0, 1:2, :], lam_ref[0, 2:3, :]

    def apow(p):
        mag = jnp.exp(p * (lr * dt))
        ang = p * (li * dt)
        return mag * jnp.cos(ang), mag * jnp.sin(ang)

    tau = lax.broadcasted_iota(jnp.int32, (ch, lanes), 0).astype(F32)
    p0r, p0i = apow(tau)
    p1r, p1i = apow(tau + 1.0)
    base = 8
    doublings = []
    while base << len(doublings) < n_steps:
        doublings.append(ch * (base << len(doublings)))
    assert base << len(doublings) == n_steps and len(doublings) <= 6, n_steps
    sel = lax.broadcasted_iota(jnp.int32, (8, lanes), 0)
    pw = jnp.zeros((8, lanes), F32)
    for r, p in enumerate([ch, ch * n_steps] + doublings):
        pw = jnp.where(sel == r, float(p), pw)
    kr, ki = apow(pw)
    k_out[0, 0] = kr[0:2]
    k_out[0, 1] = ki[0:2]
    qr, qi = apow(sel.astype(F32) * float(ch))
    for m in range(len(doublings)):
        dr, di = kr[2 + m:3 + m], ki[2 + m:3 + m]
        qr, qi = (jnp.concatenate([qr, qr * dr - qi * di], axis=0),
                  jnp.concatenate([qi, qr * di + qi * dr], axis=0))
    apw_out[0, 0] = qr
    apw_out[0, 1] = qi

    to_col = lambda p: lax.dot_general(eye_ref[...], p, NT_DIMS, precision=hi,
                                       preferred_element_type=F32)
    c0r, c0i, c1r, c1i = to_col(p0r), to_col(p0i), to_col(p1r), to_col(p1i)
    lrc, lic = lamc_ref[0, :, 0:1], lamc_ref[0, :, 1:2]
    ab_re, ab_im = c1r[:, 0:1], c1i[:, 0:1]
    nr = ab_re - 1.0
    den = lrc * lrc + lic * lic
    f_re = (nr * lrc + ab_im * lic) / den
    f_im = (ab_im * lrc - nr * lic) / den
    br, bi = b_ref[0, 0], b_ref[0, 1]
    bb_re = f_re * br - f_im * bi
    bb_im = f_re * bi + f_im * br
    bx_re = jnp.dot(bb_re, etile_ref[...], precision=hi, preferred_element_type=F32)
    bx_im = jnp.dot(bb_im, etile_ref[...], precision=hi, preferred_element_type=F32)
    px_re = jnp.dot(c0r, erep_ref[...], precision=hi, preferred_element_type=F32)
    px_im = jnp.dot(c0i, erep_ref[...], precision=hi, preferred_element_type=F32)
    w_re = px_re * bx_re - px_im * bx_im
    w_im = px_re * bx_im + px_im * bx_re
    first = lax.broadcasted_iota(jnp.int32, (lanes, SSM_VEC), 0) < SSM_STATE
    pack = lambda w: jnp.concatenate([jnp.where(first, w, 0.0), jnp.where(first, 0.0, w)], axis=1)
    wre_out[0] = pack(w_re).astype(BF16)
    wim_out[0] = pack(w_im).astype(BF16)

    blk = lax.broadcasted_iota(jnp.int32, (SSM_VEC, SSM_VEC), 1) // gd
    for g in range(2):
        cr, ci = c_ref[g, 0], c_ref[g, 1]
        stack = lambda x: x.reshape(ch * gd, lanes)
        ca_re = stack(cr[None] * p0r[:, None, :] - ci[None] * p0i[:, None, :])
        ca_im = stack(cr[None] * p0i[:, None, :] + ci[None] * p0r[:, None, :])
        kt = (jnp.dot(ca_re, bx_re, precision=hi, preferred_element_type=F32)
              - jnp.dot(ca_im, bx_im, precision=hi, preferred_element_type=F32))
        m = jnp.where(blk == 0, kt, 0.0)
        for j in range(1, ch):
            shifted = jnp.concatenate([jnp.zeros((j * gd, SSM_VEC), F32), kt[:SSM_VEC - j * gd]], axis=0)
            m = jnp.where(blk == j, shifted, m)
        m_out[g] = m.astype(BF16)
        vre_out[g] = stack(cr[None] * p1r[:, None, :] - ci[None] * p1i[:, None, :]).astype(BF16)
        vim_out[g] = (-stack(cr[None] * p1i[:, None, :] + ci[None] * p1r[:, None, :])).astype(BF16)


def _ssm_tables(lam_re, lam_im, log_dt, b_re, b_im, c_re, c_im, d_skip, n_steps):
    g_, n_, c_ = SSM_GROUPS, SSM_STATE, SSM_GROUP_DIM
    np_ = g_ // 2
    lanes = 2 * n_
    dt = jnp.broadcast_to(jnp.exp(log_dt.astype(F32))[:, None], (g_, n_))
    lam = jnp.stack([lam_re.astype(F32), lam_im.astype(F32), dt], axis=1)
    lam_row = lam.reshape(np_, 2, 3, n_).transpose(0, 2, 1, 3).reshape(np_, 3, lanes)
    lam_col = lam_row.transpose(0, 2, 1)
    b2 = jnp.stack([b_re.astype(F32).reshape(np_, lanes, c_), b_im.astype(F32).reshape(np_, lanes, c_)],
                   axis=1)
    c2 = jnp.stack([c_re.astype(F32), c_im.astype(F32)], axis=1)
    z = jnp.zeros_like(c2)
    even = (jnp.arange(g_) % 2 == 0)[:, None, None, None]
    c2 = jnp.where(even, jnp.concatenate([c2, z], axis=3), jnp.concatenate([z, c2], axis=3))
    col = jnp.arange(SSM_VEC)
    erep = (SSM_CHUNK - 1 - col[None, :] // c_ == jnp.arange(SSM_CHUNK)[:, None]).astype(F32)
    etile = (col[None, :] % c_ == jnp.arange(c_)[:, None]).astype(F32)
    lead = lambda shape: pl.BlockSpec(shape, lambda p: (p,) + (0,) * (len(shape) - 1))
    m, wre, wim, vre, vim, consts, apw = pl.pallas_call(
        functools.partial(_ssm_tables_body, n_steps=n_steps),
        grid=(np_,),
        in_specs=[lead((1, 3, lanes)), lead((1, lanes, 3)), lead((1, 2, lanes, c_)),
                  lead((2, 2, c_, lanes)), _const_spec((SSM_CHUNK, SSM_VEC)), _const_spec((c_, SSM_VEC)),
                  _const_spec((lanes, lanes))],
        out_specs=[lead((2, SSM_VEC, SSM_VEC)), lead((1, lanes, 2 * SSM_VEC)), lead((1, lanes, 2 * SSM_VEC)),
                   lead((2, SSM_VEC, lanes)), lead((2, SSM_VEC, lanes)),
                   lead((1, 2, 2, lanes)), lead((1, 2, n_steps, lanes))],
        out_shape=[jax.ShapeDtypeStruct((g_, SSM_VEC, SSM_VEC), BF16),
                   jax.ShapeDtypeStruct((np_, lanes, 2 * SSM_VEC), BF16),
                   jax.ShapeDtypeStruct((np_, lanes, 2 * SSM_VEC), BF16),
                   jax.ShapeDtypeStruct((g_, SSM_VEC, lanes), BF16),
                   jax.ShapeDtypeStruct((g_, SSM_VEC, lanes), BF16),
                   jax.ShapeDtypeStruct((np_, 2, 2, lanes), F32),
                   jax.ShapeDtypeStruct((np_, 2, n_steps, lanes), F32)],
        compiler_params=_cparams(("parallel",)),
        name="s5_operators",
    )(lam_row, lam_col, b2, c2, erep, etile, jnp.eye(lanes, dtype=F32))
    dcol = jnp.tile(d_skip.astype(F32).reshape(g_, 1, c_), (1, SSM_CHUNK, 1)).reshape(g_, SSM_VEC, 1)
    return m, wre, wim, vre, vim, consts, apw, dcol


def _ssm_body(u_ref, eye_ref, m_ref, wre_ref, wim_ref, vre_ref, vim_ref, c_ref, apw_ref, d_ref,
              y_ref, ut, yt, pre, pim, sre, sim, *, n_steps, halves, pitch):
    lanes = 2 * SSM_STATE
    nk = u_ref.shape[1]
    gd = SSM_GROUP_DIM
    for t in range(SSM_CHUNK):
        xt = lax.dot_general(eye_ref[...], u_ref[t], NT_DIMS,
                             preferred_element_type=F32).astype(BF16)
        for g in range(SSM_LANE_GROUPS):
            ut[g, t * gd:(t + 1) * gd, :] = xt[g * gd:(g + 1) * gd, :]

    row = lax.broadcasted_iota(jnp.int32, (SSM_SEQS, lanes), 0)
    seq_start = (row % halves) == 0
    shift = lambda x: jnp.where(seq_start, 0.0, pltpu.roll(x, 1, axis=0))

    npairs = SSM_LANE_GROUPS // 2
    for pp in range(npairs):
        ucat = jnp.concatenate([ut[2 * pp], ut[2 * pp + 1]], axis=0)
        p_re = jnp.dot(wre_ref[pp], ucat, preferred_element_type=F32).T
        p_im = jnp.dot(wim_ref[pp], ucat, preferred_element_type=F32).T
        if pitch == n_steps:
            pre[pp] = p_re
            pim[pp] = p_im
        else:
            for s in range(SSM_SEQS):
                pre[pp, s * pitch:s * pitch + n_steps, :] = p_re[s * n_steps:(s + 1) * n_steps]
                pim[pp, s * pitch:s * pitch + n_steps, :] = p_im[s * n_steps:(s + 1) * n_steps]

    a16 = [(jnp.broadcast_to(c_ref[pp, 0, 0:1, :], (SSM_SEQS, lanes)),
            jnp.broadcast_to(c_ref[pp, 1, 0:1, :], (SSM_SEQS, lanes))) for pp in range(npairs)]

    def step(i, carry):
        idx = pl.ds(i, SSM_SEQS, stride=pitch)
        out = []
        for pp in range(npairs):
            s_re, s_im = carry[pp]
            ar, ai = a16[pp]
            sre[pp, idx, :] = s_re
            sim[pp, idx, :] = s_im
            out.append((ar * s_re - ai * s_im + pre[pp, idx, :],
                        ar * s_im + ai * s_re + pim[pp, idx, :]))
        return tuple(out)

    zero = jnp.zeros((SSM_SEQS, lanes), F32)
    ends = lax.fori_loop(0, n_steps, step, ((zero, zero),) * npairs, unroll=2)

    for pp in range(npairs):
        e_re, e_im = ends[pp]
        br = jnp.broadcast_to(c_ref[pp, 0, 1:2, :], (SSM_SEQS, lanes))
        bi = jnp.broadcast_to(c_ref[pp, 1, 1:2, :], (SSM_SEQS, lanes))
        c_re = jnp.zeros_like(e_re)
        c_im = jnp.zeros_like(e_im)
        for _ in range(halves - 1):
            n_re = e_re + br * c_re - bi * c_im
            n_im = e_im + br * c_im + bi * c_re
            c_re, c_im = shift(n_re), shift(n_im)

        apr = apw_ref[pp, 0][None, :, :]
        api = apw_ref[pp, 1][None, :, :]
        if pitch == n_steps:
            s3r = sre[pp].reshape(SSM_SEQS, n_steps, lanes)
            s3i = sim[pp].reshape(SSM_SEQS, n_steps, lanes)
        else:
            s3r = jnp.stack([sre[pp, s * pitch:s * pitch + n_steps, :] for s in range(SSM_SEQS)])
            s3i = jnp.stack([sim[pp, s * pitch:s * pitch + n_steps, :] for s in range(SSM_SEQS)])
        t_re = (s3r + apr * c_re[:, None, :] - api * c_im[:, None, :]).reshape(nk, lanes)
        t_im = (s3i + apr * c_im[:, None, :] + api * c_re[:, None, :]).reshape(nk, lanes)
        tr_hi = t_re.astype(BF16)
        tr_lo = (t_re - tr_hi.astype(F32)).astype(BF16)
        ti_hi = t_im.astype(BF16)
        ti_lo = (t_im - ti_hi.astype(F32)).astype(BF16)
        for g in (2 * pp, 2 * pp + 1):
            ug = ut[g]
            y = jnp.dot(m_ref[g], ug, preferred_element_type=F32)
            y += lax.dot_general(vre_ref[g], tr_hi, NT_DIMS, preferred_element_type=F32)
            y += lax.dot_general(vre_ref[g], tr_lo, NT_DIMS, preferred_element_type=F32)
            y += lax.dot_general(vim_ref[g], ti_hi, NT_DIMS, preferred_element_type=F32)
            y += lax.dot_general(vim_ref[g], ti_lo, NT_DIMS, preferred_element_type=F32)
            yt[g] = y + d_ref[g] * ug.astype(F32)

    for t in range(SSM_CHUNK):
        rows = jnp.concatenate([yt[g, t * gd:(t + 1) * gd, :] for g in range(SSM_LANE_GROUPS)], axis=0)
        y_ref[t] = rows.T


def _ssm(u3, tables, eye, bsz, seq):
    m, wre, wim, vre, vim, consts, apw, dcol = tables
    nk = u3.shape[1]
    halves = SSM_SEQS // bsz
    n_steps = seq // (SSM_CHUNK * halves)
    lanes = 2 * SSM_STATE
    lg = SSM_LANE_GROUPS
    pitch = n_steps + 8 if (n_steps // 8) % 2 == 0 else n_steps
    lead = lambda shape: pl.BlockSpec(shape, lambda b: (b,) + (0,) * (len(shape) - 1))
    return pl.pallas_call(
        functools.partial(_ssm_body, n_steps=n_steps, halves=halves, pitch=pitch),
        grid=(SSM_GROUPS // lg,),
        in_specs=[pl.BlockSpec((SSM_CHUNK, nk, LANES), lambda b: (0, 0, b)),
                  _const_spec((LANES, LANES)),
                  lead((lg, SSM_VEC, SSM_VEC)),
                  lead((lg // 2, lanes, 2 * SSM_VEC)), lead((lg // 2, lanes, 2 * SSM_VEC)),
                  lead((lg, SSM_VEC, lanes)), lead((lg, SSM_VEC, lanes)),
                  lead((lg // 2, 2, 2, lanes)), lead((lg // 2, 2, n_steps, lanes)),
                  lead((lg, SSM_VEC, 1))],
        out_specs=pl.BlockSpec((SSM_CHUNK, nk, LANES), lambda b: (0, 0, b)),
        out_shape=jax.ShapeDtypeStruct((SSM_CHUNK, nk, SSM_WIDTH), F32),
        scratch_shapes=[pltpu.VMEM((lg, SSM_VEC, nk), BF16), pltpu.VMEM((lg, SSM_VEC, nk), F32)]
                       + [pltpu.VMEM((lg // 2, SSM_SEQS * pitch, lanes), F32)] * 4,
        compiler_params=_cparams(("parallel",)),
        name="s5_ssm",
    )(u3, eye, m, wre, wim, vre, vim, consts, apw, dcol)


def _mix_math(x_ref, a_ref, y3_ref, wglu_ref, sg_ref, wo_ref, fg_ref, yscr):
    nlb = SSM_WIDTH // LANES
    for t in range(SSM_CHUNK):
        for lb in range(nlb):
            yscr[lb, pl.ds(t, ROW_TILE // SSM_CHUNK, stride=SSM_CHUNK), :] = (
                y3_ref[t, :, lb * LANES:(lb + 1) * LANES])
    y = jax.nn.gelu(jnp.concatenate([yscr[lb] for lb in range(nlb)], axis=1))
    z = y * jax.nn.sigmoid(jnp.dot(y.astype(BF16), wglu_ref[...], preferred_element_type=F32))
    ms = jnp.mean(z * z, axis=-1, keepdims=True)
    sn = (z * lax.rsqrt(ms + NORM_EPS) * sg_ref[...]).astype(BF16)
    x1 = (x_ref[...]
          + jnp.dot(a_ref[...], wo_ref[:ATTN_WIDTH, :], preferred_element_type=F32)
          + jnp.dot(sn, wo_ref[ATTN_WIDTH:, :], preferred_element_type=F32))
    ms1 = jnp.mean(x1 * x1, axis=-1, keepdims=True)
    return x1, x1 * lax.rsqrt(ms1 + NORM_EPS) * fg_ref[...]


def _mix_router_body(x_ref, a_ref, y3_ref, wglu_ref, sg_ref, wo_ref, fg_ref, rwt_ref,
                     x1_out, lg_out, yscr):
    x1, hn = _mix_math(x_ref, a_ref, y3_ref, wglu_ref, sg_ref, wo_ref, fg_ref, yscr)
    x1_out[...] = x1
    lane = lax.broadcasted_iota(jnp.int32, lg_out.shape, 1)
    lg = jnp.zeros(lg_out.shape, F32)
    for e in range(lg_out.shape[1]):
        lg = jnp.where(lane == e, jnp.sum(hn * rwt_ref[e:e + 1, :], axis=-1, keepdims=True), lg)
    lg_out[...] = lg


def _mix_ffn_body(x_ref, a_ref, y3_ref, wglu_ref, sg_ref, wo_ref, fg_ref, wg_ref, wu_ref, wd_ref,
                  o_ref, yscr):
    x1, hn = _mix_math(x_ref, a_ref, y3_ref, wglu_ref, sg_ref, wo_ref, fg_ref, yscr)
    h = hn.astype(BF16)
    o_ref[...] = x1
    for c in range(D_FF // FF_CHUNK):
        sl = slice(c * FF_CHUNK, (c + 1) * FF_CHUNK)
        g = jnp.dot(h, wg_ref[:, sl], preferred_element_type=F32)
        u = jnp.dot(h, wu_ref[:, sl], preferred_element_type=F32)
        a = (jax.nn.silu(g) * u).astype(BF16)
        o_ref[...] += jnp.dot(a, wd_ref[sl, :], preferred_element_type=F32)


def _mix_specs():
    row = lambda w: pl.BlockSpec((ROW_TILE, w), lambda i: (i, 0))
    in_specs = [row(D_MODEL), row(ATTN_WIDTH),
                pl.BlockSpec((SSM_CHUNK, ROW_TILE // SSM_CHUNK, SSM_WIDTH), lambda i: (0, i, 0)),
                _resident_spec((SSM_WIDTH, SSM_WIDTH)), _const_spec((1, SSM_WIDTH)),
                _resident_spec((D_MODEL, D_MODEL)), _const_spec((1, D_MODEL))]
    return row, in_specs, pltpu.VMEM((SSM_WIDTH // LANES, ROW_TILE, LANES), F32)


def _mix_router(x2d, attn, y3, wglu, sg, wo, fg, rwt):
    t = x2d.shape[0]
    ne = rwt.shape[0]
    row, in_specs, yscr = _mix_specs()
    return pl.pallas_call(
        _mix_router_body,
        grid=(t // ROW_TILE,),
        in_specs=in_specs + [_const_spec((ne, D_MODEL))],
        out_specs=[row(D_MODEL), row(ne)],
        out_shape=[jax.ShapeDtypeStruct((t, D_MODEL), F32), jax.ShapeDtypeStruct((t, ne), F32)],
        scratch_shapes=[yscr],
        compiler_params=_cparams(("parallel",)),
        name="mix_router",
    )(x2d, attn, y3, wglu, sg, wo, fg, rwt)


def _mix_ffn(x2d, attn, y3, wglu, sg, wo, fg, wg, wu, wd, casts=()):
    t = x2d.shape[0]
    row, in_specs, yscr = _mix_specs()
    return _row_tiled_call(
        _mix_ffn_body, t // ROW_TILE,
        in_specs=in_specs + [_resident_spec((D_MODEL, D_FF)), _resident_spec((D_MODEL, D_FF)),
                             _resident_spec((D_FF, D_MODEL))],
        out_specs=[row(D_MODEL)],
        out_shape=[jax.ShapeDtypeStruct((t, D_MODEL), F32)],
        scratch_shapes=[yscr],
        name="mix_dense_ffn", args=(x2d, attn, y3, wglu, sg, wo, fg, wg, wu, wd), casts=casts,
        vmem=BIG_VMEM_LIMIT)


def _expert_ffn_body(e_ref, nused_ref, nvalid_ref, tok_ref, dst_ref, x1_hbm, fg_ref,
                     wg_ref, wu_ref, wd_ref, out_hbm, xbuf, acc, ybuf, gsem, ssem):
    b = pl.program_id(0)
    n_used = nused_ref[0]
    n_ff = D_FF // FF_CHUNK
    rows_per_chunk = -(-MOE_ROWS // n_ff)

    def gather_row(blk, r):
        i = tok_ref[blk * MOE_ROWS + r]
        pltpu.make_async_copy(x1_hbm.at[pl.ds(i, 1)], xbuf.at[pl.ds(r, 1)], gsem).start()

    def scatter_row(blk, r):
        i = dst_ref[blk * MOE_ROWS + r]
        pltpu.make_async_copy(ybuf.at[pl.ds(r, 1)], out_hbm.at[pl.ds(i, 1)], ssem).start()

    def scatter_rows(blk, n):
        def body(r, carry):
            scatter_row(blk, r)
            return carry
        lax.fori_loop(0, n, body, 0)

    def wait_gather():
        pltpu.make_async_copy(x1_hbm.at[pl.ds(0, MOE_ROWS)], xbuf, gsem).wait()

    def wait_scatter(n):
        n8 = pl.multiple_of(n & -8, 8)

        @pl.when(n8 > 0)
        def _():
            pltpu.make_async_copy(ybuf.at[pl.ds(0, n8)], out_hbm.at[pl.ds(0, n8)], ssem).wait()

        def one(_, carry):
            pltpu.make_async_copy(ybuf.at[pl.ds(0, 1)], out_hbm.at[pl.ds(0, 1)], ssem).wait()
            return carry
        lax.fori_loop(0, n - n8, one, 0)

    def ffn(scatter_prev):
        wait_gather()
        x = xbuf[...]
        ms = jnp.mean(x * x, axis=-1, keepdims=True)
        h = (x * lax.rsqrt(ms + NORM_EPS) * fg_ref[...]).astype(BF16)
        for c in range(n_ff):
            for r in range(c * rows_per_chunk, min((c + 1) * rows_per_chunk, MOE_ROWS)):
                gather_row(b + 1, r)
                if scatter_prev:
                    scatter_row(b - 1, r)
            sl = slice(c * FF_CHUNK, (c + 1) * FF_CHUNK)
            g = jnp.dot(h, wg_ref[0, :, sl], preferred_element_type=F32)
            u = jnp.dot(h, wu_ref[0, :, sl], preferred_element_type=F32)
            a = (jax.nn.silu(g) * u).astype(BF16)
            part = jnp.dot(a, wd_ref[0, sl, :], preferred_element_type=F32)
            if c == 0:
                acc[...] = part
            else:
                acc[...] += part

    @pl.when(b < n_used)
    def _():
        @pl.when(b == 0)
        def _():
            def body(r, carry):
                gather_row(0, r)
                return carry
            lax.fori_loop(0, MOE_ROWS, body, 0, unroll=8)

        n_prev = jnp.where(b >= 1, nvalid_ref[jnp.maximum(b - 1, 0)], 0)

        @pl.when(n_prev == MOE_ROWS)
        def _():
            ffn(scatter_prev=True)

        @pl.when(n_prev != MOE_ROWS)
        def _():
            scatter_rows(b - 1, n_prev)
            ffn(scatter_prev=False)

        wait_scatter(n_prev)
        ybuf[...] = acc[...]

        @pl.when(b == n_used - 1)
        def _():
            scatter_rows(b, nvalid_ref[b])
            wait_scatter(nvalid_ref[b])
            wait_gather()


def _expert_ffn(x1, fg, tok, dst, block_e, n_used, n_valid, wg, wu, wd, nblk, out_rows):
    wspec = lambda shape: pl.BlockSpec(shape, lambda b, e, *_: (e[b], 0, 0))
    return pl.pallas_call(
        _expert_ffn_body,
        grid_spec=pltpu.PrefetchScalarGridSpec(
            num_scalar_prefetch=5, grid=(nblk,),
            in_specs=[pl.BlockSpec(memory_space=pl.ANY),
                      pl.BlockSpec((1, D_MODEL), lambda b, *_: (0, 0)),
                      wspec((1, D_MODEL, D_FF)), wspec((1, D_MODEL, D_FF)), wspec((1, D_FF, D_MODEL))],
            out_specs=pl.BlockSpec(memory_space=pl.ANY),
            scratch_shapes=[pltpu.VMEM((MOE_ROWS, D_MODEL), F32)] * 3
                           + [pltpu.SemaphoreType.DMA(()), pltpu.SemaphoreType.DMA(())]),
        out_shape=jax.ShapeDtypeStruct((out_rows, D_MODEL), F32),
        compiler_params=pltpu.CompilerParams(dimension_semantics=("arbitrary",),
                                             vmem_limit_bytes=BIG_VMEM_LIMIT),
        name="moe_expert_ffn",
    )(block_e, n_used, n_valid, tok, dst, x1, fg, wg, wu, wd)


def _combine_body(x_ref, y0_ref, y1_ref, g_ref, o_ref):
    o_ref[...] = x_ref[...] + y0_ref[...] * g_ref[:, 0:1] + y1_ref[...] * g_ref[:, 1:2]


def _combine(x1, y2, gates):
    t = x1.shape[0]
    rows = 2 * ROW_TILE if t % (2 * ROW_TILE) == 0 else ROW_TILE
    nt = t // rows
    return pl.pallas_call(
        _combine_body,
        grid=(nt,),
        in_specs=[pl.BlockSpec((rows, D_MODEL), lambda i: (i, 0)),
                  pl.BlockSpec((rows, D_MODEL), lambda i: (i, 0)),
                  pl.BlockSpec((rows, D_MODEL), lambda i: (i + nt, 0)),
                  pl.BlockSpec((rows, TOP_K), lambda i: (i, 0))],
        out_specs=pl.BlockSpec((rows, D_MODEL), lambda i: (i, 0)),
        out_shape=jax.ShapeDtypeStruct((t, D_MODEL), F32),
        compiler_params=_cparams(("parallel",)),
        name="moe_combine",
    )(x1, y2, y2, gates)


def _moe(x1, fg, logits, wg, wu, wd):
    t = x1.shape[0]
    n_assign = t * TOP_K
    nblk = n_assign // MOE_ROWS + N_EXPERTS
    cap = nblk * MOE_ROWS
    top_v, top_i = lax.top_k(logits, TOP_K)
    gates = jax.nn.softmax(top_v, axis=-1)
    flat_e = top_i.reshape(-1).astype(jnp.int32)
    order = jnp.argsort(flat_e, stable=True).astype(jnp.int32)
    experts = jnp.arange(N_EXPERTS, dtype=jnp.int32)
    counts = jnp.sum((flat_e[:, None] == experts[None, :]).astype(jnp.int32), axis=0)
    start = jnp.cumsum(counts) - counts
    padded = ((counts + MOE_ROWS - 1) // MOE_ROWS) * MOE_ROWS
    pend = jnp.cumsum(padded)
    pstart = pend - padded
    p = jnp.arange(cap, dtype=jnp.int32)
    e_p = jnp.minimum(jnp.sum((p[:, None] >= pend[None, :]).astype(jnp.int32), axis=1), N_EXPERTS - 1)
    rank = p - pstart[e_p]
    valid = jnp.logical_and(rank < counts[e_p], p < pend[-1])
    a_p = order[jnp.clip(start[e_p] + rank, 0, n_assign - 1)]
    tok = jnp.where(valid, a_p // TOP_K, 0).astype(jnp.int32)
    dst = jnp.where(valid, (a_p % TOP_K) * t + a_p // TOP_K, 0).astype(jnp.int32)
    block_e = e_p[::MOE_ROWS]
    n_used = (pend[-1:] // MOE_ROWS).astype(jnp.int32)
    n_valid = jnp.sum(valid.reshape(nblk, MOE_ROWS).astype(jnp.int32), axis=1)
    y2 = _expert_ffn(x1, fg, tok, dst, block_e, n_used, n_valid, wg, wu, wd, nblk, n_assign)
    return _combine(x1, y2, gates)


def kernel(x, attn_norm_g, w_in, q_norm_g, k_norm_g, sinks, lam_re, lam_im, log_dt, b_re, b_im,
           c_re, c_im, d_skip, w_glu, attn_out_g, ssm_out_g, w_o, ffn_norm_g, dense_wg, dense_wu,
           dense_wd, router_w, moe_wg, moe_wu, moe_wd):
    bsz, seq, _ = x.shape
    depth = w_in.shape[0]
    assert SSM_SEQS % bsz == 0 and seq % (SSM_CHUNK * (SSM_SEQS // bsz)) == 0
    assert seq % ROW_TILE == 0 and ROW_TILE % ATTN_BLOCK == 0
    x2d = x.reshape(bsz * seq, D_MODEL).astype(F32)
    head = jnp.arange(ATTN_WIDTH) // HEAD_DIM
    avg = jnp.where(head[:, None] == head[None, :], 1.0 / HEAD_DIM, 0.0).astype(BF16)
    eye = jnp.eye(ROW_TILE, dtype=BF16)
    n_steps = seq // (SSM_CHUNK * (SSM_SEQS // bsz))
    row = lambda v: v.astype(F32).reshape(1, -1)
    col = lambda v: v.astype(F32).reshape(-1, 1)
    ko, vo, uo = ATTN_WIDTH, ATTN_WIDTH + KV_WIDTH, ATTN_WIDTH + 2 * KV_WIDTH
    flat = lambda w: w.astype(F32).reshape(-1, w.shape[-1])
    ahead_bf = None
    for l in range(depth):
        i = l // 2
        dense = l % 2 == 0
        qg = jnp.tile(q_norm_g[l].astype(F32), N_HEADS) * (HEAD_DIM ** -0.5)
        kg = jnp.tile(k_norm_g[l].astype(F32), N_KV_HEADS)
        w = w_in[l].astype(BF16)
        early = [flat(dense_wg[i]), flat(dense_wu[i])] if dense else [flat(moe_wu[i])]
        late = [flat(dense_wd[i])] if dense else []
        wqvt = jnp.concatenate([w[:, :ko], w[:, vo:uo]], axis=1).T
        wku = jnp.concatenate([w[:, ko:vo], w[:, uo:]], axis=1)
        (qt, k, v3, u3), early_bf = _inproj(x2d, row(attn_norm_g[l]), wqvt, wku, col(qg), row(kg), avg,
                                            casts=early)
        (attn,), late_bf = _attention(qt, k, v3, sinks[l].astype(F32), col(attn_out_g[l]), eye, seq,
                                      casts=late)
        tables = _ssm_tables(lam_re[l], lam_im[l], log_dt[l], b_re[l], b_im[l], c_re[l], c_im[l],
                             d_skip[l], n_steps)
        y3 = _ssm(u3, tables, eye[:LANES, :LANES], bsz, seq)
        mix_args = (x2d, attn, y3, w_glu[l].astype(BF16), row(ssm_out_g[l]), w_o[l].astype(BF16),
                    row(ffn_norm_g[l]))
        if dense:
            ahead = [flat(moe_wg[i]), flat(moe_wd[i])] if l + 1 < depth else []
            (x2d,), ahead_bf = _mix_ffn(*mix_args, early_bf[0], early_bf[1], late_bf[0], casts=ahead)
        else:
            x1, logits = _mix_router(*mix_args, router_w[i].astype(F32).T)
            x2d = _moe(x1, row(ffn_norm_g[l]), logits, ahead_bf[0].reshape(moe_wg[i].shape),
                       early_bf[0].reshape(moe_wu[i].shape), ahead_bf[1].reshape(moe_wd[i].shape))
    return x2d.reshape(bsz, seq, D_MODEL)
```

```python
import functools

import jax
import jax.numpy as jnp
from jax import lax
from jax.experimental import pallas as pl
from jax.experimental.pallas import tpu as pltpu

F32 = jnp.float32
BF16 = jnp.bfloat16

D_MODEL = 1024
HEAD_DIM = 64
N_HEADS = 8
N_KV_HEADS = 2
GQA = N_HEADS // N_KV_HEADS
ATTN_WIDTH = N_HEADS * HEAD_DIM
KV_WIDTH = N_KV_HEADS * HEAD_DIM
ATTN_BLOCK = 128
SSM_WIDTH = D_MODEL - ATTN_WIDTH
SSM_GROUP_DIM = 16
SSM_GROUPS = SSM_WIDTH // SSM_GROUP_DIM
SSM_STATE = 64
D_FF = 3584
N_EXPERTS = 8
TOP_K = 2
NORM_EPS = 1e-6
NEG_INF = -1e30

LANES = 128
SUBLANES = 8
ROW_TILE = 512
SSM_CHUNK = 16
SSM_SEQS = SUBLANES
SSM_VEC = SSM_CHUNK * SSM_GROUP_DIM
SSM_LANE_GROUPS = LANES // SSM_GROUP_DIM
FF_CHUNK = 256
MOE_ROWS = 512
VMEM_LIMIT = 56 * 1024 * 1024
BIG_VMEM_LIMIT = 60 * 1024 * 1024

NT_DIMS = (((1,), (1,)), ((), ()))


def _cparams(sem, vmem=VMEM_LIMIT):
    return pltpu.CompilerParams(dimension_semantics=sem, vmem_limit_bytes=vmem)


def _const_spec(shape):
    n = len(shape)
    return pl.BlockSpec(shape, lambda *_: (0,) * n)


def _resident_spec(shape):
    n = len(shape)
    return pl.BlockSpec(shape, lambda *_: (0,) * n, pipeline_mode=pl.Buffered(1))


def _row_tiled_call(body, steps, in_specs, out_specs, out_shape, scratch_shapes, name, args, casts=(),
                    vmem=VMEM_LIMIT):
    n_in, n_out, n_c = len(in_specs), len(out_specs), len(casts)

    def hosted(*refs):
        ins, cast_in = refs[:n_in], refs[n_in:n_in + n_c]
        outs = refs[n_in + n_c:n_in + n_c + n_out]
        cast_out = refs[n_in + n_c + n_out:n_in + 2 * n_c + n_out]
        body(*ins, *outs, *refs[n_in + 2 * n_c + n_out:])
        for src, dst in zip(cast_in, cast_out):
            dst[...] = src[...].astype(BF16)

    for w in casts:
        assert w.ndim == 2 and w.shape[0] % (steps * 16) == 0, w.shape
    cast_specs = [pl.BlockSpec((w.shape[0] // steps, w.shape[1]), lambda i: (i, 0)) for w in casts]
    res = pl.pallas_call(
        hosted,
        grid=(steps,),
        in_specs=list(in_specs) + cast_specs,
        out_specs=list(out_specs) + cast_specs,
        out_shape=list(out_shape) + [jax.ShapeDtypeStruct(w.shape, BF16) for w in casts],
        scratch_shapes=scratch_shapes,
        compiler_params=_cparams(("parallel",), vmem),
        name=name,
    )(*args, *casts)
    return res[:n_out], res[n_out:]


def _inproj_body(x_ref, g_ref, wqvt_ref, wku_ref, qg_ref, kg_ref, avg_ref,
                 qt_out, k_out, v3_out, u3_out, uscr):
    x = x_ref[...]
    ms = jnp.mean(x * x, axis=-1, keepdims=True)
    hn = (x * lax.rsqrt(ms + NORM_EPS) * g_ref[...]).astype(BF16)
    qvt = lax.dot_general(wqvt_ref[...], hn, NT_DIMS, preferred_element_type=F32)
    ku = jnp.dot(hn, wku_ref[...], preferred_element_type=F32)
    qt = qvt[:ATTN_WIDTH]
    qms = jnp.dot(avg_ref[...], (qt * qt).astype(BF16), preferred_element_type=F32)
    qt_out[...] = (qt * lax.rsqrt(qms + NORM_EPS) * qg_ref[...]).astype(BF16)
    k = ku[:, :KV_WIDTH]
    kms = jnp.dot((k * k).astype(BF16), avg_ref[:KV_WIDTH, :KV_WIDTH], preferred_element_type=F32)
    k_out[...] = (k * lax.rsqrt(kms + NORM_EPS) * kg_ref[...]).astype(BF16)
    vt = qvt[ATTN_WIDTH:]
    for b in range(ROW_TILE // ATTN_BLOCK):
        v3_out[b] = vt[:, b * ATTN_BLOCK:(b + 1) * ATTN_BLOCK].astype(BF16)
    u = ku[:, KV_WIDTH:]
    for lb in range(SSM_WIDTH // LANES):
        uscr[lb] = u[:, lb * LANES:(lb + 1) * LANES]
    for t in range(SSM_CHUNK):
        for lb in range(SSM_WIDTH // LANES):
            u3_out[t, :, lb * LANES:(lb + 1) * LANES] = (
                uscr[lb, pl.ds(t, ROW_TILE // SSM_CHUNK, stride=SSM_CHUNK), :].astype(BF16))


def _inproj(x2d, g, wqvt, wku, qg, kg, avg, casts=()):
    t = x2d.shape[0]
    cpt = ROW_TILE // SSM_CHUNK
    return _row_tiled_call(
        _inproj_body, t // ROW_TILE,
        in_specs=[pl.BlockSpec((ROW_TILE, D_MODEL), lambda i: (i, 0)),
                  _const_spec((1, D_MODEL)), _const_spec((ATTN_WIDTH + KV_WIDTH, D_MODEL)),
                  _const_spec((D_MODEL, KV_WIDTH + SSM_WIDTH)), _const_spec((ATTN_WIDTH, 1)),
                  _const_spec((1, KV_WIDTH)), _const_spec((ATTN_WIDTH, ATTN_WIDTH))],
        out_specs=[pl.BlockSpec((ATTN_WIDTH, ROW_TILE), lambda i: (0, i)),
                   pl.BlockSpec((ROW_TILE, KV_WIDTH), lambda i: (i, 0)),
                   pl.BlockSpec((ROW_TILE // ATTN_BLOCK, KV_WIDTH, ATTN_BLOCK), lambda i: (i, 0, 0)),
                   pl.BlockSpec((SSM_CHUNK, cpt, SSM_WIDTH), lambda i: (0, i, 0))],
        out_shape=[jax.ShapeDtypeStruct((ATTN_WIDTH, t), BF16),
                   jax.ShapeDtypeStruct((t, KV_WIDTH), BF16),
                   jax.ShapeDtypeStruct((t // ATTN_BLOCK, KV_WIDTH, ATTN_BLOCK), BF16),
                   jax.ShapeDtypeStruct((SSM_CHUNK, t // SSM_CHUNK, SSM_WIDTH), BF16)],
        scratch_shapes=[pltpu.VMEM((SSM_WIDTH // LANES, ROW_TILE, LANES), F32)],
        name="inproj", args=(x2d, g, wqvt, wku, qg, kg, avg), casts=casts)


def _attn_body(sink_ref, qt_ref, k_ref, v3_ref, g_ref, eye_ref, o_ref, at_scr, *, blocks_per_seq):
    i = pl.program_id(0)
    key = lax.broadcasted_iota(jnp.int32, (ATTN_BLOCK, ATTN_BLOCK), 0)
    qry = lax.broadcasted_iota(jnp.int32, (ATTN_BLOCK, ATTN_BLOCK), 1)
    cur_ok = key <= qry
    zpad = jnp.zeros((HEAD_DIM, ATTN_BLOCK), BF16)
    nblk = ROW_TILE // ATTN_BLOCK
    for blk in range(nblk):
        gblk = i * nblk + blk
        pblk = jnp.maximum(gblk - 1, 0)
        row0 = pl.multiple_of(gblk * ATTN_BLOCK, ATTN_BLOCK)
        prev0 = pl.multiple_of(pblk * ATTN_BLOCK, ATTN_BLOCK)
        has_prev = (gblk % blocks_per_seq) != 0
        kc = k_ref[pl.ds(row0, ATTN_BLOCK), :]
        kp = k_ref[pl.ds(prev0, ATTN_BLOCK), :]
        vc = v3_ref[gblk]
        vp = v3_ref[pblk]
        qb = qt_ref[:, blk * ATTN_BLOCK:(blk + 1) * ATTN_BLOCK]
        keep = jnp.logical_or(cur_ok, has_prev)
        outs = []
        for h in range(N_HEADS):
            kv = h // GQA
            qh = qb[h * HEAD_DIM:(h + 1) * HEAD_DIM, :]
            qpad = jnp.concatenate([qh, zpad] if kv == 0 else [zpad, qh], axis=0)
            sc = jnp.dot(kc, qpad, preferred_element_type=F32)
            sp = jnp.dot(kp, qpad, preferred_element_type=F32)
            s = jnp.where(keep, jnp.where(cur_ok, sc, sp), NEG_INF)
            sink = sink_ref[h]
            m = jnp.maximum(jnp.max(s, axis=0, keepdims=True), sink)
            p = jnp.exp(s - m)
            den = jnp.sum(p, axis=0, keepdims=True) + jnp.exp(sink - m)
            pb = p.astype(BF16)
            zero = jnp.zeros_like(pb)
            vs = slice(kv * HEAD_DIM, (kv + 1) * HEAD_DIM)
            o = (jnp.dot(vc[vs, :], jnp.where(cur_ok, pb, zero), preferred_element_type=F32)
                 + jnp.dot(vp[vs, :], jnp.where(cur_ok, zero, pb), preferred_element_type=F32))
            outs.append(o / den)
        a = jnp.concatenate(outs, axis=0)
        ms = jnp.mean(a * a, axis=0, keepdims=True)
        at_scr[:, blk * ATTN_BLOCK:(blk + 1) * ATTN_BLOCK] = (
            a * lax.rsqrt(ms + NORM_EPS) * g_ref[...]).astype(BF16)
    o_ref[...] = lax.dot_general(eye_ref[...], at_scr[...], NT_DIMS,
                                 preferred_element_type=F32).astype(BF16)


def _attention(qt, k, v3, sinks, gcol, eye, seq, casts=()):
    t = k.shape[0]
    return _row_tiled_call(
        functools.partial(_attn_body, blocks_per_seq=seq // ATTN_BLOCK), t // ROW_TILE,
        in_specs=[pl.BlockSpec(memory_space=pltpu.SMEM),
                  pl.BlockSpec((ATTN_WIDTH, ROW_TILE), lambda i: (0, i)),
                  _const_spec((t, KV_WIDTH)), _const_spec((t // ATTN_BLOCK, KV_WIDTH, ATTN_BLOCK)),
                  _const_spec((ATTN_WIDTH, 1)), _const_spec((ROW_TILE, ROW_TILE))],
        out_specs=[pl.BlockSpec((ROW_TILE, ATTN_WIDTH), lambda i: (i, 0))],
        out_shape=[jax.ShapeDtypeStruct((t, ATTN_WIDTH), BF16)],
        scratch_shapes=[pltpu.VMEM((ATTN_WIDTH, ROW_TILE), BF16)],
        name="swa_attention", args=(sinks, qt, k, v3, gcol, eye), casts=casts)


def _ssm_tables_body(lam_ref, lamc_ref, b_ref, c_ref, erep_ref, etile_ref, eye_ref,
                     m_out, wre_out, wim_out, vre_out, vim_out, k_out, apw_out, *, n_steps):
    hi = lax.Precision.HIGHEST
    lanes = 2 * SSM_STATE
    ch, gd = SSM_CHUNK, SSM_GROUP_DIM
    lr, li, dt = lam_ref[0, 0:1, :], lam_ref[0, 1:2, :], lam_ref[0, 2:3, :]

    def apow(p):
        mag = jnp.exp(p * (lr * dt))
        ang = p * (li * dt)
        return mag * jnp.cos(ang), mag * jnp.sin(ang)

    tau = lax.broadcasted_iota(jnp.int32, (ch, lanes), 0).astype(F32)
    p0r, p0i = apow(tau)
    p1r, p1i = apow(tau + 1.0)
    base = SUBLANES
    doublings = []
    while base << len(doublings) < n_steps:
        doublings.append(ch * (base << len(doublings)))
    assert base << len(doublings) == n_steps and len(doublings) <= SUBLANES - 2, n_steps
    sel = lax.broadcasted_iota(jnp.int32, (SUBLANES, lanes), 0)
    pw = jnp.zeros((SUBLANES, lanes), F32)
    for r, p in enumerate([ch, ch * n_steps] + doublings):
        pw = jnp.where(sel == r, float(p), pw)
    kr, ki = apow(pw)
    k_out[0, 0] = kr[0:2]
    k_out[0, 1] = ki[0:2]
    qr, qi = apow(sel.astype(F32) * float(ch))
    for m in range(len(doublings)):
        dr, di = kr[2 + m:3 + m], ki[2 + m:3 + m]
        qr, qi = (jnp.concatenate([qr, qr * dr - qi * di], axis=0),
                  jnp.concatenate([qi, qr * di + qi * dr], axis=0))
    apw_out[0, 0] = qr
    apw_out[0, 1] = qi

    to_col = lambda p: lax.dot_general(eye_ref[...], p, NT_DIMS, precision=hi,
                                       preferred_element_type=F32)
    c0r, c0i, c1r, c1i = to_col(p0r), to_col(p0i), to_col(p1r), to_col(p1i)
    lrc, lic = lamc_ref[0, :, 0:1], lamc_ref[0, :, 1:2]
    ab_re, ab_im = c1r[:, 0:1], c1i[:, 0:1]
    nr = ab_re - 1.0
    den = lrc * lrc + lic * lic
    f_re = (nr * lrc + ab_im * lic) / den
    f_im = (ab_im * lrc - nr * lic) / den
    br, bi = b_ref[0, 0], b_ref[0, 1]
    bb_re = f_re * br - f_im * bi
    bb_im = f_re * bi + f_im * br
    bx_re = jnp.dot(bb_re, etile_ref[...], precision=hi, preferred_element_type=F32)
    bx_im = jnp.dot(bb_im, etile_ref[...], precision=hi, preferred_element_type=F32)
    px_re = jnp.dot(c0r, erep_ref[...], precision=hi, preferred_element_type=F32)
    px_im = jnp.dot(c0i, erep_ref[...], precision=hi, preferred_element_type=F32)
    w_re = px_re * bx_re - px_im * bx_im
    w_im = px_re * bx_im + px_im * bx_re
    first = lax.broadcasted_iota(jnp.int32, (lanes, SSM_VEC), 0) < SSM_STATE
    pack = lambda w: jnp.concatenate([jnp.where(first, w, 0.0), jnp.where(first, 0.0, w)], axis=1)
    wre_out[0] = pack(w_re).astype(BF16)
    wim_out[0] = pack(w_im).astype(BF16)

    blk = lax.broadcasted_iota(jnp.int32, (SSM_VEC, SSM_VEC), 1) // gd
    for g in range(2):
        cr, ci = c_ref[g, 0], c_ref[g, 1]
        stack = lambda x: x.reshape(ch * gd, lanes)
        ca_re = stack(cr[None] * p0r[:, None, :] - ci[None] * p0i[:, None, :])
        ca_im = stack(cr[None] * p0i[:, None, :] + ci[None] * p0r[:, None, :])
        kt = (jnp.dot(ca_re, bx_re, precision=hi, preferred_element_type=F32)
              - jnp.dot(ca_im, bx_im, precision=hi, preferred_element_type=F32))
        m = jnp.where(blk == 0, kt, 0.0)
        for j in range(1, ch):
            shifted = jnp.concatenate([jnp.zeros((j * gd, SSM_VEC), F32), kt[:SSM_VEC - j * gd]], axis=0)
            m = jnp.where(blk == j, shifted, m)
        m_out[g] = m.astype(BF16)
        vre_out[g] = stack(cr[None] * p1r[:, None, :] - ci[None] * p1i[:, None, :]).astype(BF16)
        vim_out[g] = (-stack(cr[None] * p1i[:, None, :] + ci[None] * p1r[:, None, :])).astype(BF16)


def _ssm_tables(lam_re, lam_im, log_dt, b_re, b_im, c_re, c_im, d_skip, n_steps):
    g_, n_, c_ = SSM_GROUPS, SSM_STATE, SSM_GROUP_DIM
    np_ = g_ // 2
    lanes = 2 * n_
    dt = jnp.broadcast_to(jnp.exp(log_dt.astype(F32))[:, None], (g_, n_))
    lam = jnp.stack([lam_re.astype(F32), lam_im.astype(F32), dt], axis=1)
    lam_row = lam.reshape(np_, 2, 3, n_).transpose(0, 2, 1, 3).reshape(np_, 3, lanes)
    lam_col = lam_row.transpose(0, 2, 1)
    b2 = jnp.stack([b_re.astype(F32).reshape(np_, lanes, c_), b_im.astype(F32).reshape(np_, lanes, c_)],
                   axis=1)
    c2 = jnp.stack([c_re.astype(F32), c_im.astype(F32)], axis=1)
    z = jnp.zeros_like(c2)
    even = (jnp.arange(g_) % 2 == 0)[:, None, None, None]
    c2 = jnp.where(even, jnp.concatenate([c2, z], axis=3), jnp.concatenate([z, c2], axis=3))
    col = jnp.arange(SSM_VEC)
    erep = (SSM_CHUNK - 1 - col[None, :] // c_ == jnp.arange(SSM_CHUNK)[:, None]).astype(F32)
    etile = (col[None, :] % c_ == jnp.arange(c_)[:, None]).astype(F32)
    lead = lambda shape: pl.BlockSpec(shape, lambda p: (p,) + (0,) * (len(shape) - 1))
    m, wre, wim, vre, vim, consts, apw = pl.pallas_call(
        functools.partial(_ssm_tables_body, n_steps=n_steps),
        grid=(np_,),
        in_specs=[lead((1, 3, lanes)), lead((1, lanes, 3)), lead((1, 2, lanes, c_)),
                  lead((2, 2, c_, lanes)), _const_spec((SSM_CHUNK, SSM_VEC)), _const_spec((c_, SSM_VEC)),
                  _const_spec((lanes, lanes))],
        out_specs=[lead((2, SSM_VEC, SSM_VEC)), lead((1, lanes, 2 * SSM_VEC)), lead((1, lanes, 2 * SSM_VEC)),
                   lead((2, SSM_VEC, lanes)), lead((2, SSM_VEC, lanes)),
                   lead((1, 2, 2, lanes)), lead((1, 2, n_steps, lanes))],
        out_shape=[jax.ShapeDtypeStruct((g_, SSM_VEC, SSM_VEC), BF16),
                   jax.ShapeDtypeStruct((np_, lanes, 2 * SSM_VEC), BF16),
                   jax.ShapeDtypeStruct((np_, lanes, 2 * SSM_VEC), BF16),
                   jax.ShapeDtypeStruct((g_, SSM_VEC, lanes), BF16),
                   jax.ShapeDtypeStruct((g_, SSM_VEC, lanes), BF16),
                   jax.ShapeDtypeStruct((np_, 2, 2, lanes), F32),
                   jax.ShapeDtypeStruct((np_, 2, n_steps, lanes), F32)],
        compiler_params=_cparams(("parallel",)),
        name="s5_operators",
    )(lam_row, lam_col, b2, c2, erep, etile, jnp.eye(lanes, dtype=F32))
    dcol = jnp.tile(d_skip.astype(F32).reshape(g_, 1, c_), (1, SSM_CHUNK, 1)).reshape(g_, SSM_VEC, 1)
    return m, wre, wim, vre, vim, consts, apw, dcol


def _ssm_body(u_ref, eye_ref, m_ref, wre_ref, wim_ref, vre_ref, vim_ref, c_ref, apw_ref, d_ref,
              y_ref, ut, yt, pre, pim, sre, sim, *, n_steps, halves, pitch):
    lanes = 2 * SSM_STATE
    nk = u_ref.shape[1]
    gd = SSM_GROUP_DIM
    for t in range(SSM_CHUNK):
        xt = lax.dot_general(eye_ref[...], u_ref[t], NT_DIMS,
                             preferred_element_type=F32).astype(BF16)
        for g in range(SSM_LANE_GROUPS):
            ut[g, t * gd:(t + 1) * gd, :] = xt[g * gd:(g + 1) * gd, :]

    row = lax.broadcasted_iota(jnp.int32, (SSM_SEQS, lanes), 0)
    seq_start = (row % halves) == 0
    shift = lambda x: jnp.where(seq_start, 0.0, pltpu.roll(x, 1, axis=0))

    npairs = SSM_LANE_GROUPS // 2
    for pp in range(npairs):
        ucat = jnp.concatenate([ut[2 * pp], ut[2 * pp + 1]], axis=0)
        p_re = jnp.dot(wre_ref[pp], ucat, preferred_element_type=F32).T
        p_im = jnp.dot(wim_ref[pp], ucat, preferred_element_type=F32).T
        if pitch == n_steps:
            pre[pp] = p_re
            pim[pp] = p_im
        else:
            for s in range(SSM_SEQS):
                pre[pp, s * pitch:s * pitch + n_steps, :] = p_re[s * n_steps:(s + 1) * n_steps]
                pim[pp, s * pitch:s * pitch + n_steps, :] = p_im[s * n_steps:(s + 1) * n_steps]

    a16 = [(jnp.broadcast_to(c_ref[pp, 0, 0:1, :], (SSM_SEQS, lanes)),
            jnp.broadcast_to(c_ref[pp, 1, 0:1, :], (SSM_SEQS, lanes))) for pp in range(npairs)]

    def step(i, carry):
        idx = pl.ds(i, SSM_SEQS, stride=pitch)
        out = []
        for pp in range(npairs):
            s_re, s_im = carry[pp]
            ar, ai = a16[pp]
            sre[pp, idx, :] = s_re
            sim[pp, idx, :] = s_im
            out.append((ar * s_re - ai * s_im + pre[pp, idx, :],
                        ar * s_im + ai * s_re + pim[pp, idx, :]))
        return tuple(out)

    zero = jnp.zeros((SSM_SEQS, lanes), F32)
    ends = lax.fori_loop(0, n_steps, step, ((zero, zero),) * npairs, unroll=2)

    for pp in range(npairs):
        e_re, e_im = ends[pp]
        br = jnp.broadcast_to(c_ref[pp, 0, 1:2, :], (SSM_SEQS, lanes))
        bi = jnp.broadcast_to(c_ref[pp, 1, 1:2, :], (SSM_SEQS, lanes))
        c_re = jnp.zeros_like(e_re)
        c_im = jnp.zeros_like(e_im)
        for _ in range(halves - 1):
            n_re = e_re + br * c_re - bi * c_im
            n_im = e_im + br * c_im + bi * c_re
            c_re, c_im = shift(n_re), shift(n_im)

        apr = apw_ref[pp, 0][None, :, :]
        api = apw_ref[pp, 1][None, :, :]
        if pitch == n_steps:
            s3r = sre[pp].reshape(SSM_SEQS, n_steps, lanes)
            s3i = sim[pp].reshape(SSM_SEQS, n_steps, lanes)
        else:
            s3r = jnp.stack([sre[pp, s * pitch:s * pitch + n_steps, :] for s in range(SSM_SEQS)])
            s3i = jnp.stack([sim[pp, s * pitch:s * pitch + n_steps, :] for s in range(SSM_SEQS)])
        t_re = (s3r + apr * c_re[:, None, :] - api * c_im[:, None, :]).reshape(nk, lanes)
        t_im = (s3i + apr * c_im[:, None, :] + api * c_re[:, None, :]).reshape(nk, lanes)
        tr_hi = t_re.astype(BF16)
        tr_lo = (t_re - tr_hi.astype(F32)).astype(BF16)
        ti_hi = t_im.astype(BF16)
        ti_lo = (t_im - ti_hi.astype(F32)).astype(BF16)
        for g in (2 * pp, 2 * pp + 1):
            ug = ut[g]
            y = jnp.dot(m_ref[g], ug, preferred_element_type=F32)
            y += lax.dot_general(vre_ref[g], tr_hi, NT_DIMS, preferred_element_type=F32)
            y += lax.dot_general(vre_ref[g], tr_lo, NT_DIMS, preferred_element_type=F32)
            y += lax.dot_general(vim_ref[g], ti_hi, NT_DIMS, preferred_element_type=F32)
            y += lax.dot_general(vim_ref[g], ti_lo, NT_DIMS, preferred_element_type=F32)
            yt[g] = y + d_ref[g] * ug.astype(F32)

    for t in range(SSM_CHUNK):
        rows = jnp.concatenate([yt[g, t * gd:(t + 1) * gd, :] for g in range(SSM_LANE_GROUPS)], axis=0)
        y_ref[t] = rows.T


def _ssm(u3, tables, eye, bsz, seq):
    m, wre, wim, vre, vim, consts, apw, dcol = tables
    nk = u3.shape[1]
    halves = SSM_SEQS // bsz
    n_steps = seq // (SSM_CHUNK * halves)
    lanes = 2 * SSM_STATE
    lg = SSM_LANE_GROUPS
    pitch = n_steps + SUBLANES if (n_steps // SUBLANES) % 2 == 0 else n_steps
    lead = lambda shape: pl.BlockSpec(shape, lambda b: (b,) + (0,) * (len(shape) - 1))
    return pl.pallas_call(
        functools.partial(_ssm_body, n_steps=n_steps, halves=halves, pitch=pitch),
        grid=(SSM_GROUPS // lg,),
        in_specs=[pl.BlockSpec((SSM_CHUNK, nk, LANES), lambda b: (0, 0, b)),
                  _const_spec((LANES, LANES)),
                  lead((lg, SSM_VEC, SSM_VEC)),
                  lead((lg // 2, lanes, 2 * SSM_VEC)), lead((lg // 2, lanes, 2 * SSM_VEC)),
                  lead((lg, SSM_VEC, lanes)), lead((lg, SSM_VEC, lanes)),
                  lead((lg // 2, 2, 2, lanes)), lead((lg // 2, 2, n_steps, lanes)),
                  lead((lg, SSM_VEC, 1))],
        out_specs=pl.BlockSpec((SSM_CHUNK, nk, LANES), lambda b: (0, 0, b)),
        out_shape=jax.ShapeDtypeStruct((SSM_CHUNK, nk, SSM_WIDTH), F32),
        scratch_shapes=[pltpu.VMEM((lg, SSM_VEC, nk), BF16), pltpu.VMEM((lg, SSM_VEC, nk), F32)]
                       + [pltpu.VMEM((lg // 2, SSM_SEQS * pitch, lanes), F32)] * 4,
        compiler_params=_cparams(("parallel",)),
        name="s5_ssm",
    )(u3, eye, m, wre, wim, vre, vim, consts, apw, dcol)


def _mix_math(x_ref, a_ref, y3_ref, wglu_ref, sg_ref, wo_ref, fg_ref, yscr):
    nlb = SSM_WIDTH // LANES
    for t in range(SSM_CHUNK):
        for lb in range(nlb):
            yscr[lb, pl.ds(t, ROW_TILE // SSM_CHUNK, stride=SSM_CHUNK), :] = (
                y3_ref[t, :, lb * LANES:(lb + 1) * LANES])
    y = jax.nn.gelu(jnp.concatenate([yscr[lb] for lb in range(nlb)], axis=1))
    z = y * jax.nn.sigmoid(jnp.dot(y.astype(BF16), wglu_ref[...], preferred_element_type=F32))
    ms = jnp.mean(z * z, axis=-1, keepdims=True)
    sn = (z * lax.rsqrt(ms + NORM_EPS) * sg_ref[...]).astype(BF16)
    x1 = (x_ref[...]
          + jnp.dot(a_ref[...], wo_ref[:ATTN_WIDTH, :], preferred_element_type=F32)
          + jnp.dot(sn, wo_ref[ATTN_WIDTH:, :], preferred_element_type=F32))
    ms1 = jnp.mean(x1 * x1, axis=-1, keepdims=True)
    return x1, x1 * lax.rsqrt(ms1 + NORM_EPS) * fg_ref[...]


def _mix_router_body(x_ref, a_ref, y3_ref, wglu_ref, sg_ref, wo_ref, fg_ref, rwt_ref,
                     x1_out, lg_out, yscr):
    x1, hn = _mix_math(x_ref, a_ref, y3_ref, wglu_ref, sg_ref, wo_ref, fg_ref, yscr)
    x1_out[...] = x1
    lane = lax.broadcasted_iota(jnp.int32, lg_out.shape, 1)
    lg = jnp.zeros(lg_out.shape, F32)
    for e in range(lg_out.shape[1]):
        lg = jnp.where(lane == e, jnp.sum(hn * rwt_ref[e:e + 1, :], axis=-1, keepdims=True), lg)
    lg_out[...] = lg


def _mix_ffn_body(x_ref, a_ref, y3_ref, wglu_ref, sg_ref, wo_ref, fg_ref, wg_ref, wu_ref, wd_ref,
                  o_ref, yscr):
    x1, hn = _mix_math(x_ref, a_ref, y3_ref, wglu_ref, sg_ref, wo_ref, fg_ref, yscr)
    h = hn.astype(BF16)
    o_ref[...] = x1
    for c in range(D_FF // FF_CHUNK):
        sl = slice(c * FF_CHUNK, (c + 1) * FF_CHUNK)
        g = jnp.dot(h, wg_ref[:, sl], preferred_element_type=F32)
        u = jnp.dot(h, wu_ref[:, sl], preferred_element_type=F32)
        a = (jax.nn.silu(g) * u).astype(BF16)
        o_ref[...] += jnp.dot(a, wd_ref[sl, :], preferred_element_type=F32)


def _mix_specs():
    row = lambda w: pl.BlockSpec((ROW_TILE, w), lambda i: (i, 0))
    in_specs = [row(D_MODEL), row(ATTN_WIDTH),
                pl.BlockSpec((SSM_CHUNK, ROW_TILE // SSM_CHUNK, SSM_WIDTH), lambda i: (0, i, 0)),
                _resident_spec((SSM_WIDTH, SSM_WIDTH)), _const_spec((1, SSM_WIDTH)),
                _resident_spec((D_MODEL, D_MODEL)), _const_spec((1, D_MODEL))]
    return row, in_specs, pltpu.VMEM((SSM_WIDTH // LANES, ROW_TILE, LANES), F32)


def _mix_router(x2d, attn, y3, wglu, sg, wo, fg, rwt):
    t = x2d.shape[0]
    ne = rwt.shape[0]
    row, in_specs, yscr = _mix_specs()
    return pl.pallas_call(
        _mix_router_body,
        grid=(t // ROW_TILE,),
        in_specs=in_specs + [_const_spec((ne, D_MODEL))],
        out_specs=[row(D_MODEL), row(ne)],
        out_shape=[jax.ShapeDtypeStruct((t, D_MODEL), F32), jax.ShapeDtypeStruct((t, ne), F32)],
        scratch_shapes=[yscr],
        compiler_params=_cparams(("parallel",)),
        name="mix_router",
    )(x2d, attn, y3, wglu, sg, wo, fg, rwt)


def _mix_ffn(x2d, attn, y3, wglu, sg, wo, fg, wg, wu, wd, casts=()):
    t = x2d.shape[0]
    row, in_specs, yscr = _mix_specs()
    return _row_tiled_call(
        _mix_ffn_body, t // ROW_TILE,
        in_specs=in_specs + [_resident_spec((D_MODEL, D_FF)), _resident_spec((D_MODEL, D_FF)),
                             _resident_spec((D_FF, D_MODEL))],
        out_specs=[row(D_MODEL)],
        out_shape=[jax.ShapeDtypeStruct((t, D_MODEL), F32)],
        scratch_shapes=[yscr],
        name="mix_dense_ffn", args=(x2d, attn, y3, wglu, sg, wo, fg, wg, wu, wd), casts=casts,
        vmem=BIG_VMEM_LIMIT)


def _expert_ffn_body(e_ref, nused_ref, nvalid_ref, tok_ref, dst_ref, x1_hbm, fg_ref,
                     wg_ref, wu_ref, wd_ref, out_hbm, xbuf, acc, ybuf, gsem, ssem):
    b = pl.program_id(0)
    n_used = nused_ref[0]
    n_ff = D_FF // FF_CHUNK
    rows_per_chunk = -(-MOE_ROWS // n_ff)

    def gather_row(blk, r):
        i = tok_ref[blk * MOE_ROWS + r]
        pltpu.make_async_copy(x1_hbm.at[pl.ds(i, 1)], xbuf.at[pl.ds(r, 1)], gsem).start()

    def scatter_row(blk, r):
        i = dst_ref[blk * MOE_ROWS + r]
        pltpu.make_async_copy(ybuf.at[pl.ds(r, 1)], out_hbm.at[pl.ds(i, 1)], ssem).start()

    def scatter_rows(blk, n):
        def body(r, carry):
            scatter_row(blk, r)
            return carry
        lax.fori_loop(0, n, body, 0)

    def wait_gather():
        pltpu.make_async_copy(x1_hbm.at[pl.ds(0, MOE_ROWS)], xbuf, gsem).wait()

    def wait_scatter(n):
        n8 = pl.multiple_of(n & -SUBLANES, SUBLANES)

        @pl.when(n8 > 0)
        def _():
            pltpu.make_async_copy(ybuf.at[pl.ds(0, n8)], out_hbm.at[pl.ds(0, n8)], ssem).wait()

        def one(_, carry):
            pltpu.make_async_copy(ybuf.at[pl.ds(0, 1)], out_hbm.at[pl.ds(0, 1)], ssem).wait()
            return carry
        lax.fori_loop(0, n - n8, one, 0)

    def ffn(scatter_prev):
        wait_gather()
        x = xbuf[...]
        ms = jnp.mean(x * x, axis=-1, keepdims=True)
        h = (x * lax.rsqrt(ms + NORM_EPS) * fg_ref[...]).astype(BF16)
        for c in range(n_ff):
            for r in range(c * rows_per_chunk, min((c + 1) * rows_per_chunk, MOE_ROWS)):
                gather_row(b + 1, r)
                if scatter_prev:
                    scatter_row(b - 1, r)
            sl = slice(c * FF_CHUNK, (c + 1) * FF_CHUNK)
            g = jnp.dot(h, wg_ref[0, :, sl], preferred_element_type=F32)
            u = jnp.dot(h, wu_ref[0, :, sl], preferred_element_type=F32)
            a = (jax.nn.silu(g) * u).astype(BF16)
            part = jnp.dot(a, wd_ref[0, sl, :], preferred_element_type=F32)
            if c == 0:
                acc[...] = part
            else:
                acc[...] += part

    @pl.when(b < n_used)
    def _():
        @pl.when(b == 0)
        def _():
            def body(r, carry):
                gather_row(0, r)
                return carry
            lax.fori_loop(0, MOE_ROWS, body, 0, unroll=8)

        n_prev = jnp.where(b >= 1, nvalid_ref[jnp.maximum(b - 1, 0)], 0)

        @pl.when(n_prev == MOE_ROWS)
        def _():
            ffn(scatter_prev=True)

        @pl.when(n_prev != MOE_ROWS)
        def _():
            scatter_rows(b - 1, n_prev)
            ffn(scatter_prev=False)

        wait_scatter(n_prev)
        ybuf[...] = acc[...]

        @pl.when(b == n_used - 1)
        def _():
            scatter_rows(b, nvalid_ref[b])
            wait_scatter(nvalid_ref[b])
            wait_gather()


def _expert_ffn(x1, fg, tok, dst, block_e, n_used, n_valid, wg, wu, wd, nblk, out_rows):
    wspec = lambda shape: pl.BlockSpec(shape, lambda b, e, *_: (e[b], 0, 0))
    return pl.pallas_call(
        _expert_ffn_body,
        grid_spec=pltpu.PrefetchScalarGridSpec(
            num_scalar_prefetch=5, grid=(nblk,),
            in_specs=[pl.BlockSpec(memory_space=pl.ANY),
                      pl.BlockSpec((1, D_MODEL), lambda b, *_: (0, 0)),
                      wspec((1, D_MODEL, D_FF)), wspec((1, D_MODEL, D_FF)), wspec((1, D_FF, D_MODEL))],
            out_specs=pl.BlockSpec(memory_space=pl.ANY),
            scratch_shapes=[pltpu.VMEM((MOE_ROWS, D_MODEL), F32)] * 3
                           + [pltpu.SemaphoreType.DMA(()), pltpu.SemaphoreType.DMA(())]),
        out_shape=jax.ShapeDtypeStruct((out_rows, D_MODEL), F32),
        compiler_params=pltpu.CompilerParams(dimension_semantics=("arbitrary",),
                                             vmem_limit_bytes=BIG_VMEM_LIMIT),
        name="moe_expert_ffn",
    )(block_e, n_used, n_valid, tok, dst, x1, fg, wg, wu, wd)


def _combine_body(x_ref, y0_ref, y1_ref, g_ref, o_ref):
    o_ref[...] = x_ref[...] + y0_ref[...] * g_ref[:, 0:1] + y1_ref[...] * g_ref[:, 1:2]


def _combine(x1, y2, gates):
    t = x1.shape[0]
    rows = 2 * ROW_TILE if t % (2 * ROW_TILE) == 0 else ROW_TILE
    nt = t // rows
    return pl.pallas_call(
        _combine_body,
        grid=(nt,),
        in_specs=[pl.BlockSpec((rows, D_MODEL), lambda i: (i, 0)),
                  pl.BlockSpec((rows, D_MODEL), lambda i: (i, 0)),
                  pl.BlockSpec((rows, D_MODEL), lambda i: (i + nt, 0)),
                  pl.BlockSpec((rows, TOP_K), lambda i: (i, 0))],
        out_specs=pl.BlockSpec((rows, D_MODEL), lambda i: (i, 0)),
        out_shape=jax.ShapeDtypeStruct((t, D_MODEL), F32),
        compiler_params=_cparams(("parallel",)),
        name="moe_combine",
    )(x1, y2, y2, gates)


def _moe(x1, fg, logits, wg, wu, wd):
    t = x1.shape[0]
    n_assign = t * TOP_K
    nblk = n_assign // MOE_ROWS + N_EXPERTS
    cap = nblk * MOE_ROWS
    top_v, top_i = lax.top_k(logits, TOP_K)
    gates = jax.nn.softmax(top_v, axis=-1)
    flat_e = top_i.reshape(-1).astype(jnp.int32)
    order = jnp.argsort(flat_e, stable=True).astype(jnp.int32)
    experts = jnp.arange(N_EXPERTS, dtype=jnp.int32)
    counts = jnp.sum((flat_e[:, None] == experts[None, :]).astype(jnp.int32), axis=0)
    start = jnp.cumsum(counts) - counts
    padded = ((counts + MOE_ROWS - 1) // MOE_ROWS) * MOE_ROWS
    pend = jnp.cumsum(padded)
    pstart = pend - padded
    p = jnp.arange(cap, dtype=jnp.int32)
    e_p = jnp.minimum(jnp.sum((p[:, None] >= pend[None, :]).astype(jnp.int32), axis=1), N_EXPERTS - 1)
    rank = p - pstart[e_p]
    valid = jnp.logical_and(rank < counts[e_p], p < pend[-1])
    a_p = order[jnp.clip(start[e_p] + rank, 0, n_assign - 1)]
    tok = jnp.where(valid, a_p // TOP_K, 0).astype(jnp.int32)
    dst = jnp.where(valid, (a_p % TOP_K) * t + a_p // TOP_K, 0).astype(jnp.int32)
    block_e = e_p[::MOE_ROWS]
    n_used = (pend[-1:] // MOE_ROWS).astype(jnp.int32)
    n_valid = jnp.sum(valid.reshape(nblk, MOE_ROWS).astype(jnp.int32), axis=1)
    y2 = _expert_ffn(x1, fg, tok, dst, block_e, n_used, n_valid, wg, wu, wd, nblk, n_assign)
    return _combine(x1, y2, gates)


def kernel(x, attn_norm_g, w_in, q_norm_g, k_norm_g, sinks, lam_re, lam_im, log_dt, b_re, b_im,
           c_re, c_im, d_skip, w_glu, attn_out_g, ssm_out_g, w_o, ffn_norm_g, dense_wg, dense_wu,
           dense_wd, router_w, moe_wg, moe_wu, moe_wd):
    bsz, seq, _ = x.shape
    depth = w_in.shape[0]
    assert SSM_SEQS % bsz == 0 and seq % (SSM_CHUNK * (SSM_SEQS // bsz)) == 0
    assert seq % ROW_TILE == 0 and ROW_TILE % ATTN_BLOCK == 0
    x2d = x.reshape(bsz * seq, D_MODEL).astype(F32)
    head = jnp.arange(ATTN_WIDTH) // HEAD_DIM
    avg = jnp.where(head[:, None] == head[None, :], 1.0 / HEAD_DIM, 0.0).astype(BF16)
    eye = jnp.eye(ROW_TILE, dtype=BF16)
    n_steps = seq // (SSM_CHUNK * (SSM_SEQS // bsz))
    row = lambda v: v.astype(F32).reshape(1, -1)
    col = lambda v: v.astype(F32).reshape(-1, 1)
    ko, vo, uo = ATTN_WIDTH, ATTN_WIDTH + KV_WIDTH, ATTN_WIDTH + 2 * KV_WIDTH
    flat = lambda w: w.astype(F32).reshape(-1, w.shape[-1])
    ahead_bf = None
    for l in range(depth):
        i = l // 2
        dense = l % 2 == 0
        qg = jnp.tile(q_norm_g[l].astype(F32), N_HEADS) * (HEAD_DIM ** -0.5)
        kg = jnp.tile(k_norm_g[l].astype(F32), N_KV_HEADS)
        w = w_in[l].astype(BF16)
        early = [flat(dense_wg[i]), flat(dense_wu[i])] if dense else [flat(moe_wu[i])]
        late = [flat(dense_wd[i])] if dense else []
        wqvt = jnp.concatenate([w[:, :ko], w[:, vo:uo]], axis=1).T
        wku = jnp.concatenate([w[:, ko:vo], w[:, uo:]], axis=1)
        (qt, k, v3, u3), early_bf = _inproj(x2d, row(attn_norm_g[l]), wqvt, wku, col(qg), row(kg), avg,
                                            casts=early)
        (attn,), late_bf = _attention(qt, k, v3, sinks[l].astype(F32), col(attn_out_g[l]), eye, seq,
                                      casts=late)
        tables = _ssm_tables(lam_re[l], lam_im[l], log_dt[l], b_re[l], b_im[l], c_re[l], c_im[l],
                             d_skip[l], n_steps)
        y3 = _ssm(u3, tables, eye[:LANES, :LANES], bsz, seq)
        mix_args = (x2d, attn, y3, w_glu[l].astype(BF16), row(ssm_out_g[l]), w_o[l].astype(BF16),
                    row(ffn_norm_g[l]))
        if dense:
            ahead = [flat(moe_wg[i]), flat(moe_wd[i])] if l + 1 < depth else []
            (x2d,), ahead_bf = _mix_ffn(*mix_args, early_bf[0], early_bf[1], late_bf[0], casts=ahead)
        else:
            x1, logits = _mix_router(*mix_args, router_w[i].astype(F32).T)
            x2d = _moe(x1, row(ffn_norm_g[l]), logits, ahead_bf[0].reshape(moe_wg[i].shape),
                       early_bf[0].reshape(moe_wu[i].shape), ahead_bf[1].reshape(moe_wd[i].shape))
    return x2d.reshape(bsz, seq, D_MODEL)
```

```python
import functools

import jax
import jax.numpy as jnp
from jax import lax
from jax.experimental import pallas as pl
from jax.experimental.pallas import tpu as pltpu

F32 = jnp.float32
BF16 = jnp.bfloat16

D_MODEL = 1024
HEAD_DIM = 64
N_HEADS = 8
N_KV_HEADS = 2
GQA = N_HEADS // N_KV_HEADS
ATTN_WIDTH = N_HEADS * HEAD_DIM
KV_WIDTH = N_KV_HEADS * HEAD_DIM
ATTN_BLOCK = 128
SSM_WIDTH = D_MODEL - ATTN_WIDTH
SSM_GROUP_DIM = 16
SSM_GROUPS = SSM_WIDTH // SSM_GROUP_DIM
SSM_STATE = 64
D_FF = 3584
N_EXPERTS = 8
TOP_K = 2
NORM_EPS = 1e-6
NEG_INF = -1e30

LANES = 128
SUBLANES = 8
ROW_TILE = 512
SSM_CHUNK = 16
SSM_SEQS = SUBLANES
SSM_VEC = SSM_CHUNK * SSM_GROUP_DIM
SSM_LANE_GROUPS = LANES // SSM_GROUP_DIM
FF_CHUNK = 256
MOE_ROWS = 512
VMEM_LIMIT = 56 * 1024 * 1024
BIG_VMEM_LIMIT = 60 * 1024 * 1024

NT_DIMS = (((1,), (1,)), ((), ()))


def _cparams(sem, vmem=VMEM_LIMIT):
    return pltpu.CompilerParams(dimension_semantics=sem, vmem_limit_bytes=vmem)


def _const_spec(shape):
    n = len(shape)
    return pl.BlockSpec(shape, lambda *_: (0,) * n)


def _resident_spec(shape):
    n = len(shape)
    return pl.BlockSpec(shape, lambda *_: (0,) * n, pipeline_mode=pl.Buffered(1))


def _row_tiled_call(body, steps, in_specs, out_specs, out_shape, scratch_shapes, name, args, casts=(),
                    vmem=VMEM_LIMIT):
    n_in, n_out, n_c = len(in_specs), len(out_specs), len(casts)

    def hosted(*refs):
        ins, cast_in = refs[:n_in], refs[n_in:n_in + n_c]
        outs = refs[n_in + n_c:n_in + n_c + n_out]
        cast_out = refs[n_in + n_c + n_out:n_in + 2 * n_c + n_out]
        body(*ins, *outs, *refs[n_in + 2 * n_c + n_out:])
        for src, dst in zip(cast_in, cast_out):
            dst[...] = src[...].astype(BF16)

    for w in casts:
        assert w.ndim == 2 and w.shape[0] % (steps * 16) == 0, w.shape
    cast_specs = [pl.BlockSpec((w.shape[0] // steps, w.shape[1]), lambda i: (i, 0)) for w in casts]
    res = pl.pallas_call(
        hosted,
        grid=(steps,),
        in_specs=list(in_specs) + cast_specs,
        out_specs=list(out_specs) + cast_specs,
        out_shape=list(out_shape) + [jax.ShapeDtypeStruct(w.shape, BF16) for w in casts],
        scratch_shapes=scratch_shapes,
        compiler_params=_cparams(("parallel",), vmem),
        name=name,
    )(*args, *casts)
    return res[:n_out], res[n_out:]


def _inproj_body(x_ref, g_ref, wqvt_ref, wku_ref, qg_ref, kg_ref, avg_ref,
                 qt_out, k_out, v3_out, u3_out, uscr):
    x = x_ref[...]
    ms = jnp.mean(x * x, axis=-1, keepdims=True)
    hn = (x * lax.rsqrt(ms + NORM_EPS) * g_ref[...]).astype(BF16)
    qvt = lax.dot_general(wqvt_ref[...], hn, NT_DIMS, preferred_element_type=F32)
    ku = jnp.dot(hn, wku_ref[...], preferred_element_type=F32)
    qt = qvt[:ATTN_WIDTH]
    qms = jnp.dot(avg_ref[...], (qt * qt).astype(BF16), preferred_element_type=F32)
    qt_out[...] = (qt * lax.rsqrt(qms + NORM_EPS) * qg_ref[...]).astype(BF16)
    k = ku[:, :KV_WIDTH]
    kms = jnp.dot((k * k).astype(BF16), avg_ref[:KV_WIDTH, :KV_WIDTH], preferred_element_type=F32)
    k_out[...] = (k * lax.rsqrt(kms + NORM_EPS) * kg_ref[...]).astype(BF16)
    vt = qvt[ATTN_WIDTH:]
    for b in range(ROW_TILE // ATTN_BLOCK):
        v3_out[b] = vt[:, b * ATTN_BLOCK:(b + 1) * ATTN_BLOCK].astype(BF16)
    u = ku[:, KV_WIDTH:]
    for lb in range(SSM_WIDTH // LANES):
        uscr[lb] = u[:, lb * LANES:(lb + 1) * LANES]
    for t in range(SSM_CHUNK):
        for lb in range(SSM_WIDTH // LANES):
            u3_out[t, :, lb * LANES:(lb + 1) * LANES] = (
                uscr[lb, pl.ds(t, ROW_TILE // SSM_CHUNK, stride=SSM_CHUNK), :].astype(BF16))


def _inproj(x2d, g, wqvt, wku, qg, kg, avg, casts=()):
    t = x2d.shape[0]
    cpt = ROW_TILE // SSM_CHUNK
    return _row_tiled_call(
        _inproj_body, t // ROW_TILE,
        in_specs=[pl.BlockSpec((ROW_TILE, D_MODEL), lambda i: (i, 0)),
                  _const_spec((1, D_MODEL)), _const_spec((ATTN_WIDTH + KV_WIDTH, D_MODEL)),
                  _const_spec((D_MODEL, KV_WIDTH + SSM_WIDTH)), _const_spec((ATTN_WIDTH, 1)),
                  _const_spec((1, KV_WIDTH)), _const_spec((ATTN_WIDTH, ATTN_WIDTH))],
        out_specs=[pl.BlockSpec((ATTN_WIDTH, ROW_TILE), lambda i: (0, i)),
                   pl.BlockSpec((ROW_TILE, KV_WIDTH), lambda i: (i, 0)),
                   pl.BlockSpec((ROW_TILE // ATTN_BLOCK, KV_WIDTH, ATTN_BLOCK), lambda i: (i, 0, 0)),
                   pl.BlockSpec((SSM_CHUNK, cpt, SSM_WIDTH), lambda i: (0, i, 0))],
        out_shape=[jax.ShapeDtypeStruct((ATTN_WIDTH, t), BF16),
                   jax.ShapeDtypeStruct((t, KV_WIDTH), BF16),
                   jax.ShapeDtypeStruct((t // ATTN_BLOCK, KV_WIDTH, ATTN_BLOCK), BF16),
                   jax.ShapeDtypeStruct((SSM_CHUNK, t // SSM_CHUNK, SSM_WIDTH), BF16)],
        scratch_shapes=[pltpu.VMEM((SSM_WIDTH // LANES, ROW_TILE, LANES), F32)],
        name="inproj", args=(x2d, g, wqvt, wku, qg, kg, avg), casts=casts)


def _attn_body(sink_ref, qt_ref, k_ref, v3_ref, g_ref, eye_ref, o_ref, at_scr, *, blocks_per_seq):
    i = pl.program_id(0)
    key = lax.broadcasted_iota(jnp.int32, (ATTN_BLOCK, ATTN_BLOCK), 0)
    qry = lax.broadcasted_iota(jnp.int32, (ATTN_BLOCK, ATTN_BLOCK), 1)
    cur_ok = key <= qry
    zpad = jnp.zeros((HEAD_DIM, ATTN_BLOCK), BF16)
    nblk = ROW_TILE // ATTN_BLOCK
    for blk in range(nblk):
        gblk = i * nblk + blk
        pblk = jnp.maximum(gblk - 1, 0)
        row0 = pl.multiple_of(gblk * ATTN_BLOCK, ATTN_BLOCK)
        prev0 = pl.multiple_of(pblk * ATTN_BLOCK, ATTN_BLOCK)
        has_prev = (gblk % blocks_per_seq) != 0
        kc = k_ref[pl.ds(row0, ATTN_BLOCK), :]
        kp = k_ref[pl.ds(prev0, ATTN_BLOCK), :]
        vc = v3_ref[gblk]
        vp = v3_ref[pblk]
        qb = qt_ref[:, blk * ATTN_BLOCK:(blk + 1) * ATTN_BLOCK]
        keep = jnp.logical_or(cur_ok, has_prev)
        outs = []
        for h in range(N_HEADS):
            kv = h // GQA
            qh = qb[h * HEAD_DIM:(h + 1) * HEAD_DIM, :]
            qpad = jnp.concatenate([qh, zpad] if kv == 0 else [zpad, qh], axis=0)
            sc = jnp.dot(kc, qpad, preferred_element_type=F32)
            sp = jnp.dot(kp, qpad, preferred_element_type=F32)
            s = jnp.where(keep, jnp.where(cur_ok, sc, sp), NEG_INF)
            sink = sink_ref[h]
            m = jnp.maximum(jnp.max(s, axis=0, keepdims=True), sink)
            p = jnp.exp(s - m)
            den = jnp.sum(p, axis=0, keepdims=True) + jnp.exp(sink - m)
            pb = p.astype(BF16)
            zero = jnp.zeros_like(pb)
            vs = slice(kv * HEAD_DIM, (kv + 1) * HEAD_DIM)
            o = (jnp.dot(vc[vs, :], jnp.where(cur_ok, pb, zero), preferred_element_type=F32)
                 + jnp.dot(vp[vs, :], jnp.where(cur_ok, zero, pb), preferred_element_type=F32))
            outs.append(o / den)
        a = jnp.concatenate(outs, axis=0)
        ms = jnp.mean(a * a, axis=0, keepdims=True)
        at_scr[:, blk * ATTN_BLOCK:(blk + 1) * ATTN_BLOCK] = (
            a * lax.rsqrt(ms + NORM_EPS) * g_ref[...]).astype(BF16)
    o_ref[...] = lax.dot_general(eye_ref[...], at_scr[...], NT_DIMS,
                                 preferred_element_type=F32).astype(BF16)


def _attention(qt, k, v3, sinks, gcol, eye, seq, casts=()):
    t = k.shape[0]
    return _row_tiled_call(
        functools.partial(_attn_body, blocks_per_seq=seq // ATTN_BLOCK), t // ROW_TILE,
        in_specs=[pl.BlockSpec(memory_space=pltpu.SMEM),
                  pl.BlockSpec((ATTN_WIDTH, ROW_TILE), lambda i: (0, i)),
                  _const_spec((t, KV_WIDTH)), _const_spec((t // ATTN_BLOCK, KV_WIDTH, ATTN_BLOCK)),
                  _const_spec((ATTN_WIDTH, 1)), _const_spec((ROW_TILE, ROW_TILE))],
        out_specs=[pl.BlockSpec((ROW_TILE, ATTN_WIDTH), lambda i: (i, 0))],
        out_shape=[jax.ShapeDtypeStruct((t, ATTN_WIDTH), BF16)],
        scratch_shapes=[pltpu.VMEM((ATTN_WIDTH, ROW_TILE), BF16)],
        name="swa_attention", args=(sinks, qt, k, v3, gcol, eye), casts=casts)


def _ssm_tables_body(lam_ref, lamc_ref, b_ref, c_ref, erep_ref, etile_ref, eye_ref,
                     m_out, wre_out, wim_out, vre_out, vim_out, k_out, apw_out, *, n_steps):
    hi = lax.Precision.HIGHEST
    lanes = 2 * SSM_STATE
    ch, gd = SSM_CHUNK, SSM_GROUP_DIM
    lr, li, dt = lam_ref[0, 0:1, :], lam_ref[0, 1:2, :], lam_ref[0, 2:3, :]

    def apow(p):
        mag = jnp.exp(p * (lr * dt))
        ang = p * (li * dt)
        return mag * jnp.cos(ang), mag * jnp.sin(ang)

    tau = lax.broadcasted_iota(jnp.int32, (ch, lanes), 0).astype(F32)
    p0r, p0i = apow(tau)
    p1r, p1i = apow(tau + 1.0)
    base = SUBLANES
    doublings = []
    while base << len(doublings) < n_steps:
        doublings.append(ch * (base << len(doublings)))
    assert base << len(doublings) == n_steps and len(doublings) <= SUBLANES - 2, n_steps
    sel = lax.broadcasted_iota(jnp.int32, (SUBLANES, lanes), 0)
    pw = jnp.zeros((SUBLANES, lanes), F32)
    for r, p in enumerate([ch, ch * n_steps] + doublings):
        pw = jnp.where(sel == r, float(p), pw)
    kr, ki = apow(pw)
    k_out[0, 0] = kr[0:2]
    k_out[0, 1] = ki[0:2]
    qr, qi = apow(sel.astype(F32) * float(ch))
    for m in range(len(doublings)):
        dr, di = kr[2 + m:3 + m], ki[2 + m:3 + m]
        qr, qi = (jnp.concatenate([qr, qr * dr - qi * di], axis=0),
                  jnp.concatenate([qi, qr * di + qi * dr], axis=0))
    apw_out[0, 0] = qr
    apw_out[0, 1] = qi

    to_col = lambda p: lax.dot_general(eye_ref[...], p, NT_DIMS, precision=hi,
                                       preferred_element_type=F32)
    c0r, c0i, c1r, c1i = to_col(p0r), to_col(p0i), to_col(p1r), to_col(p1i)
    lrc, lic = lamc_ref[0, :, 0:1], lamc_ref[0, :, 1:2]
    ab_re, ab_im = c1r[:, 0:1], c1i[:, 0:1]
    nr = ab_re - 1.0
    den = lrc * lrc + lic * lic
    f_re = (nr * lrc + ab_im * lic) / den
    f_im = (ab_im * lrc - nr * lic) / den
    br, bi = b_ref[0, 0], b_ref[0, 1]
    bb_re = f_re * br - f_im * bi
    bb_im = f_re * bi + f_im * br
    bx_re = jnp.dot(bb_re, etile_ref[...], precision=hi, preferred_element_type=F32)
    bx_im = jnp.dot(bb_im, etile_ref[...], precision=hi, preferred_element_type=F32)
    px_re = jnp.dot(c0r, erep_ref[...], precision=hi, preferred_element_type=F32)
    px_im = jnp.dot(c0i, erep_ref[...], precision=hi, preferred_element_type=F32)
    w_re = px_re * bx_re - px_im * bx_im
    w_im = px_re * bx_im + px_im * bx_re
    first = lax.broadcasted_iota(jnp.int32, (lanes, SSM_VEC), 0) < SSM_STATE
    pack = lambda w: jnp.concatenate([jnp.where(first, w, 0.0), jnp.where(first, 0.0, w)], axis=1)
    wre_out[0] = pack(w_re).astype(BF16)
    wim_out[0] = pack(w_im).astype(BF16)

    blk = lax.broadcasted_iota(jnp.int32, (SSM_VEC, SSM_VEC), 1) // gd
    for g in range(2):
        cr, ci = c_ref[g, 0], c_ref[g, 1]
        stack = lambda x: x.reshape(ch * gd, lanes)
        ca_re = stack(cr[None] * p0r[:, None, :] - ci[None] * p0i[:, None, :])
        ca_im = stack(cr[None] * p0i[:, None, :] + ci[None] * p0r[:, None, :])
        kt = (jnp.dot(ca_re, bx_re, precision=hi, preferred_element_type=F32)
              - jnp.dot(ca_im, bx_im, precision=hi, preferred_element_type=F32))
        m = jnp.where(blk == 0, kt, 0.0)
        for j in range(1, ch):
            shifted = jnp.concatenate([jnp.zeros((j * gd, SSM_VEC), F32), kt[:SSM_VEC - j * gd]], axis=0)
            m = jnp.where(blk == j, shifted, m)
        m_out[g] = m.astype(BF16)
        vre_out[g] = stack(cr[None] * p1r[:, None, :] - ci[None] * p1i[:, None, :]).astype(BF16)
        vim_out[g] = (-stack(cr[None] * p1i[:, None, :] + ci[None] * p1r[:, None, :])).astype(BF16)


def _ssm_tables(lam_re, lam_im, log_dt, b_re, b_im, c_re, c_im, d_skip, n_steps):
    g_, n_, c_ = SSM_GROUPS, SSM_STATE, SSM_GROUP_DIM
    np_ = g_ // 2
    lanes = 2 * n_
    dt = jnp.broadcast_to(jnp.exp(log_dt.astype(F32))[:, None], (g_, n_))
    lam = jnp.stack([lam_re.astype(F32), lam_im.astype(F32), dt], axis=1)
    lam_row = lam.reshape(np_, 2, 3, n_).transpose(0, 2, 1, 3).reshape(np_, 3, lanes)
    lam_col = lam_row.transpose(0, 2, 1)
    b2 = jnp.stack([b_re.astype(F32).reshape(np_, lanes, c_), b_im.astype(F32).reshape(np_, lanes, c_)],
                   axis=1)
    c2 = jnp.stack([c_re.astype(F32), c_im.astype(F32)], axis=1)
    z = jnp.zeros_like(c2)
    even = (jnp.arange(g_) % 2 == 0)[:, None, None, None]
    c2 = jnp.where(even, jnp.concatenate([c2, z], axis=3), jnp.concatenate([z, c2], axis=3))
    col = jnp.arange(SSM_VEC)
    erep = (SSM_CHUNK - 1 - col[None, :] // c_ == jnp.arange(SSM_CHUNK)[:, None]).astype(F32)
    etile = (col[None, :] % c_ == jnp.arange(c_)[:, None]).astype(F32)
    lead = lambda shape: pl.BlockSpec(shape, lambda p: (p,) + (0,) * (len(shape) - 1))
    m, wre, wim, vre, vim, consts, apw = pl.pallas_call(
        functools.partial(_ssm_tables_body, n_steps=n_steps),
        grid=(np_,),
        in_specs=[lead((1, 3, lanes)), lead((1, lanes, 3)), lead((1, 2, lanes, c_)),
                  lead((2, 2, c_, lanes)), _const_spec((SSM_CHUNK, SSM_VEC)), _const_spec((c_, SSM_VEC)),
                  _const_spec((lanes, lanes))],
        out_specs=[lead((2, SSM_VEC, SSM_VEC)), lead((1, lanes, 2 * SSM_VEC)), lead((1, lanes, 2 * SSM_VEC)),
                   lead((2, SSM_VEC, lanes)), lead((2, SSM_VEC, lanes)),
                   lead((1, 2, 2, lanes)), lead((1, 2, n_steps, lanes))],
        out_shape=[jax.ShapeDtypeStruct((g_, SSM_VEC, SSM_VEC), BF16),
                   jax.ShapeDtypeStruct((np_, lanes, 2 * SSM_VEC), BF16),
                   jax.ShapeDtypeStruct((np_, lanes, 2 * SSM_VEC), BF16),
                   jax.ShapeDtypeStruct((g_, SSM_VEC, lanes), BF16),
                   jax.ShapeDtypeStruct((g_, SSM_VEC, lanes), BF16),
                   jax.ShapeDtypeStruct((np_, 2, 2, lanes), F32),
                   jax.ShapeDtypeStruct((np_, 2, n_steps, lanes), F32)],
        compiler_params=_cparams(("parallel",)),
        name="s5_operators",
    )(lam_row, lam_col, b2, c2, erep, etile, jnp.eye(lanes, dtype=F32))
    dcol = jnp.tile(d_skip.astype(F32).reshape(g_, 1, c_), (1, SSM_CHUNK, 1)).reshape(g_, SSM_VEC, 1)
    return m, wre, wim, vre, vim, consts, apw, dcol


def _ssm_body(u_ref, eye_ref, m_ref, wre_ref, wim_ref, vre_ref, vim_ref, c_ref, apw_ref, d_ref,
              y_ref, ut, yt, pre, pim, sre, sim, *, n_steps, halves, pitch):
    lanes = 2 * SSM_STATE
    nk = u_ref.shape[1]
    gd = SSM_GROUP_DIM
    for t in range(SSM_CHUNK):
        xt = lax.dot_general(eye_ref[...], u_ref[t], NT_DIMS,
                             preferred_element_type=F32).astype(BF16)
        for g in range(SSM_LANE_GROUPS):
            ut[g, t * gd:(t + 1) * gd, :] = xt[g * gd:(g + 1) * gd, :]

    row = lax.broadcasted_iota(jnp.int32, (SSM_SEQS, lanes), 0)
    seq_start = (row % halves) == 0
    shift = lambda x: jnp.where(seq_start, 0.0, pltpu.roll(x, 1, axis=0))

    npairs = SSM_LANE_GROUPS // 2
    for pp in range(npairs):
        ucat = jnp.concatenate([ut[2 * pp], ut[2 * pp + 1]], axis=0)
        p_re = jnp.dot(wre_ref[pp], ucat, preferred_element_type=F32).T
        p_im = jnp.dot(wim_ref[pp], ucat, preferred_element_type=F32).T
        if pitch == n_steps:
            pre[pp] = p_re
            pim[pp] = p_im
        else:
            for s in range(SSM_SEQS):
                pre[pp, s * pitch:s * pitch + n_steps, :] = p_re[s * n_steps:(s + 1) * n_steps]
                pim[pp, s * pitch:s * pitch + n_steps, :] = p_im[s * n_steps:(s + 1) * n_steps]

    a16 = [(jnp.broadcast_to(c_ref[pp, 0, 0:1, :], (SSM_SEQS, lanes)),
            jnp.broadcast_to(c_ref[pp, 1, 0:1, :], (SSM_SEQS, lanes))) for pp in range(npairs)]

    def step(i, carry):
        idx = pl.ds(i, SSM_SEQS, stride=pitch)
        out = []
        for pp in range(npairs):
            s_re, s_im = carry[pp]
            ar, ai = a16[pp]
            sre[pp, idx, :] = s_re
            sim[pp, idx, :] = s_im
            out.append((ar * s_re - ai * s_im + pre[pp, idx, :],
                        ar * s_im + ai * s_re + pim[pp, idx, :]))
        return tuple(out)

    zero = jnp.zeros((SSM_SEQS, lanes), F32)
    ends = lax.fori_loop(0, n_steps, step, ((zero, zero),) * npairs, unroll=2)

    for pp in range(npairs):
        e_re, e_im = ends[pp]
        br = jnp.broadcast_to(c_ref[pp, 0, 1:2, :], (SSM_SEQS, lanes))
        bi = jnp.broadcast_to(c_ref[pp, 1, 1:2, :], (SSM_SEQS, lanes))
        c_re = jnp.zeros_like(e_re)
        c_im = jnp.zeros_like(e_im)
        for _ in range(halves - 1):
            n_re = e_re + br * c_re - bi * c_im
            n_im = e_im + br * c_im + bi * c_re
            c_re, c_im = shift(n_re), shift(n_im)

        apr = apw_ref[pp, 0][None, :, :]
        api = apw_ref[pp, 1][None, :, :]
        if pitch == n_steps:
            s3r = sre[pp].reshape(SSM_SEQS, n_steps, lanes)
            s3i = sim[pp].reshape(SSM_SEQS, n_steps, lanes)
        else:
            s3r = jnp.stack([sre[pp, s * pitch:s * pitch + n_steps, :] for s in range(SSM_SEQS)])
            s3i = jnp.stack([sim[pp, s * pitch:s * pitch + n_steps, :] for s in range(SSM_SEQS)])
        t_re = (s3r + apr * c_re[:, None, :] - api * c_im[:, None, :]).reshape(nk, lanes)
        t_im = (s3i + apr * c_im[:, None, :] + api * c_re[:, None, :]).reshape(nk, lanes)
        tr_hi = t_re.astype(BF16)
        tr_lo = (t_re - tr_hi.astype(F32)).astype(BF16)
        ti_hi = t_im.astype(BF16)
        ti_lo = (t_im - ti_hi.astype(F32)).astype(BF16)
        for g in (2 * pp, 2 * pp + 1):
            ug = ut[g]
            y = jnp.dot(m_ref[g], ug, preferred_element_type=F32)
            y += lax.dot_general(vre_ref[g], tr_hi, NT_DIMS, preferred_element_type=F32)
            y += lax.dot_general(vre_ref[g], tr_lo, NT_DIMS, preferred_element_type=F32)
            y += lax.dot_general(vim_ref[g], ti_hi, NT_DIMS, preferred_element_type=F32)
            y += lax.dot_general(vim_ref[g], ti_lo, NT_DIMS, preferred_element_type=F32)
            yt[g] = y + d_ref[g] * ug.astype(F32)

    for t in range(SSM_CHUNK):
        rows = jnp.concatenate([yt[g, t * gd:(t + 1) * gd, :] for g in range(SSM_LANE_GROUPS)], axis=0)
        y_ref[t] = rows.T


def _ssm(u3, tables, eye, bsz, seq):
    m, wre, wim, vre, vim, consts, apw, dcol = tables
    nk = u3.shape[1]
    halves = SSM_SEQS // bsz
    n_steps = seq // (SSM_CHUNK * halves)
    lanes = 2 * SSM_STATE
    lg = SSM_LANE_GROUPS
    pitch = n_steps + SUBLANES if (n_steps // SUBLANES) % 2 == 0 else n_steps
    lead = lambda shape: pl.BlockSpec(shape, lambda b: (b,) + (0,) * (len(shape) - 1))
    return pl.pallas_call(
        functools.partial(_ssm_body, n_steps=n_steps, halves=halves, pitch=pitch),
        grid=(SSM_GROUPS // lg,),
        in_specs=[pl.BlockSpec((SSM_CHUNK, nk, LANES), lambda b: (0, 0, b)),
                  _const_spec((LANES, LANES)),
                  lead((lg, SSM_VEC, SSM_VEC)),
                  lead((lg // 2, lanes, 2 * SSM_VEC)), lead((lg // 2, lanes, 2 * SSM_VEC)),
                  lead((lg, SSM_VEC, lanes)), lead((lg, SSM_VEC, lanes)),
                  lead((lg // 2, 2, 2, lanes)), lead((lg // 2, 2, n_steps, lanes)),
                  lead((lg, SSM_VEC, 1))],
        out_specs=pl.BlockSpec((SSM_CHUNK, nk, LANES), lambda b: (0, 0, b)),
        out_shape=jax.ShapeDtypeStruct((SSM_CHUNK, nk, SSM_WIDTH), F32),
        scratch_shapes=[pltpu.VMEM((lg, SSM_VEC, nk), BF16), pltpu.VMEM((lg, SSM_VEC, nk), F32)]
                       + [pltpu.VMEM((lg // 2, SSM_SEQS * pitch, lanes), F32)] * 4,
        compiler_params=_cparams(("parallel",)),
        name="s5_ssm",
    )(u3, eye, m, wre, wim, vre, vim, consts, apw, dcol)


def _mix_math(x_ref, a_ref, y3_ref, wglu_ref, sg_ref, wo_ref, fg_ref, yscr):
    nlb = SSM_WIDTH // LANES
    for t in range(SSM_CHUNK):
        for lb in range(nlb):
            yscr[lb, pl.ds(t, ROW_TILE // SSM_CHUNK, stride=SSM_CHUNK), :] = (
                y3_ref[t, :, lb * LANES:(lb + 1) * LANES])
    y = jax.nn.gelu(jnp.concatenate([yscr[lb] for lb in range(nlb)], axis=1))
    z = y * jax.nn.sigmoid(jnp.dot(y.astype(BF16), wglu_ref[...], preferred_element_type=F32))
    ms = jnp.mean(z * z, axis=-1, keepdims=True)
    sn = (z * lax.rsqrt(ms + NORM_EPS) * sg_ref[...]).astype(BF16)
    x1 = (x_ref[...]
          + jnp.dot(a_ref[...], wo_ref[:ATTN_WIDTH, :], preferred_element_type=F32)
          + jnp.dot(sn, wo_ref[ATTN_WIDTH:, :], preferred_element_type=F32))
    ms1 = jnp.mean(x1 * x1, axis=-1, keepdims=True)
    return x1, x1 * lax.rsqrt(ms1 + NORM_EPS) * fg_ref[...]


def _mix_router_body(x_ref, a_ref, y3_ref, wglu_ref, sg_ref, wo_ref, fg_ref, rwt_ref,
                     x1_out, lg_out, yscr):
    x1, hn = _mix_math(x_ref, a_ref, y3_ref, wglu_ref, sg_ref, wo_ref, fg_ref, yscr)
    x1_out[...] = x1
    lane = lax.broadcasted_iota(jnp.int32, lg_out.shape, 1)
    lg = jnp.zeros(lg_out.shape, F32)
    for e in range(lg_out.shape[1]):
        lg = jnp.where(lane == e, jnp.sum(hn * rwt_ref[e:e + 1, :], axis=-1, keepdims=True), lg)
    lg_out[...] = lg


def _mix_ffn_body(x_ref, a_ref, y3_ref, wglu_ref, sg_ref, wo_ref, fg_ref, wg_ref, wu_ref, wd_ref,
                  o_ref, yscr):
    x1, hn = _mix_math(x_ref, a_ref, y3_ref, wglu_ref, sg_ref, wo_ref, fg_ref, yscr)
    h = hn.astype(BF16)
    o_ref[...] = x1
    for c in range(D_FF // FF_CHUNK):
        sl = slice(c * FF_CHUNK, (c + 1) * FF_CHUNK)
        g = jnp.dot(h, wg_ref[:, sl], preferred_element_type=F32)
        u = jnp.dot(h, wu_ref[:, sl], preferred_element_type=F32)
        a = (jax.nn.silu(g) * u).astype(BF16)
        o_ref[...] += jnp.dot(a, wd_ref[sl, :], preferred_element_type=F32)


def _mix_specs():
    row = lambda w: pl.BlockSpec((ROW_TILE, w), lambda i: (i, 0))
    in_specs = [row(D_MODEL), row(ATTN_WIDTH),
                pl.BlockSpec((SSM_CHUNK, ROW_TILE // SSM_CHUNK, SSM_WIDTH), lambda i: (0, i, 0)),
                _resident_spec((SSM_WIDTH, SSM_WIDTH)), _const_spec((1, SSM_WIDTH)),
                _resident_spec((D_MODEL, D_MODEL)), _const_spec((1, D_MODEL))]
    return row, in_specs, pltpu.VMEM((SSM_WIDTH // LANES, ROW_TILE, LANES), F32)


def _mix_router(x2d, attn, y3, wglu, sg, wo, fg, rwt):
    t = x2d.shape[0]
    ne = rwt.shape[0]
    row, in_specs, yscr = _mix_specs()
    return pl.pallas_call(
        _mix_router_body,
        grid=(t // ROW_TILE,),
        in_specs=in_specs + [_const_spec((ne, D_MODEL))],
        out_specs=[row(D_MODEL), row(ne)],
        out_shape=[jax.ShapeDtypeStruct((t, D_MODEL), F32), jax.ShapeDtypeStruct((t, ne), F32)],
        scratch_shapes=[yscr],
        compiler_params=_cparams(("parallel",)),
        name="mix_router",
    )(x2d, attn, y3, wglu, sg, wo, fg, rwt)


def _mix_ffn(x2d, attn, y3, wglu, sg, wo, fg, wg, wu, wd, casts=()):
    t = x2d.shape[0]
    row, in_specs, yscr = _mix_specs()
    return _row_tiled_call(
        _mix_ffn_body, t // ROW_TILE,
        in_specs=in_specs + [_resident_spec((D_MODEL, D_FF)), _resident_spec((D_MODEL, D_FF)),
                             _resident_spec((D_FF, D_MODEL))],
        out_specs=[row(D_MODEL)],
        out_shape=[jax.ShapeDtypeStruct((t, D_MODEL), F32)],
        scratch_shapes=[yscr],
        name="mix_dense_ffn", args=(x2d, attn, y3, wglu, sg, wo, fg, wg, wu, wd), casts=casts,
        vmem=BIG_VMEM_LIMIT)


def _expert_ffn_body(e_ref, nused_ref, nvalid_ref, tok_ref, dst_ref, x1_hbm, fg_ref,
                     wg_ref, wu_ref, wd_ref, out_hbm, xbuf, acc, ybuf, gsem, ssem):
    b = pl.program_id(0)
    n_used = nused_ref[0]
    n_ff = D_FF // FF_CHUNK
    rows_per_chunk = -(-MOE_ROWS // n_ff)

    def gather_row(blk, r):
        i = tok_ref[blk * MOE_ROWS + r]
        pltpu.make_async_copy(x1_hbm.at[pl.ds(i, 1)], xbuf.at[pl.ds(r, 1)], gsem).start()

    def scatter_row(blk, r, priority=0):
        i = dst_ref[blk * MOE_ROWS + r]
        pltpu.make_async_copy(ybuf.at[pl.ds(r, 1)], out_hbm.at[pl.ds(i, 1)], ssem).start(priority)

    def scatter_rows(blk, n):
        def body(r, carry):
            scatter_row(blk, r)
            return carry
        lax.fori_loop(0, n, body, 0)

    def wait_gather():
        pltpu.make_async_copy(x1_hbm.at[pl.ds(0, MOE_ROWS)], xbuf, gsem).wait()

    def wait_scatter(n):
        n8 = pl.multiple_of(n & -SUBLANES, SUBLANES)

        @pl.when(n8 > 0)
        def _():
            pltpu.make_async_copy(ybuf.at[pl.ds(0, n8)], out_hbm.at[pl.ds(0, n8)], ssem).wait()

        def one(_, carry):
            pltpu.make_async_copy(ybuf.at[pl.ds(0, 1)], out_hbm.at[pl.ds(0, 1)], ssem).wait()
            return carry
        lax.fori_loop(0, n - n8, one, 0)

    def ffn(scatter_prev):
        wait_gather()
        x = xbuf[...]
        ms = jnp.mean(x * x, axis=-1, keepdims=True)
        h = (x * lax.rsqrt(ms + NORM_EPS) * fg_ref[...]).astype(BF16)
        for c in range(n_ff):
            for r in range(c * rows_per_chunk, min((c + 1) * rows_per_chunk, MOE_ROWS)):
                gather_row(b + 1, r)
                if scatter_prev:
                    scatter_row(b - 1, r, priority=r % 2)
            sl = slice(c * FF_CHUNK, (c + 1) * FF_CHUNK)
            g = jnp.dot(h, wg_ref[0, :, sl], preferred_element_type=F32)
            u = jnp.dot(h, wu_ref[0, :, sl], preferred_element_type=F32)
            a = (jax.nn.silu(g) * u).astype(BF16)
            part = jnp.dot(a, wd_ref[0, sl, :], preferred_element_type=F32)
            if c == 0:
                acc[...] = part
            else:
                acc[...] += part

    @pl.when(b < n_used)
    def _():
        @pl.when(b == 0)
        def _():
            def body(r, carry):
                gather_row(0, r)
                return carry
            lax.fori_loop(0, MOE_ROWS, body, 0, unroll=8)

        n_prev = jnp.where(b >= 1, nvalid_ref[jnp.maximum(b - 1, 0)], 0)

        @pl.when(n_prev == MOE_ROWS)
        def _():
            ffn(scatter_prev=True)

        @pl.when(n_prev != MOE_ROWS)
        def _():
            scatter_rows(b - 1, n_prev)
            ffn(scatter_prev=False)

        wait_scatter(n_prev)
        ybuf[...] = acc[...]

        @pl.when(b == n_used - 1)
        def _():
            scatter_rows(b, nvalid_ref[b])
            wait_scatter(nvalid_ref[b])
            wait_gather()


def _expert_ffn(x1, fg, tok, dst, block_e, n_used, n_valid, wg, wu, wd, nblk, out_rows):
    wspec = lambda shape: pl.BlockSpec(shape, lambda b, e, *_: (e[b], 0, 0))
    return pl.pallas_call(
        _expert_ffn_body,
        grid_spec=pltpu.PrefetchScalarGridSpec(
            num_scalar_prefetch=5, grid=(nblk,),
            in_specs=[pl.BlockSpec(memory_space=pl.ANY),
                      pl.BlockSpec((1, D_MODEL), lambda b, *_: (0, 0)),
                      wspec((1, D_MODEL, D_FF)), wspec((1, D_MODEL, D_FF)), wspec((1, D_FF, D_MODEL))],
            out_specs=pl.BlockSpec(memory_space=pl.ANY),
            scratch_shapes=[pltpu.VMEM((MOE_ROWS, D_MODEL), F32)] * 3
                           + [pltpu.SemaphoreType.DMA(()), pltpu.SemaphoreType.DMA(())]),
        out_shape=jax.ShapeDtypeStruct((out_rows, D_MODEL), F32),
        compiler_params=pltpu.CompilerParams(dimension_semantics=("arbitrary",),
                                             vmem_limit_bytes=BIG_VMEM_LIMIT),
        name="moe_expert_ffn",
    )(block_e, n_used, n_valid, tok, dst, x1, fg, wg, wu, wd)


def _combine_body(x_ref, y0_ref, y1_ref, g_ref, o_ref):
    o_ref[...] = x_ref[...] + y0_ref[...] * g_ref[:, 0:1] + y1_ref[...] * g_ref[:, 1:2]


def _combine(x1, y2, gates):
    t = x1.shape[0]
    rows = 2 * ROW_TILE if t % (2 * ROW_TILE) == 0 else ROW_TILE
    nt = t // rows
    return pl.pallas_call(
        _combine_body,
        grid=(nt,),
        in_specs=[pl.BlockSpec((rows, D_MODEL), lambda i: (i, 0)),
                  pl.BlockSpec((rows, D_MODEL), lambda i: (i, 0)),
                  pl.BlockSpec((rows, D_MODEL), lambda i: (i + nt, 0)),
                  pl.BlockSpec((rows, TOP_K), lambda i: (i, 0))],
        out_specs=pl.BlockSpec((rows, D_MODEL), lambda i: (i, 0)),
        out_shape=jax.ShapeDtypeStruct((t, D_MODEL), F32),
        compiler_params=_cparams(("parallel",)),
        name="moe_combine",
    )(x1, y2, y2, gates)


def _moe(x1, fg, logits, wg, wu, wd):
    t = x1.shape[0]
    n_assign = t * TOP_K
    nblk = n_assign // MOE_ROWS + N_EXPERTS
    cap = nblk * MOE_ROWS
    top_v, top_i = lax.top_k(logits, TOP_K)
    gates = jax.nn.softmax(top_v, axis=-1)
    flat_e = top_i.reshape(-1).astype(jnp.int32)
    order = jnp.argsort(flat_e, stable=True).astype(jnp.int32)
    experts = jnp.arange(N_EXPERTS, dtype=jnp.int32)
    counts = jnp.sum((flat_e[:, None] == experts[None, :]).astype(jnp.int32), axis=0)
    start = jnp.cumsum(counts) - counts
    padded = ((counts + MOE_ROWS - 1) // MOE_ROWS) * MOE_ROWS
    pend = jnp.cumsum(padded)
    pstart = pend - padded
    p = jnp.arange(cap, dtype=jnp.int32)
    e_p = jnp.minimum(jnp.sum((p[:, None] >= pend[None, :]).astype(jnp.int32), axis=1), N_EXPERTS - 1)
    rank = p - pstart[e_p]
    valid = jnp.logical_and(rank < counts[e_p], p < pend[-1])
    a_p = order[jnp.clip(start[e_p] + rank, 0, n_assign - 1)]
    tok = jnp.where(valid, a_p // TOP_K, 0).astype(jnp.int32)
    dst = jnp.where(valid, (a_p % TOP_K) * t + a_p // TOP_K, 0).astype(jnp.int32)
    block_e = e_p[::MOE_ROWS]
    n_used = (pend[-1:] // MOE_ROWS).astype(jnp.int32)
    n_valid = jnp.sum(valid.reshape(nblk, MOE_ROWS).astype(jnp.int32), axis=1)
    y2 = _expert_ffn(x1, fg, tok, dst, block_e, n_used, n_valid, wg, wu, wd, nblk, n_assign)
    return _combine(x1, y2, gates)


def kernel(x, attn_norm_g, w_in, q_norm_g, k_norm_g, sinks, lam_re, lam_im, log_dt, b_re, b_im,
           c_re, c_im, d_skip, w_glu, attn_out_g, ssm_out_g, w_o, ffn_norm_g, dense_wg, dense_wu,
           dense_wd, router_w, moe_wg, moe_wu, moe_wd):
    bsz, seq, _ = x.shape
    depth = w_in.shape[0]
    assert SSM_SEQS % bsz == 0 and seq % (SSM_CHUNK * (SSM_SEQS // bsz)) == 0
    assert seq % ROW_TILE == 0 and ROW_TILE % ATTN_BLOCK == 0
    x2d = x.reshape(bsz * seq, D_MODEL).astype(F32)
    head = jnp.arange(ATTN_WIDTH) // HEAD_DIM
    avg = jnp.where(head[:, None] == head[None, :], 1.0 / HEAD_DIM, 0.0).astype(BF16)
    eye = jnp.eye(ROW_TILE, dtype=BF16)
    n_steps = seq // (SSM_CHUNK * (SSM_SEQS // bsz))
    row = lambda v: v.astype(F32).reshape(1, -1)
    col = lambda v: v.astype(F32).reshape(-1, 1)
    ko, vo, uo = ATTN_WIDTH, ATTN_WIDTH + KV_WIDTH, ATTN_WIDTH + 2 * KV_WIDTH
    flat = lambda w: w.astype(F32).reshape(-1, w.shape[-1])
    ahead_bf = None
    for l in range(depth):
        i = l // 2
        dense = l % 2 == 0
        has_moe_next = dense and l + 1 < depth
        qg = jnp.tile(q_norm_g[l].astype(F32), N_HEADS) * (HEAD_DIM ** -0.5)
        kg = jnp.tile(k_norm_g[l].astype(F32), N_KV_HEADS)
        w = w_in[l].astype(BF16)
        early = [flat(dense_wg[i]), flat(dense_wu[i])] if dense else []
        late = [flat(dense_wd[i])] + ([flat(moe_wu[i])] if has_moe_next else []) if dense else []
        wqvt = jnp.concatenate([w[:, :ko], w[:, vo:uo]], axis=1).T
        wku = jnp.concatenate([w[:, ko:vo], w[:, uo:]], axis=1)
        (qt, k, v3, u3), early_bf = _inproj(x2d, row(attn_norm_g[l]), wqvt, wku, col(qg), row(kg), avg,
                                            casts=early)
        (attn,), late_bf = _attention(qt, k, v3, sinks[l].astype(F32), col(attn_out_g[l]), eye, seq,
                                      casts=late)
        tables = _ssm_tables(lam_re[l], lam_im[l], log_dt[l], b_re[l], b_im[l], c_re[l], c_im[l],
                             d_skip[l], n_steps)
        y3 = _ssm(u3, tables, eye[:LANES, :LANES], bsz, seq)
        mix_args = (x2d, attn, y3, w_glu[l].astype(BF16), row(ssm_out_g[l]), w_o[l].astype(BF16),
                    row(ffn_norm_g[l]))
        if dense:
            ahead = [flat(moe_wg[i]), flat(moe_wd[i])] if has_moe_next else []
            (x2d,), ahead_bf = _mix_ffn(*mix_args, early_bf[0], early_bf[1], late_bf[0], casts=ahead)
            ahead_bf = list(ahead_bf) + list(late_bf[1:])
        else:
            x1, logits = _mix_router(*mix_args, router_w[i].astype(F32).T)
            x2d = _moe(x1, row(ffn_norm_g[l]), logits, ahead_bf[0].reshape(moe_wg[i].shape),
                       ahead_bf[2].reshape(moe_wu[i].shape), ahead_bf[1].reshape(moe_wd[i].shape))
    return x2d.reshape(bsz, seq, D_MODEL)
```

```python
import functools

import jax
import jax.numpy as jnp
from jax import lax
from jax.experimental import pallas as pl
from jax.experimental.pallas import tpu as pltpu

F32 = jnp.float32
BF16 = jnp.bfloat16

D_MODEL = 1024
HEAD_DIM = 64
N_HEADS = 8
N_KV_HEADS = 2
GQA = N_HEADS // N_KV_HEADS
ATTN_WIDTH = N_HEADS * HEAD_DIM
KV_WIDTH = N_KV_HEADS * HEAD_DIM
ATTN_BLOCK = 128
SSM_WIDTH = D_MODEL - ATTN_WIDTH
SSM_GROUP_DIM = 16
SSM_GROUPS = SSM_WIDTH // SSM_GROUP_DIM
SSM_STATE = 64
D_FF = 3584
N_EXPERTS = 8
TOP_K = 2
NORM_EPS = 1e-6
NEG_INF = -1e30

LANES = 128
SUBLANES = 8
ROW_TILE = 512
SSM_CHUNK = 16
SSM_SEQS = SUBLANES
SSM_VEC = SSM_CHUNK * SSM_GROUP_DIM
SSM_LANE_GROUPS = LANES // SSM_GROUP_DIM
FF_CHUNK = 256
MOE_ROWS = 512
VMEM_LIMIT = 56 * 1024 * 1024
BIG_VMEM_LIMIT = 60 * 1024 * 1024

NT_DIMS = (((1,), (1,)), ((), ()))


def _cparams(sem, vmem=VMEM_LIMIT):
    return pltpu.CompilerParams(dimension_semantics=sem, vmem_limit_bytes=vmem)


def _const_spec(shape):
    n = len(shape)
    return pl.BlockSpec(shape, lambda *_: (0,) * n)


def _resident_spec(shape):
    n = len(shape)
    return pl.BlockSpec(shape, lambda *_: (0,) * n, pipeline_mode=pl.Buffered(1))


def _row_tiled_call(body, steps, in_specs, out_specs, out_shape, scratch_shapes, name, args, casts=(),
                    vmem=VMEM_LIMIT):
    n_in, n_out, n_c = len(in_specs), len(out_specs), len(casts)

    def hosted(*refs):
        ins, cast_in = refs[:n_in], refs[n_in:n_in + n_c]
        outs = refs[n_in + n_c:n_in + n_c + n_out]
        cast_out = refs[n_in + n_c + n_out:n_in + 2 * n_c + n_out]
        body(*ins, *outs, *refs[n_in + 2 * n_c + n_out:])
        for src, dst in zip(cast_in, cast_out):
            dst[...] = src[...].astype(BF16)

    for w in casts:
        assert w.ndim == 2 and w.shape[0] % (steps * 16) == 0, w.shape
    cast_specs = [pl.BlockSpec((w.shape[0] // steps, w.shape[1]), lambda i: (i, 0)) for w in casts]
    res = pl.pallas_call(
        hosted,
        grid=(steps,),
        in_specs=list(in_specs) + cast_specs,
        out_specs=list(out_specs) + cast_specs,
        out_shape=list(out_shape) + [jax.ShapeDtypeStruct(w.shape, BF16) for w in casts],
        scratch_shapes=scratch_shapes,
        compiler_params=_cparams(("parallel",), vmem),
        name=name,
    )(*args, *casts)
    return res[:n_out], res[n_out:]


def _inproj_body(x_ref, g_ref, wqvt_ref, wku_ref, qg_ref, kg_ref, avg_ref,
                 qt_out, k_out, v3_out, u3_out, uscr):
    x = x_ref[...]
    ms = jnp.mean(x * x, axis=-1, keepdims=True)
    hn = (x * lax.rsqrt(ms + NORM_EPS) * g_ref[...]).astype(BF16)
    qvt = lax.dot_general(wqvt_ref[...], hn, NT_DIMS, preferred_element_type=F32)
    ku = jnp.dot(hn, wku_ref[...], preferred_element_type=F32)
    qt = qvt[:ATTN_WIDTH]
    qms = jnp.dot(avg_ref[...], (qt * qt).astype(BF16), preferred_element_type=F32)
    qt_out[...] = (qt * lax.rsqrt(qms + NORM_EPS) * qg_ref[...]).astype(BF16)
    k = ku[:, :KV_WIDTH]
    kms = jnp.dot((k * k).astype(BF16), avg_ref[:KV_WIDTH, :KV_WIDTH], preferred_element_type=F32)
    k_out[...] = (k * lax.rsqrt(kms + NORM_EPS) * kg_ref[...]).astype(BF16)
    vt = qvt[ATTN_WIDTH:]
    for b in range(ROW_TILE // ATTN_BLOCK):
        v3_out[b] = vt[:, b * ATTN_BLOCK:(b + 1) * ATTN_BLOCK].astype(BF16)
    u = ku[:, KV_WIDTH:]
    for lb in range(SSM_WIDTH // LANES):
        uscr[lb] = u[:, lb * LANES:(lb + 1) * LANES]
    for t in range(SSM_CHUNK):
        for lb in range(SSM_WIDTH // LANES):
            u3_out[t, :, lb * LANES:(lb + 1) * LANES] = (
                uscr[lb, pl.ds(t, ROW_TILE // SSM_CHUNK, stride=SSM_CHUNK), :].astype(BF16))


def _inproj(x2d, g, wqvt, wku, qg, kg, avg, casts=()):
    t = x2d.shape[0]
    cpt = ROW_TILE // SSM_CHUNK
    return _row_tiled_call(
        _inproj_body, t // ROW_TILE,
        in_specs=[pl.BlockSpec((ROW_TILE, D_MODEL), lambda i: (i, 0)),
                  _const_spec((1, D_MODEL)), _const_spec((ATTN_WIDTH + KV_WIDTH, D_MODEL)),
                  _const_spec((D_MODEL, KV_WIDTH + SSM_WIDTH)), _const_spec((ATTN_WIDTH, 1)),
                  _const_spec((1, KV_WIDTH)), _const_spec((ATTN_WIDTH, ATTN_WIDTH))],
        out_specs=[pl.BlockSpec((ATTN_WIDTH, ROW_TILE), lambda i: (0, i)),
                   pl.BlockSpec((ROW_TILE, KV_WIDTH), lambda i: (i, 0)),
                   pl.BlockSpec((ROW_TILE // ATTN_BLOCK, KV_WIDTH, ATTN_BLOCK), lambda i: (i, 0, 0)),
                   pl.BlockSpec((SSM_CHUNK, cpt, SSM_WIDTH), lambda i: (0, i, 0))],
        out_shape=[jax.ShapeDtypeStruct((ATTN_WIDTH, t), BF16),
                   jax.ShapeDtypeStruct((t, KV_WIDTH), BF16),
                   jax.ShapeDtypeStruct((t // ATTN_BLOCK, KV_WIDTH, ATTN_BLOCK), BF16),
                   jax.ShapeDtypeStruct((SSM_CHUNK, t // SSM_CHUNK, SSM_WIDTH), BF16)],
        scratch_shapes=[pltpu.VMEM((SSM_WIDTH // LANES, ROW_TILE, LANES), F32)],
        name="inproj", args=(x2d, g, wqvt, wku, qg, kg, avg), casts=casts)


def _attn_body(sink_ref, qt_ref, k_ref, v3_ref, g_ref, eye_ref, o_ref, at_scr, *, blocks_per_seq):
    i = pl.program_id(0)
    key = lax.broadcasted_iota(jnp.int32, (ATTN_BLOCK, ATTN_BLOCK), 0)
    qry = lax.broadcasted_iota(jnp.int32, (ATTN_BLOCK, ATTN_BLOCK), 1)
    cur_ok = key <= qry
    zpad = jnp.zeros((HEAD_DIM, ATTN_BLOCK), BF16)
    nblk = ROW_TILE // ATTN_BLOCK
    for blk in range(nblk):
        gblk = i * nblk + blk
        pblk = jnp.maximum(gblk - 1, 0)
        row0 = pl.multiple_of(gblk * ATTN_BLOCK, ATTN_BLOCK)
        prev0 = pl.multiple_of(pblk * ATTN_BLOCK, ATTN_BLOCK)
        has_prev = (gblk % blocks_per_seq) != 0
        kc = k_ref[pl.ds(row0, ATTN_BLOCK), :]
        kp = k_ref[pl.ds(prev0, ATTN_BLOCK), :]
        vc = v3_ref[gblk]
        vp = v3_ref[pblk]
        qb = qt_ref[:, blk * ATTN_BLOCK:(blk + 1) * ATTN_BLOCK]
        keep = jnp.logical_or(cur_ok, has_prev)
        outs = []
        for h in range(N_HEADS):
            kv = h // GQA
            qh = qb[h * HEAD_DIM:(h + 1) * HEAD_DIM, :]
            qpad = jnp.concatenate([qh, zpad] if kv == 0 else [zpad, qh], axis=0)
            sc = jnp.dot(kc, qpad, preferred_element_type=F32)
            sp = jnp.dot(kp, qpad, preferred_element_type=F32)
            s = jnp.where(keep, jnp.where(cur_ok, sc, sp), NEG_INF)
            sink = sink_ref[h]
            m = jnp.maximum(jnp.max(s, axis=0, keepdims=True), sink)
            p = jnp.exp(s - m)
            den = jnp.sum(p, axis=0, keepdims=True) + jnp.exp(sink - m)
            pb = p.astype(BF16)
            zero = jnp.zeros_like(pb)
            vs = slice(kv * HEAD_DIM, (kv + 1) * HEAD_DIM)
            o = (jnp.dot(vc[vs, :], jnp.where(cur_ok, pb, zero), preferred_element_type=F32)
                 + jnp.dot(vp[vs, :], jnp.where(cur_ok, zero, pb), preferred_element_type=F32))
            outs.append(o / den)
        a = jnp.concatenate(outs, axis=0)
        ms = jnp.mean(a * a, axis=0, keepdims=True)
        at_scr[:, blk * ATTN_BLOCK:(blk + 1) * ATTN_BLOCK] = (
            a * lax.rsqrt(ms + NORM_EPS) * g_ref[...]).astype(BF16)
    o_ref[...] = lax.dot_general(eye_ref[...], at_scr[...], NT_DIMS,
                                 preferred_element_type=F32).astype(BF16)


def _attention(qt, k, v3, sinks, gcol, eye, seq, casts=()):
    t = k.shape[0]
    return _row_tiled_call(
        functools.partial(_attn_body, blocks_per_seq=seq // ATTN_BLOCK), t // ROW_TILE,
        in_specs=[pl.BlockSpec(memory_space=pltpu.SMEM),
                  pl.BlockSpec((ATTN_WIDTH, ROW_TILE), lambda i: (0, i)),
                  _const_spec((t, KV_WIDTH)), _const_spec((t // ATTN_BLOCK, KV_WIDTH, ATTN_BLOCK)),
                  _const_spec((ATTN_WIDTH, 1)), _const_spec((ROW_TILE, ROW_TILE))],
        out_specs=[pl.BlockSpec((ROW_TILE, ATTN_WIDTH), lambda i: (i, 0))],
        out_shape=[jax.ShapeDtypeStruct((t, ATTN_WIDTH), BF16)],
        scratch_shapes=[pltpu.VMEM((ATTN_WIDTH, ROW_TILE), BF16)],
        name="swa_attention", args=(sinks, qt, k, v3, gcol, eye), casts=casts)


def _ssm_tables_body(lam_ref, lamc_ref, b_ref, c_ref, erep_ref, etile_ref, eye_ref,
                     m_out, wre_out, wim_out, vre_out, vim_out, k_out, apw_out, *, n_steps):
    hi = lax.Precision.HIGHEST
    lanes = 2 * SSM_STATE
    ch, gd = SSM_CHUNK, SSM_GROUP_DIM
    lr, li, dt = lam_ref[0, 0:1, :], lam_ref[0, 1:2, :], lam_ref[0, 2:3, :]

    def apow(p):
        mag = jnp.exp(p * (lr * dt))
        ang = p * (li * dt)
        return mag * jnp.cos(ang), mag * jnp.sin(ang)

    tau = lax.broadcasted_iota(jnp.int32, (ch, lanes), 0).astype(F32)
    p0r, p0i = apow(tau)
    p1r, p1i = apow(tau + 1.0)
    base = SUBLANES
    doublings = []
    while base << len(doublings) < n_steps:
        doublings.append(ch * (base << len(doublings)))
    assert base << len(doublings) == n_steps and len(doublings) <= SUBLANES - 2, n_steps
    sel = lax.broadcasted_iota(jnp.int32, (SUBLANES, lanes), 0)
    pw = jnp.zeros((SUBLANES, lanes), F32)
    for r, p in enumerate([ch, ch * n_steps] + doublings):
        pw = jnp.where(sel == r, float(p), pw)
    kr, ki = apow(pw)
    k_out[0, 0] = kr[0:2]
    k_out[0, 1] = ki[0:2]
    qr, qi = apow(sel.astype(F32) * float(ch))
    for m in range(len(doublings)):
        dr, di = kr[2 + m:3 + m], ki[2 + m:3 + m]
        qr, qi = (jnp.concatenate([qr, qr * dr - qi * di], axis=0),
                  jnp.concatenate([qi, qr * di + qi * dr], axis=0))
    apw_out[0, 0] = qr
    apw_out[0, 1] = qi

    to_col = lambda p: lax.dot_general(eye_ref[...], p, NT_DIMS, precision=hi,
                                       preferred_element_type=F32)
    c0r, c0i, c1r, c1i = to_col(p0r), to_col(p0i), to_col(p1r), to_col(p1i)
    lrc, lic = lamc_ref[0, :, 0:1], lamc_ref[0, :, 1:2]
    ab_re, ab_im = c1r[:, 0:1], c1i[:, 0:1]
    nr = ab_re - 1.0
    den = lrc * lrc + lic * lic
    f_re = (nr * lrc + ab_im * lic) / den
    f_im = (ab_im * lrc - nr * lic) / den
    br, bi = b_ref[0, 0], b_ref[0, 1]
    bb_re = f_re * br - f_im * bi
    bb_im = f_re * bi + f_im * br
    bx_re = jnp.dot(bb_re, etile_ref[...], precision=hi, preferred_element_type=F32)
    bx_im = jnp.dot(bb_im, etile_ref[...], precision=hi, preferred_element_type=F32)
    px_re = jnp.dot(c0r, erep_ref[...], precision=hi, preferred_element_type=F32)
    px_im = jnp.dot(c0i, erep_ref[...], precision=hi, preferred_element_type=F32)
    w_re = px_re * bx_re - px_im * bx_im
    w_im = px_re * bx_im + px_im * bx_re
    first = lax.broadcasted_iota(jnp.int32, (lanes, SSM_VEC), 0) < SSM_STATE
    pack = lambda w: jnp.concatenate([jnp.where(first, w, 0.0), jnp.where(first, 0.0, w)], axis=1)
    wre_out[0] = pack(w_re).astype(BF16)
    wim_out[0] = pack(w_im).astype(BF16)

    blk = lax.broadcasted_iota(jnp.int32, (SSM_VEC, SSM_VEC), 1) // gd
    for g in range(2):
        cr, ci = c_ref[g, 0], c_ref[g, 1]
        stack = lambda x: x.reshape(ch * gd, lanes)
        ca_re = stack(cr[None] * p0r[:, None, :] - ci[None] * p0i[:, None, :])
        ca_im = stack(cr[None] * p0i[:, None, :] + ci[None] * p0r[:, None, :])
        kt = (jnp.dot(ca_re, bx_re, precision=hi, preferred_element_type=F32)
              - jnp.dot(ca_im, bx_im, precision=hi, preferred_element_type=F32))
        m = jnp.where(blk == 0, kt, 0.0)
        for j in range(1, ch):
            shifted = jnp.concatenate([jnp.zeros((j * gd, SSM_VEC), F32), kt[:SSM_VEC - j * gd]], axis=0)
            m = jnp.where(blk == j, shifted, m)
        m_out[g] = m.astype(BF16)
        vre_out[g] = stack(cr[None] * p1r[:, None, :] - ci[None] * p1i[:, None, :]).astype(BF16)
        vim_out[g] = (-stack(cr[None] * p1i[:, None, :] + ci[None] * p1r[:, None, :])).astype(BF16)


def _ssm_tables(lam_re, lam_im, log_dt, b_re, b_im, c_re, c_im, d_skip, n_steps):
    g_, n_, c_ = SSM_GROUPS, SSM_STATE, SSM_GROUP_DIM
    np_ = g_ // 2
    lanes = 2 * n_
    dt = jnp.broadcast_to(jnp.exp(log_dt.astype(F32))[:, None], (g_, n_))
    lam = jnp.stack([lam_re.astype(F32), lam_im.astype(F32), dt], axis=1)
    lam_row = lam.reshape(np_, 2, 3, n_).transpose(0, 2, 1, 3).reshape(np_, 3, lanes)
    lam_col = lam_row.transpose(0, 2, 1)
    b2 = jnp.stack([b_re.astype(F32).reshape(np_, lanes, c_), b_im.astype(F32).reshape(np_, lanes, c_)],
                   axis=1)
    c2 = jnp.stack([c_re.astype(F32), c_im.astype(F32)], axis=1)
    z = jnp.zeros_like(c2)
    even = (jnp.arange(g_) % 2 == 0)[:, None, None, None]
    c2 = jnp.where(even, jnp.concatenate([c2, z], axis=3), jnp.concatenate([z, c2], axis=3))
    col = jnp.arange(SSM_VEC)
    erep = (SSM_CHUNK - 1 - col[None, :] // c_ == jnp.arange(SSM_CHUNK)[:, None]).astype(F32)
    etile = (col[None, :] % c_ == jnp.arange(c_)[:, None]).astype(F32)
    lead = lambda shape: pl.BlockSpec(shape, lambda p: (p,) + (0,) * (len(shape) - 1))
    m, wre, wim, vre, vim, consts, apw = pl.pallas_call(
        functools.partial(_ssm_tables_body, n_steps=n_steps),
        grid=(np_,),
        in_specs=[lead((1, 3, lanes)), lead((1, lanes, 3)), lead((1, 2, lanes, c_)),
                  lead((2, 2, c_, lanes)), _const_spec((SSM_CHUNK, SSM_VEC)), _const_spec((c_, SSM_VEC)),
                  _const_spec((lanes, lanes))],
        out_specs=[lead((2, SSM_VEC, SSM_VEC)), lead((1, lanes, 2 * SSM_VEC)), lead((1, lanes, 2 * SSM_VEC)),
                   lead((2, SSM_VEC, lanes)), lead((2, SSM_VEC, lanes)),
                   lead((1, 2, 2, lanes)), lead((1, 2, n_steps, lanes))],
        out_shape=[jax.ShapeDtypeStruct((g_, SSM_VEC, SSM_VEC), BF16),
                   jax.ShapeDtypeStruct((np_, lanes, 2 * SSM_VEC), BF16),
                   jax.ShapeDtypeStruct((np_, lanes, 2 * SSM_VEC), BF16),
                   jax.ShapeDtypeStruct((g_, SSM_VEC, lanes), BF16),
                   jax.ShapeDtypeStruct((g_, SSM_VEC, lanes), BF16),
                   jax.ShapeDtypeStruct((np_, 2, 2, lanes), F32),
                   jax.ShapeDtypeStruct((np_, 2, n_steps, lanes), F32)],
        compiler_params=_cparams(("parallel",)),
        name="s5_operators",
    )(lam_row, lam_col, b2, c2, erep, etile, jnp.eye(lanes, dtype=F32))
    dcol = jnp.tile(d_skip.astype(F32).reshape(g_, 1, c_), (1, SSM_CHUNK, 1)).reshape(g_, SSM_VEC, 1)
    return m, wre, wim, vre, vim, consts, apw, dcol


def _ssm_body(u_ref, eye_ref, m_ref, wre_ref, wim_ref, vre_ref, vim_ref, c_ref, apw_ref, d_ref,
              y_ref, ut, yt, pre, pim, sre, sim, *, n_steps, halves, pitch):
    lanes = 2 * SSM_STATE
    nk = u_ref.shape[1]
    gd = SSM_GROUP_DIM
    for t in range(SSM_CHUNK):
        xt = lax.dot_general(eye_ref[...], u_ref[t], NT_DIMS,
                             preferred_element_type=F32).astype(BF16)
        for g in range(SSM_LANE_GROUPS):
            ut[g, t * gd:(t + 1) * gd, :] = xt[g * gd:(g + 1) * gd, :]

    row = lax.broadcasted_iota(jnp.int32, (SSM_SEQS, lanes), 0)
    seq_start = (row % halves) == 0
    shift = lambda x: jnp.where(seq_start, 0.0, pltpu.roll(x, 1, axis=0))

    npairs = SSM_LANE_GROUPS // 2
    for pp in range(npairs):
        ucat = jnp.concatenate([ut[2 * pp], ut[2 * pp + 1]], axis=0)
        p_re = jnp.dot(wre_ref[pp], ucat, preferred_element_type=F32).T
        p_im = jnp.dot(wim_ref[pp], ucat, preferred_element_type=F32).T
        if pitch == n_steps:
            pre[pp] = p_re
            pim[pp] = p_im
        else:
            for s in range(SSM_SEQS):
                pre[pp, s * pitch:s * pitch + n_steps, :] = p_re[s * n_steps:(s + 1) * n_steps]
                pim[pp, s * pitch:s * pitch + n_steps, :] = p_im[s * n_steps:(s + 1) * n_steps]

    a16 = [(jnp.broadcast_to(c_ref[pp, 0, 0:1, :], (SSM_SEQS, lanes)),
            jnp.broadcast_to(c_ref[pp, 1, 0:1, :], (SSM_SEQS, lanes))) for pp in range(npairs)]

    def step(i, carry):
        idx = pl.ds(i, SSM_SEQS, stride=pitch)
        out = []
        for pp in range(npairs):
            s_re, s_im = carry[pp]
            ar, ai = a16[pp]
            sre[pp, idx, :] = s_re
            sim[pp, idx, :] = s_im
            out.append((ar * s_re - ai * s_im + pre[pp, idx, :],
                        ar * s_im + ai * s_re + pim[pp, idx, :]))
        return tuple(out)

    zero = jnp.zeros((SSM_SEQS, lanes), F32)
    ends = lax.fori_loop(0, n_steps, step, ((zero, zero),) * npairs, unroll=2)

    for pp in range(npairs):
        e_re, e_im = ends[pp]
        br = jnp.broadcast_to(c_ref[pp, 0, 1:2, :], (SSM_SEQS, lanes))
        bi = jnp.broadcast_to(c_ref[pp, 1, 1:2, :], (SSM_SEQS, lanes))
        c_re = jnp.zeros_like(e_re)
        c_im = jnp.zeros_like(e_im)
        for _ in range(halves - 1):
            n_re = e_re + br * c_re - bi * c_im
            n_im = e_im + br * c_im + bi * c_re
            c_re, c_im = shift(n_re), shift(n_im)

        apr = apw_ref[pp, 0][None, :, :]
        api = apw_ref[pp, 1][None, :, :]
        if pitch == n_steps:
            s3r = sre[pp].reshape(SSM_SEQS, n_steps, lanes)
            s3i = sim[pp].reshape(SSM_SEQS, n_steps, lanes)
        else:
            s3r = jnp.stack([sre[pp, s * pitch:s * pitch + n_steps, :] for s in range(SSM_SEQS)])
            s3i = jnp.stack([sim[pp, s * pitch:s * pitch + n_steps, :] for s in range(SSM_SEQS)])
        t_re = (s3r + apr * c_re[:, None, :] - api * c_im[:, None, :]).reshape(nk, lanes)
        t_im = (s3i + apr * c_im[:, None, :] + api * c_re[:, None, :]).reshape(nk, lanes)
        tr_hi = t_re.astype(BF16)
        tr_lo = (t_re - tr_hi.astype(F32)).astype(BF16)
        ti_hi = t_im.astype(BF16)
        ti_lo = (t_im - ti_hi.astype(F32)).astype(BF16)
        for g in (2 * pp, 2 * pp + 1):
            ug = ut[g]
            y = jnp.dot(m_ref[g], ug, preferred_element_type=F32)
            y += lax.dot_general(vre_ref[g], tr_hi, NT_DIMS, preferred_element_type=F32)
            y += lax.dot_general(vre_ref[g], tr_lo, NT_DIMS, preferred_element_type=F32)
            y += lax.dot_general(vim_ref[g], ti_hi, NT_DIMS, preferred_element_type=F32)
            y += lax.dot_general(vim_ref[g], ti_lo, NT_DIMS, preferred_element_type=F32)
            yt[g] = y + d_ref[g] * ug.astype(F32)

    for t in range(SSM_CHUNK):
        rows = jnp.concatenate([yt[g, t * gd:(t + 1) * gd, :] for g in range(SSM_LANE_GROUPS)], axis=0)
        y_ref[t] = rows.T


def _ssm(u3, tables, eye, bsz, seq):
    m, wre, wim, vre, vim, consts, apw, dcol = tables
    nk = u3.shape[1]
    halves = SSM_SEQS // bsz
    n_steps = seq // (SSM_CHUNK * halves)
    lanes = 2 * SSM_STATE
    lg = SSM_LANE_GROUPS
    pitch = n_steps + SUBLANES if (n_steps // SUBLANES) % 2 == 0 else n_steps
    lead = lambda shape: pl.BlockSpec(shape, lambda b: (b,) + (0,) * (len(shape) - 1))
    return pl.pallas_call(
        functools.partial(_ssm_body, n_steps=n_steps, halves=halves, pitch=pitch),
        grid=(SSM_GROUPS // lg,),
        in_specs=[pl.BlockSpec((SSM_CHUNK, nk, LANES), lambda b: (0, 0, b)),
                  _const_spec((LANES, LANES)),
                  lead((lg, SSM_VEC, SSM_VEC)),
                  lead((lg // 2, lanes, 2 * SSM_VEC)), lead((lg // 2, lanes, 2 * SSM_VEC)),
                  lead((lg, SSM_VEC, lanes)), lead((lg, SSM_VEC, lanes)),
                  lead((lg // 2, 2, 2, lanes)), lead((lg // 2, 2, n_steps, lanes)),
                  lead((lg, SSM_VEC, 1))],
        out_specs=pl.BlockSpec((SSM_CHUNK, nk, LANES), lambda b: (0, 0, b)),
        out_shape=jax.ShapeDtypeStruct((SSM_CHUNK, nk, SSM_WIDTH), F32),
        scratch_shapes=[pltpu.VMEM((lg, SSM_VEC, nk), BF16), pltpu.VMEM((lg, SSM_VEC, nk), F32)]
                       + [pltpu.VMEM((lg // 2, SSM_SEQS * pitch, lanes), F32)] * 4,
        compiler_params=_cparams(("parallel",)),
        name="s5_ssm",
    )(u3, eye, m, wre, wim, vre, vim, consts, apw, dcol)


def _mix_math(x_ref, a_ref, y3_ref, wglu_ref, sg_ref, wo_ref, fg_ref, yscr):
    nlb = SSM_WIDTH // LANES
    for t in range(SSM_CHUNK):
        for lb in range(nlb):
            yscr[lb, pl.ds(t, ROW_TILE // SSM_CHUNK, stride=SSM_CHUNK), :] = (
                y3_ref[t, :, lb * LANES:(lb + 1) * LANES])
    y = jax.nn.gelu(jnp.concatenate([yscr[lb] for lb in range(nlb)], axis=1))
    z = y * jax.nn.sigmoid(jnp.dot(y.astype(BF16), wglu_ref[...], preferred_element_type=F32))
    ms = jnp.mean(z * z, axis=-1, keepdims=True)
    sn = (z * lax.rsqrt(ms + NORM_EPS) * sg_ref[...]).astype(BF16)
    x1 = (x_ref[...]
          + jnp.dot(a_ref[...], wo_ref[:ATTN_WIDTH, :], preferred_element_type=F32)
          + jnp.dot(sn, wo_ref[ATTN_WIDTH:, :], preferred_element_type=F32))
    ms1 = jnp.mean(x1 * x1, axis=-1, keepdims=True)
    return x1, x1 * lax.rsqrt(ms1 + NORM_EPS) * fg_ref[...]


def _mix_router_body(x_ref, a_ref, y3_ref, wglu_ref, sg_ref, wo_ref, fg_ref, rwt_ref,
                     x1_out, lg_out, yscr):
    x1, hn = _mix_math(x_ref, a_ref, y3_ref, wglu_ref, sg_ref, wo_ref, fg_ref, yscr)
    x1_out[...] = x1
    lane = lax.broadcasted_iota(jnp.int32, lg_out.shape, 1)
    lg = jnp.zeros(lg_out.shape, F32)
    for e in range(lg_out.shape[1]):
        lg = jnp.where(lane == e, jnp.sum(hn * rwt_ref[e:e + 1, :], axis=-1, keepdims=True), lg)
    lg_out[...] = lg


def _mix_ffn_body(x_ref, a_ref, y3_ref, wglu_ref, sg_ref, wo_ref, fg_ref, wg_ref, wu_ref, wd_ref,
                  o_ref, yscr):
    x1, hn = _mix_math(x_ref, a_ref, y3_ref, wglu_ref, sg_ref, wo_ref, fg_ref, yscr)
    h = hn.astype(BF16)
    o_ref[...] = x1
    for c in range(D_FF // FF_CHUNK):
        sl = slice(c * FF_CHUNK, (c + 1) * FF_CHUNK)
        g = jnp.dot(h, wg_ref[:, sl], preferred_element_type=F32)
        u = jnp.dot(h, wu_ref[:, sl], preferred_element_type=F32)
        a = (jax.nn.silu(g) * u).astype(BF16)
        o_ref[...] += jnp.dot(a, wd_ref[sl, :], preferred_element_type=F32)


def _mix_specs():
    row = lambda w: pl.BlockSpec((ROW_TILE, w), lambda i: (i, 0))
    in_specs = [row(D_MODEL), row(ATTN_WIDTH),
                pl.BlockSpec((SSM_CHUNK, ROW_TILE // SSM_CHUNK, SSM_WIDTH), lambda i: (0, i, 0)),
                _resident_spec((SSM_WIDTH, SSM_WIDTH)), _const_spec((1, SSM_WIDTH)),
                _resident_spec((D_MODEL, D_MODEL)), _const_spec((1, D_MODEL))]
    return row, in_specs, pltpu.VMEM((SSM_WIDTH // LANES, ROW_TILE, LANES), F32)


def _mix_router(x2d, attn, y3, wglu, sg, wo, fg, rwt):
    t = x2d.shape[0]
    ne = rwt.shape[0]
    row, in_specs, yscr = _mix_specs()
    return pl.pallas_call(
        _mix_router_body,
        grid=(t // ROW_TILE,),
        in_specs=in_specs + [_const_spec((ne, D_MODEL))],
        out_specs=[row(D_MODEL), row(ne)],
        out_shape=[jax.ShapeDtypeStruct((t, D_MODEL), F32), jax.ShapeDtypeStruct((t, ne), F32)],
        scratch_shapes=[yscr],
        compiler_params=_cparams(("parallel",)),
        name="mix_router",
    )(x2d, attn, y3, wglu, sg, wo, fg, rwt)


def _mix_ffn(x2d, attn, y3, wglu, sg, wo, fg, wg, wu, wd, casts=()):
    t = x2d.shape[0]
    row, in_specs, yscr = _mix_specs()
    return _row_tiled_call(
        _mix_ffn_body, t // ROW_TILE,
        in_specs=in_specs + [_resident_spec((D_MODEL, D_FF)), _resident_spec((D_MODEL, D_FF)),
                             _resident_spec((D_FF, D_MODEL))],
        out_specs=[row(D_MODEL)],
        out_shape=[jax.ShapeDtypeStruct((t, D_MODEL), F32)],
        scratch_shapes=[yscr],
        name="mix_dense_ffn", args=(x2d, attn, y3, wglu, sg, wo, fg, wg, wu, wd), casts=casts,
        vmem=BIG_VMEM_LIMIT)


def _expert_ffn_body(e_ref, nused_ref, nvalid_ref, tok_ref, dst_ref, x1_hbm, fg_ref,
                     wg_ref, wu_ref, wd_ref, out_hbm, xbuf, acc, ybuf, gsem, ssem):
    b = pl.program_id(0)
    n_used = nused_ref[0]
    n_ff = D_FF // FF_CHUNK
    rows_per_chunk = -(-MOE_ROWS // n_ff)

    def gather_row(blk, r, priority=0):
        i = tok_ref[blk * MOE_ROWS + r]
        pltpu.make_async_copy(x1_hbm.at[pl.ds(i, 1)], xbuf.at[pl.ds(r, 1)], gsem).start(priority)

    def scatter_row(blk, r):
        i = dst_ref[blk * MOE_ROWS + r]
        pltpu.make_async_copy(ybuf.at[pl.ds(r, 1)], out_hbm.at[pl.ds(i, 1)], ssem).start()

    def scatter_rows(blk, n):
        def body(r, carry):
            scatter_row(blk, r)
            return carry
        lax.fori_loop(0, n, body, 0)

    def wait_gather():
        pltpu.make_async_copy(x1_hbm.at[pl.ds(0, MOE_ROWS)], xbuf, gsem).wait()

    def wait_scatter(n):
        n8 = pl.multiple_of(n & -SUBLANES, SUBLANES)

        @pl.when(n8 > 0)
        def _():
            pltpu.make_async_copy(ybuf.at[pl.ds(0, n8)], out_hbm.at[pl.ds(0, n8)], ssem).wait()

        def one(_, carry):
            pltpu.make_async_copy(ybuf.at[pl.ds(0, 1)], out_hbm.at[pl.ds(0, 1)], ssem).wait()
            return carry
        lax.fori_loop(0, n - n8, one, 0)

    def ffn(scatter_prev):
        wait_gather()
        x = xbuf[...]
        ms = jnp.mean(x * x, axis=-1, keepdims=True)
        h = (x * lax.rsqrt(ms + NORM_EPS) * fg_ref[...]).astype(BF16)
        for c in range(n_ff):
            for r in range(c * rows_per_chunk, min((c + 1) * rows_per_chunk, MOE_ROWS)):
                gather_row(b + 1, r, priority=r % 2)
                if scatter_prev:
                    scatter_row(b - 1, r)
            sl = slice(c * FF_CHUNK, (c + 1) * FF_CHUNK)
            g = jnp.dot(h, wg_ref[0, :, sl], preferred_element_type=F32)
            u = jnp.dot(h, wu_ref[0, :, sl], preferred_element_type=F32)
            a = (jax.nn.silu(g) * u).astype(BF16)
            part = jnp.dot(a, wd_ref[0, sl, :], preferred_element_type=F32)
            if c == 0:
                acc[...] = part
            else:
                acc[...] += part

    @pl.when(b < n_used)
    def _():
        @pl.when(b == 0)
        def _():
            def body(r, carry):
                gather_row(0, r)
                return carry
            lax.fori_loop(0, MOE_ROWS, body, 0, unroll=8)

        n_prev = jnp.where(b >= 1, nvalid_ref[jnp.maximum(b - 1, 0)], 0)

        @pl.when(n_prev == MOE_ROWS)
        def _():
            ffn(scatter_prev=True)

        @pl.when(n_prev != MOE_ROWS)
        def _():
            scatter_rows(b - 1, n_prev)
            ffn(scatter_prev=False)

        wait_scatter(n_prev)
        ybuf[...] = acc[...]

        @pl.when(b == n_used - 1)
        def _():
            scatter_rows(b, nvalid_ref[b])
            wait_scatter(nvalid_ref[b])
            wait_gather()


def _expert_ffn(x1, fg, tok, dst, block_e, n_used, n_valid, wg, wu, wd, nblk, out_rows):
    wspec = lambda shape: pl.BlockSpec(shape, lambda b, e, *_: (e[b], 0, 0))
    return pl.pallas_call(
        _expert_ffn_body,
        grid_spec=pltpu.PrefetchScalarGridSpec(
            num_scalar_prefetch=5, grid=(nblk,),
            in_specs=[pl.BlockSpec(memory_space=pl.ANY),
                      pl.BlockSpec((1, D_MODEL), lambda b, *_: (0, 0)),
                      wspec((1, D_MODEL, D_FF)), wspec((1, D_MODEL, D_FF)), wspec((1, D_FF, D_MODEL))],
            out_specs=pl.BlockSpec(memory_space=pl.ANY),
            scratch_shapes=[pltpu.VMEM((MOE_ROWS, D_MODEL), F32)] * 3
                           + [pltpu.SemaphoreType.DMA(()), pltpu.SemaphoreType.DMA(())]),
        out_shape=jax.ShapeDtypeStruct((out_rows, D_MODEL), F32),
        compiler_params=pltpu.CompilerParams(dimension_semantics=("arbitrary",),
                                             vmem_limit_bytes=BIG_VMEM_LIMIT),
        name="moe_expert_ffn",
    )(block_e, n_used, n_valid, tok, dst, x1, fg, wg, wu, wd)


def _combine_body(x_ref, y0_ref, y1_ref, g_ref, o_ref):
    o_ref[...] = x_ref[...] + y0_ref[...] * g_ref[:, 0:1] + y1_ref[...] * g_ref[:, 1:2]


def _combine(x1, y2, gates):
    t = x1.shape[0]
    rows = 2 * ROW_TILE if t % (2 * ROW_TILE) == 0 else ROW_TILE
    nt = t // rows
    return pl.pallas_call(
        _combine_body,
        grid=(nt,),
        in_specs=[pl.BlockSpec((rows, D_MODEL), lambda i: (i, 0)),
                  pl.BlockSpec((rows, D_MODEL), lambda i: (i, 0)),
                  pl.BlockSpec((rows, D_MODEL), lambda i: (i + nt, 0)),
                  pl.BlockSpec((rows, TOP_K), lambda i: (i, 0))],
        out_specs=pl.BlockSpec((rows, D_MODEL), lambda i: (i, 0)),
        out_shape=jax.ShapeDtypeStruct((t, D_MODEL), F32),
        compiler_params=_cparams(("parallel",)),
        name="moe_combine",
    )(x1, y2, y2, gates)


def _moe(x1, fg, logits, wg, wu, wd):
    t = x1.shape[0]
    n_assign = t * TOP_K
    nblk = n_assign // MOE_ROWS + N_EXPERTS
    cap = nblk * MOE_ROWS
    top_v, top_i = lax.top_k(logits, TOP_K)
    gates = jax.nn.softmax(top_v, axis=-1)
    flat_e = top_i.reshape(-1).astype(jnp.int32)
    order = jnp.argsort(flat_e, stable=True).astype(jnp.int32)
    experts = jnp.arange(N_EXPERTS, dtype=jnp.int32)
    counts = jnp.sum((flat_e[:, None] == experts[None, :]).astype(jnp.int32), axis=0)
    start = jnp.cumsum(counts) - counts
    padded = ((counts + MOE_ROWS - 1) // MOE_ROWS) * MOE_ROWS
    pend = jnp.cumsum(padded)
    pstart = pend - padded
    p = jnp.arange(cap, dtype=jnp.int32)
    e_p = jnp.minimum(jnp.sum((p[:, None] >= pend[None, :]).astype(jnp.int32), axis=1), N_EXPERTS - 1)
    rank = p - pstart[e_p]
    valid = jnp.logical_and(rank < counts[e_p], p < pend[-1])
    a_p = order[jnp.clip(start[e_p] + rank, 0, n_assign - 1)]
    tok = jnp.where(valid, a_p // TOP_K, 0).astype(jnp.int32)
    dst = jnp.where(valid, (a_p % TOP_K) * t + a_p // TOP_K, 0).astype(jnp.int32)
    block_e = e_p[::MOE_ROWS]
    n_used = (pend[-1:] // MOE_ROWS).astype(jnp.int32)
    n_valid = jnp.sum(valid.reshape(nblk, MOE_ROWS).astype(jnp.int32), axis=1)
    y2 = _expert_ffn(x1, fg, tok, dst, block_e, n_used, n_valid, wg, wu, wd, nblk, n_assign)
    return _combine(x1, y2, gates)


def kernel(x, attn_norm_g, w_in, q_norm_g, k_norm_g, sinks, lam_re, lam_im, log_dt, b_re, b_im,
           c_re, c_im, d_skip, w_glu, attn_out_g, ssm_out_g, w_o, ffn_norm_g, dense_wg, dense_wu,
           dense_wd, router_w, moe_wg, moe_wu, moe_wd):
    bsz, seq, _ = x.shape
    depth = w_in.shape[0]
    assert SSM_SEQS % bsz == 0 and seq % (SSM_CHUNK * (SSM_SEQS // bsz)) == 0
    assert seq % ROW_TILE == 0 and ROW_TILE % ATTN_BLOCK == 0
    x2d = x.reshape(bsz * seq, D_MODEL).astype(F32)
    head = jnp.arange(ATTN_WIDTH) // HEAD_DIM
    avg = jnp.where(head[:, None] == head[None, :], 1.0 / HEAD_DIM, 0.0).astype(BF16)
    eye = jnp.eye(ROW_TILE, dtype=BF16)
    n_steps = seq // (SSM_CHUNK * (SSM_SEQS // bsz))
    row = lambda v: v.astype(F32).reshape(1, -1)
    col = lambda v: v.astype(F32).reshape(-1, 1)
    ko, vo, uo = ATTN_WIDTH, ATTN_WIDTH + KV_WIDTH, ATTN_WIDTH + 2 * KV_WIDTH
    flat = lambda w: w.astype(F32).reshape(-1, w.shape[-1])
    ahead_bf = None
    for l in range(depth):
        i = l // 2
        dense = l % 2 == 0
        has_moe_next = dense and l + 1 < depth
        qg = jnp.tile(q_norm_g[l].astype(F32), N_HEADS) * (HEAD_DIM ** -0.5)
        kg = jnp.tile(k_norm_g[l].astype(F32), N_KV_HEADS)
        w = w_in[l].astype(BF16)
        early = [flat(dense_wg[i]), flat(dense_wu[i])] if dense else []
        late = [flat(dense_wd[i])] + ([flat(moe_wu[i])] if has_moe_next else []) if dense else []
        wqvt = jnp.concatenate([w[:, :ko], w[:, vo:uo]], axis=1).T
        wku = jnp.concatenate([w[:, ko:vo], w[:, uo:]], axis=1)
        (qt, k, v3, u3), early_bf = _inproj(x2d, row(attn_norm_g[l]), wqvt, wku, col(qg), row(kg), avg,
                                            casts=early)
        (attn,), late_bf = _attention(qt, k, v3, sinks[l].astype(F32), col(attn_out_g[l]), eye, seq,
                                      casts=late)
        tables = _ssm_tables(lam_re[l], lam_im[l], log_dt[l], b_re[l], b_im[l], c_re[l], c_im[l],
                             d_skip[l], n_steps)
        y3 = _ssm(u3, tables, eye[:LANES, :LANES], bsz, seq)
        mix_args = (x2d, attn, y3, w_glu[l].astype(BF16), row(ssm_out_g[l]), w_o[l].astype(BF16),
                    row(ffn_norm_g[l]))
        if dense:
            ahead = [flat(moe_wg[i]), flat(moe_wd[i])] if has_moe_next else []
            (x2d,), ahead_bf = _mix_ffn(*mix_args, early_bf[0], early_bf[1], late_bf[0], casts=ahead)
            ahead_bf = list(ahead_bf) + list(late_bf[1:])
        else:
            x1, logits = _mix_router(*mix_args, router_w[i].astype(F32).T)
            x2d = _moe(x1, row(ffn_norm_g[l]), logits, ahead_bf[0].reshape(moe_wg[i].shape),
                       ahead_bf[2].reshape(moe_wu[i].shape), ahead_bf[1].reshape(moe_wd[i].shape))
    return x2d.reshape(bsz, seq, D_MODEL)
```
